```python
import math
import jax, jax.numpy as jnp
from jax import lax
import numpy as np

D_MODEL = 2048
BATCH = 8
SEQ = 4096
DEPTH = 4

ATTN_HEADS = 8
ATTN_KV_HEADS = 2
HEAD_DIM = 128
WINDOW = 128
ROPE_THETA = 10000.0
DN_HEADS = 4
DN_HEAD_DIM = 128
DN_CONV = 4
DN_CHUNK = 64
S5_GROUPS = 32
S5_GROUP_CH = 16
S5_STATE = 64
ATTN_WIDTH = ATTN_HEADS * HEAD_DIM
ATTN_KV_WIDTH = ATTN_KV_HEADS * HEAD_DIM
DN_WIDTH = DN_HEADS * DN_HEAD_DIM
S5_WIDTH = S5_GROUPS * S5_GROUP_CH
MIX_WIDTH = ATTN_WIDTH + DN_WIDTH + S5_WIDTH
IN_SPLITS = (ATTN_WIDTH, ATTN_KV_WIDTH, ATTN_KV_WIDTH, 3 * DN_WIDTH, DN_WIDTH, DN_HEADS, DN_HEADS, S5_WIDTH)
IN_WIDTH = sum(IN_SPLITS)
D_FF = 5632
FFN_RES_WEIGHT = 0.5
NORM_EPS = 1e-6

kernel_name = "hymba_style_swa_deltanet_s5_macaron"


def rms_norm(x, gain):
    xf = x.astype(jnp.float32)
    y = xf * lax.rsqrt(jnp.mean(xf * xf, axis=-1, keepdims=True) + NORM_EPS)
    return (y * gain.astype(jnp.float32)).astype(x.dtype)


def l2_norm(x):
    return x * lax.rsqrt(jnp.sum(x * x, axis=-1, keepdims=True) + NORM_EPS)


def swiglu(h, w_gate, w_up, w_down):
    return (jax.nn.silu(h @ w_gate) * (h @ w_up)) @ w_down


def rope_tables(seq):
    half = HEAD_DIM // 2
    inv_freq = ROPE_THETA ** (-jnp.arange(half, dtype=jnp.float32) / half)
    ang = jnp.arange(seq, dtype=jnp.float32)[:, None] * inv_freq[None, :]
    return jnp.cos(ang), jnp.sin(ang)


def apply_rope(x, cos, sin):
    half = HEAD_DIM // 2
    xf = x.astype(jnp.float32)
    x1, x2 = xf[..., :half], xf[..., half:]
    c = cos[None, :, None, :]
    s = sin[None, :, None, :]
    return jnp.concatenate([x1 * c - x2 * s, x2 * c + x1 * s], axis=-1).astype(x.dtype)


def sliding_window_attention(q, k, v, sinks):
    b, s, hq, d = q.shape
    hkv = k.shape[2]
    grp = hq // hkv
    nb = s // WINDOW
    qb = q.reshape(b, nb, WINDOW, hkv, grp, d)
    kb = k.reshape(b, nb, WINDOW, hkv, d)
    vb = v.reshape(b, nb, WINDOW, hkv, d)
    pad = ((0, 0), (1, 0), (0, 0), (0, 0), (0, 0))
    kk = jnp.concatenate([jnp.pad(kb, pad)[:, :-1], kb], axis=2)
    vv = jnp.concatenate([jnp.pad(vb, pad)[:, :-1], vb], axis=2)
    scores = jnp.einsum('bnqhgd,bnkhd->bnhgqk', qb, kk, preferred_element_type=jnp.float32) * (d ** -0.5)
    qi = jnp.arange(WINDOW)[:, None] + WINDOW
    kj = jnp.arange(2 * WINDOW)[None, :]
    rel = qi - kj
    band = (rel >= 0) & (rel < WINDOW)
    first = (jnp.arange(nb) == 0)[:, None, None] & (kj < WINDOW)[None]
    mask = band[None] & jnp.logical_not(first)
    scores = jnp.where(mask[None, :, None, None], scores, -jnp.inf)
    sink = sinks.astype(jnp.float32).reshape(hkv, grp)[None, None, :, :, None, None]
    m = jnp.maximum(jnp.max(scores, axis=-1, keepdims=True), sink)
    p = jnp.exp(scores - m)
    p = p / (jnp.sum(p, axis=-1, keepdims=True) + jnp.exp(sink - m))
    out = jnp.einsum('bnhgqk,bnkhd->bnqhgd', p.astype(vv.dtype), vv)
    return out.reshape(b, s, hq * d)


def causal_depthwise_conv(u, w):
    taps = w.shape[0]
    return lax.conv_general_dilated(u, w[:, None, :], window_strides=(1,), padding=[(taps - 1, 0)],
                                    dimension_numbers=('NWC', 'WIO', 'NWC'), feature_group_count=u.shape[-1])


def gated_delta_rule(q, k, v, g, beta):
    b, s, h, dk = q.shape
    dv = v.shape[-1]
    c = DN_CHUNK
    n = s // c

    def chunks(t):
        return t.reshape(b, n, c, h, -1).transpose(0, 1, 3, 2, 4)

    q = chunks(q) * (dk ** -0.5)
    k = chunks(k)
    v = chunks(v)
    beta = chunks(beta[..., None])[..., 0]
    g = jnp.cumsum(chunks(g[..., None])[..., 0], axis=-1)
    causal = jnp.tril(jnp.ones((c, c), bool))
    strict = jnp.tril(jnp.ones((c, c), bool), -1)
    decay = jnp.exp(jnp.where(causal, g[..., :, None] - g[..., None, :], -jnp.inf))
    k_beta = k * beta[..., None]
    lower = jnp.where(strict, jnp.einsum('bnhid,bnhjd->bnhij', k_beta, k) * decay, 0.0) + jnp.eye(c, dtype=jnp.float32)
    rhs = jnp.concatenate([v * beta[..., None], k_beta * jnp.exp(g)[..., None]], axis=-1)
    uw = lax.linalg.triangular_solve(lower, rhs, left_side=True, lower=True, unit_diagonal=True)
    u, w = uw[..., :dv], uw[..., dv:]
    attn = jnp.where(causal, jnp.einsum('bnhid,bnhjd->bnhij', q, k) * decay, 0.0)
    q_dec = q * jnp.exp(g)[..., None]
    g_last = g[..., -1]
    k_dec = k * jnp.exp(g_last[..., None] - g)[..., None]

    def step(state, inp):
        q_c, k_c, u_c, w_c, a_c, gl = inp
        v_new = u_c - jnp.einsum('bhck,bhkv->bhcv', w_c, state)
        o = jnp.einsum('bhck,bhkv->bhcv', q_c, state) + jnp.einsum('bhij,bhjv->bhiv', a_c, v_new)
        state = state * jnp.exp(gl)[..., None, None] + jnp.einsum('bhck,bhcv->bhkv', k_c, v_new)
        return state, o

    mv = lambda t: jnp.moveaxis(t, 1, 0)
    s0 = jnp.zeros((b, h, dk, dv), jnp.float32)
    _, o = lax.scan(step, s0, (mv(q_dec), mv(k_dec), mv(u), mv(w), mv(attn), mv(g_last)))
    return o.transpose(1, 0, 3, 2, 4).reshape(b, s, h, dv)


def gated_deltanet(qkv_raw, z_gate, b_raw, a_raw, conv_w, a_log, dt_bias, norm_w):
    b, s, _ = qkv_raw.shape
    qkv = jax.nn.silu(causal_depthwise_conv(qkv_raw, conv_w)).astype(jnp.float32)
    q, k, v = jnp.split(qkv, 3, axis=-1)
    shp = (b, s, DN_HEADS, DN_HEAD_DIM)
    q = l2_norm(q.reshape(shp))
    k = l2_norm(k.reshape(shp))
    v = v.reshape(shp)
    beta = jax.nn.sigmoid(b_raw.astype(jnp.float32))
    g = -jnp.exp(a_log.astype(jnp.float32)) * jax.nn.softplus(a_raw.astype(jnp.float32) + dt_bias.astype(jnp.float32))
    o = gated_delta_rule(q, k, v, g, beta)
    o = o * lax.rsqrt(jnp.mean(o * o, axis=-1, keepdims=True) + NORM_EPS) * norm_w.astype(jnp.float32)
    o = o * jax.nn.silu(z_gate.astype(jnp.float32).reshape(shp))
    return o.reshape(b, s, DN_WIDTH).astype(qkv_raw.dtype)


def s5_mixer(u, a_re, a_im, log_dt, b_re, b_im, c_re, c_im, d_skip, glu_w, glu_b):
    b, s, _ = u.shape
    uf = u.astype(jnp.float32).reshape(b, s, S5_GROUPS, S5_GROUP_CH)
    lam = lax.complex(a_re.astype(jnp.float32), a_im.astype(jnp.float32))
    dt = jnp.exp(log_dt.astype(jnp.float32))[:, None]
    a_bar = jnp.exp(lam * dt)
    b_c = lax.complex(b_re.astype(jnp.float32), b_im.astype(jnp.float32))
    b_bar = ((a_bar - 1.0) / lam)[..., None] * b_c
    bu = jnp.einsum('bsgh,gph->bsgp', uf.astype(jnp.complex64), b_bar)
    a_seq = jnp.broadcast_to(a_bar, bu.shape)

    def combine(e1, e2):
        a1, x1 = e1
        a2, x2 = e2
        return a1 * a2, a2 * x1 + x2

    _, states = lax.associative_scan(combine, (a_seq, bu), axis=1)
    c_c = lax.complex(c_re.astype(jnp.float32), c_im.astype(jnp.float32))
    y = jnp.real(jnp.einsum('bsgp,ghp->bsgh', states, c_c))
    y = y + d_skip.astype(jnp.float32).reshape(S5_GROUPS, S5_GROUP_CH) * uf
    y = jax.nn.gelu(y.reshape(b, s, S5_WIDTH)).astype(u.dtype)
    return y * jax.nn.sigmoid(y @ glu_w + glu_b)


def _fwd_setup_inputs(seed: int = 0) -> dict:
    key = jax.random.key(seed)
    ks = iter(jax.random.split(key, 40))
    f32 = jnp.float32

    def nrm(shape, scale):
        return scale * jax.random.normal(next(ks), shape, f32)

    def gain(width=D_MODEL):
        return 1.0 + nrm((DEPTH, width), 0.02)

    x = nrm((BATCH, SEQ, D_MODEL), 1.0)
    ff1_norm_pre = gain()
    ff1_w_gate = nrm((DEPTH, D_MODEL, D_FF), D_MODEL ** -0.5)
    ff1_w_up = nrm((DEPTH, D_MODEL, D_FF), D_MODEL ** -0.5)
    ff1_w_down = nrm((DEPTH, D_FF, D_MODEL), D_FF ** -0.5)
    ff1_norm_post = gain()
    mix_norm_pre = gain()
    w_in = nrm((DEPTH, D_MODEL, IN_WIDTH), D_MODEL ** -0.5)
    attn_sinks = nrm((DEPTH, ATTN_HEADS), 0.5)
    dn_conv_w = nrm((DEPTH, DN_CONV, 3 * DN_WIDTH), DN_CONV ** -0.5)
    dn_a_log = jnp.log(jax.random.uniform(next(ks), (DEPTH, DN_HEADS), f32, 1.0, 16.0))
    dn_dt = jnp.exp(jax.random.uniform(next(ks), (DEPTH, DN_HEADS), f32, math.log(1e-3), math.log(1e-1)))
    dn_dt_bias = dn_dt + jnp.log(-jnp.expm1(-dn_dt))
    dn_norm_w = gain(DN_HEAD_DIM)
    s5_a_re = -0.5 + nrm((DEPTH, S5_GROUPS, S5_STATE), 0.01)
    s5_a_im = math.pi * jnp.arange(S5_STATE, dtype=f32)[None, None, :] + nrm((DEPTH, S5_GROUPS, S5_STATE), 0.01)
    s5_log_dt = jax.random.uniform(next(ks), (DEPTH, S5_GROUPS), f32, math.log(1e-3), math.log(1e-1))
    s5_b_re = nrm((DEPTH, S5_GROUPS, S5_STATE, S5_GROUP_CH), (2 * S5_GROUP_CH) ** -0.5)
    s5_b_im = nrm((DEPTH, S5_GROUPS, S5_STATE, S5_GROUP_CH), (2 * S5_GROUP_CH) ** -0.5)
    s5_c_re = nrm((DEPTH, S5_GROUPS, S5_GROUP_CH, S5_STATE), (2 * S5_STATE) ** -0.5)
    s5_c_im = nrm((DEPTH, S5_GROUPS, S5_GROUP_CH, S5_STATE), (2 * S5_STATE) ** -0.5)
    s5_d = nrm((DEPTH, S5_WIDTH), 1.0)
    s5_glu_w = nrm((DEPTH, S5_WIDTH, S5_WIDTH), S5_WIDTH ** -0.5)
    s5_glu_b = nrm((DEPTH, S5_WIDTH), 0.01)
    w_out = nrm((DEPTH, MIX_WIDTH, D_MODEL), MIX_WIDTH ** -0.5)
    mix_norm_post = gain()
    ff2_norm_pre = gain()
    ff2_w_gate = nrm((DEPTH, D_MODEL, D_FF), D_MODEL ** -0.5)
    ff2_w_up = nrm((DEPTH, D_MODEL, D_FF), D_MODEL ** -0.5)
    ff2_w_down = nrm((DEPTH, D_FF, D_MODEL), D_FF ** -0.5)
    ff2_norm_post = gain()
    return {"x": x, "ff1_norm_pre": ff1_norm_pre, "ff1_w_gate": ff1_w_gate, "ff1_w_up": ff1_w_up,
            "ff1_w_down": ff1_w_down, "ff1_norm_post": ff1_norm_post, "mix_norm_pre": mix_norm_pre,
            "w_in": w_in, "attn_sinks": attn_sinks, "dn_conv_w": dn_conv_w, "dn_a_log": dn_a_log,
            "dn_dt_bias": dn_dt_bias, "dn_norm_w": dn_norm_w, "s5_a_re": s5_a_re, "s5_a_im": s5_a_im,
            "s5_log_dt": s5_log_dt, "s5_b_re": s5_b_re, "s5_b_im": s5_b_im, "s5_c_re": s5_c_re,
            "s5_c_im": s5_c_im, "s5_d": s5_d, "s5_glu_w": s5_glu_w, "s5_glu_b": s5_glu_b, "w_out": w_out,
            "mix_norm_post": mix_norm_post, "ff2_norm_pre": ff2_norm_pre, "ff2_w_gate": ff2_w_gate,
            "ff2_w_up": ff2_w_up, "ff2_w_down": ff2_w_down, "ff2_norm_post": ff2_norm_post}


def _fwd_reference(x, ff1_norm_pre, ff1_w_gate, ff1_w_up, ff1_w_down, ff1_norm_post, mix_norm_pre, w_in,
              attn_sinks, dn_conv_w, dn_a_log, dn_dt_bias, dn_norm_w, s5_a_re, s5_a_im, s5_log_dt,
              s5_b_re, s5_b_im, s5_c_re, s5_c_im, s5_d, s5_glu_w, s5_glu_b, w_out, mix_norm_post,
              ff2_norm_pre, ff2_w_gate, ff2_w_up, ff2_w_down, ff2_norm_post):
    b, s, _ = x.shape
    cos, sin = rope_tables(s)
    offsets = np.cumsum(IN_SPLITS)[:-1].tolist()
    for l in range(DEPTH):
        h = rms_norm(x, ff1_norm_pre[l])
        x = x + FFN_RES_WEIGHT * rms_norm(swiglu(h, ff1_w_gate[l], ff1_w_up[l], ff1_w_down[l]), ff1_norm_post[l])
        h = rms_norm(x, mix_norm_pre[l])
        z = h @ w_in[l]
        aq, ak, av, dn_qkv, dn_z, dn_b, dn_a, s5_u = jnp.split(z, offsets, axis=-1)
        aq = apply_rope(aq.reshape(b, s, ATTN_HEADS, HEAD_DIM), cos, sin)
        ak = apply_rope(ak.reshape(b, s, ATTN_KV_HEADS, HEAD_DIM), cos, sin)
        av = av.reshape(b, s, ATTN_KV_HEADS, HEAD_DIM)
        y_attn = sliding_window_attention(aq, ak, av, attn_sinks[l])
        y_dn = gated_deltanet(dn_qkv, dn_z, dn_b, dn_a, dn_conv_w[l], dn_a_log[l],
                              dn_dt_bias[l], dn_norm_w[l])
        y_s5 = s5_mixer(s5_u, s5_a_re[l], s5_a_im[l], s5_log_dt[l], s5_b_re[l], s5_b_im[l],
                        s5_c_re[l], s5_c_im[l], s5_d[l], s5_glu_w[l], s5_glu_b[l])
        mixed = jnp.concatenate([y_attn, y_dn, y_s5], axis=-1) @ w_out[l]
        x = x + rms_norm(mixed, mix_norm_post[l])
        h = rms_norm(x, ff2_norm_pre[l])
        x = x + FFN_RES_WEIGHT * rms_norm(swiglu(h, ff2_w_gate[l], ff2_w_up[l], ff2_w_down[l]), ff2_norm_post[l])
    return x


import jax as _jax
import jax.numpy as _jnp

TWIN_FORMAT = 'train_step'
FWD_PARAMS = ['x', 'ff1_norm_pre', 'ff1_w_gate', 'ff1_w_up', 'ff1_w_down', 'ff1_norm_post', 'mix_norm_pre', 'w_in', 'attn_sinks', 'dn_conv_w', 'dn_a_log', 'dn_dt_bias', 'dn_norm_w', 's5_a_re', 's5_a_im', 's5_log_dt', 's5_b_re', 's5_b_im', 's5_c_re', 's5_c_im', 's5_d', 's5_glu_w', 's5_glu_b', 'w_out', 'mix_norm_post', 'ff2_norm_pre', 'ff2_w_gate', 'ff2_w_up', 'ff2_w_down', 'ff2_norm_post']
TWIN_WEIGHTS = ['ff1_norm_pre', 'ff1_w_gate', 'ff1_w_up', 'ff1_w_down', 'ff1_norm_post', 'mix_norm_pre', 'w_in', 'attn_sinks', 'dn_conv_w', 'dn_a_log', 'dn_dt_bias', 'dn_norm_w', 's5_a_re', 's5_a_im', 's5_log_dt', 's5_b_re', 's5_b_im', 's5_c_re', 's5_c_im', 's5_d', 's5_glu_w', 's5_glu_b', 'w_out', 'mix_norm_post', 'ff2_norm_pre', 'ff2_w_gate', 'ff2_w_up', 'ff2_w_down', 'ff2_norm_post']
TWIN_DIFF_INPUT = 'x'
TWIN_INPUTS = ['x', 'ff1_norm_pre', 'ff1_w_gate', 'ff1_w_up', 'ff1_w_down', 'ff1_norm_post', 'mix_norm_pre', 'w_in', 'attn_sinks', 'dn_conv_w', 'dn_a_log', 'dn_dt_bias', 'dn_norm_w', 's5_a_re', 's5_a_im', 's5_log_dt', 's5_b_re', 's5_b_im', 's5_c_re', 's5_c_im', 's5_d', 's5_glu_w', 's5_glu_b', 'w_out', 'mix_norm_post', 'ff2_norm_pre', 'ff2_w_gate', 'ff2_w_up', 'ff2_w_down', 'ff2_norm_post', 'loss_target', 'm_ff1_norm_pre', 'm_ff1_w_gate', 'm_ff1_w_up', 'm_ff1_w_down', 'm_ff1_norm_post', 'm_mix_norm_pre', 'm_w_in', 'm_attn_sinks', 'm_dn_conv_w', 'm_dn_a_log', 'm_dn_dt_bias', 'm_dn_norm_w', 'm_s5_a_re', 'm_s5_a_im', 'm_s5_log_dt', 'm_s5_b_re', 'm_s5_b_im', 'm_s5_c_re', 'm_s5_c_im', 'm_s5_d', 'm_s5_glu_w', 'm_s5_glu_b', 'm_w_out', 'm_mix_norm_post', 'm_ff2_norm_pre', 'm_ff2_w_gate', 'm_ff2_w_up', 'm_ff2_w_down', 'm_ff2_norm_post', 'v_ff1_norm_pre', 'v_ff1_w_gate', 'v_ff1_w_up', 'v_ff1_w_down', 'v_ff1_norm_post', 'v_mix_norm_pre', 'v_w_in', 'v_attn_sinks', 'v_dn_conv_w', 'v_dn_a_log', 'v_dn_dt_bias', 'v_dn_norm_w', 'v_s5_a_re', 'v_s5_a_im', 'v_s5_log_dt', 'v_s5_b_re', 'v_s5_b_im', 'v_s5_c_re', 'v_s5_c_im', 'v_s5_d', 'v_s5_glu_w', 'v_s5_glu_b', 'v_w_out', 'v_mix_norm_post', 'v_ff2_norm_pre', 'v_ff2_w_gate', 'v_ff2_w_up', 'v_ff2_w_down', 'v_ff2_norm_post']
TWIN_OUTPUTS = ['loss', 'grad_x', 'grad_ff1_norm_pre', 'grad_ff1_w_gate', 'grad_ff1_w_up', 'grad_ff1_w_down', 'grad_ff1_norm_post', 'grad_mix_norm_pre', 'grad_w_in', 'grad_attn_sinks', 'grad_dn_conv_w', 'grad_dn_a_log', 'grad_dn_dt_bias', 'grad_dn_norm_w', 'grad_s5_a_re', 'grad_s5_a_im', 'grad_s5_log_dt', 'grad_s5_b_re', 'grad_s5_b_im', 'grad_s5_c_re', 'grad_s5_c_im', 'grad_s5_d', 'grad_s5_glu_w', 'grad_s5_glu_b', 'grad_w_out', 'grad_mix_norm_post', 'grad_ff2_norm_pre', 'grad_ff2_w_gate', 'grad_ff2_w_up', 'grad_ff2_w_down', 'grad_ff2_norm_post', 'delta_ff1_norm_pre', 'delta_ff1_w_gate', 'delta_ff1_w_up', 'delta_ff1_w_down', 'delta_ff1_norm_post', 'delta_mix_norm_pre', 'delta_w_in', 'delta_attn_sinks', 'delta_dn_conv_w', 'delta_dn_a_log', 'delta_dn_dt_bias', 'delta_dn_norm_w', 'delta_s5_a_re', 'delta_s5_a_im', 'delta_s5_log_dt', 'delta_s5_b_re', 'delta_s5_b_im', 'delta_s5_c_re', 'delta_s5_c_im', 'delta_s5_d', 'delta_s5_glu_w', 'delta_s5_glu_b', 'delta_w_out', 'delta_mix_norm_post', 'delta_ff2_norm_pre', 'delta_ff2_w_gate', 'delta_ff2_w_up', 'delta_ff2_w_down', 'delta_ff2_norm_post', 'new_m_ff1_norm_pre', 'new_m_ff1_w_gate', 'new_m_ff1_w_up', 'new_m_ff1_w_down', 'new_m_ff1_norm_post', 'new_m_mix_norm_pre', 'new_m_w_in', 'new_m_attn_sinks', 'new_m_dn_conv_w', 'new_m_dn_a_log', 'new_m_dn_dt_bias', 'new_m_dn_norm_w', 'new_m_s5_a_re', 'new_m_s5_a_im', 'new_m_s5_log_dt', 'new_m_s5_b_re', 'new_m_s5_b_im', 'new_m_s5_c_re', 'new_m_s5_c_im', 'new_m_s5_d', 'new_m_s5_glu_w', 'new_m_s5_glu_b', 'new_m_w_out', 'new_m_mix_norm_post', 'new_m_ff2_norm_pre', 'new_m_ff2_w_gate', 'new_m_ff2_w_up', 'new_m_ff2_w_down', 'new_m_ff2_norm_post', 'new_v_ff1_norm_pre', 'new_v_ff1_w_gate', 'new_v_ff1_w_up', 'new_v_ff1_w_down', 'new_v_ff1_norm_post', 'new_v_mix_norm_pre', 'new_v_w_in', 'new_v_attn_sinks', 'new_v_dn_conv_w', 'new_v_dn_a_log', 'new_v_dn_dt_bias', 'new_v_dn_norm_w', 'new_v_s5_a_re', 'new_v_s5_a_im', 'new_v_s5_log_dt', 'new_v_s5_b_re', 'new_v_s5_b_im', 'new_v_s5_c_re', 'new_v_s5_c_im', 'new_v_s5_d', 'new_v_s5_glu_w', 'new_v_s5_glu_b', 'new_v_w_out', 'new_v_mix_norm_post', 'new_v_ff2_norm_pre', 'new_v_ff2_w_gate', 'new_v_ff2_w_up', 'new_v_ff2_w_down', 'new_v_ff2_norm_post']
TWIN_LEAF_KINDS = {'loss': 'loss', 'grad_x': 'grad_x', 'grad_ff1_norm_pre': 'grad_w', 'grad_ff1_w_gate': 'grad_w', 'grad_ff1_w_up': 'grad_w', 'grad_ff1_w_down': 'grad_w', 'grad_ff1_norm_post': 'grad_w', 'grad_mix_norm_pre': 'grad_w', 'grad_w_in': 'grad_w', 'grad_attn_sinks': 'grad_w', 'grad_dn_conv_w': 'grad_w', 'grad_dn_a_log': 'grad_w', 'grad_dn_dt_bias': 'grad_w', 'grad_dn_norm_w': 'grad_w', 'grad_s5_a_re': 'grad_w', 'grad_s5_a_im': 'grad_w', 'grad_s5_log_dt': 'grad_w', 'grad_s5_b_re': 'grad_w', 'grad_s5_b_im': 'grad_w', 'grad_s5_c_re': 'grad_w', 'grad_s5_c_im': 'grad_w', 'grad_s5_d': 'grad_w', 'grad_s5_glu_w': 'grad_w', 'grad_s5_glu_b': 'grad_w', 'grad_w_out': 'grad_w', 'grad_mix_norm_post': 'grad_w', 'grad_ff2_norm_pre': 'grad_w', 'grad_ff2_w_gate': 'grad_w', 'grad_ff2_w_up': 'grad_w', 'grad_ff2_w_down': 'grad_w', 'grad_ff2_norm_post': 'grad_w', 'delta_ff1_norm_pre': 'delta_w', 'delta_ff1_w_gate': 'delta_w', 'delta_ff1_w_up': 'delta_w', 'delta_ff1_w_down': 'delta_w', 'delta_ff1_norm_post': 'delta_w', 'delta_mix_norm_pre': 'delta_w', 'delta_w_in': 'delta_w', 'delta_attn_sinks': 'delta_w', 'delta_dn_conv_w': 'delta_w', 'delta_dn_a_log': 'delta_w', 'delta_dn_dt_bias': 'delta_w', 'delta_dn_norm_w': 'delta_w', 'delta_s5_a_re': 'delta_w', 'delta_s5_a_im': 'delta_w', 'delta_s5_log_dt': 'delta_w', 'delta_s5_b_re': 'delta_w', 'delta_s5_b_im': 'delta_w', 'delta_s5_c_re': 'delta_w', 'delta_s5_c_im': 'delta_w', 'delta_s5_d': 'delta_w', 'delta_s5_glu_w': 'delta_w', 'delta_s5_glu_b': 'delta_w', 'delta_w_out': 'delta_w', 'delta_mix_norm_post': 'delta_w', 'delta_ff2_norm_pre': 'delta_w', 'delta_ff2_w_gate': 'delta_w', 'delta_ff2_w_up': 'delta_w', 'delta_ff2_w_down': 'delta_w', 'delta_ff2_norm_post': 'delta_w', 'new_m_ff1_norm_pre': 'new_m', 'new_m_ff1_w_gate': 'new_m', 'new_m_ff1_w_up': 'new_m', 'new_m_ff1_w_down': 'new_m', 'new_m_ff1_norm_post': 'new_m', 'new_m_mix_norm_pre': 'new_m', 'new_m_w_in': 'new_m', 'new_m_attn_sinks': 'new_m', 'new_m_dn_conv_w': 'new_m', 'new_m_dn_a_log': 'new_m', 'new_m_dn_dt_bias': 'new_m', 'new_m_dn_norm_w': 'new_m', 'new_m_s5_a_re': 'new_m', 'new_m_s5_a_im': 'new_m', 'new_m_s5_log_dt': 'new_m', 'new_m_s5_b_re': 'new_m', 'new_m_s5_b_im': 'new_m', 'new_m_s5_c_re': 'new_m', 'new_m_s5_c_im': 'new_m', 'new_m_s5_d': 'new_m', 'new_m_s5_glu_w': 'new_m', 'new_m_s5_glu_b': 'new_m', 'new_m_w_out': 'new_m', 'new_m_mix_norm_post': 'new_m', 'new_m_ff2_norm_pre': 'new_m', 'new_m_ff2_w_gate': 'new_m', 'new_m_ff2_w_up': 'new_m', 'new_m_ff2_w_down': 'new_m', 'new_m_ff2_norm_post': 'new_m', 'new_v_ff1_norm_pre': 'new_v', 'new_v_ff1_w_gate': 'new_v', 'new_v_ff1_w_up': 'new_v', 'new_v_ff1_w_down': 'new_v', 'new_v_ff1_norm_post': 'new_v', 'new_v_mix_norm_pre': 'new_v', 'new_v_w_in': 'new_v', 'new_v_attn_sinks': 'new_v', 'new_v_dn_conv_w': 'new_v', 'new_v_dn_a_log': 'new_v', 'new_v_dn_dt_bias': 'new_v', 'new_v_dn_norm_w': 'new_v', 'new_v_s5_a_re': 'new_v', 'new_v_s5_a_im': 'new_v', 'new_v_s5_log_dt': 'new_v', 'new_v_s5_b_re': 'new_v', 'new_v_s5_b_im': 'new_v', 'new_v_s5_c_re': 'new_v', 'new_v_s5_c_im': 'new_v', 'new_v_s5_d': 'new_v', 'new_v_s5_glu_w': 'new_v', 'new_v_s5_glu_b': 'new_v', 'new_v_w_out': 'new_v', 'new_v_mix_norm_post': 'new_v', 'new_v_ff2_norm_pre': 'new_v', 'new_v_ff2_w_gate': 'new_v', 'new_v_ff2_w_up': 'new_v', 'new_v_ff2_w_down': 'new_v', 'new_v_ff2_norm_post': 'new_v'}


def _forward(args):
    return _fwd_reference(*[args[k] for k in FWD_PARAMS])


def _output_shape():
    def fwd():
        inp = _fwd_setup_inputs(0)
        return _fwd_reference(*[inp[k] for k in FWD_PARAMS])
    out = _jax.eval_shape(fwd)
    return out.shape, out.dtype

N_MICROBATCH = 1
ADAM_LR = 0.001
ADAM_B1 = 0.9
ADAM_B2 = 0.999
ADAM_EPS = 1e-08
ADAM_WD = 0.01
ADAM_STEP = 10
PER_EXAMPLE_BATCH_AXIS = {'x': 0, 'loss_target': 0}
SHARED_INPUTS = []
_WEIGHT_DTYPES = {'ff1_norm_pre': _jnp.float32, 'ff1_w_gate': _jnp.float32, 'ff1_w_up': _jnp.float32, 'ff1_w_down': _jnp.float32, 'ff1_norm_post': _jnp.float32, 'mix_norm_pre': _jnp.float32, 'w_in': _jnp.float32, 'attn_sinks': _jnp.float32, 'dn_conv_w': _jnp.float32, 'dn_a_log': _jnp.float32, 'dn_dt_bias': _jnp.float32, 'dn_norm_w': _jnp.float32, 's5_a_re': _jnp.float32, 's5_a_im': _jnp.float32, 's5_log_dt': _jnp.float32, 's5_b_re': _jnp.float32, 's5_b_im': _jnp.float32, 's5_c_re': _jnp.float32, 's5_c_im': _jnp.float32, 's5_d': _jnp.float32, 's5_glu_w': _jnp.float32, 's5_glu_b': _jnp.float32, 'w_out': _jnp.float32, 'mix_norm_post': _jnp.float32, 'ff2_norm_pre': _jnp.float32, 'ff2_w_gate': _jnp.float32, 'ff2_w_up': _jnp.float32, 'ff2_w_down': _jnp.float32, 'ff2_norm_post': _jnp.float32}
MOMENT_SCALE = {'ff1_norm_pre': 1.309117e+00, 'ff1_w_gate': 3.937817e-01, 'ff1_w_up': 5.803654e-01, 'ff1_w_down': 9.818129e-01, 'ff1_norm_post': 4.022779e+00, 'mix_norm_pre': 5.756091e+00, 'w_in': 4.114149e+00, 'attn_sinks': 8.639083e-01, 'dn_conv_w': 3.517806e+00, 'dn_a_log': 1.077528e+01, 'dn_dt_bias': 1.038255e+01, 'dn_norm_w': 1.316144e+01, 's5_a_re': 1.415194e-01, 's5_a_im': 1.295786e-01, 's5_log_dt': 2.496502e+01, 's5_b_re': 1.293342e-01, 's5_b_im': 1.058337e-01, 's5_c_re': 2.080836e-01, 's5_c_im': 2.071146e-01, 's5_d': 8.790583e+00, 's5_glu_w': 1.225795e+00, 's5_glu_b': 3.403175e+00, 'w_out': 7.099536e+00, 'mix_norm_post': 1.785060e+01, 'ff2_norm_pre': 1.479018e+00, 'ff2_w_gate': 4.181961e-01, 'ff2_w_up': 7.532689e-01, 'ff2_w_down': 1.249123e+00, 'ff2_norm_post': 4.172884e+00}


def _to_microbatches(a, axis):
    t = _jnp.moveaxis(a, axis, 0)
    t = t.reshape((N_MICROBATCH, t.shape[0] // N_MICROBATCH) + t.shape[1:])
    return _jnp.moveaxis(t, 1, axis + 1)


def setup_inputs(seed: int = 0) -> dict:
    inp = _fwd_setup_inputs(seed)
    key = _jax.random.fold_in(_jax.random.key(seed), 7919)
    shape, _ = _output_shape()
    out = dict(inp)
    out["loss_target"] = _jax.random.normal(_jax.random.fold_in(key, 0), shape, _jnp.float32)
    for i, name in enumerate(TWIN_WEIGHTS):
        w = inp[name].astype(_jnp.float32)
        if MOMENT_SCALE is None:
            s = _jnp.sqrt(_jnp.mean(_jnp.square(w)) + 1e-30)
        else:
            s = MOMENT_SCALE[name]
        km, kv = _jax.random.split(_jax.random.fold_in(key, i + 1))
        out[name] = w
        out["m_" + name] = s * _jax.random.normal(km, w.shape, _jnp.float32)
        out["v_" + name] = (s * s) * _jax.random.uniform(kv, w.shape, _jnp.float32, 0.5, 1.5)
    if N_MICROBATCH > 1:
        for name, axis in PER_EXAMPLE_BATCH_AXIS.items():
            out[name] = _to_microbatches(out[name], axis)
    return {'x': out['x'], 'ff1_norm_pre': out['ff1_norm_pre'], 'ff1_w_gate': out['ff1_w_gate'], 'ff1_w_up': out['ff1_w_up'], 'ff1_w_down': out['ff1_w_down'], 'ff1_norm_post': out['ff1_norm_post'], 'mix_norm_pre': out['mix_norm_pre'], 'w_in': out['w_in'], 'attn_sinks': out['attn_sinks'], 'dn_conv_w': out['dn_conv_w'], 'dn_a_log': out['dn_a_log'], 'dn_dt_bias': out['dn_dt_bias'], 'dn_norm_w': out['dn_norm_w'], 's5_a_re': out['s5_a_re'], 's5_a_im': out['s5_a_im'], 's5_log_dt': out['s5_log_dt'], 's5_b_re': out['s5_b_re'], 's5_b_im': out['s5_b_im'], 's5_c_re': out['s5_c_re'], 's5_c_im': out['s5_c_im'], 's5_d': out['s5_d'], 's5_glu_w': out['s5_glu_w'], 's5_glu_b': out['s5_glu_b'], 'w_out': out['w_out'], 'mix_norm_post': out['mix_norm_post'], 'ff2_norm_pre': out['ff2_norm_pre'], 'ff2_w_gate': out['ff2_w_gate'], 'ff2_w_up': out['ff2_w_up'], 'ff2_w_down': out['ff2_w_down'], 'ff2_norm_post': out['ff2_norm_post'], 'loss_target': out['loss_target'], 'm_ff1_norm_pre': out['m_ff1_norm_pre'], 'm_ff1_w_gate': out['m_ff1_w_gate'], 'm_ff1_w_up': out['m_ff1_w_up'], 'm_ff1_w_down': out['m_ff1_w_down'], 'm_ff1_norm_post': out['m_ff1_norm_post'], 'm_mix_norm_pre': out['m_mix_norm_pre'], 'm_w_in': out['m_w_in'], 'm_attn_sinks': out['m_attn_sinks'], 'm_dn_conv_w': out['m_dn_conv_w'], 'm_dn_a_log': out['m_dn_a_log'], 'm_dn_dt_bias': out['m_dn_dt_bias'], 'm_dn_norm_w': out['m_dn_norm_w'], 'm_s5_a_re': out['m_s5_a_re'], 'm_s5_a_im': out['m_s5_a_im'], 'm_s5_log_dt': out['m_s5_log_dt'], 'm_s5_b_re': out['m_s5_b_re'], 'm_s5_b_im': out['m_s5_b_im'], 'm_s5_c_re': out['m_s5_c_re'], 'm_s5_c_im': out['m_s5_c_im'], 'm_s5_d': out['m_s5_d'], 'm_s5_glu_w': out['m_s5_glu_w'], 'm_s5_glu_b': out['m_s5_glu_b'], 'm_w_out': out['m_w_out'], 'm_mix_norm_post': out['m_mix_norm_post'], 'm_ff2_norm_pre': out['m_ff2_norm_pre'], 'm_ff2_w_gate': out['m_ff2_w_gate'], 'm_ff2_w_up': out['m_ff2_w_up'], 'm_ff2_w_down': out['m_ff2_w_down'], 'm_ff2_norm_post': out['m_ff2_norm_post'], 'v_ff1_norm_pre': out['v_ff1_norm_pre'], 'v_ff1_w_gate': out['v_ff1_w_gate'], 'v_ff1_w_up': out['v_ff1_w_up'], 'v_ff1_w_down': out['v_ff1_w_down'], 'v_ff1_norm_post': out['v_ff1_norm_post'], 'v_mix_norm_pre': out['v_mix_norm_pre'], 'v_w_in': out['v_w_in'], 'v_attn_sinks': out['v_attn_sinks'], 'v_dn_conv_w': out['v_dn_conv_w'], 'v_dn_a_log': out['v_dn_a_log'], 'v_dn_dt_bias': out['v_dn_dt_bias'], 'v_dn_norm_w': out['v_dn_norm_w'], 'v_s5_a_re': out['v_s5_a_re'], 'v_s5_a_im': out['v_s5_a_im'], 'v_s5_log_dt': out['v_s5_log_dt'], 'v_s5_b_re': out['v_s5_b_re'], 'v_s5_b_im': out['v_s5_b_im'], 'v_s5_c_re': out['v_s5_c_re'], 'v_s5_c_im': out['v_s5_c_im'], 'v_s5_d': out['v_s5_d'], 'v_s5_glu_w': out['v_s5_glu_w'], 'v_s5_glu_b': out['v_s5_glu_b'], 'v_w_out': out['v_w_out'], 'v_mix_norm_post': out['v_mix_norm_post'], 'v_ff2_norm_pre': out['v_ff2_norm_pre'], 'v_ff2_w_gate': out['v_ff2_w_gate'], 'v_ff2_w_up': out['v_ff2_w_up'], 'v_ff2_w_down': out['v_ff2_w_down'], 'v_ff2_norm_post': out['v_ff2_norm_post']}


def _loss(weights, diff, rest, loss_target):
    with _jax.named_scope("forward"):
        args = {**rest, TWIN_DIFF_INPUT: diff, **{k: w.astype(_WEIGHT_DTYPES[k]) for k, w in weights.items()}}
        y = _forward(args)
    with _jax.named_scope("loss_head"):
        err = _jnp.square(y.astype(_jnp.float32) - loss_target)
        return 0.5 * _jnp.sum(_jnp.mean(err, axis=-1)) if err.ndim else 0.5 * err


def _adamw(w, g, m, v):
    m = ADAM_B1 * m + (1.0 - ADAM_B1) * g
    v = ADAM_B2 * v + (1.0 - ADAM_B2) * _jnp.square(g)
    m_hat = m / (1.0 - ADAM_B1 ** ADAM_STEP)
    v_hat = v / (1.0 - ADAM_B2 ** ADAM_STEP)
    delta = -ADAM_LR * (m_hat / (_jnp.sqrt(v_hat) + ADAM_EPS) + ADAM_WD * w)
    return delta, m, v


def reference(x, ff1_norm_pre, ff1_w_gate, ff1_w_up, ff1_w_down, ff1_norm_post, mix_norm_pre, w_in, attn_sinks, dn_conv_w, dn_a_log, dn_dt_bias, dn_norm_w, s5_a_re, s5_a_im, s5_log_dt, s5_b_re, s5_b_im, s5_c_re, s5_c_im, s5_d, s5_glu_w, s5_glu_b, w_out, mix_norm_post, ff2_norm_pre, ff2_w_gate, ff2_w_up, ff2_w_down, ff2_norm_post, loss_target, m_ff1_norm_pre, m_ff1_w_gate, m_ff1_w_up, m_ff1_w_down, m_ff1_norm_post, m_mix_norm_pre, m_w_in, m_attn_sinks, m_dn_conv_w, m_dn_a_log, m_dn_dt_bias, m_dn_norm_w, m_s5_a_re, m_s5_a_im, m_s5_log_dt, m_s5_b_re, m_s5_b_im, m_s5_c_re, m_s5_c_im, m_s5_d, m_s5_glu_w, m_s5_glu_b, m_w_out, m_mix_norm_post, m_ff2_norm_pre, m_ff2_w_gate, m_ff2_w_up, m_ff2_w_down, m_ff2_norm_post, v_ff1_norm_pre, v_ff1_w_gate, v_ff1_w_up, v_ff1_w_down, v_ff1_norm_post, v_mix_norm_pre, v_w_in, v_attn_sinks, v_dn_conv_w, v_dn_a_log, v_dn_dt_bias, v_dn_norm_w, v_s5_a_re, v_s5_a_im, v_s5_log_dt, v_s5_b_re, v_s5_b_im, v_s5_c_re, v_s5_c_im, v_s5_d, v_s5_glu_w, v_s5_glu_b, v_w_out, v_mix_norm_post, v_ff2_norm_pre, v_ff2_w_gate, v_ff2_w_up, v_ff2_w_down, v_ff2_norm_post):
    given = dict(x=x, ff1_norm_pre=ff1_norm_pre, ff1_w_gate=ff1_w_gate, ff1_w_up=ff1_w_up, ff1_w_down=ff1_w_down, ff1_norm_post=ff1_norm_post, mix_norm_pre=mix_norm_pre, w_in=w_in, attn_sinks=attn_sinks, dn_conv_w=dn_conv_w, dn_a_log=dn_a_log, dn_dt_bias=dn_dt_bias, dn_norm_w=dn_norm_w, s5_a_re=s5_a_re, s5_a_im=s5_a_im, s5_log_dt=s5_log_dt, s5_b_re=s5_b_re, s5_b_im=s5_b_im, s5_c_re=s5_c_re, s5_c_im=s5_c_im, s5_d=s5_d, s5_glu_w=s5_glu_w, s5_glu_b=s5_glu_b, w_out=w_out, mix_norm_post=mix_norm_post, ff2_norm_pre=ff2_norm_pre, ff2_w_gate=ff2_w_gate, ff2_w_up=ff2_w_up, ff2_w_down=ff2_w_down, ff2_norm_post=ff2_norm_post, loss_target=loss_target, m_ff1_norm_pre=m_ff1_norm_pre, m_ff1_w_gate=m_ff1_w_gate, m_ff1_w_up=m_ff1_w_up, m_ff1_w_down=m_ff1_w_down, m_ff1_norm_post=m_ff1_norm_post, m_mix_norm_pre=m_mix_norm_pre, m_w_in=m_w_in, m_attn_sinks=m_attn_sinks, m_dn_conv_w=m_dn_conv_w, m_dn_a_log=m_dn_a_log, m_dn_dt_bias=m_dn_dt_bias, m_dn_norm_w=m_dn_norm_w, m_s5_a_re=m_s5_a_re, m_s5_a_im=m_s5_a_im, m_s5_log_dt=m_s5_log_dt, m_s5_b_re=m_s5_b_re, m_s5_b_im=m_s5_b_im, m_s5_c_re=m_s5_c_re, m_s5_c_im=m_s5_c_im, m_s5_d=m_s5_d, m_s5_glu_w=m_s5_glu_w, m_s5_glu_b=m_s5_glu_b, m_w_out=m_w_out, m_mix_norm_post=m_mix_norm_post, m_ff2_norm_pre=m_ff2_norm_pre, m_ff2_w_gate=m_ff2_w_gate, m_ff2_w_up=m_ff2_w_up, m_ff2_w_down=m_ff2_w_down, m_ff2_norm_post=m_ff2_norm_post, v_ff1_norm_pre=v_ff1_norm_pre, v_ff1_w_gate=v_ff1_w_gate, v_ff1_w_up=v_ff1_w_up, v_ff1_w_down=v_ff1_w_down, v_ff1_norm_post=v_ff1_norm_post, v_mix_norm_pre=v_mix_norm_pre, v_w_in=v_w_in, v_attn_sinks=v_attn_sinks, v_dn_conv_w=v_dn_conv_w, v_dn_a_log=v_dn_a_log, v_dn_dt_bias=v_dn_dt_bias, v_dn_norm_w=v_dn_norm_w, v_s5_a_re=v_s5_a_re, v_s5_a_im=v_s5_a_im, v_s5_log_dt=v_s5_log_dt, v_s5_b_re=v_s5_b_re, v_s5_b_im=v_s5_b_im, v_s5_c_re=v_s5_c_re, v_s5_c_im=v_s5_c_im, v_s5_d=v_s5_d, v_s5_glu_w=v_s5_glu_w, v_s5_glu_b=v_s5_glu_b, v_w_out=v_w_out, v_mix_norm_post=v_mix_norm_post, v_ff2_norm_pre=v_ff2_norm_pre, v_ff2_w_gate=v_ff2_w_gate, v_ff2_w_up=v_ff2_w_up, v_ff2_w_down=v_ff2_w_down, v_ff2_norm_post=v_ff2_norm_post)
    weights = {n: given[n] for n in TWIN_WEIGHTS}
    shared = {n: given[n] for n in SHARED_INPUTS}
    per_example = {n: given[n] for n in ['x']}
    grad_fn = _jax.value_and_grad(_loss, argnums=(0, 1))

    def one_microbatch(ex, loss_target):
        ex = dict(ex)
        diff = ex.pop(TWIN_DIFF_INPUT)
        return grad_fn(weights, diff, {**shared, **ex}, loss_target)

    if N_MICROBATCH == 1:
        loss, (grad_w, grad_x) = one_microbatch(per_example, given["loss_target"])
    else:
        def body(carry, xs):
            loss_sum, grad_sum = carry
            l_k, (gw_k, gx_k) = one_microbatch(xs[0], xs[1])
            with _jax.named_scope("update"):
                return (loss_sum + l_k, _jax.tree.map(_jnp.add, grad_sum, gw_k)), gx_k

        init = (_jnp.zeros((), _jnp.float32), _jax.tree.map(_jnp.zeros_like, weights))
        (loss, grad_w), grad_x = _jax.lax.scan(body, init, (per_example, given["loss_target"]))
    with _jax.named_scope("update"):
        delta_w, new_m, new_v = {}, {}, {}
        for n in TWIN_WEIGHTS:
            delta_w[n], new_m[n], new_v[n] = _adamw(weights[n], grad_w[n], given["m_" + n], given["v_" + n])
    return (loss, grad_x, *[grad_w[n] for n in TWIN_WEIGHTS], *[delta_w[n] for n in TWIN_WEIGHTS],
            *[new_m[n] for n in TWIN_WEIGHTS], *[new_v[n] for n in TWIN_WEIGHTS])
```

```python
import functools
import math

import jax
import jax.numpy as jnp
import numpy as np
from jax import lax
from jax.experimental import pallas as pl
from jax.experimental.pallas import tpu as pltpu

F32 = jnp.float32
BF16 = jnp.bfloat16

N_DEV = 8
DEPTH = 4
ATTN_HEADS = 8
ATTN_KV_HEADS = 2
HEAD_DIM = 128
WINDOW = 128
ROPE_THETA = 10000.0
DN_HEADS = 4
DN_HEAD_DIM = 128
DN_CONV = 4
DN_CHUNK = 64
S5_GROUPS = 32
S5_GROUP_CH = 16
S5_STATE = 64
ATTN_WIDTH = ATTN_HEADS * HEAD_DIM
ATTN_KV_WIDTH = ATTN_KV_HEADS * HEAD_DIM
DN_WIDTH = DN_HEADS * DN_HEAD_DIM
S5_WIDTH = S5_GROUPS * S5_GROUP_CH
S5_P = S5_GROUPS * S5_STATE
MIX_WIDTH = ATTN_WIDTH + DN_WIDTH + S5_WIDTH
IN_WIDTH = ATTN_WIDTH + 2 * ATTN_KV_WIDTH + 4 * DN_WIDTH + 2 * DN_HEADS + S5_WIDTH
Z_Q, Z_K, Z_V = 0, ATTN_WIDTH, ATTN_WIDTH + ATTN_KV_WIDTH
Z_DN = ATTN_WIDTH + 2 * ATTN_KV_WIDTH
Z_ZG = Z_DN + 3 * DN_WIDTH
Z_S5 = Z_ZG + DN_WIDTH
Z_SM = Z_S5 + S5_WIDTH
Z_ALL = Z_SM + 128
FFN_RES_WEIGHT = 0.5
NORM_EPS = 1e-6
ADAM_LR, ADAM_B1, ADAM_B2, ADAM_EPS, ADAM_WD, ADAM_STEP = 0.001, 0.9, 0.999, 1e-08, 0.01, 10

VMEM_LIMIT = 56 * 1024 * 1024
HI = lax.Precision.HIGHEST
NEG = -1e30

NN = (((1,), (0,)), ((), ()))
NT = (((1,), (1,)), ((), ()))
TN = (((0,), (0,)), ((), ()))


def _dot(a, b, dims=NN, prec=None):
    return lax.dot_general(a, b, dims, preferred_element_type=F32, precision=prec)


def _tile(n, pref, mult=8):
    if n <= pref:
        return n
    t = (pref // mult) * mult
    while t > mult and n % t:
        t -= mult
    assert n % t == 0, (n, pref)
    return t


def _call(body, name, grid, in_specs, out_specs, out_shape, scratch=(), sem=None):
    if sem is None:
        sem = ("arbitrary",) * len(grid)
    return pl.pallas_call(
        body, name=name, grid=grid, in_specs=in_specs, out_specs=out_specs, out_shape=out_shape,
        scratch_shapes=list(scratch),
        compiler_params=pltpu.CompilerParams(dimension_semantics=sem, vmem_limit_bytes=VMEM_LIMIT))


def _sds(shape, dtype):
    return jax.ShapeDtypeStruct(tuple(shape), dtype)


def _sigmoid(x):
    return 1.0 / (1.0 + jnp.exp(-x))


def _silu_and_grad(a):
    sg = _sigmoid(a)
    return a * sg, sg * (1.0 + a * (1.0 - sg))


def _softplus(x):
    return jnp.maximum(x, 0.0) + jnp.log(1.0 + jnp.exp(-jnp.abs(x)))


def _rms(x, g):
    r = lax.rsqrt(jnp.mean(x * x, axis=-1, keepdims=True) + NORM_EPS)
    return x * r * g


def _rms_bwd(dout, y, g):
    r = lax.rsqrt(jnp.mean(y * y, axis=-1, keepdims=True) + NORM_EPS)
    n = y * r
    dn = dout * g
    dy = r * (dn - n * jnp.mean(dn * n, axis=-1, keepdims=True))
    return dy, jnp.sum(dout * n, axis=0, keepdims=True)


def _rope(x, cos_f, sin_s):
    return x * cos_f + pltpu.roll(x, HEAD_DIM // 2, 1) * sin_s


def _rope_bwd(d, cos_f, sin_s):
    return d * cos_f + pltpu.roll(d * sin_s, HEAD_DIM // 2, 1)


def rmsnorm_fwd(x, g, name):
    s, d = x.shape
    tm = _tile(s, 512)

    def body(x_ref, g_ref, o_ref):
        o_ref[...] = _rms(x_ref[...], g_ref[...]).astype(BF16)

    return _call(body, name, (s // tm,),
                 [pl.BlockSpec((tm, d), lambda i: (i, 0)), pl.BlockSpec((1, d), lambda i: (0, 0))],
                 pl.BlockSpec((tm, d), lambda i: (i, 0)), _sds((s, d), BF16), sem=("parallel",))(x, g)


def norm_bwd(dout, y, g, scale, resid, out_dtype, name):
    s, d = y.shape
    tm = _tile(s, 256)
    has_res = resid is not None

    def body(*refs):
        if has_res:
            do_ref, y_ref, g_ref, r_ref, dy_ref, dg_ref = refs
        else:
            do_ref, y_ref, g_ref, dy_ref, dg_ref = refs
        dy, dg = _rms_bwd(do_ref[...] * scale, y_ref[...], g_ref[...])
        if has_res:
            dy = dy + r_ref[...]
        dy_ref[...] = dy.astype(out_dtype)

        @pl.when(pl.program_id(0) == 0)
        def _():
            dg_ref[...] = jnp.zeros_like(dg_ref)

        dg_ref[...] += dg

    row = pl.BlockSpec((tm, d), lambda i: (i, 0))
    vec = pl.BlockSpec((1, d), lambda i: (0, 0))
    ins = [dout, y, g] + ([resid] if has_res else [])
    return _call(body, name, (s // tm,), [row, row, vec] + ([row] if has_res else []),
                 [row, vec], [_sds((s, d), out_dtype), _sds((1, d), F32)])(*ins)


def ffn_up(h, wg, wu, name):
    s, d = h.shape
    nj, _, fs = wg.shape
    tm = _tile(s, 512)

    def body(h_ref, wg_ref, wu_ref, a_ref, b_ref, u_ref):
        hh = h_ref[...]
        a = _dot(hh, wg_ref[...])
        b = _dot(hh, wu_ref[...])
        a_ref[...] = a.astype(BF16)
        b_ref[...] = b.astype(BF16)
        u_ref[...] = (a * _sigmoid(a) * b).astype(BF16)

    wspec = pl.BlockSpec((None, d, fs), lambda j, i: (j, 0, 0))
    ospec = pl.BlockSpec((None, tm, fs), lambda j, i: (j, i, 0))
    osd = _sds((nj, s, fs), BF16)
    return _call(body, name, (nj, s // tm), [pl.BlockSpec((tm, d), lambda j, i: (i, 0)), wspec, wspec],
                 [ospec, ospec, ospec], [osd, osd, osd], sem=("parallel", "parallel"))(h, wg, wu)


def down_norm(u3, w3, x, g, scale, name):
    nj, s, k = u3.shape
    d = w3.shape[2]
    tm = _tile(s, 512)

    def body(u_ref, w_ref, x_ref, g_ref, y_ref, xn_ref, acc_ref):
        j = pl.program_id(1)

        @pl.when(j == 0)
        def _():
            acc_ref[...] = jnp.zeros_like(acc_ref)

        acc_ref[...] += _dot(u_ref[...], w_ref[...])

        @pl.when(j == nj - 1)
        def _():
            y = acc_ref[...]
            y_ref[...] = y
            xn_ref[...] = x_ref[...] + scale * _rms(y, g_ref[...])

    row = pl.BlockSpec((tm, d), lambda i, j: (i, 0))
    return _call(body, name, (s // tm, nj),
                 [pl.BlockSpec((None, tm, k), lambda i, j: (j, i, 0)), pl.BlockSpec((None, k, d), lambda i, j: (j, 0, 0)),
                  row, pl.BlockSpec((1, d), lambda i, j: (0, 0))],
                 [row, row], [_sds((s, d), F32), _sds((s, d), F32)],
                 scratch=[pltpu.VMEM((tm, d), F32)], sem=("parallel", "arbitrary"))(u3, w3, x, g)


def ffn_down_bwd(dy, wd, a, b, name):
    nj, fs, d = wd.shape
    s = dy.shape[0]
    tm = _tile(s, 512)

    def body(dy_ref, w_ref, a_ref, b_ref, da_ref, db_ref):
        du = _dot(dy_ref[...], w_ref[...], NT)
        aa = a_ref[...].astype(F32)
        bb = b_ref[...].astype(F32)
        sl, dsl = _silu_and_grad(aa)
        da_ref[...] = (du * bb * dsl).astype(BF16)
        db_ref[...] = (du * sl).astype(BF16)

    hspec = pl.BlockSpec((None, tm, fs), lambda j, i: (j, i, 0))
    osd = _sds((nj, s, fs), BF16)
    return _call(body, name, (nj, s // tm),
                 [pl.BlockSpec((tm, d), lambda j, i: (i, 0)), pl.BlockSpec((None, fs, d), lambda j, i: (j, 0, 0)), hspec, hspec],
                 [hspec, hspec], [osd, osd], sem=("parallel", "parallel"))(dy, wd, a, b)


def mm_tn(a3, b3, out_dtype, name, tmm=2048, tn=1408, tk=512):
    ja, s, m = a3.shape
    jb, _, n = b3.shape
    nj = max(ja, jb)
    tmm, tn, tk = _tile(m, tmm, 128), _tile(n, tn, 128), _tile(s, tk)
    nk = s // tk

    def body(a_ref, b_ref, o_ref, acc_ref):
        k = pl.program_id(3)

        @pl.when(k == 0)
        def _():
            acc_ref[...] = jnp.zeros_like(acc_ref)

        acc_ref[...] += _dot(a_ref[...].astype(BF16), b_ref[...].astype(BF16), TN)

        @pl.when(k == nk - 1)
        def _():
            o_ref[...] = acc_ref[...].astype(out_dtype)

    aj = (lambda j: j) if ja > 1 else (lambda j: 0)
    bj = (lambda j: j) if jb > 1 else (lambda j: 0)
    return _call(body, name, (nj, m // tmm, n // tn, nk),
                 [pl.BlockSpec((None, tk, tmm), lambda j, im, jn, k: (aj(j), k, im)),
                  pl.BlockSpec((None, tk, tn), lambda j, im, jn, k: (bj(j), k, jn))],
                 pl.BlockSpec((None, tmm, tn), lambda j, im, jn, k: (j, im, jn)), _sds((nj, m, n), out_dtype),
                 scratch=[pltpu.VMEM((tmm, tn), F32)],
                 sem=("parallel", "parallel", "parallel", "arbitrary"))(a3, b3)


def mm_nt_acc(pairs, name, tm=512, tn=512):
    nj, s, _ = pairs[0][0].shape
    n = pairs[0][1].shape[1]
    tm, tn = _tile(s, tm), _tile(n, tn, 128)
    npair = len(pairs)

    def body(*refs):
        o_ref, acc_ref = refs[2 * npair], refs[2 * npair + 1]
        j = pl.program_id(2)

        @pl.when(j == 0)
        def _():
            acc_ref[...] = jnp.zeros_like(acc_ref)

        t = None
        for p in range(npair):
            c = _dot(refs[2 * p][...].astype(BF16), refs[2 * p + 1][...], NT)
            t = c if t is None else t + c
        acc_ref[...] += t

        @pl.when(j == nj - 1)
        def _():
            o_ref[...] = acc_ref[...]

    ins, specs = [], []
    for a3, w3 in pairs:
        k = a3.shape[2]
        ins += [a3, w3]
        specs += [pl.BlockSpec((None, tm, k), lambda i, jn, j: (j, i, 0)),
                  pl.BlockSpec((None, tn, k), lambda i, jn, j: (j, jn, 0))]
    return _call(body, name, (s // tm, n // tn, nj), specs,
                 pl.BlockSpec((tm, tn), lambda i, jn, j: (i, jn)), _sds((s, n), F32),
                 scratch=[pltpu.VMEM((tm, tn), F32)], sem=("parallel", "parallel", "arbitrary"))(*ins)


def mm_nn(a, w, name, tm=512, tn=1408, col0=0, kdim=None):
    s = a.shape[0]
    k, n = w.shape
    assert col0 % k == 0
    tm, tn = _tile(s, tm), _tile(n, tn, 128)
    cb = col0 // k

    def body(a_ref, w_ref, o_ref):
        o_ref[...] = _dot(a_ref[...].astype(BF16), w_ref[...])

    return _call(body, name, (n // tn, s // tm),
                 [pl.BlockSpec((tm, k), lambda jn, i: (i, cb)), pl.BlockSpec((k, tn), lambda jn, i: (0, jn))],
                 pl.BlockSpec((tm, tn), lambda jn, i: (i, jn)), _sds((s, n), F32), sem=("parallel", "parallel"))(a, w)


def loss_and_grad(xl, target, name):
    s, d = xl.shape
    tm = _tile(s, 512)

    def body(x_ref, t_ref, l_ref, dx_ref):
        e = x_ref[...] - t_ref[...]
        dx_ref[...] = e * (1.0 / d)

        @pl.when(pl.program_id(0) == 0)
        def _():
            l_ref[...] = jnp.zeros_like(l_ref)

        part = jnp.sum(jnp.sum(e * e, axis=-1, keepdims=True), axis=0, keepdims=True) * (0.5 / d)
        l_ref[...] += jnp.broadcast_to(part, l_ref.shape)

    row = pl.BlockSpec((tm, d), lambda i: (i, 0))
    return _call(body, name, (s // tm,), [row, row], [pl.BlockSpec((1, 128), lambda i: (0, 0)), row],
                 [_sds((1, 128), F32), _sds((s, d), F32)])(xl, target)


def cast_bf16(w4, layer, name):
    _, r, c = w4.shape
    tr = _tile(r, 512)

    def body(w_ref, o_ref):
        o_ref[...] = w_ref[...].astype(BF16)

    return _call(body, name, (r // tr,), [pl.BlockSpec((None, tr, c), lambda i: (layer, i, 0))],
                 pl.BlockSpec((tr, c), lambda i: (i, 0)), _sds((r, c), BF16), sem=("parallel",))(w4)


def _attn_specs(nb):
    w = WINDOW
    prev = lambda i: jnp.maximum(i - 1, 0)
    q = pl.BlockSpec((w, ATTN_WIDTH), lambda i: (i, 0))
    kc = pl.BlockSpec((w, ATTN_KV_WIDTH), lambda i: (i, Z_K // ATTN_KV_WIDTH))
    kp = pl.BlockSpec((w, ATTN_KV_WIDTH), lambda i: (prev(i), Z_K // ATTN_KV_WIDTH))
    vc = pl.BlockSpec((w, ATTN_KV_WIDTH), lambda i: (i, Z_V // ATTN_KV_WIDTH))
    vp = pl.BlockSpec((w, ATTN_KV_WIDTH), lambda i: (prev(i), Z_V // ATTN_KV_WIDTH))
    tc = pl.BlockSpec((w, HEAD_DIM), lambda i: (i, 0))
    tp = pl.BlockSpec((w, HEAD_DIM), lambda i: (prev(i), 0))
    sink = pl.BlockSpec(memory_space=pltpu.SMEM)
    return [q, kc, kp, vc, vp, tc, tc, tp, tp, sink]


def _attn_mask(i):
    w = WINDOW
    qi = lax.broadcasted_iota(jnp.int32, (w, 2 * w), 0) + w
    kj = lax.broadcasted_iota(jnp.int32, (w, 2 * w), 1)
    rel = qi - kj
    band = (rel >= 0) & (rel < w)
    return band & jnp.logical_not((i == 0) & (kj < w))


def _attn_probs(q, kk, sink, mask):
    s = _dot(q, kk, NT) * (HEAD_DIM ** -0.5)
    s = jnp.where(mask, s, NEG)
    m = jnp.maximum(jnp.max(s, axis=-1, keepdims=True), sink)
    p = jnp.exp(s - m)
    es = jnp.exp(sink - m)
    inv = 1.0 / (jnp.sum(p, axis=-1, keepdims=True) + es)
    return p * inv, es * inv


def attn_fwd(z, cos_f, sin_s, sinks, name):
    s = z.shape[0]
    nb = s // WINDOW
    hd = HEAD_DIM
    grp = ATTN_HEADS // ATTN_KV_HEADS

    def body(q_ref, kc_ref, kp_ref, vc_ref, vp_ref, cc_ref, sc_ref, cp_ref, sp_ref, sink_ref, o_ref):
        i = pl.program_id(0)
        mask = _attn_mask(i)
        cc, sc, cp, sp = cc_ref[...], sc_ref[...], cp_ref[...], sp_ref[...]
        for kv in range(ATTN_KV_HEADS):
            ksl = slice(kv * hd, (kv + 1) * hd)
            kk = jnp.concatenate([_rope(kp_ref[:, ksl], cp, sp), _rope(kc_ref[:, ksl], cc, sc)], axis=0).astype(BF16)
            vv = jnp.concatenate([vp_ref[:, ksl], vc_ref[:, ksl]], axis=0).astype(BF16)
            for g in range(grp):
                h = kv * grp + g
                hsl = slice(h * hd, (h + 1) * hd)
                q = _rope(q_ref[:, hsl], cc, sc).astype(BF16)
                pn, _ = _attn_probs(q, kk, sink_ref[h], mask)
                o_ref[:, hsl] = _dot(pn.astype(BF16), vv).astype(BF16)

    return _call(body, name, (nb,), _attn_specs(nb), pl.BlockSpec((WINDOW, ATTN_WIDTH), lambda i: (i, 0)),
                 _sds((s, ATTN_WIDTH), BF16), sem=("parallel",))(z, z, z, z, z, cos_f, sin_s, cos_f, sin_s, sinks)


def attn_bwd(z, cos_f, sin_s, sinks, dy, name):
    s = z.shape[0]
    nb = s // WINDOW
    hd = HEAD_DIM
    grp = ATTN_HEADS // ATTN_KV_HEADS
    scale = HEAD_DIM ** -0.5

    def body(q_ref, kc_ref, kp_ref, vc_ref, vp_ref, cc_ref, sc_ref, cp_ref, sp_ref, sink_ref, dy_ref,
             dq_ref, dkc_ref, dkp_ref, dvc_ref, dvp_ref, ds_ref):
        i = pl.program_id(0)
        mask = _attn_mask(i)
        cc, sc, cp, sp = cc_ref[...], sc_ref[...], cp_ref[...], sp_ref[...]

        @pl.when(i == 0)
        def _():
            ds_ref[...] = jnp.zeros_like(ds_ref)

        for kv in range(ATTN_KV_HEADS):
            ksl = slice(kv * hd, (kv + 1) * hd)
            kk = jnp.concatenate([_rope(kp_ref[:, ksl], cp, sp), _rope(kc_ref[:, ksl], cc, sc)], axis=0).astype(BF16)
            vv = jnp.concatenate([vp_ref[:, ksl], vc_ref[:, ksl]], axis=0).astype(BF16)
            dkk = jnp.zeros((2 * WINDOW, hd), F32)
            dvv = jnp.zeros((2 * WINDOW, hd), F32)
            for g in range(grp):
                h = kv * grp + g
                hsl = slice(h * hd, (h + 1) * hd)
                q = _rope(q_ref[:, hsl], cc, sc).astype(BF16)
                pn, psink = _attn_probs(q, kk, sink_ref[h], mask)
                do = dy_ref[:, hsl].astype(BF16)
                dpn = _dot(do, vv, NT)
                dvv = dvv + _dot(pn.astype(BF16), do, TN)
                tot = jnp.sum(pn * dpn, axis=-1, keepdims=True)
                dsc = (pn * (dpn - tot) * scale).astype(BF16)
                dq_ref[:, hsl] = _rope_bwd(_dot(dsc, kk), cc, sc)
                dkk = dkk + _dot(dsc, q, TN)
                dsink = jnp.sum(-psink * tot, axis=0, keepdims=True)
                ds_ref[h:h + 1, :] += jnp.broadcast_to(dsink, (1, 128))
            dkp_ref[:, ksl] = _rope_bwd(dkk[:WINDOW], cp, sp)
            dkc_ref[:, ksl] = _rope_bwd(dkk[WINDOW:], cc, sc)
            dvp_ref[:, ksl] = dvv[:WINDOW]
            dvc_ref[:, ksl] = dvv[WINDOW:]

    kvo = pl.BlockSpec((WINDOW, ATTN_KV_WIDTH), lambda i: (i, 0))
    kvs = _sds((s, ATTN_KV_WIDTH), F32)
    return _call(body, name, (nb,), _attn_specs(nb) + [pl.BlockSpec((WINDOW, ATTN_WIDTH), lambda i: (i, 0))],
                 [pl.BlockSpec((WINDOW, ATTN_WIDTH), lambda i: (i, 0)), kvo, kvo, kvo, kvo,
                  pl.BlockSpec((ATTN_HEADS, 128), lambda i: (0, 0))],
                 [_sds((s, ATTN_WIDTH), F32), kvs, kvs, kvs, kvs, _sds((ATTN_HEADS, 128), F32)])(
        z, z, z, z, z, cos_f, sin_s, cos_f, sin_s, sinks, dy)


def _shift_down(x, d, row):
    return jnp.where(row >= d, pltpu.roll(x, d, 0), 0.0)


def _shift_up(x, d, row, n):
    return jnp.where(row < n - d, pltpu.roll(x, n - d, 0), 0.0)


def _conv_taps(u, w_ref, row):
    c = w_ref[DN_CONV - 1:DN_CONV, :] * u
    for k in range(DN_CONV - 1):
        c = c + w_ref[k:k + 1, :] * _shift_down(u, DN_CONV - 1 - k, row)
    return c


def dn_pre_fwd(z, conv_w, name):
    s = z.shape[0]
    nblk = 3 * DN_WIDTH // 128
    nqk = 2 * DN_WIDTH // 128

    def body(u_ref, w_ref, o_ref):
        row = lax.broadcasted_iota(jnp.int32, (s, 128), 0)
        c = _conv_taps(u_ref[...], w_ref, row)
        sl = c * _sigmoid(c)
        j = pl.program_id(0)

        @pl.when(j < nqk)
        def _():
            o_ref[...] = sl * lax.rsqrt(jnp.sum(sl * sl, axis=-1, keepdims=True) + NORM_EPS)

        @pl.when(j >= nqk)
        def _():
            o_ref[...] = sl

    return _call(body, name, (nblk,),
                 [pl.BlockSpec((s, 128), lambda j: (0, Z_DN // 128 + j)), pl.BlockSpec((DN_CONV, 128), lambda j: (0, j))],
                 pl.BlockSpec((s, 128), lambda j: (0, j)), _sds((s, 3 * DN_WIDTH), F32), sem=("parallel",))(z, conv_w)


def dn_pre_bwd(z, conv_w, dout, name):
    s = z.shape[0]
    nblk = 3 * DN_WIDTH // 128
    nqk = 2 * DN_WIDTH // 128

    def body(u_ref, w_ref, do_ref, du_ref, dw_ref, ds_ref):
        row = lax.broadcasted_iota(jnp.int32, (s, 128), 0)
        u = u_ref[...]
        c = _conv_taps(u, w_ref, row)
        sl, dsl = _silu_and_grad(c)
        do = do_ref[...]
        j = pl.program_id(0)

        @pl.when(j < nqk)
        def _():
            r = lax.rsqrt(jnp.sum(sl * sl, axis=-1, keepdims=True) + NORM_EPS)
            ds_ref[...] = r * do - sl * (r * r * r) * jnp.sum(do * sl, axis=-1, keepdims=True)

        @pl.when(j >= nqk)
        def _():
            ds_ref[...] = do

        dc = ds_ref[...] * dsl
        du = w_ref[DN_CONV - 1:DN_CONV, :] * dc
        dw_ref[DN_CONV - 1:DN_CONV, :] = jnp.sum(dc * u, axis=0, keepdims=True)
        for k in range(DN_CONV - 1):
            d = DN_CONV - 1 - k
            du = du + w_ref[k:k + 1, :] * _shift_up(dc, d, row, s)
            dw_ref[k:k + 1, :] = jnp.sum(dc * _shift_down(u, d, row), axis=0, keepdims=True)
        du_ref[...] = du

    blk = pl.BlockSpec((s, 128), lambda j: (0, j))
    wsp = pl.BlockSpec((DN_CONV, 128), lambda j: (0, j))
    return _call(body, name, (nblk,), [pl.BlockSpec((s, 128), lambda j: (0, Z_DN // 128 + j)), wsp, blk],
                 [blk, wsp], [_sds((s, 3 * DN_WIDTH), F32), _sds((DN_CONV, 3 * DN_WIDTH), F32)],
                 scratch=[pltpu.VMEM((s, 128), F32)], sem=("parallel",))(z, conv_w, dout)


def _lane_col(x, lane, idx):
    return jnp.sum(jnp.where(lane == idx, x, 0.0), axis=-1, keepdims=True)


def dn_gates_fwd(z, alog_b, dtb_b, name):
    s = z.shape[0]
    tm = _tile(s, 512)

    def body(zs_ref, al_ref, dt_ref, beta_ref, g_ref):
        zs = zs_ref[...]
        lane = lax.broadcasted_iota(jnp.int32, zs.shape, 1)
        for h in range(DN_HEADS):
            b_raw = _lane_col(zs, lane, h)
            a_raw = _lane_col(zs, lane, DN_HEADS + h)
            beta_ref[h] = jnp.broadcast_to(_sigmoid(b_raw), (tm, 128))
            g_ref[h] = -jnp.exp(al_ref[h:h + 1, :]) * _softplus(a_raw + dt_ref[h:h + 1, :])

    osp = pl.BlockSpec((DN_HEADS, tm, 128), lambda i: (0, i, 0))
    psp = pl.BlockSpec((DN_HEADS, 128), lambda i: (0, 0))
    osd = _sds((DN_HEADS, s, 128), F32)
    return _call(body, name, (s // tm,), [pl.BlockSpec((tm, 128), lambda i: (i, Z_SM // 128)), psp, psp],
                 [osp, osp], [osd, osd], sem=("parallel",))(z, alog_b, dtb_b)


def dn_gates_bwd(z, alog_b, dtb_b, dbeta, dg, name):
    s = z.shape[0]
    tm = _tile(s, 512)

    def body(zs_ref, al_ref, dt_ref, dbeta_ref, dg_ref, dz_ref, dal_ref, ddt_ref):
        @pl.when(pl.program_id(0) == 0)
        def _():
            dal_ref[...] = jnp.zeros_like(dal_ref)
            ddt_ref[...] = jnp.zeros_like(ddt_ref)

        zs = zs_ref[...]
        lane = lax.broadcasted_iota(jnp.int32, zs.shape, 1)
        dz = jnp.zeros_like(zs)
        for h in range(DN_HEADS):
            b_raw = _lane_col(zs, lane, h)
            a_raw = _lane_col(zs, lane, DN_HEADS + h)
            dbe = jnp.sum(dbeta_ref[h], axis=-1, keepdims=True)
            dgg = jnp.sum(dg_ref[h], axis=-1, keepdims=True)
            beta = _sigmoid(b_raw)
            ea = jnp.exp(al_ref[h:h + 1, :])
            pre = a_raw + dt_ref[h:h + 1, :]
            da_raw = dgg * (-ea) * _sigmoid(pre)
            dz = dz + jnp.where(lane == h, dbe * beta * (1.0 - beta), 0.0) + jnp.where(lane == DN_HEADS + h, da_raw, 0.0)
            ddt_ref[h:h + 1, :] += jnp.sum(da_raw, axis=0, keepdims=True)
            dal_ref[h:h + 1, :] += jnp.sum(dgg * (-ea) * _softplus(pre), axis=0, keepdims=True)
        dz_ref[...] = dz

    hsp = pl.BlockSpec((DN_HEADS, tm, 128), lambda i: (0, i, 0))
    psp = pl.BlockSpec((DN_HEADS, 128), lambda i: (0, 0))
    return _call(body, name, (s // tm,), [pl.BlockSpec((tm, 128), lambda i: (i, Z_SM // 128)), psp, psp, hsp, hsp],
                 [pl.BlockSpec((tm, 128), lambda i: (i, 0)), psp, psp],
                 [_sds((s, 128), F32), _sds((DN_HEADS, 128), F32), _sds((DN_HEADS, 128), F32)])(z, alog_b, dtb_b, dbeta, dg)


def _dn_intra(q, k, v, gb, bb):
    c = DN_CHUNK
    ri = lax.broadcasted_iota(jnp.int32, (c, c), 0)
    ci = lax.broadcasted_iota(jnp.int32, (c, c), 1)
    causal = ri >= ci
    strict = ri > ci
    gc = _dot(causal.astype(F32), gb, NN, HI)
    e0 = (lax.broadcasted_iota(jnp.int32, (c, 128), 1) == 0).astype(F32)
    grow = _dot(e0, gc, NT, HI)
    decay = jnp.where(causal, jnp.exp(jnp.where(causal, gc[:, :c] - grow, 0.0)), 0.0)
    qs = q * (DN_HEAD_DIM ** -0.5)
    kb = k * bb
    lower = jnp.where(strict, _dot(kb, k, NT, HI) * decay, 0.0)
    eye = (ri == ci).astype(F32)
    t = eye - lower
    p = lower
    for _ in range(5):
        p = _dot(p, p, NN, HI)
        t = t + _dot(t, p, NN, HI)
    eg = jnp.exp(gc)
    u = _dot(t, v * bb, NN, HI)
    w = _dot(t, kb * eg, NN, HI)
    attn = jnp.where(causal, _dot(qs, k, NT, HI) * decay, 0.0)
    glast = _dot((ci == c - 1).astype(F32), gc, NN, HI)
    return u, w, qs * eg, k * jnp.exp(glast - gc), attn, jnp.exp(glast[:8])


def _dn_step(st, qd, kd, u, w, attn, egl):
    v_new = u - _dot(w, st, NN, HI)
    o = _dot(qd, st, NN, HI) + _dot(attn, v_new, NN, HI)
    st_new = st * egl[0:1, :] + _dot(kd, v_new, TN, HI)
    return o, st_new


def _dn_chunk_specs(nc):
    c = DN_CHUNK
    wide = pl.BlockSpec((c, DN_WIDTH), lambda i: (i, 0))
    att = pl.BlockSpec((c, DN_HEADS * c), lambda i: (i, 0))
    egl = pl.BlockSpec((8, DN_WIDTH), lambda i: (i, 0))
    return wide, att, egl


def dn_intra_fwd(qkv, gb, bb, name):
    s = qkv.shape[0]
    c, hd = DN_CHUNK, DN_HEAD_DIM
    nc = s // c

    def body(q_ref, k_ref, v_ref, g_ref, b_ref, u_ref, w_ref, qd_ref, kd_ref, at_ref, eg_ref):
        for h in range(DN_HEADS):
            hs = slice(h * hd, (h + 1) * hd)
            u, w, qd, kd, at, eg = _dn_intra(q_ref[:, hs], k_ref[:, hs], v_ref[:, hs], g_ref[h], b_ref[h])
            u_ref[:, hs], w_ref[:, hs], qd_ref[:, hs], kd_ref[:, hs] = u, w, qd, kd
            at_ref[:, h * c:(h + 1) * c] = at
            eg_ref[:, hs] = eg

    wide, att, egl = _dn_chunk_specs(nc)
    hsp = pl.BlockSpec((DN_HEADS, c, 128), lambda i: (0, i, 0))
    wsd = _sds((s, DN_WIDTH), F32)
    return _call(body, name, (nc,),
                 [pl.BlockSpec((c, DN_WIDTH), lambda i: (i, 0)), pl.BlockSpec((c, DN_WIDTH), lambda i: (i, 1)),
                  pl.BlockSpec((c, DN_WIDTH), lambda i: (i, 2)), hsp, hsp],
                 [wide, wide, wide, wide, att, egl],
                 [wsd, wsd, wsd, wsd, _sds((s, DN_HEADS * c), F32), _sds((nc * 8, DN_WIDTH), F32)],
                 sem=("parallel",))(qkv, qkv, qkv, gb, bb)


def dn_intra_bwd(qkv, gb, bb, cts, name):
    s = qkv.shape[0]
    c, hd = DN_CHUNK, DN_HEAD_DIM
    nc = s // c

    def body(q_ref, k_ref, v_ref, g_ref, b_ref, du_ref, dw_ref, dqd_ref, dkd_ref, dat_ref, deg_ref,
             dq_ref, dk_ref, dv_ref, dg_ref, db_ref):
        for h in range(DN_HEADS):
            hs = slice(h * hd, (h + 1) * hd)
            _, vjp = jax.vjp(_dn_intra, q_ref[:, hs], k_ref[:, hs], v_ref[:, hs], g_ref[h], b_ref[h])
            dq, dk, dv, dg, db = vjp((du_ref[:, hs], dw_ref[:, hs], dqd_ref[:, hs], dkd_ref[:, hs],
                                      dat_ref[:, h * c:(h + 1) * c], deg_ref[:, hs]))
            dq_ref[:, hs], dk_ref[:, hs], dv_ref[:, hs] = dq, dk, dv
            dg_ref[h] = dg
            db_ref[h] = db

    wide, att, egl = _dn_chunk_specs(nc)
    hsp = pl.BlockSpec((DN_HEADS, c, 128), lambda i: (0, i, 0))
    hsd = _sds((DN_HEADS, s, 128), F32)
    wsd = _sds((s, DN_WIDTH), F32)
    return _call(body, name, (nc,),
                 [pl.BlockSpec((c, DN_WIDTH), lambda i: (i, 0)), pl.BlockSpec((c, DN_WIDTH), lambda i: (i, 1)),
                  pl.BlockSpec((c, DN_WIDTH), lambda i: (i, 2)), hsp, hsp, wide, wide, wide, wide, att, egl],
                 [wide, wide, wide, hsp, hsp], [wsd, wsd, wsd, hsd, hsd], sem=("parallel",))(qkv, qkv, qkv, gb, bb, *cts)


def dn_scan_fwd(u, w, qd, kd, at, eg, name):
    s = u.shape[0]
    c, hd = DN_CHUNK, DN_HEAD_DIM
    nc = s // c

    def body(u_ref, w_ref, qd_ref, kd_ref, at_ref, eg_ref, o_ref, st_ref, state):
        @pl.when(pl.program_id(0) == 0)
        def _():
            state[...] = jnp.zeros_like(state)

        for h in range(DN_HEADS):
            hs = slice(h * hd, (h + 1) * hd)
            st = state[h]
            st_ref[h] = st
            o, st_new = _dn_step(st, qd_ref[:, hs], kd_ref[:, hs], u_ref[:, hs], w_ref[:, hs],
                                 at_ref[:, h * c:(h + 1) * c], eg_ref[:, hs])
            o_ref[:, hs] = o
            state[h] = st_new

    wide, att, egl = _dn_chunk_specs(nc)
    return _call(body, name, (nc,), [wide, wide, wide, wide, att, egl],
                 [wide, pl.BlockSpec((None, DN_HEADS, hd, hd), lambda i: (i, 0, 0, 0))],
                 [_sds((s, DN_WIDTH), F32), _sds((nc, DN_HEADS, hd, hd), F32)],
                 scratch=[pltpu.VMEM((DN_HEADS, hd, hd), F32)])(u, w, qd, kd, at, eg)


def dn_scan_bwd(u, w, qd, kd, at, eg, states, do, name):
    s = u.shape[0]
    c, hd = DN_CHUNK, DN_HEAD_DIM
    nc = s // c

    def body(u_ref, w_ref, qd_ref, kd_ref, at_ref, eg_ref, st_ref, do_ref,
             du_ref, dw_ref, dqd_ref, dkd_ref, dat_ref, deg_ref, dstate):
        @pl.when(pl.program_id(0) == 0)
        def _():
            dstate[...] = jnp.zeros_like(dstate)

        for h in range(DN_HEADS):
            hs = slice(h * hd, (h + 1) * hd)
            asl = slice(h * c, (h + 1) * c)
            _, vjp = jax.vjp(_dn_step, st_ref[h], qd_ref[:, hs], kd_ref[:, hs], u_ref[:, hs], w_ref[:, hs],
                             at_ref[:, asl], eg_ref[:, hs])
            dst, dqd, dkd, du, dw, dat, deg = vjp((do_ref[:, hs], dstate[h]))
            dstate[h] = dst
            du_ref[:, hs], dw_ref[:, hs], dqd_ref[:, hs], dkd_ref[:, hs] = du, dw, dqd, dkd
            dat_ref[:, asl] = dat
            deg_ref[:, hs] = deg

    rev = lambda i: nc - 1 - i
    wide = pl.BlockSpec((c, DN_WIDTH), lambda i: (rev(i), 0))
    att = pl.BlockSpec((c, DN_HEADS * c), lambda i: (rev(i), 0))
    egl = pl.BlockSpec((8, DN_WIDTH), lambda i: (rev(i), 0))
    wsd = _sds((s, DN_WIDTH), F32)
    return _call(body, name, (nc,),
                 [wide, wide, wide, wide, att, egl, pl.BlockSpec((None, DN_HEADS, hd, hd), lambda i: (rev(i), 0, 0, 0)), wide],
                 [wide, wide, wide, wide, att, egl],
                 [wsd, wsd, wsd, wsd, _sds((s, DN_HEADS * c), F32), _sds((nc * 8, DN_WIDTH), F32)],
                 scratch=[pltpu.VMEM((DN_HEADS, hd, hd), F32)])(u, w, qd, kd, at, eg, states, do)


def _dn_out(o, zg, nw):
    n = o * lax.rsqrt(jnp.mean(o * o, axis=-1, keepdims=True) + NORM_EPS) * nw
    return n * (zg * _sigmoid(zg))


def dn_out_fwd(o, z, nw, name):
    s = o.shape[0]
    tm = _tile(s, 512)
    hd = DN_HEAD_DIM

    def body(o_ref, zg_ref, nw_ref, y_ref):
        for h in range(DN_HEADS):
            hs = slice(h * hd, (h + 1) * hd)
            y_ref[:, hs] = _dn_out(o_ref[:, hs], zg_ref[:, hs], nw_ref[...]).astype(BF16)

    return _call(body, name, (s // tm,),
                 [pl.BlockSpec((tm, DN_WIDTH), lambda i: (i, 0)), pl.BlockSpec((tm, DN_WIDTH), lambda i: (i, Z_ZG // DN_WIDTH)),
                  pl.BlockSpec((1, hd), lambda i: (0, 0))],
                 pl.BlockSpec((tm, DN_WIDTH), lambda i: (i, 0)), _sds((s, DN_WIDTH), BF16), sem=("parallel",))(o, z, nw)


def dn_out_bwd(o, z, nw, dycat, name):
    s = o.shape[0]
    tm = _tile(s, 512)
    hd = DN_HEAD_DIM

    def body(o_ref, zg_ref, nw_ref, dy_ref, do_ref, dzg_ref, dnw_ref):
        @pl.when(pl.program_id(0) == 0)
        def _():
            dnw_ref[...] = jnp.zeros_like(dnw_ref)

        for h in range(DN_HEADS):
            hs = slice(h * hd, (h + 1) * hd)
            _, vjp = jax.vjp(_dn_out, o_ref[:, hs], zg_ref[:, hs], nw_ref[...])
            do, dzg, dnw = vjp(dy_ref[:, hs])
            do_ref[:, hs] = do
            dzg_ref[:, hs] = dzg
            dnw_ref[...] += dnw

    wide = pl.BlockSpec((tm, DN_WIDTH), lambda i: (i, 0))
    wsd = _sds((s, DN_WIDTH), F32)
    return _call(body, name, (s // tm,),
                 [wide, pl.BlockSpec((tm, DN_WIDTH), lambda i: (i, Z_ZG // DN_WIDTH)), pl.BlockSpec((1, hd), lambda i: (0, 0)),
                  pl.BlockSpec((tm, DN_WIDTH), lambda i: (i, ATTN_WIDTH // DN_WIDTH))],
                 [wide, wide, pl.BlockSpec((1, hd), lambda i: (0, 0))], [wsd, wsd, _sds((1, hd), F32)])(o, z, nw, dycat)


def _s5_param_fn(are, aim, ldt, bre, bim):
    dt = jnp.exp(ldt)
    er = jnp.exp(are * dt)
    abr = er * jnp.cos(aim * dt)
    abi = er * jnp.sin(aim * dt)
    den = are * are + aim * aim
    cr = ((abr - 1.0) * are + abi * aim) / den
    ci = (abi * are - (abr - 1.0) * aim) / den
    return abr, abi, cr * bre - ci * bim, cr * bim + ci * bre


def s5_params_fwd(are, aim, ldt, bre, bim, name):
    p, hh = bre.shape

    def body(a_ref, b_ref, c_ref, d_ref, e_ref, o1, o2, o3, o4):
        o1[...], o2[...], o3[...], o4[...] = _s5_param_fn(a_ref[...], b_ref[...], c_ref[...], d_ref[...], e_ref[...])

    col = pl.BlockSpec((p, 1), lambda: (0, 0))
    mat = pl.BlockSpec((p, hh), lambda: (0, 0))
    return _call(body, name, (), [col, col, col, mat, mat], [col, col, mat, mat],
                 [_sds((p, 1), F32), _sds((p, 1), F32), _sds((p, hh), F32), _sds((p, hh), F32)])(are, aim, ldt, bre, bim)


def s5_params_bwd(are, aim, ldt, bre, bim, cts, name):
    p, hh = bre.shape

    def body(a_ref, b_ref, c_ref, d_ref, e_ref, g1, g2, g3, g4, o1, o2, o3, o4, o5):
        _, vjp = jax.vjp(_s5_param_fn, a_ref[...], b_ref[...], c_ref[...], d_ref[...], e_ref[...])
        o1[...], o2[...], o3[...], o4[...], o5[...] = vjp((g1[...], g2[...], g3[...], g4[...]))

    col = pl.BlockSpec((p, 1), lambda: (0, 0))
    mat = pl.BlockSpec((p, hh), lambda: (0, 0))
    csd, msd = _sds((p, 1), F32), _sds((p, hh), F32)
    return _call(body, name, (), [col, col, col, mat, mat, col, col, mat, mat], [col, col, col, mat, mat],
                 [csd, csd, csd, msd, msd])(are, aim, ldt, bre, bim, *cts)


def _cmul(ar, ai, br, bi):
    return ar * br - ai * bi, ar * bi + ai * br


def _s5_scan_block(xr, xi, ar, ai, n, reverse):
    row = lax.broadcasted_iota(jnp.int32, xr.shape, 0)
    d = 1
    while d < n:
        if reverse:
            sr, si = _shift_up(xr, d, row, n), _shift_up(xi, d, row, n)
        else:
            sr, si = _shift_down(xr, d, row), _shift_down(xi, d, row)
        pr, pi = _cmul(ar, ai, sr, si)
        xr, xi = xr + pr, xi + pi
        ar, ai = _cmul(ar, ai, ar, ai)
        d *= 2
    return xr, xi


def s5_scan_fwd(bu, abr, abi, name):
    s = bu.shape[0]
    npb = S5_P // 128

    def body(br_ref, bi_ref, ar_ref, ai_ref, xr_ref, xi_ref):
        xr_ref[...], xi_ref[...] = _s5_scan_block(br_ref[...], bi_ref[...], ar_ref[...], ai_ref[...], s, False)

    re = pl.BlockSpec((s, 128), lambda j: (0, j))
    im = pl.BlockSpec((s, 128), lambda j: (0, npb + j))
    av = pl.BlockSpec((1, 128), lambda j: (0, j))
    osd = _sds((s, S5_P), F32)
    return _call(body, name, (npb,), [re, im, av, av], [re, re], [osd, osd], sem=("parallel",))(bu, bu, abr, abi)


def s5_scan_bwd(dx, x, abr, abi, name):
    s = dx.shape[0]
    npb = S5_P // 128

    def body(dr_ref, di_ref, xr_ref, xi_ref, ar_ref, ai_ref, gr_ref, gi_ref, dar_ref, dai_ref):
        ar, ai = ar_ref[...], ai_ref[...]
        gr, gi = _s5_scan_block(dr_ref[...], di_ref[...], ar, -ai, s, True)
        gr_ref[...], gi_ref[...] = gr, gi
        row = lax.broadcasted_iota(jnp.int32, gr.shape, 0)
        pr, pi = _shift_down(xr_ref[...], 1, row), _shift_down(xi_ref[...], 1, row)
        dar_ref[...] = jnp.sum(gr * pr + gi * pi, axis=0, keepdims=True)
        dai_ref[...] = jnp.sum(gi * pr - gr * pi, axis=0, keepdims=True)

    re = pl.BlockSpec((s, 128), lambda j: (0, j))
    im = pl.BlockSpec((s, 128), lambda j: (0, npb + j))
    av = pl.BlockSpec((1, 128), lambda j: (0, j))
    osd, asd = _sds((s, S5_P), F32), _sds((1, S5_P), F32)
    return _call(body, name, (npb,), [re, im, re, im, av, av], [re, re, av, av], [osd, osd, asd, asd],
                 sem=("parallel",))(dx, dx, x, x, abr, abi)


def _gelu(y):
    return 0.5 * y * (1.0 + jnp.tanh(math.sqrt(2.0 / math.pi) * (y + 0.044715 * y * y * y)))


def s5_out_fwd(ypre, z, dvec, glu_w, glu_b, name):
    s = ypre.shape[0]
    tm = _tile(s, 512)
    wd = S5_WIDTH

    def body(yp_ref, u_ref, d_ref, w_ref, b_ref, y_ref, o_ref):
        y = yp_ref[...] + d_ref[...] * u_ref[...]
        y_ref[...] = y
        g = _gelu(y)
        t = _dot(g.astype(BF16), w_ref[...]) + b_ref[...]
        o_ref[...] = (g * _sigmoid(t)).astype(BF16)

    row = pl.BlockSpec((tm, wd), lambda i: (i, 0))
    vec = pl.BlockSpec((1, wd), lambda i: (0, 0))
    return _call(body, name, (s // tm,),
                 [row, pl.BlockSpec((tm, wd), lambda i: (i, Z_S5 // wd)), vec, pl.BlockSpec((wd, wd), lambda i: (0, 0)), vec],
                 [row, row], [_sds((s, wd), F32), _sds((s, wd), BF16)], sem=("parallel",))(ypre, z, dvec, glu_w, glu_b)


def s5_out_bwd(y, z, glu_w, glu_b, dycat, name):
    s = y.shape[0]
    tm = _tile(s, 512)
    wd = S5_WIDTH

    def body(y_ref, u_ref, w_ref, b_ref, do_ref, dy_ref, dd_ref, dw_ref, db_ref):
        @pl.when(pl.program_id(0) == 0)
        def _():
            dd_ref[...] = jnp.zeros_like(dd_ref)
            dw_ref[...] = jnp.zeros_like(dw_ref)
            db_ref[...] = jnp.zeros_like(db_ref)

        g, gvjp = jax.vjp(_gelu, y_ref[...])
        gb = g.astype(BF16)
        sg = _sigmoid(_dot(gb, w_ref[...]) + b_ref[...])
        do = do_ref[...]
        dt = do * g * sg * (1.0 - sg)
        dtb = dt.astype(BF16)
        dg = do * sg + _dot(dtb, w_ref[...], NT)
        (dy,) = gvjp(dg)
        dy_ref[...] = dy
        dd_ref[...] += jnp.sum(dy * u_ref[...], axis=0, keepdims=True)
        dw_ref[...] += _dot(gb, dtb, TN)
        db_ref[...] += jnp.sum(dt, axis=0, keepdims=True)

    row = pl.BlockSpec((tm, wd), lambda i: (i, 0))
    vec = pl.BlockSpec((1, wd), lambda i: (0, 0))
    mat = pl.BlockSpec((wd, wd), lambda i: (0, 0))
    return _call(body, name, (s // tm,),
                 [row, pl.BlockSpec((tm, wd), lambda i: (i, Z_S5 // wd)), mat, vec,
                  pl.BlockSpec((tm, wd), lambda i: (i, (ATTN_WIDTH + DN_WIDTH) // wd))],
                 [row, vec, mat, vec], [_sds((s, wd), F32), _sds((1, wd), F32), _sds((wd, wd), F32), _sds((1, wd), F32)])(
        y, z, glu_w, glu_b, dycat)


def assemble_dz(dq, dkc, dkp, dvc, dvp, ddn, dzg, dus, dys, dvec, dzs, name):
    s = dq.shape[0]
    w = WINDOW
    nb = s // w
    nxt = lambda i: jnp.minimum(i + 1, nb - 1)

    def body(dq_ref, dkc_ref, dkp_ref, dvc_ref, dvp_ref, ddn_ref, dzg_ref, dus_ref, dys_ref, dv_ref, dzs_ref, o_ref):
        live = (pl.program_id(0) < nb - 1).astype(F32)
        o_ref[:, Z_Q:Z_K] = dq_ref[...].astype(BF16)
        o_ref[:, Z_K:Z_V] = (dkc_ref[...] + live * dkp_ref[...]).astype(BF16)
        o_ref[:, Z_V:Z_DN] = (dvc_ref[...] + live * dvp_ref[...]).astype(BF16)
        o_ref[:, Z_DN:Z_ZG] = ddn_ref[...].astype(BF16)
        o_ref[:, Z_ZG:Z_S5] = dzg_ref[...].astype(BF16)
        o_ref[:, Z_S5:Z_SM] = (dus_ref[...] + dv_ref[...] * dys_ref[...]).astype(BF16)
        o_ref[:, Z_SM:Z_ALL] = dzs_ref[...].astype(BF16)

    def blk(width, f=lambda i: i):
        return pl.BlockSpec((w, width), lambda i: (f(i), 0))

    return _call(body, name, (nb,),
                 [blk(ATTN_WIDTH), blk(ATTN_KV_WIDTH), blk(ATTN_KV_WIDTH, nxt), blk(ATTN_KV_WIDTH), blk(ATTN_KV_WIDTH, nxt),
                  blk(3 * DN_WIDTH), blk(DN_WIDTH), blk(S5_WIDTH), blk(S5_WIDTH), pl.BlockSpec((1, S5_WIDTH), lambda i: (0, 0)),
                  blk(128)],
                 blk(Z_ALL), _sds((s, Z_ALL), BF16), sem=("parallel",))(dq, dkc, dkp, dvc, dvp, ddn, dzg, dus, dys, dvec, dzs)


def _my_place():
    return lax.axis_index("x"), lax.axis_index("y"), lax.axis_index("c")


def _peer(place, p):
    x, y, c = place
    px = 1 - x if p & 4 else x
    py = 1 - y if p & 2 else y
    pc = 1 - c if p & 1 else c
    return (px, py, pc), 4 * px + 2 * py + pc


def exchange(arrays, scatter, name):
    na = len(arrays)

    def body(*refs):
        srcs, dsts = refs[:na], refs[na:2 * na]
        send_sems, recv_sems, local_sems = refs[2 * na:]
        place = _my_place()
        me = 4 * place[0] + 2 * place[1] + place[2]
        copies = []
        for k in range(na):
            mine = srcs[k].at[me] if scatter else srcs[k]
            loc = pltpu.make_async_copy(mine, dsts[k].at[me], local_sems.at[k])
            loc.start()
            copies.append(loc)
        sends = []
        for p in range(1, N_DEV):
            peer, pid = _peer(place, p)
            for k in range(na):
                src = srcs[k].at[pid] if scatter else srcs[k]
                cp = pltpu.make_async_remote_copy(src_ref=src, dst_ref=dsts[k].at[me], send_sem=send_sems.at[k, p - 1],
                                                  recv_sem=recv_sems.at[k, p - 1], device_id=peer,
                                                  device_id_type=pl.DeviceIdType.MESH)
                cp.start()
                sends.append(cp)
        for p in range(1, N_DEV):
            peer, pid = _peer(place, p)
            for k in range(na):
                src = srcs[k].at[me] if scatter else srcs[k]
                pltpu.make_async_remote_copy(src_ref=src, dst_ref=dsts[k].at[pid], send_sem=send_sems.at[k, p - 1],
                                             recv_sem=recv_sems.at[k, p - 1], device_id=peer,
                                             device_id_type=pl.DeviceIdType.MESH).wait_recv()
        for cp in sends:
            cp.wait_send()
        for cp in copies:
            cp.wait()

    outs = [_sds((N_DEV,) + tuple(a.shape[1:] if scatter else a.shape), a.dtype) for a in arrays]
    anyspec = pl.BlockSpec(memory_space=pl.ANY)
    return pl.pallas_call(
        body, name=name, in_specs=[anyspec] * na, out_specs=[anyspec] * na, out_shape=outs,
        scratch_shapes=[pltpu.SemaphoreType.DMA((na, N_DEV - 1)), pltpu.SemaphoreType.DMA((na, N_DEV - 1)),
                        pltpu.SemaphoreType.DMA((na,))])(*arrays)


def _adamw(w, g, m, v):
    m = ADAM_B1 * m + (1.0 - ADAM_B1) * g
    v = ADAM_B2 * v + (1.0 - ADAM_B2) * (g * g)
    m_hat = m / (1.0 - ADAM_B1 ** ADAM_STEP)
    v_hat = v / (1.0 - ADAM_B2 ** ADAM_STEP)
    return -ADAM_LR * (m_hat / (jnp.sqrt(v_hat) + ADAM_EPS) + ADAM_WD * w), m, v


def reduce_adamw(parts, w4, m4, v4, layer, name):
    _, r, c = parts.shape
    tr = _tile(r, 256)

    def body(p_ref, w_ref, m_ref, v_ref, g_ref, d_ref, nm_ref, nv_ref):
        g = p_ref[0].astype(F32)
        for d in range(1, N_DEV):
            g = g + p_ref[d].astype(F32)
        g_ref[...] = g
        d_ref[...], nm_ref[...], nv_ref[...] = _adamw(w_ref[...], g, m_ref[...], v_ref[...])

    lay = pl.BlockSpec((None, tr, c), lambda i: (layer, i, 0))
    out = pl.BlockSpec((tr, c), lambda i: (i, 0))
    osd = _sds((r, c), F32)
    return _call(body, name, (r // tr,), [pl.BlockSpec((N_DEV, tr, c), lambda i: (0, i, 0)), lay, lay, lay],
                 [out, out, out, out], [osd, osd, osd, osd], sem=("parallel",))(parts, w4, m4, v4)


_SM_NAT = ATTN_WIDTH + 2 * ATTN_KV_WIDTH + 4 * DN_WIDTH


def _win_to_zall(w):
    pad = jnp.zeros((w.shape[0], 128 - 2 * DN_HEADS), w.dtype)
    return jnp.concatenate([w[:, :_SM_NAT], w[:, _SM_NAT + 2 * DN_HEADS:], w[:, _SM_NAT:_SM_NAT + 2 * DN_HEADS], pad], axis=1)


def _zall_to_win(g):
    return jnp.concatenate([g[:, :Z_ZG + DN_WIDTH], g[:, Z_SM:Z_SM + 2 * DN_HEADS], g[:, Z_S5:Z_SM]], axis=1)


def _block_diag(t):
    g, a, b = t.shape
    eye = jnp.eye(g, dtype=t.dtype)
    return (t[:, :, None, :] * eye[:, None, :, None]).reshape(g * a, g * b)


def _block_diag_extract(m, g):
    a, b = m.shape[0] // g, m.shape[1] // g
    eye = jnp.eye(g, dtype=m.dtype)
    return jnp.sum(m.reshape(g, a, g, b) * eye[:, None, :, None], axis=2)


def _rope_tables(s):
    half = HEAD_DIM // 2
    inv_freq = ROPE_THETA ** (-jnp.arange(half, dtype=F32) / half)
    ang = jnp.arange(s, dtype=F32)[:, None] * inv_freq[None, :]
    cos, sin = jnp.cos(ang), jnp.sin(ang)
    return jnp.concatenate([cos, cos], axis=1), jnp.concatenate([-sin, sin], axis=1)


def _row(v):
    return v.reshape(1, -1)


def _ffn_fwd(x, g_pre, g_post, wg, wu, wd, tag):
    h = rmsnorm_fwd(x, g_pre, f"{tag}_norm")
    a, b, u = ffn_up(h, wg, wu, f"{tag}_up")
    y, xn = down_norm(u, wd, x, g_post, FFN_RES_WEIGHT, f"{tag}_down")
    return xn, (x, h, a, b, u, y)


def _ffn_bwd(dxn, saved, g_pre, g_post, wg, wu, wd, tag):
    x, h, a, b, u, y = saved
    dy, dg_post = norm_bwd(dxn, y, g_post, FFN_RES_WEIGHT, None, BF16, f"{tag}_bnorm_post")
    da, db = ffn_down_bwd(dy, wd, a, b, f"{tag}_bdown")
    dwd = mm_tn(u, dy[None], BF16, f"{tag}_dwd")
    dwg = mm_tn(h[None], da, BF16, f"{tag}_dwg")
    dwu = mm_tn(h[None], db, BF16, f"{tag}_dwu")
    dh = mm_nt_acc([(da, wg), (db, wu)], f"{tag}_dh")
    dx, dg_pre = norm_bwd(dh, x, g_pre, 1.0, dxn, F32, f"{tag}_bnorm_pre")
    return dx, dict(g_pre=dg_pre, g_post=dg_post, wg=dwg, wu=dwu, wd=dwd)


def _s5_layouts(p):
    are, aim = p["s5_a_re"].reshape(S5_P, 1), p["s5_a_im"].reshape(S5_P, 1)
    ldt = jnp.repeat(p["s5_log_dt"], S5_STATE).reshape(S5_P, 1)
    bre, bim = p["s5_b_re"].reshape(S5_P, S5_GROUP_CH), p["s5_b_im"].reshape(S5_P, S5_GROUP_CH)
    return are, aim, ldt, bre, bim


def _mix_fwd(x, p, w_all, w_out, glu_w, cos_f, sin_s, tag):
    s = x.shape[0]
    h = rmsnorm_fwd(x, _row(p["mix_norm_pre"]), f"{tag}_norm")
    z = mm_nn(h, w_all, f"{tag}_win")
    y_attn = attn_fwd(z, cos_f, sin_s, p["attn_sinks"], f"{tag}_attn")
    alog_b = jnp.broadcast_to(p["dn_a_log"][:, None], (DN_HEADS, 128))
    dtb_b = jnp.broadcast_to(p["dn_dt_bias"][:, None], (DN_HEADS, 128))
    qkv = dn_pre_fwd(z, p["dn_conv_w"], f"{tag}_dnpre")
    bb, gb = dn_gates_fwd(z, alog_b, dtb_b, f"{tag}_dngate")
    u, w, qd, kd, at, eg = dn_intra_fwd(qkv, gb, bb, f"{tag}_dnintra")
    o, states = dn_scan_fwd(u, w, qd, kd, at, eg, f"{tag}_dnscan")
    y_dn = dn_out_fwd(o, z, _row(p["dn_norm_w"]), f"{tag}_dnout")
    s5cols = _s5_layouts(p)
    abr, abi, bbr, bbi = s5_params_fwd(*s5cols, f"{tag}_s5par")
    tb = lambda t: jnp.transpose(t.reshape(S5_GROUPS, S5_STATE, S5_GROUP_CH), (0, 2, 1))
    b_blk = jnp.concatenate([_block_diag(tb(bbr)), _block_diag(tb(bbi))], axis=1).astype(BF16)
    tc = lambda t: jnp.transpose(t, (0, 2, 1))
    c_blk = jnp.concatenate([_block_diag(tc(p["s5_c_re"])), -_block_diag(tc(p["s5_c_im"]))], axis=0).astype(BF16)
    bu = mm_nn(z, b_blk, f"{tag}_s5bu", tn=1024, col0=Z_S5)
    xr, xi = s5_scan_fwd(bu, abr.reshape(1, S5_P), abi.reshape(1, S5_P), f"{tag}_s5scan")
    xs = jnp.concatenate([xr, xi], axis=1)
    ypre = mm_nn(xs, c_blk, f"{tag}_s5c", tm=256)
    y5, y_s5 = s5_out_fwd(ypre, z, _row(p["s5_d"]), glu_w, _row(p["s5_glu_b"]), f"{tag}_s5out")
    ycat = jnp.concatenate([y_attn, y_dn, y_s5], axis=1)
    mixed, xn = down_norm(ycat[None], w_out[None], x, _row(p["mix_norm_post"]), 1.0, f"{tag}_wout")
    saved = dict(x=x, h=h, z=z, qkv=qkv, bb=bb, gb=gb, dn=(u, w, qd, kd, at, eg), states=states, o=o, s5cols=s5cols,
                 abr=abr, abi=abi, b_blk=b_blk, c_blk=c_blk, xs=xs, y5=y5, ycat=ycat, mixed=mixed,
                 alog_b=alog_b, dtb_b=dtb_b)
    return xn, saved


def _mix_bwd(dxn, sv, p, w_all, w_out, glu_w, cos_f, sin_s, tag):
    z = sv["z"]
    g = {}
    dmixed, g["mix_norm_post"] = norm_bwd(dxn, sv["mixed"], _row(p["mix_norm_post"]), 1.0, None, BF16, f"{tag}_bnorm_post")
    g["w_out"] = mm_tn(sv["ycat"][None], dmixed[None], BF16, f"{tag}_dwout", tn=1024)[0]
    dycat = mm_nt_acc([(dmixed[None], w_out[None])], f"{tag}_dycat")
    dq, dkc, dkp, dvc, dvp, dsink = attn_bwd(z, cos_f, sin_s, p["attn_sinks"], dycat, f"{tag}_battn")
    g["attn_sinks"] = dsink[:, 0]
    do, dzg, dnw = dn_out_bwd(sv["o"], z, _row(p["dn_norm_w"]), dycat, f"{tag}_bdnout")
    g["dn_norm_w"] = dnw[0]
    cts = dn_scan_bwd(*sv["dn"], sv["states"], do, f"{tag}_bdnscan")
    dqn, dkn, dvn, dgb, dbb = dn_intra_bwd(sv["qkv"], sv["gb"], sv["bb"], cts, f"{tag}_bdnintra")
    dzs, dal, ddt = dn_gates_bwd(z, sv["alog_b"], sv["dtb_b"], dbb, dgb, f"{tag}_bdngate")
    g["dn_a_log"], g["dn_dt_bias"] = dal[:, 0], ddt[:, 0]
    ddn, g["dn_conv_w"] = dn_pre_bwd(z, p["dn_conv_w"], jnp.concatenate([dqn, dkn, dvn], axis=1), f"{tag}_bdnpre")
    dy5, dd, dglu, dglub = s5_out_bwd(sv["y5"], z, glu_w, _row(p["s5_glu_b"]), dycat, f"{tag}_bs5out")
    g["s5_d"], g["s5_glu_w"], g["s5_glu_b"] = dd[0], dglu, dglub[0]
    dxs = mm_nt_acc([(dy5[None], sv["c_blk"][None])], f"{tag}_bs5c", tn=1024)
    dc_blk = mm_tn(sv["xs"][None], dy5[None], F32, f"{tag}_ds5c", tk=256)[0]
    ex = lambda m: jnp.transpose(_block_diag_extract(m, S5_GROUPS), (0, 2, 1))
    g["s5_c_re"], g["s5_c_im"] = ex(dc_blk[:S5_P]), -ex(dc_blk[S5_P:])
    gr, gi, dar, dai = s5_scan_bwd(dxs, sv["xs"], sv["abr"].reshape(1, S5_P), sv["abi"].reshape(1, S5_P), f"{tag}_bs5scan")
    dbu = jnp.concatenate([gr, gi], axis=1)
    dus = mm_nt_acc([(dbu[None], sv["b_blk"][None])], f"{tag}_bs5bu")
    u_s5 = z[:, Z_S5:Z_SM]
    db_blk = mm_tn(u_s5[None], dbu[None], F32, f"{tag}_ds5b", tn=1024)[0]
    exb = lambda m: jnp.transpose(_block_diag_extract(m, S5_GROUPS), (0, 2, 1)).reshape(S5_P, S5_GROUP_CH)
    dcols = s5_params_bwd(*sv["s5cols"], (dar.reshape(S5_P, 1), dai.reshape(S5_P, 1), exb(db_blk[:, :S5_P]), exb(db_blk[:, S5_P:])),
                          f"{tag}_bs5par")
    g["s5_a_re"] = dcols[0].reshape(S5_GROUPS, S5_STATE)
    g["s5_a_im"] = dcols[1].reshape(S5_GROUPS, S5_STATE)
    g["s5_log_dt"] = jnp.sum(dcols[2].reshape(S5_GROUPS, S5_STATE), axis=1)
    g["s5_b_re"] = dcols[3].reshape(S5_GROUPS, S5_STATE, S5_GROUP_CH)
    g["s5_b_im"] = dcols[4].reshape(S5_GROUPS, S5_STATE, S5_GROUP_CH)
    dz = assemble_dz(dq, dkc, dkp, dvc, dvp, ddn, dzg, dus, dy5, _row(p["s5_d"]), dzs, f"{tag}_dz")
    g["w_all"] = mm_tn(sv["h"][None], dz[None], BF16, f"{tag}_dwin")[0]
    dh = mm_nt_acc([(dz[None], w_all[None])], f"{tag}_dh")
    dx, g["mix_norm_pre"] = norm_bwd(dh, sv["x"], _row(p["mix_norm_pre"]), 1.0, dxn, F32, f"{tag}_bnorm_pre")
    return dx, g


BIG = ("ff1_w_gate", "ff1_w_up", "ff1_w_down", "w_in", "s5_glu_w", "w_out", "ff2_w_gate", "ff2_w_up", "ff2_w_down")
SMALL = ("ff1_norm_pre", "ff1_norm_post", "mix_norm_pre", "attn_sinks", "dn_conv_w", "dn_a_log", "dn_dt_bias", "dn_norm_w",
         "s5_a_re", "s5_a_im", "s5_log_dt", "s5_b_re", "s5_b_im", "s5_c_re", "s5_c_im", "s5_d", "s5_glu_b",
         "mix_norm_post", "ff2_norm_pre", "ff2_norm_post")
WEIGHTS = ("ff1_norm_pre", "ff1_w_gate", "ff1_w_up", "ff1_w_down", "ff1_norm_post", "mix_norm_pre", "w_in", "attn_sinks",
           "dn_conv_w", "dn_a_log", "dn_dt_bias", "dn_norm_w", "s5_a_re", "s5_a_im", "s5_log_dt", "s5_b_re", "s5_b_im",
           "s5_c_re", "s5_c_im", "s5_d", "s5_glu_w", "s5_glu_b", "w_out", "mix_norm_post", "ff2_norm_pre", "ff2_w_gate",
           "ff2_w_up", "ff2_w_down", "ff2_norm_post")


def _pack(parts):
    flat = jnp.concatenate([a.reshape(-1) for a in parts])
    n = flat.shape[0]
    rows = -(-n // 1024) * 8
    return jnp.pad(flat, (0, rows * 128 - n)).reshape(rows, 128)


def _unpack(mat, shapes):
    flat = mat.reshape(-1)
    out, off = [], 0
    for shp in shapes:
        n = int(np.prod(shp))
        out.append(flat[off:off + n].reshape(shp))
        off += n
    return out


def local_step(x, target, fulls, smalls):
    depth = len(fulls)
    d_model = x.shape[1]
    cos_f, sin_s = _rope_tables(x.shape[0])
    xs = x
    saved = []
    for l in range(depth):
        fw, p = fulls[l], smalls[l]
        xs, s1 = _ffn_fwd(xs, _row(p["ff1_norm_pre"]), _row(p["ff1_norm_post"]), fw["ff1_w_gate"], fw["ff1_w_up"], fw["ff1_w_down"], "ff1")
        xs, s2 = _mix_fwd(xs, p, fw["w_all"], fw["w_out"], fw["s5_glu_w"], cos_f, sin_s, "mix")
        xs, s3 = _ffn_fwd(xs, _row(p["ff2_norm_pre"]), _row(p["ff2_norm_post"]), fw["ff2_w_gate"], fw["ff2_w_up"], fw["ff2_w_down"], "ff2")
        saved.append((s1, s2, s3))

    loss_vec, dx = loss_and_grad(xs, target, "loss")

    big_g, small_g = [None] * depth, [None] * depth
    for l in reversed(range(depth)):
        fw, p = fulls[l], smalls[l]
        s1, s2, s3 = saved[l]
        dx, g3 = _ffn_bwd(dx, s3, _row(p["ff2_norm_pre"]), _row(p["ff2_norm_post"]), fw["ff2_w_gate"], fw["ff2_w_up"], fw["ff2_w_down"], "ff2")
        dx, g2 = _mix_bwd(dx, s2, p, fw["w_all"], fw["w_out"], fw["s5_glu_w"], cos_f, sin_s, "mix")
        dx, g1 = _ffn_bwd(dx, s1, _row(p["ff1_norm_pre"]), _row(p["ff1_norm_post"]), fw["ff1_w_gate"], fw["ff1_w_up"], fw["ff1_w_down"], "ff1")
        dwin = _zall_to_win(g2["w_all"])
        big_g[l] = {
            "ff1_w_gate": g1["wg"], "ff1_w_up": g1["wu"], "ff1_w_down": g1["wd"],
            "w_in": jnp.transpose(dwin.reshape(d_model, N_DEV, IN_WIDTH // N_DEV), (1, 0, 2)),
            "s5_glu_w": g2["s5_glu_w"].astype(BF16).reshape(N_DEV, S5_WIDTH // N_DEV, S5_WIDTH),
            "w_out": g2["w_out"].reshape(N_DEV, MIX_WIDTH // N_DEV, d_model),
            "ff2_w_gate": g3["wg"], "ff2_w_up": g3["wu"], "ff2_w_down": g3["wd"],
        }
        sg = {n: g2[n] for n in SMALL if n in g2}
        sg.update(ff1_norm_pre=g1["g_pre"][0], ff1_norm_post=g1["g_post"][0], ff2_norm_pre=g3["g_pre"][0], ff2_norm_post=g3["g_post"][0],
                  mix_norm_pre=g2["mix_norm_pre"][0], mix_norm_post=g2["mix_norm_post"][0])
        small_g[l] = sg
    return loss_vec, dx, big_g, small_g


def kernel(x, ff1_norm_pre, ff1_w_gate, ff1_w_up, ff1_w_down, ff1_norm_post, mix_norm_pre, w_in, attn_sinks, dn_conv_w, dn_a_log, dn_dt_bias, dn_norm_w, s5_a_re, s5_a_im, s5_log_dt, s5_b_re, s5_b_im, s5_c_re, s5_c_im, s5_d, s5_glu_w, s5_glu_b, w_out, mix_norm_post, ff2_norm_pre, ff2_w_gate, ff2_w_up, ff2_w_down, ff2_norm_post, loss_target, m_ff1_norm_pre, m_ff1_w_gate, m_ff1_w_up, m_ff1_w_down, m_ff1_norm_post, m_mix_norm_pre, m_w_in, m_attn_sinks, m_dn_conv_w, m_dn_a_log, m_dn_dt_bias, m_dn_norm_w, m_s5_a_re, m_s5_a_im, m_s5_log_dt, m_s5_b_re, m_s5_b_im, m_s5_c_re, m_s5_c_im, m_s5_d, m_s5_glu_w, m_s5_glu_b, m_w_out, m_mix_norm_post, m_ff2_norm_pre, m_ff2_w_gate, m_ff2_w_up, m_ff2_w_down, m_ff2_norm_post, v_ff1_norm_pre, v_ff1_w_gate, v_ff1_w_up, v_ff1_w_down, v_ff1_norm_post, v_mix_norm_pre, v_w_in, v_attn_sinks, v_dn_conv_w, v_dn_a_log, v_dn_dt_bias, v_dn_norm_w, v_s5_a_re, v_s5_a_im, v_s5_log_dt, v_s5_b_re, v_s5_b_im, v_s5_c_re, v_s5_c_im, v_s5_d, v_s5_glu_w, v_s5_glu_b, v_w_out, v_mix_norm_post, v_ff2_norm_pre, v_ff2_w_gate, v_ff2_w_up, v_ff2_w_down, v_ff2_norm_post):
    args = dict(locals())
    W = {n: args[n] for n in WEIGHTS}
    M = {n: args["m_" + n] for n in WEIGHTS}
    V = {n: args["v_" + n] for n in WEIGHTS}
    depth = ff1_norm_pre.shape[0]
    d_model = x.shape[2]
    me = 4 * lax.axis_index("x") + 2 * lax.axis_index("y") + lax.axis_index("c")

    conv_sh = dn_conv_w.shape[2]
    conv_all = exchange([dn_conv_w.reshape(depth * DN_CONV, conv_sh)], False, "gather_conv")[0]
    conv_full = jnp.transpose(conv_all.reshape(N_DEV, depth, DN_CONV, conv_sh), (1, 2, 0, 3)).reshape(depth, DN_CONV, N_DEV * conv_sh)

    def gather_layer(l):
        shards = [cast_bf16(W[n], l, f"cast_{n}") for n in BIG]
        full = dict(zip(BIG, exchange(shards, False, "gather_weights")))
        win = jnp.transpose(full["w_in"], (1, 0, 2)).reshape(d_model, IN_WIDTH)
        full["w_all"] = _win_to_zall(win)
        full["w_out"] = full["w_out"].reshape(MIX_WIDTH, d_model)
        full["s5_glu_w"] = full["s5_glu_w"].reshape(S5_WIDTH, S5_WIDTH)
        return full

    def small_params(l):
        p = {n: W[n][l] for n in SMALL}
        p["dn_conv_w"] = conv_full[l]
        return p

    fulls = [gather_layer(l) for l in range(depth)]
    smalls = [small_params(l) for l in range(depth)]
    loss_vec, dx, big_g, small_g = local_step(x[0], loss_target[0], fulls, smalls)
    loss = lax.psum(loss_vec[0, 0], ("x", "y", "c"))

    big_out = {n: [None] * depth for n in BIG}
    for l in reversed(range(depth)):
        recv = dict(zip(BIG, exchange([big_g[l][n] for n in BIG], True, "scatter_grads")))
        for n in BIG:
            big_out[n][l] = reduce_adamw(recv[n], W[n], M[n], V[n], l, f"adamw_{n}")

    shapes = [(depth,) + ((DN_CONV, N_DEV * conv_sh) if n == "dn_conv_w" else tuple(W[n].shape[1:])) for n in SMALL]
    packed = _pack([jnp.stack([small_g[l][n] for l in range(depth)]) for n in SMALL])
    gathered = exchange([packed], False, "gather_small_grads")[0]

    def shard_of(n, full):
        return lax.dynamic_slice_in_dim(full, me * conv_sh, conv_sh, axis=2) if n == "dn_conv_w" else full

    conv_pad = lambda t: jnp.tile(t, (1, 1, N_DEV))
    wp = _pack([conv_pad(W[n]) if n == "dn_conv_w" else W[n] for n in SMALL])
    mp = _pack([conv_pad(M[n]) if n == "dn_conv_w" else M[n] for n in SMALL])
    vp = _pack([conv_pad(V[n]) if n == "dn_conv_w" else V[n] for n in SMALL])
    sm = reduce_adamw(gathered, wp[None], mp[None], vp[None], 0, "adamw_small")
    small_out = [dict(zip(SMALL, [shard_of(n, t) for n, t in zip(SMALL, _unpack(o, shapes))])) for o in sm]

    outs = []
    for kind in range(4):
        for n in WEIGHTS:
            if n in BIG:
                outs.append(jnp.stack([big_out[n][l][kind] for l in range(depth)]))
            else:
                outs.append(small_out[kind][n])
    return (loss, dx[None], *outs)
```

```python
import functools
import math

import jax
import jax.numpy as jnp
import numpy as np
from jax import lax
from jax.experimental import pallas as pl
from jax.experimental.pallas import tpu as pltpu

F32 = jnp.float32
BF16 = jnp.bfloat16

N_DEV = 8
DEPTH = 4
ATTN_HEADS = 8
ATTN_KV_HEADS = 2
HEAD_DIM = 128
WINDOW = 128
ROPE_THETA = 10000.0
DN_HEADS = 4
DN_HEAD_DIM = 128
DN_CONV = 4
DN_CHUNK = 64
S5_GROUPS = 32
S5_GROUP_CH = 16
S5_STATE = 64
ATTN_WIDTH = ATTN_HEADS * HEAD_DIM
ATTN_KV_WIDTH = ATTN_KV_HEADS * HEAD_DIM
DN_WIDTH = DN_HEADS * DN_HEAD_DIM
S5_WIDTH = S5_GROUPS * S5_GROUP_CH
S5_P = S5_GROUPS * S5_STATE
MIX_WIDTH = ATTN_WIDTH + DN_WIDTH + S5_WIDTH
IN_WIDTH = ATTN_WIDTH + 2 * ATTN_KV_WIDTH + 4 * DN_WIDTH + 2 * DN_HEADS + S5_WIDTH
Z_Q, Z_K, Z_V = 0, ATTN_WIDTH, ATTN_WIDTH + ATTN_KV_WIDTH
Z_DN = ATTN_WIDTH + 2 * ATTN_KV_WIDTH
Z_ZG = Z_DN + 3 * DN_WIDTH
Z_S5 = Z_ZG + DN_WIDTH
Z_SM = Z_S5 + S5_WIDTH
Z_ALL = Z_SM + 128
FFN_RES_WEIGHT = 0.5
NORM_EPS = 1e-6
ADAM_LR, ADAM_B1, ADAM_B2, ADAM_EPS, ADAM_WD, ADAM_STEP = 0.001, 0.9, 0.999, 1e-08, 0.01, 10

VMEM_LIMIT = 56 * 1024 * 1024
HI = lax.Precision.HIGHEST
NEG = -1e30

NN = (((1,), (0,)), ((), ()))
NT = (((1,), (1,)), ((), ()))
TN = (((0,), (0,)), ((), ()))


def _dot(a, b, dims=NN, prec=None):
    return lax.dot_general(a, b, dims, preferred_element_type=F32, precision=prec)


def _tile(n, pref, mult=8):
    if n <= pref:
        return n
    t = (pref // mult) * mult
    while t > mult and n % t:
        t -= mult
    assert n % t == 0, (n, pref)
    return t


def _call(body, name, grid, in_specs, out_specs, out_shape, scratch=(), sem=None):
    if sem is None:
        sem = ("arbitrary",) * len(grid)
    return pl.pallas_call(
        body, name=name, grid=grid, in_specs=in_specs, out_specs=out_specs, out_shape=out_shape,
        scratch_shapes=list(scratch),
        compiler_params=pltpu.CompilerParams(dimension_semantics=sem, vmem_limit_bytes=VMEM_LIMIT))


def _sds(shape, dtype):
    return jax.ShapeDtypeStruct(tuple(shape), dtype)


def _sigmoid(x):
    return 1.0 / (1.0 + jnp.exp(-x))


def _silu_and_grad(a):
    sg = _sigmoid(a)
    return a * sg, sg * (1.0 + a * (1.0 - sg))


def _softplus(x):
    return jnp.maximum(x, 0.0) + jnp.log(1.0 + jnp.exp(-jnp.abs(x)))


def _rms(x, g):
    r = lax.rsqrt(jnp.mean(x * x, axis=-1, keepdims=True) + NORM_EPS)
    return x * r * g


def _rms_bwd(dout, y, g):
    r = lax.rsqrt(jnp.mean(y * y, axis=-1, keepdims=True) + NORM_EPS)
    n = y * r
    dn = dout * g
    dy = r * (dn - n * jnp.mean(dn * n, axis=-1, keepdims=True))
    return dy, jnp.sum(dout * n, axis=0, keepdims=True)


def _rope(x, cos_f, sin_s):
    return x * cos_f + pltpu.roll(x, HEAD_DIM // 2, 1) * sin_s


def _rope_bwd(d, cos_f, sin_s):
    return d * cos_f + pltpu.roll(d * sin_s, HEAD_DIM // 2, 1)


def rmsnorm_fwd(x, g, name):
    s, d = x.shape
    tm = _tile(s, 512)

    def body(x_ref, g_ref, o_ref):
        o_ref[...] = _rms(x_ref[...], g_ref[...]).astype(BF16)

    return _call(body, name, (s // tm,),
                 [pl.BlockSpec((tm, d), lambda i: (i, 0)), pl.BlockSpec((1, d), lambda i: (0, 0))],
                 pl.BlockSpec((tm, d), lambda i: (i, 0)), _sds((s, d), BF16), sem=("parallel",))(x, g)


def norm_bwd(dout, y, g, scale, resid, out_dtype, name):
    s, d = y.shape
    tm = _tile(s, 256)
    has_res = resid is not None

    def body(*refs):
        if has_res:
            do_ref, y_ref, g_ref, r_ref, dy_ref, dg_ref = refs
        else:
            do_ref, y_ref, g_ref, dy_ref, dg_ref = refs
        dy, dg = _rms_bwd(do_ref[...] * scale, y_ref[...], g_ref[...])
        if has_res:
            dy = dy + r_ref[...]
        dy_ref[...] = dy.astype(out_dtype)

        @pl.when(pl.program_id(0) == 0)
        def _():
            dg_ref[...] = jnp.zeros_like(dg_ref)

        dg_ref[...] += dg

    row = pl.BlockSpec((tm, d), lambda i: (i, 0))
    vec = pl.BlockSpec((1, d), lambda i: (0, 0))
    ins = [dout, y, g] + ([resid] if has_res else [])
    return _call(body, name, (s // tm,), [row, row, vec] + ([row] if has_res else []),
                 [row, vec], [_sds((s, d), out_dtype), _sds((1, d), F32)])(*ins)


def ffn_up(h, wg, wu, name):
    s, d = h.shape
    nj, _, fs = wg.shape
    tm = _tile(s, 512)

    def body(h_ref, wg_ref, wu_ref, a_ref, b_ref, u_ref):
        hh = h_ref[...]
        a = _dot(hh, wg_ref[...])
        b = _dot(hh, wu_ref[...])
        a_ref[...] = a.astype(BF16)
        b_ref[...] = b.astype(BF16)
        u_ref[...] = (a * _sigmoid(a) * b).astype(BF16)

    wspec = pl.BlockSpec((None, d, fs), lambda j, i: (j, 0, 0))
    ospec = pl.BlockSpec((None, tm, fs), lambda j, i: (j, i, 0))
    osd = _sds((nj, s, fs), BF16)
    return _call(body, name, (nj, s // tm), [pl.BlockSpec((tm, d), lambda j, i: (i, 0)), wspec, wspec],
                 [ospec, ospec, ospec], [osd, osd, osd], sem=("parallel", "parallel"))(h, wg, wu)


def down_norm(u3, w3, x, g, scale, name):
    nj, s, k = u3.shape
    d = w3.shape[2]
    tm = _tile(s, 512)

    def body(u_ref, w_ref, x_ref, g_ref, y_ref, xn_ref, acc_ref):
        j = pl.program_id(1)

        @pl.when(j == 0)
        def _():
            acc_ref[...] = jnp.zeros_like(acc_ref)

        acc_ref[...] += _dot(u_ref[...], w_ref[...])

        @pl.when(j == nj - 1)
        def _():
            y = acc_ref[...]
            y_ref[...] = y
            xn_ref[...] = x_ref[...] + scale * _rms(y, g_ref[...])

    row = pl.BlockSpec((tm, d), lambda i, j: (i, 0))
    return _call(body, name, (s // tm, nj),
                 [pl.BlockSpec((None, tm, k), lambda i, j: (j, i, 0)), pl.BlockSpec((None, k, d), lambda i, j: (j, 0, 0)),
                  row, pl.BlockSpec((1, d), lambda i, j: (0, 0))],
                 [row, row], [_sds((s, d), F32), _sds((s, d), F32)],
                 scratch=[pltpu.VMEM((tm, d), F32)], sem=("parallel", "arbitrary"))(u3, w3, x, g)


def ffn_down_bwd(dy, wd, a, b, name):
    nj, fs, d = wd.shape
    s = dy.shape[0]
    tm = _tile(s, 512)

    def body(dy_ref, w_ref, a_ref, b_ref, da_ref, db_ref):
        du = _dot(dy_ref[...], w_ref[...], NT)
        aa = a_ref[...].astype(F32)
        bb = b_ref[...].astype(F32)
        sl, dsl = _silu_and_grad(aa)
        da_ref[...] = (du * bb * dsl).astype(BF16)
        db_ref[...] = (du * sl).astype(BF16)

    hspec = pl.BlockSpec((None, tm, fs), lambda j, i: (j, i, 0))
    osd = _sds((nj, s, fs), BF16)
    return _call(body, name, (nj, s // tm),
                 [pl.BlockSpec((tm, d), lambda j, i: (i, 0)), pl.BlockSpec((None, fs, d), lambda j, i: (j, 0, 0)), hspec, hspec],
                 [hspec, hspec], [osd, osd], sem=("parallel", "parallel"))(dy, wd, a, b)


def mm_tn(a3, b3, out_dtype, name, tmm=2048, tn=1408, tk=512):
    ja, s, m = a3.shape
    jb, _, n = b3.shape
    nj = max(ja, jb)
    tmm, tn, tk = _tile(m, tmm, 128), _tile(n, tn, 128), _tile(s, tk)
    nk = s // tk

    def body(a_ref, b_ref, o_ref, acc_ref):
        k = pl.program_id(3)

        @pl.when(k == 0)
        def _():
            acc_ref[...] = jnp.zeros_like(acc_ref)

        acc_ref[...] += _dot(a_ref[...].astype(BF16), b_ref[...].astype(BF16), TN)

        @pl.when(k == nk - 1)
        def _():
            o_ref[...] = acc_ref[...].astype(out_dtype)

    aj = (lambda j: j) if ja > 1 else (lambda j: 0)
    bj = (lambda j: j) if jb > 1 else (lambda j: 0)
    return _call(body, name, (nj, m // tmm, n // tn, nk),
                 [pl.BlockSpec((None, tk, tmm), lambda j, im, jn, k: (aj(j), k, im)),
                  pl.BlockSpec((None, tk, tn), lambda j, im, jn, k: (bj(j), k, jn))],
                 pl.BlockSpec((None, tmm, tn), lambda j, im, jn, k: (j, im, jn)), _sds((nj, m, n), out_dtype),
                 scratch=[pltpu.VMEM((tmm, tn), F32)],
                 sem=("parallel", "parallel", "parallel", "arbitrary"))(a3, b3)


def mm_nt_acc(pairs, name, tm=512, tn=512):
    nj, s, _ = pairs[0][0].shape
    n = pairs[0][1].shape[1]
    tm, tn = _tile(s, tm), _tile(n, tn, 128)
    npair = len(pairs)

    def body(*refs):
        o_ref, acc_ref = refs[2 * npair], refs[2 * npair + 1]
        j = pl.program_id(2)

        @pl.when(j == 0)
        def _():
            acc_ref[...] = jnp.zeros_like(acc_ref)

        t = None
        for p in range(npair):
            c = _dot(refs[2 * p][...].astype(BF16), refs[2 * p + 1][...], NT)
            t = c if t is None else t + c
        acc_ref[...] += t

        @pl.when(j == nj - 1)
        def _():
            o_ref[...] = acc_ref[...]

    ins, specs = [], []
    for a3, w3 in pairs:
        k = a3.shape[2]
        ins += [a3, w3]
        specs += [pl.BlockSpec((None, tm, k), lambda i, jn, j: (j, i, 0)),
                  pl.BlockSpec((None, tn, k), lambda i, jn, j: (j, jn, 0))]
    return _call(body, name, (s // tm, n // tn, nj), specs,
                 pl.BlockSpec((tm, tn), lambda i, jn, j: (i, jn)), _sds((s, n), F32),
                 scratch=[pltpu.VMEM((tm, tn), F32)], sem=("parallel", "parallel", "arbitrary"))(*ins)


def mm_nn(a, w, name, tm=512, tn=1408, col0=0, kdim=None):
    s = a.shape[0]
    k, n = w.shape
    assert col0 % k == 0
    tm, tn = _tile(s, tm), _tile(n, tn, 128)
    cb = col0 // k

    def body(a_ref, w_ref, o_ref):
        o_ref[...] = _dot(a_ref[...].astype(BF16), w_ref[...])

    return _call(body, name, (n // tn, s // tm),
                 [pl.BlockSpec((tm, k), lambda jn, i: (i, cb)), pl.BlockSpec((k, tn), lambda jn, i: (0, jn))],
                 pl.BlockSpec((tm, tn), lambda jn, i: (i, jn)), _sds((s, n), F32), sem=("parallel", "parallel"))(a, w)


def loss_and_grad(xl, target, name):
    s, d = xl.shape
    tm = _tile(s, 512)

    def body(x_ref, t_ref, l_ref, dx_ref):
        e = x_ref[...] - t_ref[...]
        dx_ref[...] = e * (1.0 / d)

        @pl.when(pl.program_id(0) == 0)
        def _():
            l_ref[...] = jnp.zeros_like(l_ref)

        part = jnp.sum(jnp.sum(e * e, axis=-1, keepdims=True), axis=0, keepdims=True) * (0.5 / d)
        l_ref[...] += jnp.broadcast_to(part, l_ref.shape)

    row = pl.BlockSpec((tm, d), lambda i: (i, 0))
    return _call(body, name, (s // tm,), [row, row], [pl.BlockSpec((1, 128), lambda i: (0, 0)), row],
                 [_sds((1, 128), F32), _sds((s, d), F32)])(xl, target)


def cast_bf16(w4, layer, name):
    _, r, c = w4.shape
    tr = _tile(r, 512)

    def body(w_ref, o_ref):
        o_ref[...] = w_ref[...].astype(BF16)

    return _call(body, name, (r // tr,), [pl.BlockSpec((None, tr, c), lambda i: (layer, i, 0))],
                 pl.BlockSpec((tr, c), lambda i: (i, 0)), _sds((r, c), BF16), sem=("parallel",))(w4)


def _attn_specs(nb):
    w = WINDOW
    prev = lambda i: jnp.maximum(i - 1, 0)
    q = pl.BlockSpec((w, ATTN_WIDTH), lambda i: (i, 0))
    kc = pl.BlockSpec((w, ATTN_KV_WIDTH), lambda i: (i, Z_K // ATTN_KV_WIDTH))
    kp = pl.BlockSpec((w, ATTN_KV_WIDTH), lambda i: (prev(i), Z_K // ATTN_KV_WIDTH))
    vc = pl.BlockSpec((w, ATTN_KV_WIDTH), lambda i: (i, Z_V // ATTN_KV_WIDTH))
    vp = pl.BlockSpec((w, ATTN_KV_WIDTH), lambda i: (prev(i), Z_V // ATTN_KV_WIDTH))
    tc = pl.BlockSpec((w, HEAD_DIM), lambda i: (i, 0))
    tp = pl.BlockSpec((w, HEAD_DIM), lambda i: (prev(i), 0))
    sink = pl.BlockSpec(memory_space=pltpu.SMEM)
    return [q, kc, kp, vc, vp, tc, tc, tp, tp, sink]


def _attn_mask(i):
    w = WINDOW
    qi = lax.broadcasted_iota(jnp.int32, (w, 2 * w), 0) + w
    kj = lax.broadcasted_iota(jnp.int32, (w, 2 * w), 1)
    rel = qi - kj
    band = (rel >= 0) & (rel < w)
    return band & jnp.logical_not((i == 0) & (kj < w))


def _attn_probs(q, kk, sink, mask):
    s = _dot(q, kk, NT) * (HEAD_DIM ** -0.5)
    s = jnp.where(mask, s, NEG)
    m = jnp.maximum(jnp.max(s, axis=-1, keepdims=True), sink)
    p = jnp.exp(s - m)
    es = jnp.exp(sink - m)
    inv = 1.0 / (jnp.sum(p, axis=-1, keepdims=True) + es)
    return p * inv, es * inv


def attn_fwd(z, cos_f, sin_s, sinks, name):
    s = z.shape[0]
    nb = s // WINDOW
    hd = HEAD_DIM
    grp = ATTN_HEADS // ATTN_KV_HEADS

    def body(q_ref, kc_ref, kp_ref, vc_ref, vp_ref, cc_ref, sc_ref, cp_ref, sp_ref, sink_ref, o_ref):
        i = pl.program_id(0)
        mask = _attn_mask(i)
        cc, sc, cp, sp = cc_ref[...], sc_ref[...], cp_ref[...], sp_ref[...]
        for kv in range(ATTN_KV_HEADS):
            ksl = slice(kv * hd, (kv + 1) * hd)
            kk = jnp.concatenate([_rope(kp_ref[:, ksl], cp, sp), _rope(kc_ref[:, ksl], cc, sc)], axis=0).astype(BF16)
            vv = jnp.concatenate([vp_ref[:, ksl], vc_ref[:, ksl]], axis=0).astype(BF16)
            for g in range(grp):
                h = kv * grp + g
                hsl = slice(h * hd, (h + 1) * hd)
                q = _rope(q_ref[:, hsl], cc, sc).astype(BF16)
                pn, _ = _attn_probs(q, kk, sink_ref[h], mask)
                o_ref[:, hsl] = _dot(pn.astype(BF16), vv).astype(BF16)

    return _call(body, name, (nb,), _attn_specs(nb), pl.BlockSpec((WINDOW, ATTN_WIDTH), lambda i: (i, 0)),
                 _sds((s, ATTN_WIDTH), BF16), sem=("parallel",))(z, z, z, z, z, cos_f, sin_s, cos_f, sin_s, sinks)


def attn_bwd(z, cos_f, sin_s, sinks, dy, name):
    s = z.shape[0]
    nb = s // WINDOW
    hd = HEAD_DIM
    grp = ATTN_HEADS // ATTN_KV_HEADS
    scale = HEAD_DIM ** -0.5

    def body(q_ref, kc_ref, kp_ref, vc_ref, vp_ref, cc_ref, sc_ref, cp_ref, sp_ref, sink_ref, dy_ref,
             dq_ref, dkc_ref, dkp_ref, dvc_ref, dvp_ref, ds_ref):
        i = pl.program_id(0)
        mask = _attn_mask(i)
        cc, sc, cp, sp = cc_ref[...], sc_ref[...], cp_ref[...], sp_ref[...]

        @pl.when(i == 0)
        def _():
            ds_ref[...] = jnp.zeros_like(ds_ref)

        for kv in range(ATTN_KV_HEADS):
            ksl = slice(kv * hd, (kv + 1) * hd)
            kk = jnp.concatenate([_rope(kp_ref[:, ksl], cp, sp), _rope(kc_ref[:, ksl], cc, sc)], axis=0).astype(BF16)
            vv = jnp.concatenate([vp_ref[:, ksl], vc_ref[:, ksl]], axis=0).astype(BF16)
            dkk = jnp.zeros((2 * WINDOW, hd), F32)
            dvv = jnp.zeros((2 * WINDOW, hd), F32)
            for g in range(grp):
                h = kv * grp + g
                hsl = slice(h * hd, (h + 1) * hd)
                q = _rope(q_ref[:, hsl], cc, sc).astype(BF16)
                pn, psink = _attn_probs(q, kk, sink_ref[h], mask)
                do = dy_ref[:, hsl].astype(BF16)
                dpn = _dot(do, vv, NT)
                dvv = dvv + _dot(pn.astype(BF16), do, TN)
                tot = jnp.sum(pn * dpn, axis=-1, keepdims=True)
                dsc = (pn * (dpn - tot) * scale).astype(BF16)
                dq_ref[:, hsl] = _rope_bwd(_dot(dsc, kk), cc, sc)
                dkk = dkk + _dot(dsc, q, TN)
                dsink = jnp.sum(-psink * tot, axis=0, keepdims=True)
                ds_ref[h:h + 1, :] += jnp.broadcast_to(dsink, (1, 128))
            dkp_ref[:, ksl] = _rope_bwd(dkk[:WINDOW], cp, sp)
            dkc_ref[:, ksl] = _rope_bwd(dkk[WINDOW:], cc, sc)
            dvp_ref[:, ksl] = dvv[:WINDOW]
            dvc_ref[:, ksl] = dvv[WINDOW:]

    kvo = pl.BlockSpec((WINDOW, ATTN_KV_WIDTH), lambda i: (i, 0))
    kvs = _sds((s, ATTN_KV_WIDTH), F32)
    return _call(body, name, (nb,), _attn_specs(nb) + [pl.BlockSpec((WINDOW, ATTN_WIDTH), lambda i: (i, 0))],
                 [pl.BlockSpec((WINDOW, ATTN_WIDTH), lambda i: (i, 0)), kvo, kvo, kvo, kvo,
                  pl.BlockSpec((ATTN_HEADS, 128), lambda i: (0, 0))],
                 [_sds((s, ATTN_WIDTH), F32), kvs, kvs, kvs, kvs, _sds((ATTN_HEADS, 128), F32)])(
        z, z, z, z, z, cos_f, sin_s, cos_f, sin_s, sinks, dy)


def _shift_down(x, d, row):
    return jnp.where(row >= d, pltpu.roll(x, d, 0), 0.0)


def _shift_up(x, d, row, n):
    return jnp.where(row < n - d, pltpu.roll(x, n - d, 0), 0.0)


def _conv_taps(u, w_ref, row):
    c = w_ref[DN_CONV - 1:DN_CONV, :] * u
    for k in range(DN_CONV - 1):
        c = c + w_ref[k:k + 1, :] * _shift_down(u, DN_CONV - 1 - k, row)
    return c


def dn_pre_fwd(z, conv_w, name):
    s = z.shape[0]
    nblk = 3 * DN_WIDTH // 128
    nqk = 2 * DN_WIDTH // 128

    def body(u_ref, w_ref, o_ref):
        row = lax.broadcasted_iota(jnp.int32, (s, 128), 0)
        c = _conv_taps(u_ref[...], w_ref, row)
        sl = c * _sigmoid(c)
        j = pl.program_id(0)

        @pl.when(j < nqk)
        def _():
            o_ref[...] = sl * lax.rsqrt(jnp.sum(sl * sl, axis=-1, keepdims=True) + NORM_EPS)

        @pl.when(j >= nqk)
        def _():
            o_ref[...] = sl

    return _call(body, name, (nblk,),
                 [pl.BlockSpec((s, 128), lambda j: (0, Z_DN // 128 + j)), pl.BlockSpec((DN_CONV, 128), lambda j: (0, j))],
                 pl.BlockSpec((s, 128), lambda j: (0, j)), _sds((s, 3 * DN_WIDTH), F32), sem=("parallel",))(z, conv_w)


def dn_pre_bwd(z, conv_w, dout, name):
    s = z.shape[0]
    nblk = 3 * DN_WIDTH // 128
    nqk = 2 * DN_WIDTH // 128

    def body(u_ref, w_ref, do_ref, du_ref, dw_ref, ds_ref):
        row = lax.broadcasted_iota(jnp.int32, (s, 128), 0)
        u = u_ref[...]
        c = _conv_taps(u, w_ref, row)
        sl, dsl = _silu_and_grad(c)
        do = do_ref[...]
        j = pl.program_id(0)

        @pl.when(j < nqk)
        def _():
            r = lax.rsqrt(jnp.sum(sl * sl, axis=-1, keepdims=True) + NORM_EPS)
            ds_ref[...] = r * do - sl * (r * r * r) * jnp.sum(do * sl, axis=-1, keepdims=True)

        @pl.when(j >= nqk)
        def _():
            ds_ref[...] = do

        dc = ds_ref[...] * dsl
        du = w_ref[DN_CONV - 1:DN_CONV, :] * dc
        dw_ref[DN_CONV - 1:DN_CONV, :] = jnp.sum(dc * u, axis=0, keepdims=True)
        for k in range(DN_CONV - 1):
            d = DN_CONV - 1 - k
            du = du + w_ref[k:k + 1, :] * _shift_up(dc, d, row, s)
            dw_ref[k:k + 1, :] = jnp.sum(dc * _shift_down(u, d, row), axis=0, keepdims=True)
        du_ref[...] = du

    blk = pl.BlockSpec((s, 128), lambda j: (0, j))
    wsp = pl.BlockSpec((DN_CONV, 128), lambda j: (0, j))
    return _call(body, name, (nblk,), [pl.BlockSpec((s, 128), lambda j: (0, Z_DN // 128 + j)), wsp, blk],
                 [blk, wsp], [_sds((s, 3 * DN_WIDTH), F32), _sds((DN_CONV, 3 * DN_WIDTH), F32)],
                 scratch=[pltpu.VMEM((s, 128), F32)], sem=("parallel",))(z, conv_w, dout)


def _lane_col(x, lane, idx):
    return jnp.sum(jnp.where(lane == idx, x, 0.0), axis=-1, keepdims=True)


def dn_gates_fwd(z, alog_b, dtb_b, name):
    s = z.shape[0]
    tm = _tile(s, 512)

    def body(zs_ref, al_ref, dt_ref, beta_ref, g_ref):
        zs = zs_ref[...]
        lane = lax.broadcasted_iota(jnp.int32, zs.shape, 1)
        for h in range(DN_HEADS):
            b_raw = _lane_col(zs, lane, h)
            a_raw = _lane_col(zs, lane, DN_HEADS + h)
            beta_ref[h] = jnp.broadcast_to(_sigmoid(b_raw), (tm, 128))
            g_ref[h] = -jnp.exp(al_ref[h:h + 1, :]) * _softplus(a_raw + dt_ref[h:h + 1, :])

    osp = pl.BlockSpec((DN_HEADS, tm, 128), lambda i: (0, i, 0))
    psp = pl.BlockSpec((DN_HEADS, 128), lambda i: (0, 0))
    osd = _sds((DN_HEADS, s, 128), F32)
    return _call(body, name, (s // tm,), [pl.BlockSpec((tm, 128), lambda i: (i, Z_SM // 128)), psp, psp],
                 [osp, osp], [osd, osd], sem=("parallel",))(z, alog_b, dtb_b)


def dn_gates_bwd(z, alog_b, dtb_b, dbeta, dg, name):
    s = z.shape[0]
    tm = _tile(s, 512)

    def body(zs_ref, al_ref, dt_ref, dbeta_ref, dg_ref, dz_ref, dal_ref, ddt_ref):
        @pl.when(pl.program_id(0) == 0)
        def _():
            dal_ref[...] = jnp.zeros_like(dal_ref)
            ddt_ref[...] = jnp.zeros_like(ddt_ref)

        zs = zs_ref[...]
        lane = lax.broadcasted_iota(jnp.int32, zs.shape, 1)
        dz = jnp.zeros_like(zs)
        for h in range(DN_HEADS):
            b_raw = _lane_col(zs, lane, h)
            a_raw = _lane_col(zs, lane, DN_HEADS + h)
            dbe = jnp.sum(dbeta_ref[h], axis=-1, keepdims=True)
            dgg = jnp.sum(dg_ref[h], axis=-1, keepdims=True)
            beta = _sigmoid(b_raw)
            ea = jnp.exp(al_ref[h:h + 1, :])
            pre = a_raw + dt_ref[h:h + 1, :]
            da_raw = dgg * (-ea) * _sigmoid(pre)
            dz = dz + jnp.where(lane == h, dbe * beta * (1.0 - beta), 0.0) + jnp.where(lane == DN_HEADS + h, da_raw, 0.0)
            ddt_ref[h:h + 1, :] += jnp.sum(da_raw, axis=0, keepdims=True)
            dal_ref[h:h + 1, :] += jnp.sum(dgg * (-ea) * _softplus(pre), axis=0, keepdims=True)
        dz_ref[...] = dz

    hsp = pl.BlockSpec((DN_HEADS, tm, 128), lambda i: (0, i, 0))
    psp = pl.BlockSpec((DN_HEADS, 128), lambda i: (0, 0))
    return _call(body, name, (s // tm,), [pl.BlockSpec((tm, 128), lambda i: (i, Z_SM // 128)), psp, psp, hsp, hsp],
                 [pl.BlockSpec((tm, 128), lambda i: (i, 0)), psp, psp],
                 [_sds((s, 128), F32), _sds((DN_HEADS, 128), F32), _sds((DN_HEADS, 128), F32)])(z, alog_b, dtb_b, dbeta, dg)


def _dn_intra(q, k, v, gb, bb):
    c = DN_CHUNK
    ri = lax.broadcasted_iota(jnp.int32, (c, c), 0)
    ci = lax.broadcasted_iota(jnp.int32, (c, c), 1)
    causal = ri >= ci
    strict = ri > ci
    gc = _dot(causal.astype(F32), gb, NN, HI)
    e0 = (lax.broadcasted_iota(jnp.int32, (c, 128), 1) == 0).astype(F32)
    grow = _dot(e0, gc, NT, HI)
    decay = jnp.where(causal, jnp.exp(jnp.where(causal, gc[:, :c] - grow, 0.0)), 0.0)
    qs = q * (DN_HEAD_DIM ** -0.5)
    kb = k * bb
    lower = jnp.where(strict, _dot(kb, k, NT) * decay, 0.0)
    t = _unit_lower_inverse(lower)
    eg = jnp.exp(gc)
    u = _dot(t, v * bb)
    w = _dot(t, kb * eg)
    attn = jnp.where(causal, _dot(qs, k, NT) * decay, 0.0)
    glast = _dot((ci == c - 1).astype(F32), gc, NN, HI)
    return u, w, qs * eg, k * jnp.exp(glast - gc), attn, jnp.exp(glast[:8])


@jax.custom_vjp
def _unit_lower_inverse(lower):
    n = lower.shape[0]
    eye = (lax.broadcasted_iota(jnp.int32, (n, n), 0) == lax.broadcasted_iota(jnp.int32, (n, n), 1)).astype(F32)
    t = eye - lower
    p = lower
    for _ in range(5):
        p = _dot(p, p, NN, HI)
        t = t + _dot(t, p, NN, HI)
    return t


def _unit_lower_inverse_fwd(lower):
    t = _unit_lower_inverse(lower)
    return t, t


def _unit_lower_inverse_bwd(t, dt):
    return (-_dot(_dot(t, dt, TN, HI), t, NT, HI),)


_unit_lower_inverse.defvjp(_unit_lower_inverse_fwd, _unit_lower_inverse_bwd)


def _dn_step(st, qd, kd, u, w, attn, egl):
    v_new = u - _dot(w, st)
    o = _dot(qd, st) + _dot(attn, v_new)
    st_new = st * egl[0:1, :] + _dot(kd, v_new, TN)
    return o, st_new


def _dn_chunk_specs(nc):
    c = DN_CHUNK
    wide = pl.BlockSpec((c, DN_WIDTH), lambda i: (i, 0))
    att = pl.BlockSpec((c, DN_HEADS * c), lambda i: (i, 0))
    egl = pl.BlockSpec((8, DN_WIDTH), lambda i: (i, 0))
    return wide, att, egl


def dn_intra_fwd(qkv, gb, bb, name):
    s = qkv.shape[0]
    c, hd = DN_CHUNK, DN_HEAD_DIM
    nc = s // c

    def body(q_ref, k_ref, v_ref, g_ref, b_ref, u_ref, w_ref, qd_ref, kd_ref, at_ref, eg_ref):
        for h in range(DN_HEADS):
            hs = slice(h * hd, (h + 1) * hd)
            u, w, qd, kd, at, eg = _dn_intra(q_ref[:, hs], k_ref[:, hs], v_ref[:, hs], g_ref[h], b_ref[h])
            u_ref[:, hs], w_ref[:, hs], qd_ref[:, hs], kd_ref[:, hs] = u, w, qd, kd
            at_ref[:, h * c:(h + 1) * c] = at
            eg_ref[:, hs] = eg

    wide, att, egl = _dn_chunk_specs(nc)
    hsp = pl.BlockSpec((DN_HEADS, c, 128), lambda i: (0, i, 0))
    wsd = _sds((s, DN_WIDTH), F32)
    return _call(body, name, (nc,),
                 [pl.BlockSpec((c, DN_WIDTH), lambda i: (i, 0)), pl.BlockSpec((c, DN_WIDTH), lambda i: (i, 1)),
                  pl.BlockSpec((c, DN_WIDTH), lambda i: (i, 2)), hsp, hsp],
                 [wide, wide, wide, wide, att, egl],
                 [wsd, wsd, wsd, wsd, _sds((s, DN_HEADS * c), F32), _sds((nc * 8, DN_WIDTH), F32)],
                 sem=("parallel",))(qkv, qkv, qkv, gb, bb)


def dn_intra_bwd(qkv, gb, bb, cts, name):
    s = qkv.shape[0]
    c, hd = DN_CHUNK, DN_HEAD_DIM
    nc = s // c

    def body(q_ref, k_ref, v_ref, g_ref, b_ref, du_ref, dw_ref, dqd_ref, dkd_ref, dat_ref, deg_ref,
             dq_ref, dk_ref, dv_ref, dg_ref, db_ref):
        for h in range(DN_HEADS):
            hs = slice(h * hd, (h + 1) * hd)
            _, vjp = jax.vjp(_dn_intra, q_ref[:, hs], k_ref[:, hs], v_ref[:, hs], g_ref[h], b_ref[h])
            dq, dk, dv, dg, db = vjp((du_ref[:, hs], dw_ref[:, hs], dqd_ref[:, hs], dkd_ref[:, hs],
                                      dat_ref[:, h * c:(h + 1) * c], deg_ref[:, hs]))
            dq_ref[:, hs], dk_ref[:, hs], dv_ref[:, hs] = dq, dk, dv
            dg_ref[h] = dg
            db_ref[h] = db

    wide, att, egl = _dn_chunk_specs(nc)
    hsp = pl.BlockSpec((DN_HEADS, c, 128), lambda i: (0, i, 0))
    hsd = _sds((DN_HEADS, s, 128), F32)
    wsd = _sds((s, DN_WIDTH), F32)
    return _call(body, name, (nc,),
                 [pl.BlockSpec((c, DN_WIDTH), lambda i: (i, 0)), pl.BlockSpec((c, DN_WIDTH), lambda i: (i, 1)),
                  pl.BlockSpec((c, DN_WIDTH), lambda i: (i, 2)), hsp, hsp, wide, wide, wide, wide, att, egl],
                 [wide, wide, wide, hsp, hsp], [wsd, wsd, wsd, hsd, hsd], sem=("parallel",))(qkv, qkv, qkv, gb, bb, *cts)


def dn_scan_fwd(u, w, qd, kd, at, eg, name):
    s = u.shape[0]
    c, hd = DN_CHUNK, DN_HEAD_DIM
    nc = s // c

    def body(u_ref, w_ref, qd_ref, kd_ref, at_ref, eg_ref, o_ref, st_ref, state):
        @pl.when(pl.program_id(0) == 0)
        def _():
            state[...] = jnp.zeros_like(state)

        for h in range(DN_HEADS):
            hs = slice(h * hd, (h + 1) * hd)
            st = state[h]
            st_ref[h] = st
            o, st_new = _dn_step(st, qd_ref[:, hs], kd_ref[:, hs], u_ref[:, hs], w_ref[:, hs],
                                 at_ref[:, h * c:(h + 1) * c], eg_ref[:, hs])
            o_ref[:, hs] = o
            state[h] = st_new

    wide, att, egl = _dn_chunk_specs(nc)
    return _call(body, name, (nc,), [wide, wide, wide, wide, att, egl],
                 [wide, pl.BlockSpec((None, DN_HEADS, hd, hd), lambda i: (i, 0, 0, 0))],
                 [_sds((s, DN_WIDTH), F32), _sds((nc, DN_HEADS, hd, hd), F32)],
                 scratch=[pltpu.VMEM((DN_HEADS, hd, hd), F32)])(u, w, qd, kd, at, eg)


def dn_scan_bwd(u, w, qd, kd, at, eg, states, do, name):
    s = u.shape[0]
    c, hd = DN_CHUNK, DN_HEAD_DIM
    nc = s // c

    def body(u_ref, w_ref, qd_ref, kd_ref, at_ref, eg_ref, st_ref, do_ref,
             du_ref, dw_ref, dqd_ref, dkd_ref, dat_ref, deg_ref, dstate):
        @pl.when(pl.program_id(0) == 0)
        def _():
            dstate[...] = jnp.zeros_like(dstate)

        for h in range(DN_HEADS):
            hs = slice(h * hd, (h + 1) * hd)
            asl = slice(h * c, (h + 1) * c)
            _, vjp = jax.vjp(_dn_step, st_ref[h], qd_ref[:, hs], kd_ref[:, hs], u_ref[:, hs], w_ref[:, hs],
                             at_ref[:, asl], eg_ref[:, hs])
            dst, dqd, dkd, du, dw, dat, deg = vjp((do_ref[:, hs], dstate[h]))
            dstate[h] = dst
            du_ref[:, hs], dw_ref[:, hs], dqd_ref[:, hs], dkd_ref[:, hs] = du, dw, dqd, dkd
            dat_ref[:, asl] = dat
            deg_ref[:, hs] = deg

    rev = lambda i: nc - 1 - i
    wide = pl.BlockSpec((c, DN_WIDTH), lambda i: (rev(i), 0))
    att = pl.BlockSpec((c, DN_HEADS * c), lambda i: (rev(i), 0))
    egl = pl.BlockSpec((8, DN_WIDTH), lambda i: (rev(i), 0))
    wsd = _sds((s, DN_WIDTH), F32)
    return _call(body, name, (nc,),
                 [wide, wide, wide, wide, att, egl, pl.BlockSpec((None, DN_HEADS, hd, hd), lambda i: (rev(i), 0, 0, 0)), wide],
                 [wide, wide, wide, wide, att, egl],
                 [wsd, wsd, wsd, wsd, _sds((s, DN_HEADS * c), F32), _sds((nc * 8, DN_WIDTH), F32)],
                 scratch=[pltpu.VMEM((DN_HEADS, hd, hd), F32)])(u, w, qd, kd, at, eg, states, do)


def _dn_out(o, zg, nw):
    n = o * lax.rsqrt(jnp.mean(o * o, axis=-1, keepdims=True) + NORM_EPS) * nw
    return n * (zg * _sigmoid(zg))


def dn_out_fwd(o, z, nw, name):
    s = o.shape[0]
    tm = _tile(s, 512)
    hd = DN_HEAD_DIM

    def body(o_ref, zg_ref, nw_ref, y_ref):
        for h in range(DN_HEADS):
            hs = slice(h * hd, (h + 1) * hd)
            y_ref[:, hs] = _dn_out(o_ref[:, hs], zg_ref[:, hs], nw_ref[...]).astype(BF16)

    return _call(body, name, (s // tm,),
                 [pl.BlockSpec((tm, DN_WIDTH), lambda i: (i, 0)), pl.BlockSpec((tm, DN_WIDTH), lambda i: (i, Z_ZG // DN_WIDTH)),
                  pl.BlockSpec((1, hd), lambda i: (0, 0))],
                 pl.BlockSpec((tm, DN_WIDTH), lambda i: (i, 0)), _sds((s, DN_WIDTH), BF16), sem=("parallel",))(o, z, nw)


def dn_out_bwd(o, z, nw, dycat, name):
    s = o.shape[0]
    tm = _tile(s, 512)
    hd = DN_HEAD_DIM

    def body(o_ref, zg_ref, nw_ref, dy_ref, do_ref, dzg_ref, dnw_ref):
        @pl.when(pl.program_id(0) == 0)
        def _():
            dnw_ref[...] = jnp.zeros_like(dnw_ref)

        for h in range(DN_HEADS):
            hs = slice(h * hd, (h + 1) * hd)
            _, vjp = jax.vjp(_dn_out, o_ref[:, hs], zg_ref[:, hs], nw_ref[...])
            do, dzg, dnw = vjp(dy_ref[:, hs])
            do_ref[:, hs] = do
            dzg_ref[:, hs] = dzg
            dnw_ref[...] += dnw

    wide = pl.BlockSpec((tm, DN_WIDTH), lambda i: (i, 0))
    wsd = _sds((s, DN_WIDTH), F32)
    return _call(body, name, (s // tm,),
                 [wide, pl.BlockSpec((tm, DN_WIDTH), lambda i: (i, Z_ZG // DN_WIDTH)), pl.BlockSpec((1, hd), lambda i: (0, 0)),
                  pl.BlockSpec((tm, DN_WIDTH), lambda i: (i, ATTN_WIDTH // DN_WIDTH))],
                 [wide, wide, pl.BlockSpec((1, hd), lambda i: (0, 0))], [wsd, wsd, _sds((1, hd), F32)])(o, z, nw, dycat)


def _s5_param_fn(are, aim, ldt, bre, bim):
    dt = jnp.exp(ldt)
    er = jnp.exp(are * dt)
    abr = er * jnp.cos(aim * dt)
    abi = er * jnp.sin(aim * dt)
    den = are * are + aim * aim
    cr = ((abr - 1.0) * are + abi * aim) / den
    ci = (abi * are - (abr - 1.0) * aim) / den
    return abr, abi, cr * bre - ci * bim, cr * bim + ci * bre


def s5_params_fwd(are, aim, ldt, bre, bim, name):
    p, hh = bre.shape

    def body(a_ref, b_ref, c_ref, d_ref, e_ref, o1, o2, o3, o4):
        o1[...], o2[...], o3[...], o4[...] = _s5_param_fn(a_ref[...], b_ref[...], c_ref[...], d_ref[...], e_ref[...])

    col = pl.BlockSpec((p, 1), lambda: (0, 0))
    mat = pl.BlockSpec((p, hh), lambda: (0, 0))
    return _call(body, name, (), [col, col, col, mat, mat], [col, col, mat, mat],
                 [_sds((p, 1), F32), _sds((p, 1), F32), _sds((p, hh), F32), _sds((p, hh), F32)])(are, aim, ldt, bre, bim)


def s5_params_bwd(are, aim, ldt, bre, bim, cts, name):
    p, hh = bre.shape

    def body(a_ref, b_ref, c_ref, d_ref, e_ref, g1, g2, g3, g4, o1, o2, o3, o4, o5):
        _, vjp = jax.vjp(_s5_param_fn, a_ref[...], b_ref[...], c_ref[...], d_ref[...], e_ref[...])
        o1[...], o2[...], o3[...], o4[...], o5[...] = vjp((g1[...], g2[...], g3[...], g4[...]))

    col = pl.BlockSpec((p, 1), lambda: (0, 0))
    mat = pl.BlockSpec((p, hh), lambda: (0, 0))
    csd, msd = _sds((p, 1), F32), _sds((p, hh), F32)
    return _call(body, name, (), [col, col, col, mat, mat, col, col, mat, mat], [col, col, col, mat, mat],
                 [csd, csd, csd, msd, msd])(are, aim, ldt, bre, bim, *cts)


def _cmul(ar, ai, br, bi):
    return ar * br - ai * bi, ar * bi + ai * br


def _s5_scan_block(xr, xi, ar, ai, n, reverse):
    row = lax.broadcasted_iota(jnp.int32, xr.shape, 0)
    d = 1
    while d < n:
        if reverse:
            sr, si = _shift_up(xr, d, row, n), _shift_up(xi, d, row, n)
        else:
            sr, si = _shift_down(xr, d, row), _shift_down(xi, d, row)
        pr, pi = _cmul(ar, ai, sr, si)
        xr, xi = xr + pr, xi + pi
        ar, ai = _cmul(ar, ai, ar, ai)
        d *= 2
    return xr, xi


def s5_scan_fwd(bu, abr, abi, name):
    s = bu.shape[0]
    npb = S5_P // 128

    def body(br_ref, bi_ref, ar_ref, ai_ref, xr_ref, xi_ref):
        xr_ref[...], xi_ref[...] = _s5_scan_block(br_ref[...], bi_ref[...], ar_ref[...], ai_ref[...], s, False)

    re = pl.BlockSpec((s, 128), lambda j: (0, j))
    im = pl.BlockSpec((s, 128), lambda j: (0, npb + j))
    av = pl.BlockSpec((1, 128), lambda j: (0, j))
    osd = _sds((s, S5_P), F32)
    return _call(body, name, (npb,), [re, im, av, av], [re, re], [osd, osd], sem=("parallel",))(bu, bu, abr, abi)


def s5_scan_bwd(dx, x, abr, abi, name):
    s = dx.shape[0]
    npb = S5_P // 128

    def body(dr_ref, di_ref, xr_ref, xi_ref, ar_ref, ai_ref, gr_ref, gi_ref, dar_ref, dai_ref):
        ar, ai = ar_ref[...], ai_ref[...]
        gr, gi = _s5_scan_block(dr_ref[...], di_ref[...], ar, -ai, s, True)
        gr_ref[...], gi_ref[...] = gr, gi
        row = lax.broadcasted_iota(jnp.int32, gr.shape, 0)
        pr, pi = _shift_down(xr_ref[...], 1, row), _shift_down(xi_ref[...], 1, row)
        dar_ref[...] = jnp.sum(gr * pr + gi * pi, axis=0, keepdims=True)
        dai_ref[...] = jnp.sum(gi * pr - gr * pi, axis=0, keepdims=True)

    re = pl.BlockSpec((s, 128), lambda j: (0, j))
    im = pl.BlockSpec((s, 128), lambda j: (0, npb + j))
    av = pl.BlockSpec((1, 128), lambda j: (0, j))
    osd, asd = _sds((s, S5_P), F32), _sds((1, S5_P), F32)
    return _call(body, name, (npb,), [re, im, re, im, av, av], [re, re, av, av], [osd, osd, asd, asd],
                 sem=("parallel",))(dx, dx, x, x, abr, abi)


def _gelu(y):
    return 0.5 * y * (1.0 + jnp.tanh(math.sqrt(2.0 / math.pi) * (y + 0.044715 * y * y * y)))


def s5_out_fwd(ypre, z, dvec, glu_w, glu_b, name):
    s = ypre.shape[0]
    tm = _tile(s, 512)
    wd = S5_WIDTH

    def body(yp_ref, u_ref, d_ref, w_ref, b_ref, y_ref, o_ref):
        y = yp_ref[...] + d_ref[...] * u_ref[...]
        y_ref[...] = y
        g = _gelu(y)
        t = _dot(g.astype(BF16), w_ref[...]) + b_ref[...]
        o_ref[...] = (g * _sigmoid(t)).astype(BF16)

    row = pl.BlockSpec((tm, wd), lambda i: (i, 0))
    vec = pl.BlockSpec((1, wd), lambda i: (0, 0))
    return _call(body, name, (s // tm,),
                 [row, pl.BlockSpec((tm, wd), lambda i: (i, Z_S5 // wd)), vec, pl.BlockSpec((wd, wd), lambda i: (0, 0)), vec],
                 [row, row], [_sds((s, wd), F32), _sds((s, wd), BF16)], sem=("parallel",))(ypre, z, dvec, glu_w, glu_b)


def s5_out_bwd(y, z, glu_w, glu_b, dycat, name):
    s = y.shape[0]
    tm = _tile(s, 512)
    wd = S5_WIDTH

    def body(y_ref, u_ref, w_ref, b_ref, do_ref, dy_ref, dd_ref, dw_ref, db_ref):
        @pl.when(pl.program_id(0) == 0)
        def _():
            dd_ref[...] = jnp.zeros_like(dd_ref)
            dw_ref[...] = jnp.zeros_like(dw_ref)
            db_ref[...] = jnp.zeros_like(db_ref)

        g, gvjp = jax.vjp(_gelu, y_ref[...])
        gb = g.astype(BF16)
        sg = _sigmoid(_dot(gb, w_ref[...]) + b_ref[...])
        do = do_ref[...]
        dt = do * g * sg * (1.0 - sg)
        dtb = dt.astype(BF16)
        dg = do * sg + _dot(dtb, w_ref[...], NT)
        (dy,) = gvjp(dg)
        dy_ref[...] = dy
        dd_ref[...] += jnp.sum(dy * u_ref[...], axis=0, keepdims=True)
        dw_ref[...] += _dot(gb, dtb, TN)
        db_ref[...] += jnp.sum(dt, axis=0, keepdims=True)

    row = pl.BlockSpec((tm, wd), lambda i: (i, 0))
    vec = pl.BlockSpec((1, wd), lambda i: (0, 0))
    mat = pl.BlockSpec((wd, wd), lambda i: (0, 0))
    return _call(body, name, (s // tm,),
                 [row, pl.BlockSpec((tm, wd), lambda i: (i, Z_S5 // wd)), mat, vec,
                  pl.BlockSpec((tm, wd), lambda i: (i, (ATTN_WIDTH + DN_WIDTH) // wd))],
                 [row, vec, mat, vec], [_sds((s, wd), F32), _sds((1, wd), F32), _sds((wd, wd), F32), _sds((1, wd), F32)])(
        y, z, glu_w, glu_b, dycat)


def assemble_dz(dq, dkc, dkp, dvc, dvp, ddn, dzg, dus, dys, dvec, dzs, name):
    s = dq.shape[0]
    w = WINDOW
    nb = s // w
    nxt = lambda i: jnp.minimum(i + 1, nb - 1)

    def body(dq_ref, dkc_ref, dkp_ref, dvc_ref, dvp_ref, ddn_ref, dzg_ref, dus_ref, dys_ref, dv_ref, dzs_ref, o_ref):
        live = (pl.program_id(0) < nb - 1).astype(F32)
        o_ref[:, Z_Q:Z_K] = dq_ref[...].astype(BF16)
        o_ref[:, Z_K:Z_V] = (dkc_ref[...] + live * dkp_ref[...]).astype(BF16)
        o_ref[:, Z_V:Z_DN] = (dvc_ref[...] + live * dvp_ref[...]).astype(BF16)
        o_ref[:, Z_DN:Z_ZG] = ddn_ref[...].astype(BF16)
        o_ref[:, Z_ZG:Z_S5] = dzg_ref[...].astype(BF16)
        o_ref[:, Z_S5:Z_SM] = (dus_ref[...] + dv_ref[...] * dys_ref[...]).astype(BF16)
        o_ref[:, Z_SM:Z_ALL] = dzs_ref[...].astype(BF16)

    def blk(width, f=lambda i: i):
        return pl.BlockSpec((w, width), lambda i: (f(i), 0))

    return _call(body, name, (nb,),
                 [blk(ATTN_WIDTH), blk(ATTN_KV_WIDTH), blk(ATTN_KV_WIDTH, nxt), blk(ATTN_KV_WIDTH), blk(ATTN_KV_WIDTH, nxt),
                  blk(3 * DN_WIDTH), blk(DN_WIDTH), blk(S5_WIDTH), blk(S5_WIDTH), pl.BlockSpec((1, S5_WIDTH), lambda i: (0, 0)),
                  blk(128)],
                 blk(Z_ALL), _sds((s, Z_ALL), BF16), sem=("parallel",))(dq, dkc, dkp, dvc, dvp, ddn, dzg, dus, dys, dvec, dzs)


def _my_place():
    return lax.axis_index("x"), lax.axis_index("y"), lax.axis_index("c")


def _peer(place, p):
    x, y, c = place
    px = 1 - x if p & 4 else x
    py = 1 - y if p & 2 else y
    pc = 1 - c if p & 1 else c
    return (px, py, pc), 4 * px + 2 * py + pc


def exchange(arrays, scatter, name):
    na = len(arrays)

    def body(*refs):
        srcs, dsts = refs[:na], refs[na:2 * na]
        send_sems, recv_sems, local_sems = refs[2 * na:]
        place = _my_place()
        me = 4 * place[0] + 2 * place[1] + place[2]
        copies = []
        for k in range(na):
            mine = srcs[k].at[me] if scatter else srcs[k]
            loc = pltpu.make_async_copy(mine, dsts[k].at[me], local_sems.at[k])
            loc.start()
            copies.append(loc)
        sends = []
        for p in range(1, N_DEV):
            peer, pid = _peer(place, p)
            for k in range(na):
                src = srcs[k].at[pid] if scatter else srcs[k]
                cp = pltpu.make_async_remote_copy(src_ref=src, dst_ref=dsts[k].at[me], send_sem=send_sems.at[k, p - 1],
                                                  recv_sem=recv_sems.at[k, p - 1], device_id=peer,
                                                  device_id_type=pl.DeviceIdType.MESH)
                cp.start()
                sends.append(cp)
        for p in range(1, N_DEV):
            peer, pid = _peer(place, p)
            for k in range(na):
                src = srcs[k].at[me] if scatter else srcs[k]
                pltpu.make_async_remote_copy(src_ref=src, dst_ref=dsts[k].at[pid], send_sem=send_sems.at[k, p - 1],
                                             recv_sem=recv_sems.at[k, p - 1], device_id=peer,
                                             device_id_type=pl.DeviceIdType.MESH).wait_recv()
        for cp in sends:
            cp.wait_send()
        for cp in copies:
            cp.wait()

    outs = [_sds((N_DEV,) + tuple(a.shape[1:] if scatter else a.shape), a.dtype) for a in arrays]
    anyspec = pl.BlockSpec(memory_space=pl.ANY)
    return pl.pallas_call(
        body, name=name, in_specs=[anyspec] * na, out_specs=[anyspec] * na, out_shape=outs,
        scratch_shapes=[pltpu.SemaphoreType.DMA((na, N_DEV - 1)), pltpu.SemaphoreType.DMA((na, N_DEV - 1)),
                        pltpu.SemaphoreType.DMA((na,))])(*arrays)


_HBM = pl.BlockSpec(memory_space=pltpu.HBM)
_SEM = pl.BlockSpec(memory_space=pltpu.SEMAPHORE)
_DATAFLOW = pltpu.SideEffectType.DATAFLOW_SIDE_EFFECTING


def _split_copies(srcs, lands, send_sems, recv_sems, scatter, arriving):
    place = _my_place()
    me = 4 * place[0] + 2 * place[1] + place[2]
    out = []
    for p in range(1, N_DEV):
        peer, pid = _peer(place, p)
        for k in range(len(srcs)):
            i = k * (N_DEV - 1) + p - 1
            src = srcs[k].at[pid] if scatter else srcs[k]
            dst = lands[k].at[pid] if arriving else lands[k].at[me]
            out.append(pltpu.make_async_remote_copy(src_ref=src, dst_ref=dst, send_sem=send_sems.at[i], recv_sem=recv_sems.at[i],
                                                    device_id=peer, device_id_type=pl.DeviceIdType.MESH))
    return out


def exchange_start(arrays, scatter, name):
    na = len(arrays)
    me = 4 * lax.axis_index("x") + 2 * lax.axis_index("y") + lax.axis_index("c")
    lands = []
    for a in arrays:
        own = lax.dynamic_index_in_dim(a, me, 0, keepdims=True) if scatter else a[None]
        shape = (N_DEV,) + tuple(own.shape[1:])
        land = lax.dynamic_update_slice(lax.empty(shape, a.dtype), own, (me,) + (0,) * (len(shape) - 1))
        lands.append(pltpu.with_memory_space_constraint(land, pltpu.HBM))
    srcs = [pltpu.with_memory_space_constraint(a, pltpu.HBM) for a in arrays]

    def body(*refs):
        src_refs, land_refs = refs[:na], refs[na:2 * na]
        send_sems, recv_sems = refs[2 * na], refs[2 * na + 1]
        token = refs[-1]
        for send in _split_copies(src_refs, land_refs, send_sems, recv_sems, scatter, False):
            send.start()
        token[...] = jnp.zeros_like(token)

    sem = pltpu.SemaphoreType.DMA((na * (N_DEV - 1),))
    outs = pl.pallas_call(
        body, name=name,
        out_shape=(sem, sem, *[pltpu.HBM(a.shape, a.dtype) for a in srcs], *[pltpu.HBM(a.shape, a.dtype) for a in lands],
                   _sds((8, 128), F32)),
        in_specs=[_HBM] * (2 * na), out_specs=(_SEM, _SEM, *[_HBM] * (2 * na), pl.BlockSpec(memory_space=pltpu.VMEM)),
        input_output_aliases={i: 2 + i for i in range(2 * na)},
        compiler_params=pltpu.CompilerParams(has_side_effects=_DATAFLOW))(*srcs, *lands)
    return (outs[0], outs[1], outs[2:2 + na], outs[2 + na:2 + 2 * na]), outs[-1]


def exchange_wait(handle, scatter, name, after):
    send_sems, recv_sems, srcs, lands = handle
    na = len(srcs)

    def body(*refs):
        src_refs, land_refs = refs[:na], refs[na:2 * na]
        for send in _split_copies(src_refs, land_refs, refs[2 * na], refs[2 * na + 1], scatter, False):
            send.wait_send()
        for recv in _split_copies(src_refs, land_refs, refs[2 * na], refs[2 * na + 1], scatter, True):
            recv.wait_recv()

    outs = pl.pallas_call(
        body, name=name, out_shape=tuple(pltpu.HBM(a.shape, a.dtype) for a in (*srcs, *lands)),
        in_specs=[_HBM] * (2 * na) + [_SEM, _SEM, pl.BlockSpec(memory_space=pl.ANY)], out_specs=tuple([_HBM] * (2 * na)),
        input_output_aliases={i: i for i in range(2 * na)},
        compiler_params=pltpu.CompilerParams(has_side_effects=_DATAFLOW))(*srcs, *lands, send_sems, recv_sems, after)
    return list(outs[na:])


def _adamw(w, g, m, v):
    m = ADAM_B1 * m + (1.0 - ADAM_B1) * g
    v = ADAM_B2 * v + (1.0 - ADAM_B2) * (g * g)
    m_hat = m / (1.0 - ADAM_B1 ** ADAM_STEP)
    v_hat = v / (1.0 - ADAM_B2 ** ADAM_STEP)
    return -ADAM_LR * (m_hat / (jnp.sqrt(v_hat) + ADAM_EPS) + ADAM_WD * w), m, v


def reduce_adamw(parts, w4, m4, v4, layer, name):
    _, r, c = parts.shape
    tr = _tile(r, 256)

    def body(p_ref, w_ref, m_ref, v_ref, g_ref, d_ref, nm_ref, nv_ref):
        g = p_ref[0].astype(F32)
        for d in range(1, N_DEV):
            g = g + p_ref[d].astype(F32)
        g_ref[...] = g
        d_ref[...], nm_ref[...], nv_ref[...] = _adamw(w_ref[...], g, m_ref[...], v_ref[...])

    lay = pl.BlockSpec((None, tr, c), lambda i: (layer, i, 0))
    out = pl.BlockSpec((tr, c), lambda i: (i, 0))
    osd = _sds((r, c), F32)
    return _call(body, name, (r // tr,), [pl.BlockSpec((N_DEV, tr, c), lambda i: (0, i, 0)), lay, lay, lay],
                 [out, out, out, out], [osd, osd, osd, osd], sem=("parallel",))(parts, w4, m4, v4)


_SM_NAT = ATTN_WIDTH + 2 * ATTN_KV_WIDTH + 4 * DN_WIDTH


def _win_to_zall(w):
    pad = jnp.zeros((w.shape[0], 128 - 2 * DN_HEADS), w.dtype)
    return jnp.concatenate([w[:, :_SM_NAT], w[:, _SM_NAT + 2 * DN_HEADS:], w[:, _SM_NAT:_SM_NAT + 2 * DN_HEADS], pad], axis=1)


def _zall_to_win(g):
    return jnp.concatenate([g[:, :Z_ZG + DN_WIDTH], g[:, Z_SM:Z_SM + 2 * DN_HEADS], g[:, Z_S5:Z_SM]], axis=1)


def _block_diag(t):
    g, a, b = t.shape
    eye = jnp.eye(g, dtype=t.dtype)
    return (t[:, :, None, :] * eye[:, None, :, None]).reshape(g * a, g * b)


def _block_diag_extract(m, g):
    a, b = m.shape[0] // g, m.shape[1] // g
    eye = jnp.eye(g, dtype=m.dtype)
    return jnp.sum(m.reshape(g, a, g, b) * eye[:, None, :, None], axis=2)


def _rope_tables(s):
    half = HEAD_DIM // 2
    inv_freq = ROPE_THETA ** (-jnp.arange(half, dtype=F32) / half)
    ang = jnp.arange(s, dtype=F32)[:, None] * inv_freq[None, :]
    cos, sin = jnp.cos(ang), jnp.sin(ang)
    return jnp.concatenate([cos, cos], axis=1), jnp.concatenate([-sin, sin], axis=1)


def _row(v):
    return v.reshape(1, -1)


def _ffn_fwd(x, g_pre, g_post, wg, wu, wd, tag):
    h = rmsnorm_fwd(x, g_pre, f"{tag}_norm")
    a, b, u = ffn_up(h, wg, wu, f"{tag}_up")
    y, xn = down_norm(u, wd, x, g_post, FFN_RES_WEIGHT, f"{tag}_down")
    return xn, (x, h, a, b, u, y)


def _ffn_bwd(dxn, saved, g_pre, g_post, wg, wu, wd, tag):
    x, h, a, b, u, y = saved
    dy, dg_post = norm_bwd(dxn, y, g_post, FFN_RES_WEIGHT, None, BF16, f"{tag}_bnorm_post")
    da, db = ffn_down_bwd(dy, wd, a, b, f"{tag}_bdown")
    dwd = mm_tn(u, dy[None], BF16, f"{tag}_dwd")
    dwg = mm_tn(h[None], da, BF16, f"{tag}_dwg")
    dwu = mm_tn(h[None], db, BF16, f"{tag}_dwu")
    dh = mm_nt_acc([(da, wg), (db, wu)], f"{tag}_dh")
    dx, dg_pre = norm_bwd(dh, x, g_pre, 1.0, dxn, F32, f"{tag}_bnorm_pre")
    return dx, dict(g_pre=dg_pre, g_post=dg_post, wg=dwg, wu=dwu, wd=dwd)


def _s5_layouts(p):
    are, aim = p["s5_a_re"].reshape(S5_P, 1), p["s5_a_im"].reshape(S5_P, 1)
    ldt = jnp.repeat(p["s5_log_dt"], S5_STATE).reshape(S5_P, 1)
    bre, bim = p["s5_b_re"].reshape(S5_P, S5_GROUP_CH), p["s5_b_im"].reshape(S5_P, S5_GROUP_CH)
    return are, aim, ldt, bre, bim


def _mix_fwd(x, p, w_all, w_out, glu_w, cos_f, sin_s, tag):
    s = x.shape[0]
    h = rmsnorm_fwd(x, _row(p["mix_norm_pre"]), f"{tag}_norm")
    z = mm_nn(h, w_all, f"{tag}_win")
    y_attn = attn_fwd(z, cos_f, sin_s, p["attn_sinks"], f"{tag}_attn")
    alog_b = jnp.broadcast_to(p["dn_a_log"][:, None], (DN_HEADS, 128))
    dtb_b = jnp.broadcast_to(p["dn_dt_bias"][:, None], (DN_HEADS, 128))
    qkv = dn_pre_fwd(z, p["dn_conv_w"], f"{tag}_dnpre")
    bb, gb = dn_gates_fwd(z, alog_b, dtb_b, f"{tag}_dngate")
    u, w, qd, kd, at, eg = dn_intra_fwd(qkv, gb, bb, f"{tag}_dnintra")
    o, states = dn_scan_fwd(u, w, qd, kd, at, eg, f"{tag}_dnscan")
    y_dn = dn_out_fwd(o, z, _row(p["dn_norm_w"]), f"{tag}_dnout")
    s5cols = _s5_layouts(p)
    abr, abi, bbr, bbi = s5_params_fwd(*s5cols, f"{tag}_s5par")
    tb = lambda t: jnp.transpose(t.reshape(S5_GROUPS, S5_STATE, S5_GROUP_CH), (0, 2, 1))
    b_blk = jnp.concatenate([_block_diag(tb(bbr)), _block_diag(tb(bbi))], axis=1).astype(BF16)
    tc = lambda t: jnp.transpose(t, (0, 2, 1))
    c_blk = jnp.concatenate([_block_diag(tc(p["s5_c_re"])), -_block_diag(tc(p["s5_c_im"]))], axis=0).astype(BF16)
    bu = mm_nn(z, b_blk, f"{tag}_s5bu", tn=1024, col0=Z_S5)
    xr, xi = s5_scan_fwd(bu, abr.reshape(1, S5_P), abi.reshape(1, S5_P), f"{tag}_s5scan")
    xs = jnp.concatenate([xr, xi], axis=1)
    ypre = mm_nn(xs, c_blk, f"{tag}_s5c", tm=256)
    y5, y_s5 = s5_out_fwd(ypre, z, _row(p["s5_d"]), glu_w, _row(p["s5_glu_b"]), f"{tag}_s5out")
    ycat = jnp.concatenate([y_attn, y_dn, y_s5], axis=1)
    mixed, xn = down_norm(ycat[None], w_out[None], x, _row(p["mix_norm_post"]), 1.0, f"{tag}_wout")
    saved = dict(x=x, h=h, z=z, qkv=qkv, bb=bb, gb=gb, dn=(u, w, qd, kd, at, eg), states=states, o=o, s5cols=s5cols,
                 abr=abr, abi=abi, b_blk=b_blk, c_blk=c_blk, xs=xs, y5=y5, ycat=ycat, mixed=mixed,
                 alog_b=alog_b, dtb_b=dtb_b)
    return xn, saved


def _mix_bwd(dxn, sv, p, w_all, w_out, glu_w, cos_f, sin_s, tag):
    z = sv["z"]
    g = {}
    dmixed, g["mix_norm_post"] = norm_bwd(dxn, sv["mixed"], _row(p["mix_norm_post"]), 1.0, None, BF16, f"{tag}_bnorm_post")
    g["w_out"] = mm_tn(sv["ycat"][None], dmixed[None], BF16, f"{tag}_dwout", tn=1024)[0]
    dycat = mm_nt_acc([(dmixed[None], w_out[None])], f"{tag}_dycat")
    dq, dkc, dkp, dvc, dvp, dsink = attn_bwd(z, cos_f, sin_s, p["attn_sinks"], dycat, f"{tag}_battn")
    g["attn_sinks"] = dsink[:, 0]
    do, dzg, dnw = dn_out_bwd(sv["o"], z, _row(p["dn_norm_w"]), dycat, f"{tag}_bdnout")
    g["dn_norm_w"] = dnw[0]
    cts = dn_scan_bwd(*sv["dn"], sv["states"], do, f"{tag}_bdnscan")
    dqn, dkn, dvn, dgb, dbb = dn_intra_bwd(sv["qkv"], sv["gb"], sv["bb"], cts, f"{tag}_bdnintra")
    dzs, dal, ddt = dn_gates_bwd(z, sv["alog_b"], sv["dtb_b"], dbb, dgb, f"{tag}_bdngate")
    g["dn_a_log"], g["dn_dt_bias"] = dal[:, 0], ddt[:, 0]
    ddn, g["dn_conv_w"] = dn_pre_bwd(z, p["dn_conv_w"], jnp.concatenate([dqn, dkn, dvn], axis=1), f"{tag}_bdnpre")
    dy5, dd, dglu, dglub = s5_out_bwd(sv["y5"], z, glu_w, _row(p["s5_glu_b"]), dycat, f"{tag}_bs5out")
    g["s5_d"], g["s5_glu_w"], g["s5_glu_b"] = dd[0], dglu, dglub[0]
    dxs = mm_nt_acc([(dy5[None], sv["c_blk"][None])], f"{tag}_bs5c", tn=1024)
    dc_blk = mm_tn(sv["xs"][None], dy5[None], F32, f"{tag}_ds5c", tk=256)[0]
    ex = lambda m: jnp.transpose(_block_diag_extract(m, S5_GROUPS), (0, 2, 1))
    g["s5_c_re"], g["s5_c_im"] = ex(dc_blk[:S5_P]), -ex(dc_blk[S5_P:])
    gr, gi, dar, dai = s5_scan_bwd(dxs, sv["xs"], sv["abr"].reshape(1, S5_P), sv["abi"].reshape(1, S5_P), f"{tag}_bs5scan")
    dbu = jnp.concatenate([gr, gi], axis=1)
    dus = mm_nt_acc([(dbu[None], sv["b_blk"][None])], f"{tag}_bs5bu")
    u_s5 = z[:, Z_S5:Z_SM]
    db_blk = mm_tn(u_s5[None], dbu[None], F32, f"{tag}_ds5b", tn=1024)[0]
    exb = lambda m: jnp.transpose(_block_diag_extract(m, S5_GROUPS), (0, 2, 1)).reshape(S5_P, S5_GROUP_CH)
    dcols = s5_params_bwd(*sv["s5cols"], (dar.reshape(S5_P, 1), dai.reshape(S5_P, 1), exb(db_blk[:, :S5_P]), exb(db_blk[:, S5_P:])),
                          f"{tag}_bs5par")
    g["s5_a_re"] = dcols[0].reshape(S5_GROUPS, S5_STATE)
    g["s5_a_im"] = dcols[1].reshape(S5_GROUPS, S5_STATE)
    g["s5_log_dt"] = jnp.sum(dcols[2].reshape(S5_GROUPS, S5_STATE), axis=1)
    g["s5_b_re"] = dcols[3].reshape(S5_GROUPS, S5_STATE, S5_GROUP_CH)
    g["s5_b_im"] = dcols[4].reshape(S5_GROUPS, S5_STATE, S5_GROUP_CH)
    dz = assemble_dz(dq, dkc, dkp, dvc, dvp, ddn, dzg, dus, dy5, _row(p["s5_d"]), dzs, f"{tag}_dz")
    g["w_all"] = mm_tn(sv["h"][None], dz[None], BF16, f"{tag}_dwin")[0]
    dh = mm_nt_acc([(dz[None], w_all[None])], f"{tag}_dh")
    dx, g["mix_norm_pre"] = norm_bwd(dh, sv["x"], _row(p["mix_norm_pre"]), 1.0, dxn, F32, f"{tag}_bnorm_pre")
    return dx, g


BIG = ("ff1_w_gate", "ff1_w_up", "ff1_w_down", "w_in", "s5_glu_w", "w_out", "ff2_w_gate", "ff2_w_up", "ff2_w_down")
SMALL = ("ff1_norm_pre", "ff1_norm_post", "mix_norm_pre", "attn_sinks", "dn_conv_w", "dn_a_log", "dn_dt_bias", "dn_norm_w",
         "s5_a_re", "s5_a_im", "s5_log_dt", "s5_b_re", "s5_b_im", "s5_c_re", "s5_c_im", "s5_d", "s5_glu_b",
         "mix_norm_post", "ff2_norm_pre", "ff2_norm_post")
WEIGHTS = ("ff1_norm_pre", "ff1_w_gate", "ff1_w_up", "ff1_w_down", "ff1_norm_post", "mix_norm_pre", "w_in", "attn_sinks",
           "dn_conv_w", "dn_a_log", "dn_dt_bias", "dn_norm_w", "s5_a_re", "s5_a_im", "s5_log_dt", "s5_b_re", "s5_b_im",
           "s5_c_re", "s5_c_im", "s5_d", "s5_glu_w", "s5_glu_b", "w_out", "mix_norm_post", "ff2_norm_pre", "ff2_w_gate",
           "ff2_w_up", "ff2_w_down", "ff2_norm_post")


def _pack(parts):
    flat = jnp.concatenate([a.reshape(-1) for a in parts])
    n = flat.shape[0]
    rows = -(-n // 1024) * 8
    return jnp.pad(flat, (0, rows * 128 - n)).reshape(rows, 128)


def _unpack(mat, shapes):
    flat = mat.reshape(-1)
    out, off = [], 0
    for shp in shapes:
        n = int(np.prod(shp))
        out.append(flat[off:off + n].reshape(shp))
        off += n
    return out


def local_step(x, target, smalls, weights_fn, grads_fn):
    depth = len(smalls)
    d_model = x.shape[1]
    cos_f, sin_s = _rope_tables(x.shape[0])
    xs = x
    saved, fulls = [], []
    for l in range(depth):
        fw, tok = weights_fn(l, xs)
        p = smalls[l]
        xs, s1 = _ffn_fwd(xs, _row(p["ff1_norm_pre"]) + tok[0:1, 0:1], _row(p["ff1_norm_post"]), fw["ff1_w_gate"], fw["ff1_w_up"], fw["ff1_w_down"], "ff1")
        xs, s2 = _mix_fwd(xs, p, fw["w_all"], fw["w_out"], fw["s5_glu_w"], cos_f, sin_s, "mix")
        xs, s3 = _ffn_fwd(xs, _row(p["ff2_norm_pre"]), _row(p["ff2_norm_post"]), fw["ff2_w_gate"], fw["ff2_w_up"], fw["ff2_w_down"], "ff2")
        saved.append((s1, s2, s3))
        fulls.append(fw)

    loss_vec, dx = loss_and_grad(xs, target, "loss")

    small_g = [None] * depth
    tok = jnp.zeros((8, 128), F32)
    for l in reversed(range(depth)):
        fw, p = fulls[l], smalls[l]
        s1, s2, s3 = saved[l]
        dx, g3 = _ffn_bwd(dx, s3, _row(p["ff2_norm_pre"]), _row(p["ff2_norm_post"]) + tok[0:1, 0:1], fw["ff2_w_gate"], fw["ff2_w_up"], fw["ff2_w_down"], "ff2")
        dx, g2 = _mix_bwd(dx, s2, p, fw["w_all"], fw["w_out"], fw["s5_glu_w"], cos_f, sin_s, "mix")
        dx, g1 = _ffn_bwd(dx, s1, _row(p["ff1_norm_pre"]), _row(p["ff1_norm_post"]), fw["ff1_w_gate"], fw["ff1_w_up"], fw["ff1_w_down"], "ff1")
        dwin = _zall_to_win(g2["w_all"])
        big_g = {
            "ff1_w_gate": g1["wg"], "ff1_w_up": g1["wu"], "ff1_w_down": g1["wd"],
            "w_in": jnp.transpose(dwin.reshape(d_model, N_DEV, IN_WIDTH // N_DEV), (1, 0, 2)),
            "s5_glu_w": g2["s5_glu_w"].astype(BF16).reshape(N_DEV, S5_WIDTH // N_DEV, S5_WIDTH),
            "w_out": g2["w_out"].reshape(N_DEV, MIX_WIDTH // N_DEV, d_model),
            "ff2_w_gate": g3["wg"], "ff2_w_up": g3["wu"], "ff2_w_down": g3["wd"],
        }
        sg = {n: g2[n] for n in SMALL if n in g2}
        sg.update(ff1_norm_pre=g1["g_pre"][0], ff1_norm_post=g1["g_post"][0], ff2_norm_pre=g3["g_pre"][0], ff2_norm_post=g3["g_post"][0],
                  mix_norm_pre=g2["mix_norm_pre"][0], mix_norm_post=g2["mix_norm_post"][0])
        small_g[l] = sg
        tok = grads_fn(l, big_g, dx)
    return loss_vec, dx, small_g


def kernel(x, ff1_norm_pre, ff1_w_gate, ff1_w_up, ff1_w_down, ff1_norm_post, mix_norm_pre, w_in, attn_sinks, dn_conv_w, dn_a_log, dn_dt_bias, dn_norm_w, s5_a_re, s5_a_im, s5_log_dt, s5_b_re, s5_b_im, s5_c_re, s5_c_im, s5_d, s5_glu_w, s5_glu_b, w_out, mix_norm_post, ff2_norm_pre, ff2_w_gate, ff2_w_up, ff2_w_down, ff2_norm_post, loss_target, m_ff1_norm_pre, m_ff1_w_gate, m_ff1_w_up, m_ff1_w_down, m_ff1_norm_post, m_mix_norm_pre, m_w_in, m_attn_sinks, m_dn_conv_w, m_dn_a_log, m_dn_dt_bias, m_dn_norm_w, m_s5_a_re, m_s5_a_im, m_s5_log_dt, m_s5_b_re, m_s5_b_im, m_s5_c_re, m_s5_c_im, m_s5_d, m_s5_glu_w, m_s5_glu_b, m_w_out, m_mix_norm_post, m_ff2_norm_pre, m_ff2_w_gate, m_ff2_w_up, m_ff2_w_down, m_ff2_norm_post, v_ff1_norm_pre, v_ff1_w_gate, v_ff1_w_up, v_ff1_w_down, v_ff1_norm_post, v_mix_norm_pre, v_w_in, v_attn_sinks, v_dn_conv_w, v_dn_a_log, v_dn_dt_bias, v_dn_norm_w, v_s5_a_re, v_s5_a_im, v_s5_log_dt, v_s5_b_re, v_s5_b_im, v_s5_c_re, v_s5_c_im, v_s5_d, v_s5_glu_w, v_s5_glu_b, v_w_out, v_mix_norm_post, v_ff2_norm_pre, v_ff2_w_gate, v_ff2_w_up, v_ff2_w_down, v_ff2_norm_post):
    args = dict(locals())
    W = {n: args[n] for n in WEIGHTS}
    M = {n: args["m_" + n] for n in WEIGHTS}
    V = {n: args["v_" + n] for n in WEIGHTS}
    depth = ff1_norm_pre.shape[0]
    d_model = x.shape[2]
    me = 4 * lax.axis_index("x") + 2 * lax.axis_index("y") + lax.axis_index("c")

    conv_sh = dn_conv_w.shape[2]
    conv_all = exchange([dn_conv_w.reshape(depth * DN_CONV, conv_sh)], False, "gather_conv")[0]
    conv_full = jnp.transpose(conv_all.reshape(N_DEV, depth, DN_CONV, conv_sh), (1, 2, 0, 3)).reshape(depth, DN_CONV, N_DEV * conv_sh)

    def small_params(l):
        p = {n: W[n][l] for n in SMALL}
        p["dn_conv_w"] = conv_full[l]
        return p

    gathers, scatters = {}, {}
    big_out = {n: [None] * depth for n in BIG}

    def start_gather(l, follows):
        shards = [cast_bf16(W[n], l, f"cast_{n}") for n in BIG]
        if follows is not None:
            zero = lax.bitcast_convert_type(lax.bitcast_convert_type(follows[0, 0:1, 0:1], jnp.int16) & 0, BF16)
            shards[0] = shards[0] + zero
        gathers[l], tok = exchange_start(shards, False, f"gather_start_l{l}")
        return tok

    def weights_fn(l, x_l):
        got = exchange_wait(gathers.pop(l), False, f"gather_wait_l{l}", x_l)
        tok = start_gather(l + 1, got[0]) if l + 1 < depth else jnp.zeros((8, 128), F32)
        full = dict(zip(BIG, got))
        win = jnp.transpose(full["w_in"], (1, 0, 2)).reshape(d_model, IN_WIDTH)
        full["w_all"] = _win_to_zall(win)
        full["w_out"] = full["w_out"].reshape(MIX_WIDTH, d_model)
        full["s5_glu_w"] = full["s5_glu_w"].reshape(S5_WIDTH, S5_WIDTH)
        return full, tok

    def finish_scatter(l, after):
        recv = dict(zip(BIG, exchange_wait(scatters.pop(l), True, f"scatter_wait_l{l}", after)))
        for n in BIG:
            big_out[n][l] = reduce_adamw(recv[n], W[n], M[n], V[n], l, f"adamw_{n}")

    def grads_fn(l, big_g, dx_l):
        scatters[l], tok = exchange_start([big_g[n] for n in BIG], True, f"scatter_start_l{l}")
        if l + 1 < depth:
            finish_scatter(l + 1, dx_l)
        return tok

    start_gather(0, None)
    smalls = [small_params(l) for l in range(depth)]
    loss_vec, dx, small_g = local_step(x[0], loss_target[0], smalls, weights_fn, grads_fn)
    finish_scatter(0, dx)
    loss = lax.psum(loss_vec[0, 0], ("x", "y", "c"))

    shapes = [(depth,) + ((DN_CONV, N_DEV * conv_sh) if n == "dn_conv_w" else tuple(W[n].shape[1:])) for n in SMALL]
    packed = _pack([jnp.stack([small_g[l][n] for l in range(depth)]) for n in SMALL])
    gathered = exchange([packed], False, "gather_small_grads")[0]

    def shard_of(n, full):
        return lax.dynamic_slice_in_dim(full, me * conv_sh, conv_sh, axis=2) if n == "dn_conv_w" else full

    conv_pad = lambda t: jnp.tile(t, (1, 1, N_DEV))
    wp = _pack([conv_pad(W[n]) if n == "dn_conv_w" else W[n] for n in SMALL])
    mp = _pack([conv_pad(M[n]) if n == "dn_conv_w" else M[n] for n in SMALL])
    vp = _pack([conv_pad(V[n]) if n == "dn_conv_w" else V[n] for n in SMALL])
    sm = reduce_adamw(gathered, wp[None], mp[None], vp[None], 0, "adamw_small")
    small_out = [dict(zip(SMALL, [shard_of(n, t) for n, t in zip(SMALL, _unpack(o, shapes))])) for o in sm]

    outs = []
    for kind in range(4):
        for n in WEIGHTS:
            if n in BIG:
                outs.append(jnp.stack([big_out[n][l][kind] for l in range(depth)]))
            else:
                outs.append(small_out[kind][n])
    return (loss, dx[None], *outs)
```

```python
import functools
import math

import jax
import jax.numpy as jnp
import numpy as np
from jax import lax
from jax.experimental import pallas as pl
from jax.experimental.pallas import tpu as pltpu

F32 = jnp.float32
BF16 = jnp.bfloat16

N_DEV = 8
DEPTH = 4
ATTN_HEADS = 8
ATTN_KV_HEADS = 2
HEAD_DIM = 128
WINDOW = 128
ROPE_THETA = 10000.0
DN_HEADS = 4
DN_HEAD_DIM = 128
DN_CONV = 4
DN_CHUNK = 64
S5_GROUPS = 32
S5_GROUP_CH = 16
S5_STATE = 64
ATTN_WIDTH = ATTN_HEADS * HEAD_DIM
ATTN_KV_WIDTH = ATTN_KV_HEADS * HEAD_DIM
DN_WIDTH = DN_HEADS * DN_HEAD_DIM
S5_WIDTH = S5_GROUPS * S5_GROUP_CH
S5_P = S5_GROUPS * S5_STATE
MIX_WIDTH = ATTN_WIDTH + DN_WIDTH + S5_WIDTH
IN_WIDTH = ATTN_WIDTH + 2 * ATTN_KV_WIDTH + 4 * DN_WIDTH + 2 * DN_HEADS + S5_WIDTH
Z_Q, Z_K, Z_V = 0, ATTN_WIDTH, ATTN_WIDTH + ATTN_KV_WIDTH
Z_DN = ATTN_WIDTH + 2 * ATTN_KV_WIDTH
Z_ZG = Z_DN + 3 * DN_WIDTH
Z_S5 = Z_ZG + DN_WIDTH
Z_SM = Z_S5 + S5_WIDTH
Z_ALL = Z_SM + 128
FFN_RES_WEIGHT = 0.5
NORM_EPS = 1e-6
ADAM_LR, ADAM_B1, ADAM_B2, ADAM_EPS, ADAM_WD, ADAM_STEP = 0.001, 0.9, 0.999, 1e-08, 0.01, 10

VMEM_LIMIT = 56 * 1024 * 1024
HI = lax.Precision.HIGHEST
NEG = -1e30

NN = (((1,), (0,)), ((), ()))
NT = (((1,), (1,)), ((), ()))
TN = (((0,), (0,)), ((), ()))


def _dot(a, b, dims=NN, prec=None):
    return lax.dot_general(a, b, dims, preferred_element_type=F32, precision=prec)


def _tile(n, pref, mult=8):
    if n <= pref:
        return n
    t = (pref // mult) * mult
    while t > mult and n % t:
        t -= mult
    assert n % t == 0, (n, pref)
    return t


def _call(body, name, grid, in_specs, out_specs, out_shape, scratch=(), sem=None):
    if sem is None:
        sem = ("arbitrary",) * len(grid)
    return pl.pallas_call(
        body, name=name, grid=grid, in_specs=in_specs, out_specs=out_specs, out_shape=out_shape,
        scratch_shapes=list(scratch),
        compiler_params=pltpu.CompilerParams(dimension_semantics=sem, vmem_limit_bytes=VMEM_LIMIT))


def _sds(shape, dtype):
    return jax.ShapeDtypeStruct(tuple(shape), dtype)


def _sigmoid(x):
    return 1.0 / (1.0 + jnp.exp(-x))


def _silu_and_grad(a):
    sg = _sigmoid(a)
    return a * sg, sg * (1.0 + a * (1.0 - sg))


def _softplus(x):
    return jnp.maximum(x, 0.0) + jnp.log(1.0 + jnp.exp(-jnp.abs(x)))


def _rms(x, g):
    r = lax.rsqrt(jnp.mean(x * x, axis=-1, keepdims=True) + NORM_EPS)
    return x * r * g


def _rms_bwd(dout, y, g):
    r = lax.rsqrt(jnp.mean(y * y, axis=-1, keepdims=True) + NORM_EPS)
    n = y * r
    dn = dout * g
    dy = r * (dn - n * jnp.mean(dn * n, axis=-1, keepdims=True))
    return dy, jnp.sum(dout * n, axis=0, keepdims=True)


def _rope(x, cos_f, sin_s):
    return x * cos_f + pltpu.roll(x, HEAD_DIM // 2, 1) * sin_s


def _rope_bwd(d, cos_f, sin_s):
    return d * cos_f + pltpu.roll(d * sin_s, HEAD_DIM // 2, 1)


def rmsnorm_fwd(x, g, name):
    s, d = x.shape
    tm = _tile(s, 512)

    def body(x_ref, g_ref, o_ref):
        o_ref[...] = _rms(x_ref[...], g_ref[...]).astype(BF16)

    return _call(body, name, (s // tm,),
                 [pl.BlockSpec((tm, d), lambda i: (i, 0)), pl.BlockSpec((1, d), lambda i: (0, 0))],
                 pl.BlockSpec((tm, d), lambda i: (i, 0)), _sds((s, d), BF16), sem=("parallel",))(x, g)


def norm_bwd(dout, y, g, scale, resid, out_dtype, name):
    s, d = y.shape
    tm = _tile(s, 256)
    has_res = resid is not None

    def body(*refs):
        if has_res:
            do_ref, y_ref, g_ref, r_ref, dy_ref, dg_ref = refs
        else:
            do_ref, y_ref, g_ref, dy_ref, dg_ref = refs
        dy, dg = _rms_bwd(do_ref[...] * scale, y_ref[...], g_ref[...])
        if has_res:
            dy = dy + r_ref[...]
        dy_ref[...] = dy.astype(out_dtype)

        @pl.when(pl.program_id(0) == 0)
        def _():
            dg_ref[...] = jnp.zeros_like(dg_ref)

        dg_ref[...] += dg

    row = pl.BlockSpec((tm, d), lambda i: (i, 0))
    vec = pl.BlockSpec((1, d), lambda i: (0, 0))
    ins = [dout, y, g] + ([resid] if has_res else [])
    return _call(body, name, (s // tm,), [row, row, vec] + ([row] if has_res else []),
                 [row, vec], [_sds((s, d), out_dtype), _sds((1, d), F32)])(*ins)


def ffn_up(h, wg, wu, name, dep=None):
    s, d = h.shape
    nj, _, fs = wg.shape
    tm = _tile(s, 512)

    def body(h_ref, wg_ref, wu_ref, *rest):
        a_ref, b_ref, u_ref = rest[-3:]
        hh = h_ref[...]
        a = _dot(hh, wg_ref[...])
        b = _dot(hh, wu_ref[...])
        a_ref[...] = a.astype(BF16)
        b_ref[...] = b.astype(BF16)
        u_ref[...] = (a * _sigmoid(a) * b).astype(BF16)

    wspec = pl.BlockSpec((None, d, fs), lambda j, i: (j, 0, 0))
    ospec = pl.BlockSpec((None, tm, fs), lambda j, i: (j, i, 0))
    osd = _sds((nj, s, fs), BF16)
    ins, specs = [h, wg, wu], [pl.BlockSpec((tm, d), lambda j, i: (i, 0)), wspec, wspec]
    if dep is not None:
        ins.append(dep)
        specs.append(pl.BlockSpec((8, 128), lambda j, i: (0, 0)))
    return _call(body, name, (nj, s // tm), specs, [ospec, ospec, ospec], [osd, osd, osd], sem=("parallel", "parallel"))(*ins)


def down_norm(u3, w3, x, g, scale, name):
    nj, s, k = u3.shape
    d = w3.shape[2]
    tm = _tile(s, 512)
    jb = 2 if nj % 2 == 0 else 1
    nsteps = nj // jb

    def body(u_ref, w_ref, x_ref, g_ref, y_ref, xn_ref):
        j = pl.program_id(1)
        t = _dot(u_ref[0], w_ref[0])
        for q in range(1, jb):
            t = t + _dot(u_ref[q], w_ref[q])

        @pl.when(j == 0)
        def _():
            y_ref[...] = t

        @pl.when(j > 0)
        def _():
            y_ref[...] += t

        @pl.when(j == nsteps - 1)
        def _():
            xn_ref[...] = x_ref[...] + scale * _rms(y_ref[...], g_ref[...])

    row = pl.BlockSpec((tm, d), lambda i, j: (i, 0))
    return _call(body, name, (s // tm, nsteps),
                 [pl.BlockSpec((jb, tm, k), lambda i, j: (j, i, 0)), pl.BlockSpec((jb, k, d), lambda i, j: (j, 0, 0)),
                  row, pl.BlockSpec((1, d), lambda i, j: (0, 0))],
                 [row, row], [_sds((s, d), F32), _sds((s, d), F32)], sem=("parallel", "arbitrary"))(u3, w3, x, g)


def ffn_down_bwd(dy, wd, a, b, name):
    nj, fs, d = wd.shape
    s = dy.shape[0]
    tm = _tile(s, 512)

    def body(dy_ref, w_ref, a_ref, b_ref, da_ref, db_ref):
        du = _dot(dy_ref[...], w_ref[...], NT)
        aa = a_ref[...].astype(F32)
        bb = b_ref[...].astype(F32)
        sl, dsl = _silu_and_grad(aa)
        da_ref[...] = (du * bb * dsl).astype(BF16)
        db_ref[...] = (du * sl).astype(BF16)

    hspec = pl.BlockSpec((None, tm, fs), lambda j, i: (j, i, 0))
    osd = _sds((nj, s, fs), BF16)
    return _call(body, name, (nj, s // tm),
                 [pl.BlockSpec((tm, d), lambda j, i: (i, 0)), pl.BlockSpec((None, fs, d), lambda j, i: (j, 0, 0)), hspec, hspec],
                 [hspec, hspec], [osd, osd], sem=("parallel", "parallel"))(dy, wd, a, b)


def mm_tn(a3, b3, out_dtype, name, tmm=2048, tn=1408, tk=1024):
    ja, s, m = a3.shape
    jb, _, n = b3.shape
    nj = max(ja, jb)
    tmm, tn, tk = _tile(m, tmm, 128), _tile(n, tn, 128), _tile(s, tk)
    nk = s // tk

    def body(a_ref, b_ref, o_ref, acc_ref):
        k = pl.program_id(3)

        @pl.when(k == 0)
        def _():
            acc_ref[...] = jnp.zeros_like(acc_ref)

        acc_ref[...] += _dot(a_ref[...].astype(BF16), b_ref[...].astype(BF16), TN)

        @pl.when(k == nk - 1)
        def _():
            o_ref[...] = acc_ref[...].astype(out_dtype)

    aj = (lambda j: j) if ja > 1 else (lambda j: 0)
    bj = (lambda j: j) if jb > 1 else (lambda j: 0)
    return _call(body, name, (nj, m // tmm, n // tn, nk),
                 [pl.BlockSpec((None, tk, tmm), lambda j, im, jn, k: (aj(j), k, im)),
                  pl.BlockSpec((None, tk, tn), lambda j, im, jn, k: (bj(j), k, jn))],
                 pl.BlockSpec((None, tmm, tn), lambda j, im, jn, k: (j, im, jn)), _sds((nj, m, n), out_dtype),
                 scratch=[pltpu.VMEM((tmm, tn), F32)],
                 sem=("parallel", "parallel", "parallel", "arbitrary"))(a3, b3)


def mm_nt_acc(pairs, name, tm=512, tn=512, dep=None):
    nj, s, _ = pairs[0][0].shape
    n = pairs[0][1].shape[1]
    tm, tn = _tile(s, tm), _tile(n, tn, 128)
    npair = len(pairs)

    def body(*refs):
        o_ref = refs[-1]
        j = pl.program_id(2)
        t = None
        for p in range(npair):
            c = _dot(refs[2 * p][...].astype(BF16), refs[2 * p + 1][...], NT)
            t = c if t is None else t + c
        if nj == 1:
            o_ref[...] = t
        else:
            @pl.when(j == 0)
            def _():
                o_ref[...] = t

            @pl.when(j > 0)
            def _():
                o_ref[...] += t

    ins, specs = [], []
    for a3, w3 in pairs:
        k = a3.shape[2]
        ins += [a3, w3]
        specs += [pl.BlockSpec((None, tm, k), lambda i, jn, j: (j, i, 0)),
                  pl.BlockSpec((None, tn, k), lambda i, jn, j: (j, jn, 0))]
    if dep is not None:
        ins.append(dep)
        specs.append(pl.BlockSpec((8, 128), lambda i, jn, j: (0, 0)))
    return _call(body, name, (s // tm, n // tn, nj), specs,
                 pl.BlockSpec((tm, tn), lambda i, jn, j: (i, jn)), _sds((s, n), F32),
                 sem=("parallel", "parallel", "arbitrary"))(*ins)


def mm_nn(a, w, name, tm=512, tn=1408, col0=0, kdim=None):
    s = a.shape[0]
    k, n = w.shape
    assert col0 % k == 0
    tm, tn = _tile(s, tm), _tile(n, tn, 128)
    cb = col0 // k

    def body(a_ref, w_ref, o_ref):
        o_ref[...] = _dot(a_ref[...].astype(BF16), w_ref[...])

    return _call(body, name, (n // tn, s // tm),
                 [pl.BlockSpec((tm, k), lambda jn, i: (i, cb)), pl.BlockSpec((k, tn), lambda jn, i: (0, jn))],
                 pl.BlockSpec((tm, tn), lambda jn, i: (i, jn)), _sds((s, n), F32), sem=("parallel", "parallel"))(a, w)


def loss_and_grad(xl, target, name):
    s, d = xl.shape
    tm = _tile(s, 512)

    def body(x_ref, t_ref, l_ref, dx_ref):
        e = x_ref[...] - t_ref[...]
        dx_ref[...] = e * (1.0 / d)

        @pl.when(pl.program_id(0) == 0)
        def _():
            l_ref[...] = jnp.zeros_like(l_ref)

        part = jnp.sum(jnp.sum(e * e, axis=-1, keepdims=True), axis=0, keepdims=True) * (0.5 / d)
        l_ref[...] += jnp.broadcast_to(part, l_ref.shape)

    row = pl.BlockSpec((tm, d), lambda i: (i, 0))
    return _call(body, name, (s // tm,), [row, row], [pl.BlockSpec((1, 128), lambda i: (0, 0)), row],
                 [_sds((1, 128), F32), _sds((s, d), F32)])(xl, target)


def cast_bf16(w4, layer, name):
    _, r, c = w4.shape
    tr = _tile(r, 512)

    def body(w_ref, o_ref):
        o_ref[...] = w_ref[...].astype(BF16)

    return _call(body, name, (r // tr,), [pl.BlockSpec((None, tr, c), lambda i: (layer, i, 0))],
                 pl.BlockSpec((tr, c), lambda i: (i, 0)), _sds((r, c), BF16), sem=("parallel",))(w4)


def _attn_specs(nb):
    w = WINDOW
    prev = lambda i: jnp.maximum(i - 1, 0)
    q = pl.BlockSpec((w, ATTN_WIDTH), lambda i: (i, 0))
    kc = pl.BlockSpec((w, ATTN_KV_WIDTH), lambda i: (i, Z_K // ATTN_KV_WIDTH))
    kp = pl.BlockSpec((w, ATTN_KV_WIDTH), lambda i: (prev(i), Z_K // ATTN_KV_WIDTH))
    vc = pl.BlockSpec((w, ATTN_KV_WIDTH), lambda i: (i, Z_V // ATTN_KV_WIDTH))
    vp = pl.BlockSpec((w, ATTN_KV_WIDTH), lambda i: (prev(i), Z_V // ATTN_KV_WIDTH))
    tc = pl.BlockSpec((w, HEAD_DIM), lambda i: (i, 0))
    tp = pl.BlockSpec((w, HEAD_DIM), lambda i: (prev(i), 0))
    sink = pl.BlockSpec(memory_space=pltpu.SMEM)
    return [q, kc, kp, vc, vp, tc, tc, tp, tp, sink]


def _attn_mask(i):
    w = WINDOW
    qi = lax.broadcasted_iota(jnp.int32, (w, 2 * w), 0) + w
    kj = lax.broadcasted_iota(jnp.int32, (w, 2 * w), 1)
    rel = qi - kj
    band = (rel >= 0) & (rel < w)
    return band & jnp.logical_not((i == 0) & (kj < w))


def _attn_probs(q, kk, sink, mask):
    s = _dot(q, kk, NT) * (HEAD_DIM ** -0.5)
    s = jnp.where(mask, s, NEG)
    m = jnp.maximum(jnp.max(s, axis=-1, keepdims=True), sink)
    p = jnp.exp(s - m)
    es = jnp.exp(sink - m)
    inv = 1.0 / (jnp.sum(p, axis=-1, keepdims=True) + es)
    return p * inv, es * inv


def attn_fwd(z, cos_f, sin_s, sinks, name):
    s = z.shape[0]
    nb = s // WINDOW
    hd = HEAD_DIM
    grp = ATTN_HEADS // ATTN_KV_HEADS

    def body(q_ref, kc_ref, kp_ref, vc_ref, vp_ref, cc_ref, sc_ref, cp_ref, sp_ref, sink_ref, o_ref):
        i = pl.program_id(0)
        mask = _attn_mask(i)
        cc, sc, cp, sp = cc_ref[...], sc_ref[...], cp_ref[...], sp_ref[...]
        for kv in range(ATTN_KV_HEADS):
            ksl = slice(kv * hd, (kv + 1) * hd)
            kk = jnp.concatenate([_rope(kp_ref[:, ksl], cp, sp), _rope(kc_ref[:, ksl], cc, sc)], axis=0).astype(BF16)
            vv = jnp.concatenate([vp_ref[:, ksl], vc_ref[:, ksl]], axis=0).astype(BF16)
            for g in range(grp):
                h = kv * grp + g
                hsl = slice(h * hd, (h + 1) * hd)
                q = _rope(q_ref[:, hsl], cc, sc).astype(BF16)
                pn, _ = _attn_probs(q, kk, sink_ref[h], mask)
                o_ref[:, hsl] = _dot(pn.astype(BF16), vv).astype(BF16)

    return _call(body, name, (nb,), _attn_specs(nb), pl.BlockSpec((WINDOW, ATTN_WIDTH), lambda i: (i, 0)),
                 _sds((s, ATTN_WIDTH), BF16), sem=("parallel",))(z, z, z, z, z, cos_f, sin_s, cos_f, sin_s, sinks)


def attn_bwd(z, cos_f, sin_s, sinks, dy, name):
    s = z.shape[0]
    nb = s // WINDOW
    hd = HEAD_DIM
    grp = ATTN_HEADS // ATTN_KV_HEADS
    scale = HEAD_DIM ** -0.5

    def body(q_ref, kc_ref, kp_ref, vc_ref, vp_ref, cc_ref, sc_ref, cp_ref, sp_ref, sink_ref, dy_ref,
             dq_ref, dkc_ref, dkp_ref, dvc_ref, dvp_ref, ds_ref):
        i = pl.program_id(0)
        mask = _attn_mask(i)
        cc, sc, cp, sp = cc_ref[...], sc_ref[...], cp_ref[...], sp_ref[...]

        @pl.when(i == 0)
        def _():
            ds_ref[...] = jnp.zeros_like(ds_ref)

        for kv in range(ATTN_KV_HEADS):
            ksl = slice(kv * hd, (kv + 1) * hd)
            kk = jnp.concatenate([_rope(kp_ref[:, ksl], cp, sp), _rope(kc_ref[:, ksl], cc, sc)], axis=0).astype(BF16)
            vv = jnp.concatenate([vp_ref[:, ksl], vc_ref[:, ksl]], axis=0).astype(BF16)
            dkk = jnp.zeros((2 * WINDOW, hd), F32)
            dvv = jnp.zeros((2 * WINDOW, hd), F32)
            for g in range(grp):
                h = kv * grp + g
                hsl = slice(h * hd, (h + 1) * hd)
                q = _rope(q_ref[:, hsl], cc, sc).astype(BF16)
                pn, psink = _attn_probs(q, kk, sink_ref[h], mask)
                do = dy_ref[:, hsl].astype(BF16)
                dpn = _dot(do, vv, NT)
                dvv = dvv + _dot(pn.astype(BF16), do, TN)
                tot = jnp.sum(pn * dpn, axis=-1, keepdims=True)
                dsc = (pn * (dpn - tot) * scale).astype(BF16)
                dq_ref[:, hsl] = _rope_bwd(_dot(dsc, kk), cc, sc)
                dkk = dkk + _dot(dsc, q, TN)
                dsink = jnp.sum(-psink * tot, axis=0, keepdims=True)
                ds_ref[h:h + 1, :] += jnp.broadcast_to(dsink, (1, 128))
            dkp_ref[:, ksl] = _rope_bwd(dkk[:WINDOW], cp, sp)
            dkc_ref[:, ksl] = _rope_bwd(dkk[WINDOW:], cc, sc)
            dvp_ref[:, ksl] = dvv[:WINDOW]
            dvc_ref[:, ksl] = dvv[WINDOW:]

    kvo = pl.BlockSpec((WINDOW, ATTN_KV_WIDTH), lambda i: (i, 0))
    kvs = _sds((s, ATTN_KV_WIDTH), F32)
    return _call(body, name, (nb,), _attn_specs(nb) + [pl.BlockSpec((WINDOW, ATTN_WIDTH), lambda i: (i, 0))],
                 [pl.BlockSpec((WINDOW, ATTN_WIDTH), lambda i: (i, 0)), kvo, kvo, kvo, kvo,
                  pl.BlockSpec((ATTN_HEADS, 128), lambda i: (0, 0))],
                 [_sds((s, ATTN_WIDTH), F32), kvs, kvs, kvs, kvs, _sds((ATTN_HEADS, 128), F32)])(
        z, z, z, z, z, cos_f, sin_s, cos_f, sin_s, sinks, dy)


def _shift_down(x, d, row):
    return jnp.where(row >= d, pltpu.roll(x, d, 0), 0.0)


def _shift_up(x, d, row, n):
    return jnp.where(row < n - d, pltpu.roll(x, n - d, 0), 0.0)


def _conv_taps(u, w_ref, row):
    c = w_ref[DN_CONV - 1:DN_CONV, :] * u
    for k in range(DN_CONV - 1):
        c = c + w_ref[k:k + 1, :] * _shift_down(u, DN_CONV - 1 - k, row)
    return c


def dn_pre_fwd(z, conv_w, name):
    s = z.shape[0]
    nblk = 3 * DN_WIDTH // 128
    nqk = 2 * DN_WIDTH // 128

    def body(u_ref, w_ref, o_ref):
        row = lax.broadcasted_iota(jnp.int32, (s, 128), 0)
        c = _conv_taps(u_ref[...], w_ref, row)
        sl = c * _sigmoid(c)
        j = pl.program_id(0)

        @pl.when(j < nqk)
        def _():
            o_ref[...] = sl * lax.rsqrt(jnp.sum(sl * sl, axis=-1, keepdims=True) + NORM_EPS)

        @pl.when(j >= nqk)
        def _():
            o_ref[...] = sl

    return _call(body, name, (nblk,),
                 [pl.BlockSpec((s, 128), lambda j: (0, Z_DN // 128 + j)), pl.BlockSpec((DN_CONV, 128), lambda j: (0, j))],
                 pl.BlockSpec((s, 128), lambda j: (0, j)), _sds((s, 3 * DN_WIDTH), F32), sem=("parallel",))(z, conv_w)


def dn_pre_bwd(z, conv_w, dout, name):
    s = z.shape[0]
    nblk = 3 * DN_WIDTH // 128
    nqk = 2 * DN_WIDTH // 128

    def body(u_ref, w_ref, do_ref, du_ref, dw_ref, ds_ref):
        row = lax.broadcasted_iota(jnp.int32, (s, 128), 0)
        u = u_ref[...]
        c = _conv_taps(u, w_ref, row)
        sl, dsl = _silu_and_grad(c)
        do = do_ref[...]
        j = pl.program_id(0)

        @pl.when(j < nqk)
        def _():
            r = lax.rsqrt(jnp.sum(sl * sl, axis=-1, keepdims=True) + NORM_EPS)
            ds_ref[...] = r * do - sl * (r * r * r) * jnp.sum(do * sl, axis=-1, keepdims=True)

        @pl.when(j >= nqk)
        def _():
            ds_ref[...] = do

        dc = ds_ref[...] * dsl
        du = w_ref[DN_CONV - 1:DN_CONV, :] * dc
        dw_ref[DN_CONV - 1:DN_CONV, :] = jnp.sum(dc * u, axis=0, keepdims=True)
        for k in range(DN_CONV - 1):
            d = DN_CONV - 1 - k
            du = du + w_ref[k:k + 1, :] * _shift_up(dc, d, row, s)
            dw_ref[k:k + 1, :] = jnp.sum(dc * _shift_down(u, d, row), axis=0, keepdims=True)
        du_ref[...] = du

    blk = pl.BlockSpec((s, 128), lambda j: (0, j))
    wsp = pl.BlockSpec((DN_CONV, 128), lambda j: (0, j))
    return _call(body, name, (nblk,), [pl.BlockSpec((s, 128), lambda j: (0, Z_DN // 128 + j)), wsp, blk],
                 [blk, wsp], [_sds((s, 3 * DN_WIDTH), F32), _sds((DN_CONV, 3 * DN_WIDTH), F32)],
                 scratch=[pltpu.VMEM((s, 128), F32)], sem=("parallel",))(z, conv_w, dout)


def _lane_col(x, lane, idx):
    return jnp.sum(jnp.where(lane == idx, x, 0.0), axis=-1, keepdims=True)


def dn_gates_fwd(z, alog_b, dtb_b, name):
    s = z.shape[0]
    tm = _tile(s, 512)

    def body(zs_ref, al_ref, dt_ref, beta_ref, g_ref):
        zs = zs_ref[...]
        lane = lax.broadcasted_iota(jnp.int32, zs.shape, 1)
        for h in range(DN_HEADS):
            b_raw = _lane_col(zs, lane, h)
            a_raw = _lane_col(zs, lane, DN_HEADS + h)
            beta_ref[h] = jnp.broadcast_to(_sigmoid(b_raw), (tm, 128))
            g_ref[h] = -jnp.exp(al_ref[h:h + 1, :]) * _softplus(a_raw + dt_ref[h:h + 1, :])

    osp = pl.BlockSpec((DN_HEADS, tm, 128), lambda i: (0, i, 0))
    psp = pl.BlockSpec((DN_HEADS, 128), lambda i: (0, 0))
    osd = _sds((DN_HEADS, s, 128), F32)
    return _call(body, name, (s // tm,), [pl.BlockSpec((tm, 128), lambda i: (i, Z_SM // 128)), psp, psp],
                 [osp, osp], [osd, osd], sem=("parallel",))(z, alog_b, dtb_b)


def dn_gates_bwd(z, alog_b, dtb_b, dbeta, dg, name):
    s = z.shape[0]
    tm = _tile(s, 512)

    def body(zs_ref, al_ref, dt_ref, dbeta_ref, dg_ref, dz_ref, dal_ref, ddt_ref):
        @pl.when(pl.program_id(0) == 0)
        def _():
            dal_ref[...] = jnp.zeros_like(dal_ref)
            ddt_ref[...] = jnp.zeros_like(ddt_ref)

        zs = zs_ref[...]
        lane = lax.broadcasted_iota(jnp.int32, zs.shape, 1)
        dz = jnp.zeros_like(zs)
        for h in range(DN_HEADS):
            b_raw = _lane_col(zs, lane, h)
            a_raw = _lane_col(zs, lane, DN_HEADS + h)
            dbe = jnp.sum(dbeta_ref[h], axis=-1, keepdims=True)
            dgg = jnp.sum(dg_ref[h], axis=-1, keepdims=True)
            beta = _sigmoid(b_raw)
            ea = jnp.exp(al_ref[h:h + 1, :])
            pre = a_raw + dt_ref[h:h + 1, :]
            da_raw = dgg * (-ea) * _sigmoid(pre)
            dz = dz + jnp.where(lane == h, dbe * beta * (1.0 - beta), 0.0) + jnp.where(lane == DN_HEADS + h, da_raw, 0.0)
            ddt_ref[h:h + 1, :] += jnp.sum(da_raw, axis=0, keepdims=True)
            dal_ref[h:h + 1, :] += jnp.sum(dgg * (-ea) * _softplus(pre), axis=0, keepdims=True)
        dz_ref[...] = dz

    hsp = pl.BlockSpec((DN_HEADS, tm, 128), lambda i: (0, i, 0))
    psp = pl.BlockSpec((DN_HEADS, 128), lambda i: (0, 0))
    return _call(body, name, (s // tm,), [pl.BlockSpec((tm, 128), lambda i: (i, Z_SM // 128)), psp, psp, hsp, hsp],
                 [pl.BlockSpec((tm, 128), lambda i: (i, 0)), psp, psp],
                 [_sds((s, 128), F32), _sds((DN_HEADS, 128), F32), _sds((DN_HEADS, 128), F32)])(z, alog_b, dtb_b, dbeta, dg)


def _dn_intra(q, k, v, gb, bb):
    c = DN_CHUNK
    ri = lax.broadcasted_iota(jnp.int32, (c, c), 0)
    ci = lax.broadcasted_iota(jnp.int32, (c, c), 1)
    causal = ri >= ci
    strict = ri > ci
    gc = _dot(causal.astype(F32), gb, NN, HI)
    e0 = (lax.broadcasted_iota(jnp.int32, (c, 128), 1) == 0).astype(F32)
    grow = _dot(e0, gc, NT, HI)
    decay = jnp.where(causal, jnp.exp(jnp.where(causal, gc[:, :c] - grow, 0.0)), 0.0)
    qs = q * (DN_HEAD_DIM ** -0.5)
    kb = k * bb
    lower = jnp.where(strict, _dot(kb, k, NT) * decay, 0.0)
    t = _unit_lower_inverse(lower)
    eg = jnp.exp(gc)
    u = _dot(t, v * bb)
    w = _dot(t, kb * eg)
    attn = jnp.where(causal, _dot(qs, k, NT) * decay, 0.0)
    glast = _dot((ci == c - 1).astype(F32), gc, NN, HI)
    return u, w, qs * eg, k * jnp.exp(glast - gc), attn, jnp.exp(glast[:8])


@jax.custom_vjp
def _unit_lower_inverse(lower):
    n = lower.shape[0]
    eye = (lax.broadcasted_iota(jnp.int32, (n, n), 0) == lax.broadcasted_iota(jnp.int32, (n, n), 1)).astype(F32)
    t = eye - lower
    p = lower
    for _ in range(5):
        p = _dot(p, p, NN, HI)
        t = t + _dot(t, p, NN, HI)
    return t


def _unit_lower_inverse_fwd(lower):
    t = _unit_lower_inverse(lower)
    return t, t


def _unit_lower_inverse_bwd(t, dt):
    return (-_dot(_dot(t, dt, TN, HI), t, NT, HI),)


_unit_lower_inverse.defvjp(_unit_lower_inverse_fwd, _unit_lower_inverse_bwd)


def _dn_step(st, qd, kd, u, w, attn, egl):
    v_new = u - _dot(w, st)
    o = _dot(qd, st) + _dot(attn, v_new)
    st_new = st * egl[0:1, :] + _dot(kd, v_new, TN)
    return o, st_new


def _dn_chunk_specs(m=1):
    c = DN_CHUNK
    wide = pl.BlockSpec((m * c, DN_WIDTH), lambda i: (i, 0))
    att = pl.BlockSpec((m * c, DN_HEADS * c), lambda i: (i, 0))
    egl = pl.BlockSpec((m * 8, DN_WIDTH), lambda i: (i, 0))
    return wide, att, egl


def _dn_intra_chunks(nc):
    return 2 if nc % 2 == 0 else 1


def dn_intra_fwd(qkv, gb, bb, name):
    s = qkv.shape[0]
    c, hd = DN_CHUNK, DN_HEAD_DIM
    nc = s // c
    m = _dn_intra_chunks(nc)

    def body(q_ref, k_ref, v_ref, g_ref, b_ref, u_ref, w_ref, qd_ref, kd_ref, at_ref, eg_ref):
        for t in range(m):
            rs = slice(t * c, (t + 1) * c)
            for h in range(DN_HEADS):
                hs = slice(h * hd, (h + 1) * hd)
                u, w, qd, kd, at, eg = _dn_intra(q_ref[rs, hs], k_ref[rs, hs], v_ref[rs, hs], g_ref[h, rs], b_ref[h, rs])
                u_ref[rs, hs], w_ref[rs, hs], qd_ref[rs, hs], kd_ref[rs, hs] = u, w, qd, kd
                at_ref[rs, h * c:(h + 1) * c] = at
                eg_ref[t * 8:(t + 1) * 8, hs] = eg

    wide, att, egl = _dn_chunk_specs(m)
    hsp = pl.BlockSpec((DN_HEADS, m * c, 128), lambda i: (0, i, 0))
    wsd = _sds((s, DN_WIDTH), F32)
    return _call(body, name, (nc // m,),
                 [pl.BlockSpec((m * c, DN_WIDTH), lambda i: (i, 0)), pl.BlockSpec((m * c, DN_WIDTH), lambda i: (i, 1)),
                  pl.BlockSpec((m * c, DN_WIDTH), lambda i: (i, 2)), hsp, hsp],
                 [wide, wide, wide, wide, att, egl],
                 [wsd, wsd, wsd, wsd, _sds((s, DN_HEADS * c), F32), _sds((nc * 8, DN_WIDTH), F32)],
                 sem=("parallel",))(qkv, qkv, qkv, gb, bb)


def dn_intra_bwd(qkv, gb, bb, cts, name):
    s = qkv.shape[0]
    c, hd = DN_CHUNK, DN_HEAD_DIM
    nc = s // c
    m = _dn_intra_chunks(nc)

    def body(q_ref, k_ref, v_ref, g_ref, b_ref, du_ref, dw_ref, dqd_ref, dkd_ref, dat_ref, deg_ref,
             dq_ref, dk_ref, dv_ref, dg_ref, db_ref):
        for t in range(m):
            rs = slice(t * c, (t + 1) * c)
            for h in range(DN_HEADS):
                hs = slice(h * hd, (h + 1) * hd)
                _, vjp = jax.vjp(_dn_intra, q_ref[rs, hs], k_ref[rs, hs], v_ref[rs, hs], g_ref[h, rs], b_ref[h, rs])
                dq, dk, dv, dg, db = vjp((du_ref[rs, hs], dw_ref[rs, hs], dqd_ref[rs, hs], dkd_ref[rs, hs],
                                          dat_ref[rs, h * c:(h + 1) * c], deg_ref[t * 8:(t + 1) * 8, hs]))
                dq_ref[rs, hs], dk_ref[rs, hs], dv_ref[rs, hs] = dq, dk, dv
                dg_ref[h, rs] = dg
                db_ref[h, rs] = db

    wide, att, egl = _dn_chunk_specs(m)
    hsp = pl.BlockSpec((DN_HEADS, m * c, 128), lambda i: (0, i, 0))
    hsd = _sds((DN_HEADS, s, 128), F32)
    wsd = _sds((s, DN_WIDTH), F32)
    return _call(body, name, (nc // m,),
                 [pl.BlockSpec((m * c, DN_WIDTH), lambda i: (i, 0)), pl.BlockSpec((m * c, DN_WIDTH), lambda i: (i, 1)),
                  pl.BlockSpec((m * c, DN_WIDTH), lambda i: (i, 2)), hsp, hsp, wide, wide, wide, wide, att, egl],
                 [wide, wide, wide, hsp, hsp], [wsd, wsd, wsd, hsd, hsd], sem=("parallel",))(qkv, qkv, qkv, gb, bb, *cts)


def dn_scan_fwd(u, w, qd, kd, at, eg, name):
    s = u.shape[0]
    c, hd = DN_CHUNK, DN_HEAD_DIM
    nc = s // c

    def body(u_ref, w_ref, qd_ref, kd_ref, at_ref, eg_ref, o_ref, st_ref, state):
        @pl.when(pl.program_id(0) == 0)
        def _():
            state[...] = jnp.zeros_like(state)

        for h in range(DN_HEADS):
            hs = slice(h * hd, (h + 1) * hd)
            st = state[h]
            st_ref[h] = st
            o, st_new = _dn_step(st, qd_ref[:, hs], kd_ref[:, hs], u_ref[:, hs], w_ref[:, hs],
                                 at_ref[:, h * c:(h + 1) * c], eg_ref[:, hs])
            o_ref[:, hs] = o
            state[h] = st_new

    wide, att, egl = _dn_chunk_specs()
    return _call(body, name, (nc,), [wide, wide, wide, wide, att, egl],
                 [wide, pl.BlockSpec((None, DN_HEADS, hd, hd), lambda i: (i, 0, 0, 0))],
                 [_sds((s, DN_WIDTH), F32), _sds((nc, DN_HEADS, hd, hd), F32)],
                 scratch=[pltpu.VMEM((DN_HEADS, hd, hd), F32)])(u, w, qd, kd, at, eg)


def dn_scan_bwd(u, w, qd, kd, at, eg, states, do, name):
    s = u.shape[0]
    c, hd = DN_CHUNK, DN_HEAD_DIM
    nc = s // c

    def body(u_ref, w_ref, qd_ref, kd_ref, at_ref, eg_ref, st_ref, do_ref,
             du_ref, dw_ref, dqd_ref, dkd_ref, dat_ref, deg_ref, dstate):
        @pl.when(pl.program_id(0) == 0)
        def _():
            dstate[...] = jnp.zeros_like(dstate)

        for h in range(DN_HEADS):
            hs = slice(h * hd, (h + 1) * hd)
            asl = slice(h * c, (h + 1) * c)
            _, vjp = jax.vjp(_dn_step, st_ref[h], qd_ref[:, hs], kd_ref[:, hs], u_ref[:, hs], w_ref[:, hs],
                             at_ref[:, asl], eg_ref[:, hs])
            dst, dqd, dkd, du, dw, dat, deg = vjp((do_ref[:, hs], dstate[h]))
            dstate[h] = dst
            du_ref[:, hs], dw_ref[:, hs], dqd_ref[:, hs], dkd_ref[:, hs] = du, dw, dqd, dkd
            dat_ref[:, asl] = dat
            deg_ref[:, hs] = deg

    rev = lambda i: nc - 1 - i
    wide = pl.BlockSpec((c, DN_WIDTH), lambda i: (rev(i), 0))
    att = pl.BlockSpec((c, DN_HEADS * c), lambda i: (rev(i), 0))
    egl = pl.BlockSpec((8, DN_WIDTH), lambda i: (rev(i), 0))
    wsd = _sds((s, DN_WIDTH), F32)
    return _call(body, name, (nc,),
                 [wide, wide, wide, wide, att, egl, pl.BlockSpec((None, DN_HEADS, hd, hd), lambda i: (rev(i), 0, 0, 0)), wide],
                 [wide, wide, wide, wide, att, egl],
                 [wsd, wsd, wsd, wsd, _sds((s, DN_HEADS * c), F32), _sds((nc * 8, DN_WIDTH), F32)],
                 scratch=[pltpu.VMEM((DN_HEADS, hd, hd), F32)])(u, w, qd, kd, at, eg, states, do)


def _dn_out(o, zg, nw):
    n = o * lax.rsqrt(jnp.mean(o * o, axis=-1, keepdims=True) + NORM_EPS) * nw
    return n * (zg * _sigmoid(zg))


def dn_out_fwd(o, z, nw, name):
    s = o.shape[0]
    tm = _tile(s, 512)
    hd = DN_HEAD_DIM

    def body(o_ref, zg_ref, nw_ref, y_ref):
        for h in range(DN_HEADS):
            hs = slice(h * hd, (h + 1) * hd)
            y_ref[:, hs] = _dn_out(o_ref[:, hs], zg_ref[:, hs], nw_ref[...]).astype(BF16)

    return _call(body, name, (s // tm,),
                 [pl.BlockSpec((tm, DN_WIDTH), lambda i: (i, 0)), pl.BlockSpec((tm, DN_WIDTH), lambda i: (i, Z_ZG // DN_WIDTH)),
                  pl.BlockSpec((1, hd), lambda i: (0, 0))],
                 pl.BlockSpec((tm, DN_WIDTH), lambda i: (i, 0)), _sds((s, DN_WIDTH), BF16), sem=("parallel",))(o, z, nw)


def dn_out_bwd(o, z, nw, dycat, name):
    s = o.shape[0]
    tm = _tile(s, 512)
    hd = DN_HEAD_DIM

    def body(o_ref, zg_ref, nw_ref, dy_ref, do_ref, dzg_ref, dnw_ref):
        @pl.when(pl.program_id(0) == 0)
        def _():
            dnw_ref[...] = jnp.zeros_like(dnw_ref)

        for h in range(DN_HEADS):
            hs = slice(h * hd, (h + 1) * hd)
            _, vjp = jax.vjp(_dn_out, o_ref[:, hs], zg_ref[:, hs], nw_ref[...])
            do, dzg, dnw = vjp(dy_ref[:, hs])
            do_ref[:, hs] = do
            dzg_ref[:, hs] = dzg
            dnw_ref[...] += dnw

    wide = pl.BlockSpec((tm, DN_WIDTH), lambda i: (i, 0))
    wsd = _sds((s, DN_WIDTH), F32)
    return _call(body, name, (s // tm,),
                 [wide, pl.BlockSpec((tm, DN_WIDTH), lambda i: (i, Z_ZG // DN_WIDTH)), pl.BlockSpec((1, hd), lambda i: (0, 0)),
                  pl.BlockSpec((tm, DN_WIDTH), lambda i: (i, ATTN_WIDTH // DN_WIDTH))],
                 [wide, wide, pl.BlockSpec((1, hd), lambda i: (0, 0))], [wsd, wsd, _sds((1, hd), F32)])(o, z, nw, dycat)


def _s5_param_fn(are, aim, ldt, bre, bim):
    dt = jnp.exp(ldt)
    er = jnp.exp(are * dt)
    abr = er * jnp.cos(aim * dt)
    abi = er * jnp.sin(aim * dt)
    den = are * are + aim * aim
    cr = ((abr - 1.0) * are + abi * aim) / den
    ci = (abi * are - (abr - 1.0) * aim) / den
    return abr, abi, cr * bre - ci * bim, cr * bim + ci * bre


def s5_params_fwd(are, aim, ldt, bre, bim, name):
    p, hh = bre.shape

    def body(a_ref, b_ref, c_ref, d_ref, e_ref, o1, o2, o3, o4):
        o1[...], o2[...], o3[...], o4[...] = _s5_param_fn(a_ref[...], b_ref[...], c_ref[...], d_ref[...], e_ref[...])

    col = pl.BlockSpec((p, 1), lambda: (0, 0))
    mat = pl.BlockSpec((p, hh), lambda: (0, 0))
    return _call(body, name, (), [col, col, col, mat, mat], [col, col, mat, mat],
                 [_sds((p, 1), F32), _sds((p, 1), F32), _sds((p, hh), F32), _sds((p, hh), F32)])(are, aim, ldt, bre, bim)


def s5_params_bwd(are, aim, ldt, bre, bim, cts, name):
    p, hh = bre.shape

    def body(a_ref, b_ref, c_ref, d_ref, e_ref, g1, g2, g3, g4, o1, o2, o3, o4, o5):
        _, vjp = jax.vjp(_s5_param_fn, a_ref[...], b_ref[...], c_ref[...], d_ref[...], e_ref[...])
        o1[...], o2[...], o3[...], o4[...], o5[...] = vjp((g1[...], g2[...], g3[...], g4[...]))

    col = pl.BlockSpec((p, 1), lambda: (0, 0))
    mat = pl.BlockSpec((p, hh), lambda: (0, 0))
    csd, msd = _sds((p, 1), F32), _sds((p, hh), F32)
    return _call(body, name, (), [col, col, col, mat, mat, col, col, mat, mat], [col, col, col, mat, mat],
                 [csd, csd, csd, msd, msd])(are, aim, ldt, bre, bim, *cts)


def _cmul(ar, ai, br, bi):
    return ar * br - ai * bi, ar * bi + ai * br


def _s5_scan_block(xr, xi, ar, ai, n, reverse):
    row = lax.broadcasted_iota(jnp.int32, xr.shape, 0)
    d = 1
    while d < n:
        if reverse:
            sr, si = _shift_up(xr, d, row, n), _shift_up(xi, d, row, n)
        else:
            sr, si = _shift_down(xr, d, row), _shift_down(xi, d, row)
        pr, pi = _cmul(ar, ai, sr, si)
        xr, xi = xr + pr, xi + pi
        ar, ai = _cmul(ar, ai, ar, ai)
        d *= 2
    return xr, xi


def s5_scan_fwd(bu, abr, abi, name):
    s = bu.shape[0]
    npb = S5_P // 128

    def body(br_ref, bi_ref, ar_ref, ai_ref, xr_ref, xi_ref):
        xr_ref[...], xi_ref[...] = _s5_scan_block(br_ref[...], bi_ref[...], ar_ref[...], ai_ref[...], s, False)

    re = pl.BlockSpec((s, 128), lambda j: (0, j))
    im = pl.BlockSpec((s, 128), lambda j: (0, npb + j))
    av = pl.BlockSpec((1, 128), lambda j: (0, j))
    osd = _sds((s, S5_P), F32)
    return _call(body, name, (npb,), [re, im, av, av], [re, re], [osd, osd], sem=("parallel",))(bu, bu, abr, abi)


def s5_scan_bwd(dx, x, abr, abi, name):
    s = dx.shape[0]
    npb = S5_P // 128

    def body(dr_ref, di_ref, xr_ref, xi_ref, ar_ref, ai_ref, gr_ref, gi_ref, dar_ref, dai_ref):
        ar, ai = ar_ref[...], ai_ref[...]
        gr, gi = _s5_scan_block(dr_ref[...], di_ref[...], ar, -ai, s, True)
        gr_ref[...], gi_ref[...] = gr, gi
        row = lax.broadcasted_iota(jnp.int32, gr.shape, 0)
        pr, pi = _shift_down(xr_ref[...], 1, row), _shift_down(xi_ref[...], 1, row)
        dar_ref[...] = jnp.sum(gr * pr + gi * pi, axis=0, keepdims=True)
        dai_ref[...] = jnp.sum(gi * pr - gr * pi, axis=0, keepdims=True)

    re = pl.BlockSpec((s, 128), lambda j: (0, j))
    im = pl.BlockSpec((s, 128), lambda j: (0, npb + j))
    av = pl.BlockSpec((1, 128), lambda j: (0, j))
    osd, asd = _sds((s, S5_P), F32), _sds((1, S5_P), F32)
    return _call(body, name, (npb,), [re, im, re, im, av, av], [re, re, av, av], [osd, osd, asd, asd],
                 sem=("parallel",))(dx, dx, x, x, abr, abi)


def _gelu(y):
    return 0.5 * y * (1.0 + jnp.tanh(math.sqrt(2.0 / math.pi) * (y + 0.044715 * y * y * y)))


def s5_out_fwd(ypre, z, dvec, glu_w, glu_b, name):
    s = ypre.shape[0]
    tm = _tile(s, 512)
    wd = S5_WIDTH

    def body(yp_ref, u_ref, d_ref, w_ref, b_ref, y_ref, o_ref):
        y = yp_ref[...] + d_ref[...] * u_ref[...]
        y_ref[...] = y
        g = _gelu(y)
        t = _dot(g.astype(BF16), w_ref[...]) + b_ref[...]
        o_ref[...] = (g * _sigmoid(t)).astype(BF16)

    row = pl.BlockSpec((tm, wd), lambda i: (i, 0))
    vec = pl.BlockSpec((1, wd), lambda i: (0, 0))
    return _call(body, name, (s // tm,),
                 [row, pl.BlockSpec((tm, wd), lambda i: (i, Z_S5 // wd)), vec, pl.BlockSpec((wd, wd), lambda i: (0, 0)), vec],
                 [row, row], [_sds((s, wd), F32), _sds((s, wd), BF16)], sem=("parallel",))(ypre, z, dvec, glu_w, glu_b)


def s5_out_bwd(y, z, glu_w, glu_b, dycat, name):
    s = y.shape[0]
    tm = _tile(s, 512)
    wd = S5_WIDTH

    def body(y_ref, u_ref, w_ref, b_ref, do_ref, dy_ref, dd_ref, dw_ref, db_ref):
        @pl.when(pl.program_id(0) == 0)
        def _():
            dd_ref[...] = jnp.zeros_like(dd_ref)
            dw_ref[...] = jnp.zeros_like(dw_ref)
            db_ref[...] = jnp.zeros_like(db_ref)

        g, gvjp = jax.vjp(_gelu, y_ref[...])
        gb = g.astype(BF16)
        sg = _sigmoid(_dot(gb, w_ref[...]) + b_ref[...])
        do = do_ref[...]
        dt = do * g * sg * (1.0 - sg)
        dtb = dt.astype(BF16)
        dg = do * sg + _dot(dtb, w_ref[...], NT)
        (dy,) = gvjp(dg)
        dy_ref[...] = dy
        dd_ref[...] += jnp.sum(dy * u_ref[...], axis=0, keepdims=True)
        dw_ref[...] += _dot(gb, dtb, TN)
        db_ref[...] += jnp.sum(dt, axis=0, keepdims=True)

    row = pl.BlockSpec((tm, wd), lambda i: (i, 0))
    vec = pl.BlockSpec((1, wd), lambda i: (0, 0))
    mat = pl.BlockSpec((wd, wd), lambda i: (0, 0))
    return _call(body, name, (s // tm,),
                 [row, pl.BlockSpec((tm, wd), lambda i: (i, Z_S5 // wd)), mat, vec,
                  pl.BlockSpec((tm, wd), lambda i: (i, (ATTN_WIDTH + DN_WIDTH) // wd))],
                 [row, vec, mat, vec], [_sds((s, wd), F32), _sds((1, wd), F32), _sds((wd, wd), F32), _sds((1, wd), F32)])(
        y, z, glu_w, glu_b, dycat)


def assemble_dz(dq, dkc, dkp, dvc, dvp, ddn, dzg, dus, dys, dvec, dzs, name):
    s = dq.shape[0]
    w = WINDOW
    nb = s // w
    nxt = lambda i: jnp.minimum(i + 1, nb - 1)

    def body(dq_ref, dkc_ref, dkp_ref, dvc_ref, dvp_ref, ddn_ref, dzg_ref, dus_ref, dys_ref, dv_ref, dzs_ref, o_ref):
        live = (pl.program_id(0) < nb - 1).astype(F32)
        o_ref[:, Z_Q:Z_K] = dq_ref[...].astype(BF16)
        o_ref[:, Z_K:Z_V] = (dkc_ref[...] + live * dkp_ref[...]).astype(BF16)
        o_ref[:, Z_V:Z_DN] = (dvc_ref[...] + live * dvp_ref[...]).astype(BF16)
        o_ref[:, Z_DN:Z_ZG] = ddn_ref[...].astype(BF16)
        o_ref[:, Z_ZG:Z_S5] = dzg_ref[...].astype(BF16)
        o_ref[:, Z_S5:Z_SM] = (dus_ref[...] + dv_ref[...] * dys_ref[...]).astype(BF16)
        o_ref[:, Z_SM:Z_ALL] = dzs_ref[...].astype(BF16)

    def blk(width, f=lambda i: i):
        return pl.BlockSpec((w, width), lambda i: (f(i), 0))

    return _call(body, name, (nb,),
                 [blk(ATTN_WIDTH), blk(ATTN_KV_WIDTH), blk(ATTN_KV_WIDTH, nxt), blk(ATTN_KV_WIDTH), blk(ATTN_KV_WIDTH, nxt),
                  blk(3 * DN_WIDTH), blk(DN_WIDTH), blk(S5_WIDTH), blk(S5_WIDTH), pl.BlockSpec((1, S5_WIDTH), lambda i: (0, 0)),
                  blk(128)],
                 blk(Z_ALL), _sds((s, Z_ALL), BF16), sem=("parallel",))(dq, dkc, dkp, dvc, dvp, ddn, dzg, dus, dys, dvec, dzs)


def _my_place():
    return lax.axis_index("x"), lax.axis_index("y"), lax.axis_index("c")


def _peer(place, p):
    x, y, c = place
    px = 1 - x if p & 4 else x
    py = 1 - y if p & 2 else y
    pc = 1 - c if p & 1 else c
    return (px, py, pc), 4 * px + 2 * py + pc


def exchange(arrays, scatter, name):
    na = len(arrays)

    def body(*refs):
        srcs, dsts = refs[:na], refs[na:2 * na]
        send_sems, recv_sems, local_sems = refs[2 * na:]
        place = _my_place()
        me = 4 * place[0] + 2 * place[1] + place[2]
        copies = []
        for k in range(na):
            mine = srcs[k].at[me] if scatter else srcs[k]
            loc = pltpu.make_async_copy(mine, dsts[k].at[me], local_sems.at[k])
            loc.start()
            copies.append(loc)
        sends = []
        for p in range(1, N_DEV):
            peer, pid = _peer(place, p)
            for k in range(na):
                src = srcs[k].at[pid] if scatter else srcs[k]
                cp = pltpu.make_async_remote_copy(src_ref=src, dst_ref=dsts[k].at[me], send_sem=send_sems.at[k, p - 1],
                                                  recv_sem=recv_sems.at[k, p - 1], device_id=peer,
                                                  device_id_type=pl.DeviceIdType.MESH)
                cp.start()
                sends.append(cp)
        for p in range(1, N_DEV):
            peer, pid = _peer(place, p)
            for k in range(na):
                src = srcs[k].at[me] if scatter else srcs[k]
                pltpu.make_async_remote_copy(src_ref=src, dst_ref=dsts[k].at[pid], send_sem=send_sems.at[k, p - 1],
                                             recv_sem=recv_sems.at[k, p - 1], device_id=peer,
                                             device_id_type=pl.DeviceIdType.MESH).wait_recv()
        for cp in sends:
            cp.wait_send()
        for cp in copies:
            cp.wait()

    outs = [_sds((N_DEV,) + tuple(a.shape[1:] if scatter else a.shape), a.dtype) for a in arrays]
    anyspec = pl.BlockSpec(memory_space=pl.ANY)
    return pl.pallas_call(
        body, name=name, in_specs=[anyspec] * na, out_specs=[anyspec] * na, out_shape=outs,
        scratch_shapes=[pltpu.SemaphoreType.DMA((na, N_DEV - 1)), pltpu.SemaphoreType.DMA((na, N_DEV - 1)),
                        pltpu.SemaphoreType.DMA((na,))])(*arrays)


_HBM = pl.BlockSpec(memory_space=pltpu.HBM)
_SEM = pl.BlockSpec(memory_space=pltpu.SEMAPHORE)
_DATAFLOW = pltpu.SideEffectType.DATAFLOW_SIDE_EFFECTING


def _split_copies(srcs, lands, send_sems, recv_sems, scatter, arriving):
    place = _my_place()
    me = 4 * place[0] + 2 * place[1] + place[2]
    out = []
    for p in range(1, N_DEV):
        peer, pid = _peer(place, p)
        for k in range(len(srcs)):
            i = k * (N_DEV - 1) + p - 1
            src = srcs[k].at[pid] if scatter else srcs[k]
            dst = lands[k].at[pid] if arriving else lands[k].at[me]
            out.append(pltpu.make_async_remote_copy(src_ref=src, dst_ref=dst, send_sem=send_sems.at[i], recv_sem=recv_sems.at[i],
                                                    device_id=peer, device_id_type=pl.DeviceIdType.MESH))
    return out


def exchange_start(groups, scatter, name):
    arrays = [a for g in groups for a in g]
    na, ng = len(arrays), len(groups)
    first = [sum(len(g) for g in groups[:i]) for i in range(ng)]
    me = 4 * lax.axis_index("x") + 2 * lax.axis_index("y") + lax.axis_index("c")
    lands = []
    for a in arrays:
        own = lax.dynamic_index_in_dim(a, me, 0, keepdims=True) if scatter else a[None]
        shape = (N_DEV,) + tuple(own.shape[1:])
        land = lax.dynamic_update_slice(lax.empty(shape, a.dtype), own, (me,) + (0,) * (len(shape) - 1))
        lands.append(pltpu.with_memory_space_constraint(land, pltpu.HBM))
    srcs = [pltpu.with_memory_space_constraint(a, pltpu.HBM) for a in arrays]

    def body(*refs):
        src_refs, land_refs = refs[:na], refs[na:2 * na]
        sems = refs[2 * na:2 * na + 2 * ng]
        token = refs[-1]
        for i, g in enumerate(groups):
            sl = slice(first[i], first[i] + len(g))
            for send in _split_copies(src_refs[sl], land_refs[sl], sems[2 * i], sems[2 * i + 1], scatter, False):
                send.start()
        token[...] = jnp.zeros_like(token)

    sem_shapes = []
    for g in groups:
        sem_shapes += [pltpu.SemaphoreType.DMA((len(g) * (N_DEV - 1),))] * 2
    outs = pl.pallas_call(
        body, name=name,
        out_shape=(*sem_shapes, *[pltpu.HBM(a.shape, a.dtype) for a in srcs], *[pltpu.HBM(a.shape, a.dtype) for a in lands],
                   _sds((8, 128), F32)),
        in_specs=[_HBM] * (2 * na), out_specs=(*[_SEM] * (2 * ng), *[_HBM] * (2 * na), pl.BlockSpec(memory_space=pltpu.VMEM)),
        input_output_aliases={i: 2 * ng + i for i in range(2 * na)},
        compiler_params=pltpu.CompilerParams(has_side_effects=_DATAFLOW))(*srcs, *lands)
    src_out, land_out = outs[2 * ng:2 * ng + na], outs[2 * ng + na:2 * ng + 2 * na]
    handles = [(outs[2 * i], outs[2 * i + 1], src_out[first[i]:first[i] + len(g)], land_out[first[i]:first[i] + len(g)])
               for i, g in enumerate(groups)]
    return handles, outs[-1]


def exchange_wait(handle, scatter, name, after):
    send_sems, recv_sems, srcs, lands = handle
    na = len(srcs)

    def body(*refs):
        src_refs, land_refs = refs[:na], refs[na:2 * na]
        for send in _split_copies(src_refs, land_refs, refs[2 * na], refs[2 * na + 1], scatter, False):
            send.wait_send()
        for recv in _split_copies(src_refs, land_refs, refs[2 * na], refs[2 * na + 1], scatter, True):
            recv.wait_recv()

    outs = pl.pallas_call(
        body, name=name, out_shape=tuple(pltpu.HBM(a.shape, a.dtype) for a in (*srcs, *lands)),
        in_specs=[_HBM] * (2 * na) + [_SEM, _SEM, pl.BlockSpec(memory_space=pl.ANY)], out_specs=tuple([_HBM] * (2 * na)),
        input_output_aliases={i: i for i in range(2 * na)},
        compiler_params=pltpu.CompilerParams(has_side_effects=_DATAFLOW))(*srcs, *lands, send_sems, recv_sems, after)
    return list(outs[na:])


def _adamw(w, g, m, v):
    m = ADAM_B1 * m + (1.0 - ADAM_B1) * g
    v = ADAM_B2 * v + (1.0 - ADAM_B2) * (g * g)
    m_hat = m / (1.0 - ADAM_B1 ** ADAM_STEP)
    v_hat = v / (1.0 - ADAM_B2 ** ADAM_STEP)
    return -ADAM_LR * (m_hat / (jnp.sqrt(v_hat) + ADAM_EPS) + ADAM_WD * w), m, v


def reduce_adamw(parts, w4, m4, v4, layer, name, stacked=None):
    nl, r, c = w4.shape
    tr = _tile(r, 256)

    def body(p_ref, w_ref, m_ref, v_ref, *rest):
        g_ref, d_ref, nm_ref, nv_ref = rest[-4:]
        g = p_ref[0].astype(F32)
        for d in range(1, N_DEV):
            g = g + p_ref[d].astype(F32)
        g_ref[...] = g
        d_ref[...], nm_ref[...], nv_ref[...] = _adamw(w_ref[...], g, m_ref[...], v_ref[...])

    lay = pl.BlockSpec((None, tr, c), lambda i: (layer, i, 0))
    osd = _sds((nl, r, c), F32)
    ins = [parts, w4, m4, v4] + (list(stacked) if stacked is not None else [])
    specs = [pl.BlockSpec((N_DEV, tr, c), lambda i: (0, i, 0)), lay, lay, lay]
    specs += [pl.BlockSpec(memory_space=pl.ANY)] * (len(ins) - 4)
    return pl.pallas_call(
        body, name=name, grid=(r // tr,), in_specs=specs, out_specs=[lay, lay, lay, lay], out_shape=[osd, osd, osd, osd],
        input_output_aliases={4 + k: k for k in range(len(ins) - 4)},
        compiler_params=pltpu.CompilerParams(dimension_semantics=("parallel",), vmem_limit_bytes=VMEM_LIMIT))(*ins)


_SM_NAT = ATTN_WIDTH + 2 * ATTN_KV_WIDTH + 4 * DN_WIDTH


def _win_to_zall(w):
    pad = jnp.zeros((w.shape[0], 128 - 2 * DN_HEADS), w.dtype)
    return jnp.concatenate([w[:, :_SM_NAT], w[:, _SM_NAT + 2 * DN_HEADS:], w[:, _SM_NAT:_SM_NAT + 2 * DN_HEADS], pad], axis=1)


def _zall_to_win(g):
    return jnp.concatenate([g[:, :Z_ZG + DN_WIDTH], g[:, Z_SM:Z_SM + 2 * DN_HEADS], g[:, Z_S5:Z_SM]], axis=1)


def _block_diag(t):
    g, a, b = t.shape
    eye = jnp.eye(g, dtype=t.dtype)
    return (t[:, :, None, :] * eye[:, None, :, None]).reshape(g * a, g * b)


def _block_diag_extract(m, g):
    a, b = m.shape[0] // g, m.shape[1] // g
    eye = jnp.eye(g, dtype=m.dtype)
    return jnp.sum(m.reshape(g, a, g, b) * eye[:, None, :, None], axis=2)


def _rope_tables(s):
    half = HEAD_DIM // 2
    inv_freq = ROPE_THETA ** (-jnp.arange(half, dtype=F32) / half)
    ang = jnp.arange(s, dtype=F32)[:, None] * inv_freq[None, :]
    cos, sin = jnp.cos(ang), jnp.sin(ang)
    return jnp.concatenate([cos, cos], axis=1), jnp.concatenate([-sin, sin], axis=1)


def _row(v):
    return v.reshape(1, -1)


def _ffn_fwd(x, g_pre, g_post, weight, tag):
    h = rmsnorm_fwd(x, g_pre, f"{tag}_norm")
    wg, wu = weight(f"{tag}_w_gate", h), weight(f"{tag}_w_up", h)
    a, b, u = ffn_up(h, wg, wu, f"{tag}_up", dep=weight("token", None))
    wd = weight(f"{tag}_w_down", u)
    y, xn = down_norm(u, wd, x, g_post, FFN_RES_WEIGHT, f"{tag}_down")
    return xn, (x, h, a, b, u, y, wg, wu, wd)


def _ffn_bwd(dxn, saved, g_pre, g_post, on_grads, tag):
    x, h, a, b, u, y, wg, wu, wd = saved
    dy, dg_post = norm_bwd(dxn, y, g_post, FFN_RES_WEIGHT, None, BF16, f"{tag}_bnorm_post")
    da, db = ffn_down_bwd(dy, wd, a, b, f"{tag}_bdown")
    dwd = mm_tn(u, dy[None], BF16, f"{tag}_dwd")
    dwg = mm_tn(h[None], da, BF16, f"{tag}_dwg")
    dwu = mm_tn(h[None], db, BF16, f"{tag}_dwu")
    tok = on_grads({f"{tag}_w_gate": dwg, f"{tag}_w_up": dwu, f"{tag}_w_down": dwd})
    dh = mm_nt_acc([(da, wg), (db, wu)], f"{tag}_dh", tm=512, tn=2048, dep=tok)
    dx, dg_pre = norm_bwd(dh, x, g_pre, 1.0, dxn, F32, f"{tag}_bnorm_pre")
    return dx, dict(g_pre=dg_pre, g_post=dg_post)


def _s5_layouts(p):
    are, aim = p["s5_a_re"].reshape(S5_P, 1), p["s5_a_im"].reshape(S5_P, 1)
    ldt = jnp.repeat(p["s5_log_dt"], S5_STATE).reshape(S5_P, 1)
    bre, bim = p["s5_b_re"].reshape(S5_P, S5_GROUP_CH), p["s5_b_im"].reshape(S5_P, S5_GROUP_CH)
    return are, aim, ldt, bre, bim


def _mix_fwd(x, p, weight, cos_f, sin_s, tag):
    h = rmsnorm_fwd(x, _row(p["mix_norm_pre"]), f"{tag}_norm")
    w_all, glu_w = weight("w_all", h), weight("s5_glu_w", h)
    z = mm_nn(h, w_all, f"{tag}_win")
    y_attn = attn_fwd(z, cos_f, sin_s, p["attn_sinks"], f"{tag}_attn")
    alog_b = jnp.broadcast_to(p["dn_a_log"][:, None], (DN_HEADS, 128))
    dtb_b = jnp.broadcast_to(p["dn_dt_bias"][:, None], (DN_HEADS, 128))
    qkv = dn_pre_fwd(z, p["dn_conv_w"], f"{tag}_dnpre")
    bb, gb = dn_gates_fwd(z, alog_b, dtb_b, f"{tag}_dngate")
    u, w, qd, kd, at, eg = dn_intra_fwd(qkv, gb, bb, f"{tag}_dnintra")
    o, states = dn_scan_fwd(u, w, qd, kd, at, eg, f"{tag}_dnscan")
    y_dn = dn_out_fwd(o, z, _row(p["dn_norm_w"]), f"{tag}_dnout")
    s5cols = _s5_layouts(p)
    abr, abi, bbr, bbi = s5_params_fwd(*s5cols, f"{tag}_s5par")
    tb = lambda t: jnp.transpose(t.reshape(S5_GROUPS, S5_STATE, S5_GROUP_CH), (0, 2, 1))
    b_blk = jnp.concatenate([_block_diag(tb(bbr)), _block_diag(tb(bbi))], axis=1).astype(BF16)
    tc = lambda t: jnp.transpose(t, (0, 2, 1))
    c_blk = jnp.concatenate([_block_diag(tc(p["s5_c_re"])), -_block_diag(tc(p["s5_c_im"]))], axis=0).astype(BF16)
    bu = mm_nn(z, b_blk, f"{tag}_s5bu", tn=1024, col0=Z_S5)
    xr, xi = s5_scan_fwd(bu, abr.reshape(1, S5_P), abi.reshape(1, S5_P), f"{tag}_s5scan")
    xs = jnp.concatenate([xr, xi], axis=1)
    ypre = mm_nn(xs, c_blk, f"{tag}_s5c", tm=256)
    y5, y_s5 = s5_out_fwd(ypre, z, _row(p["s5_d"]), glu_w, _row(p["s5_glu_b"]), f"{tag}_s5out")
    ycat = jnp.concatenate([y_attn, y_dn, y_s5], axis=1)
    w_out = weight("w_out", ycat)
    mixed, xn = down_norm(ycat[None], w_out[None], x, _row(p["mix_norm_post"]), 1.0, f"{tag}_wout")
    saved = dict(x=x, h=h, z=z, qkv=qkv, bb=bb, gb=gb, dn=(u, w, qd, kd, at, eg), states=states, o=o, s5cols=s5cols,
                 abr=abr, abi=abi, b_blk=b_blk, c_blk=c_blk, xs=xs, y5=y5, ycat=ycat, mixed=mixed,
                 alog_b=alog_b, dtb_b=dtb_b, w_all=w_all, w_out=w_out, glu_w=glu_w)
    return xn, saved


def _mix_bwd(dxn, sv, p, on_grads, cos_f, sin_s, tag):
    z = sv["z"]
    w_all, w_out, glu_w = sv["w_all"], sv["w_out"], sv["glu_w"]
    g = {}
    dmixed, g["mix_norm_post"] = norm_bwd(dxn, sv["mixed"], _row(p["mix_norm_post"]), 1.0, None, BF16, f"{tag}_bnorm_post")
    g["w_out"] = mm_tn(sv["ycat"][None], dmixed[None], BF16, f"{tag}_dwout", tn=1024)[0]
    dycat = mm_nt_acc([(dmixed[None], w_out[None])], f"{tag}_dycat")
    dq, dkc, dkp, dvc, dvp, dsink = attn_bwd(z, cos_f, sin_s, p["attn_sinks"], dycat, f"{tag}_battn")
    g["attn_sinks"] = dsink[:, 0]
    do, dzg, dnw = dn_out_bwd(sv["o"], z, _row(p["dn_norm_w"]), dycat, f"{tag}_bdnout")
    g["dn_norm_w"] = dnw[0]
    cts = dn_scan_bwd(*sv["dn"], sv["states"], do, f"{tag}_bdnscan")
    dqn, dkn, dvn, dgb, dbb = dn_intra_bwd(sv["qkv"], sv["gb"], sv["bb"], cts, f"{tag}_bdnintra")
    dzs, dal, ddt = dn_gates_bwd(z, sv["alog_b"], sv["dtb_b"], dbb, dgb, f"{tag}_bdngate")
    g["dn_a_log"], g["dn_dt_bias"] = dal[:, 0], ddt[:, 0]
    ddn, g["dn_conv_w"] = dn_pre_bwd(z, p["dn_conv_w"], jnp.concatenate([dqn, dkn, dvn], axis=1), f"{tag}_bdnpre")
    dy5, dd, dglu, dglub = s5_out_bwd(sv["y5"], z, glu_w, _row(p["s5_glu_b"]), dycat, f"{tag}_bs5out")
    g["s5_d"], g["s5_glu_w"], g["s5_glu_b"] = dd[0], dglu, dglub[0]
    dxs = mm_nt_acc([(dy5[None], sv["c_blk"][None])], f"{tag}_bs5c", tn=1024)
    dc_blk = mm_tn(sv["xs"][None], dy5[None], F32, f"{tag}_ds5c", tk=256)[0]
    ex = lambda m: jnp.transpose(_block_diag_extract(m, S5_GROUPS), (0, 2, 1))
    g["s5_c_re"], g["s5_c_im"] = ex(dc_blk[:S5_P]), -ex(dc_blk[S5_P:])
    gr, gi, dar, dai = s5_scan_bwd(dxs, sv["xs"], sv["abr"].reshape(1, S5_P), sv["abi"].reshape(1, S5_P), f"{tag}_bs5scan")
    dbu = jnp.concatenate([gr, gi], axis=1)
    dus = mm_nt_acc([(dbu[None], sv["b_blk"][None])], f"{tag}_bs5bu")
    u_s5 = z[:, Z_S5:Z_SM]
    db_blk = mm_tn(u_s5[None], dbu[None], F32, f"{tag}_ds5b", tn=1024)[0]
    exb = lambda m: jnp.transpose(_block_diag_extract(m, S5_GROUPS), (0, 2, 1)).reshape(S5_P, S5_GROUP_CH)
    dcols = s5_params_bwd(*sv["s5cols"], (dar.reshape(S5_P, 1), dai.reshape(S5_P, 1), exb(db_blk[:, :S5_P]), exb(db_blk[:, S5_P:])),
                          f"{tag}_bs5par")
    g["s5_a_re"] = dcols[0].reshape(S5_GROUPS, S5_STATE)
    g["s5_a_im"] = dcols[1].reshape(S5_GROUPS, S5_STATE)
    g["s5_log_dt"] = jnp.sum(dcols[2].reshape(S5_GROUPS, S5_STATE), axis=1)
    g["s5_b_re"] = dcols[3].reshape(S5_GROUPS, S5_STATE, S5_GROUP_CH)
    g["s5_b_im"] = dcols[4].reshape(S5_GROUPS, S5_STATE, S5_GROUP_CH)
    dz = assemble_dz(dq, dkc, dkp, dvc, dvp, ddn, dzg, dus, dy5, _row(p["s5_d"]), dzs, f"{tag}_dz")
    g["w_all"] = mm_tn(sv["h"][None], dz[None], BF16, f"{tag}_dwin")[0]
    dwin = _zall_to_win(g.pop("w_all"))
    d_model = dwin.shape[0]
    tok = on_grads({"w_in": jnp.transpose(dwin.reshape(d_model, N_DEV, IN_WIDTH // N_DEV), (1, 0, 2)),
                    "s5_glu_w": g.pop("s5_glu_w").astype(BF16).reshape(N_DEV, S5_WIDTH // N_DEV, S5_WIDTH),
                    "w_out": g.pop("w_out").reshape(N_DEV, MIX_WIDTH // N_DEV, d_model)})
    dh = mm_nt_acc([(dz[None], w_all[None])], f"{tag}_dh", dep=tok)
    dx, g["mix_norm_pre"] = norm_bwd(dh, sv["x"], _row(p["mix_norm_pre"]), 1.0, dxn, F32, f"{tag}_bnorm_pre")
    return dx, g


BIG = ("ff1_w_gate", "ff1_w_up", "ff1_w_down", "w_in", "s5_glu_w", "w_out", "ff2_w_gate", "ff2_w_up", "ff2_w_down")
SMALL = ("ff1_norm_pre", "ff1_norm_post", "mix_norm_pre", "attn_sinks", "dn_conv_w", "dn_a_log", "dn_dt_bias", "dn_norm_w",
         "s5_a_re", "s5_a_im", "s5_log_dt", "s5_b_re", "s5_b_im", "s5_c_re", "s5_c_im", "s5_d", "s5_glu_b",
         "mix_norm_post", "ff2_norm_pre", "ff2_norm_post")
GATHER_GROUPS = (("ff1_w_gate", "ff1_w_up"), ("ff1_w_down",), ("w_in", "s5_glu_w"), ("w_out",),
                 ("ff2_w_gate", "ff2_w_up"), ("ff2_w_down",))
WEIGHTS = ("ff1_norm_pre", "ff1_w_gate", "ff1_w_up", "ff1_w_down", "ff1_norm_post", "mix_norm_pre", "w_in", "attn_sinks",
           "dn_conv_w", "dn_a_log", "dn_dt_bias", "dn_norm_w", "s5_a_re", "s5_a_im", "s5_log_dt", "s5_b_re", "s5_b_im",
           "s5_c_re", "s5_c_im", "s5_d", "s5_glu_w", "s5_glu_b", "w_out", "mix_norm_post", "ff2_norm_pre", "ff2_w_gate",
           "ff2_w_up", "ff2_w_down", "ff2_norm_post")


def _pack(parts):
    flat = jnp.concatenate([a.reshape(-1) for a in parts])
    n = flat.shape[0]
    rows = -(-n // 1024) * 8
    return jnp.pad(flat, (0, rows * 128 - n)).reshape(rows, 128)


def _unpack(mat, shapes):
    flat = mat.reshape(-1)
    out, off = [], 0
    for shp in shapes:
        n = int(np.prod(shp))
        out.append(flat[off:off + n].reshape(shp))
        off += n
    return out


def local_step(x, target, smalls, weight, on_grads):
    depth = len(smalls)
    cos_f, sin_s = _rope_tables(x.shape[0])
    xs = x
    saved = []
    for l in range(depth):
        p = smalls[l]
        wl = functools.partial(weight, l)
        xs, s1 = _ffn_fwd(xs, _row(p["ff1_norm_pre"]), _row(p["ff1_norm_post"]), wl, "ff1")
        xs, s2 = _mix_fwd(xs, p, wl, cos_f, sin_s, "mix")
        xs, s3 = _ffn_fwd(xs, _row(p["ff2_norm_pre"]), _row(p["ff2_norm_post"]), wl, "ff2")
        saved.append((s1, s2, s3))

    loss_vec, dx = loss_and_grad(xs, target, "loss")

    small_g = [None] * depth
    for l in reversed(range(depth)):
        p = smalls[l]
        gl = functools.partial(on_grads, l)
        s1, s2, s3 = saved[l]
        dx, g3 = _ffn_bwd(dx, s3, _row(p["ff2_norm_pre"]), _row(p["ff2_norm_post"]), gl, "ff2")
        dx, g2 = _mix_bwd(dx, s2, p, gl, cos_f, sin_s, "mix")
        dx, g1 = _ffn_bwd(dx, s1, _row(p["ff1_norm_pre"]), _row(p["ff1_norm_post"]), gl, "ff1")
        sg = {n: g2[n] for n in SMALL if n in g2}
        sg.update(ff1_norm_pre=g1["g_pre"][0], ff1_norm_post=g1["g_post"][0], ff2_norm_pre=g3["g_pre"][0], ff2_norm_post=g3["g_post"][0],
                  mix_norm_pre=g2["mix_norm_pre"][0], mix_norm_post=g2["mix_norm_post"][0])
        small_g[l] = sg
    return loss_vec, dx, small_g


def kernel(x, ff1_norm_pre, ff1_w_gate, ff1_w_up, ff1_w_down, ff1_norm_post, mix_norm_pre, w_in, attn_sinks, dn_conv_w, dn_a_log, dn_dt_bias, dn_norm_w, s5_a_re, s5_a_im, s5_log_dt, s5_b_re, s5_b_im, s5_c_re, s5_c_im, s5_d, s5_glu_w, s5_glu_b, w_out, mix_norm_post, ff2_norm_pre, ff2_w_gate, ff2_w_up, ff2_w_down, ff2_norm_post, loss_target, m_ff1_norm_pre, m_ff1_w_gate, m_ff1_w_up, m_ff1_w_down, m_ff1_norm_post, m_mix_norm_pre, m_w_in, m_attn_sinks, m_dn_conv_w, m_dn_a_log, m_dn_dt_bias, m_dn_norm_w, m_s5_a_re, m_s5_a_im, m_s5_log_dt, m_s5_b_re, m_s5_b_im, m_s5_c_re, m_s5_c_im, m_s5_d, m_s5_glu_w, m_s5_glu_b, m_w_out, m_mix_norm_post, m_ff2_norm_pre, m_ff2_w_gate, m_ff2_w_up, m_ff2_w_down, m_ff2_norm_post, v_ff1_norm_pre, v_ff1_w_gate, v_ff1_w_up, v_ff1_w_down, v_ff1_norm_post, v_mix_norm_pre, v_w_in, v_attn_sinks, v_dn_conv_w, v_dn_a_log, v_dn_dt_bias, v_dn_norm_w, v_s5_a_re, v_s5_a_im, v_s5_log_dt, v_s5_b_re, v_s5_b_im, v_s5_c_re, v_s5_c_im, v_s5_d, v_s5_glu_w, v_s5_glu_b, v_w_out, v_mix_norm_post, v_ff2_norm_pre, v_ff2_w_gate, v_ff2_w_up, v_ff2_w_down, v_ff2_norm_post):
    args = dict(locals())
    W = {n: args[n] for n in WEIGHTS}
    M = {n: args["m_" + n] for n in WEIGHTS}
    V = {n: args["v_" + n] for n in WEIGHTS}
    depth = ff1_norm_pre.shape[0]
    d_model = x.shape[2]
    me = 4 * lax.axis_index("x") + 2 * lax.axis_index("y") + lax.axis_index("c")

    conv_sh = dn_conv_w.shape[2]
    conv_all = exchange([dn_conv_w.reshape(depth * DN_CONV, conv_sh)], False, "gather_conv")[0]
    conv_full = jnp.transpose(conv_all.reshape(N_DEV, depth, DN_CONV, conv_sh), (1, 2, 0, 3)).reshape(depth, DN_CONV, N_DEV * conv_sh)

    def small_params(l):
        p = {n: W[n][l] for n in SMALL}
        p["dn_conv_w"] = conv_full[l]
        return p

    gathered_w, gather_handles, tokens = {}, {}, []
    group_of = {n: i for i, grp in enumerate(GATHER_GROUPS) for n in grp}

    def follow(a, after):
        elem = after[tuple(slice(0, 1) for _ in range(after.ndim))]
        bits = lax.bitcast_convert_type(elem, jnp.int32 if after.dtype == F32 else jnp.int16)
        return a + (bits & 0).astype(a.dtype).reshape((1,) * a.ndim)

    def start_gather(l, after):
        shards = {n: cast_bf16(W[n], l, f"cast_{n}") for n in BIG}
        first = GATHER_GROUPS[0][0]
        shards[first] = follow(shards[first], after)
        handles, tok = exchange_start([[shards[n] for n in grp] for grp in GATHER_GROUPS], False, f"gather_start_l{l}")
        gather_handles.update({(l, i): h for i, h in enumerate(handles)})
        tokens.append(tok)

    def weight(l, name, after):
        if name == "token":
            return tokens.pop() if tokens else None
        key = "w_in" if name == "w_all" else name
        i = group_of[key]
        if (l, i) in gather_handles:
            got = dict(zip(GATHER_GROUPS[i], exchange_wait(gather_handles.pop((l, i)), False, f"gather_wait_l{l}_g{i}", after)))
            if i == 0 and l + 1 < depth:
                start_gather(l + 1, got[GATHER_GROUPS[0][0]])
            if "w_in" in got:
                got["w_all"] = _win_to_zall(jnp.transpose(got["w_in"], (1, 0, 2)).reshape(d_model, IN_WIDTH))
                got["s5_glu_w"] = got["s5_glu_w"].reshape(S5_WIDTH, S5_WIDTH)
            if "w_out" in got:
                got["w_out"] = got["w_out"].reshape(MIX_WIDTH, d_model)
            gathered_w.update({(l, n): a for n, a in got.items()})
        return gathered_w[(l, name)]

    stacked = {n: None for n in BIG}
    in_flight = []

    def finish_scatter(after):
        l, names, handle = in_flight.pop(0)
        recv = dict(zip(names, exchange_wait(handle, True, f"scatter_wait_l{l}_{names[0]}", after)))
        for n in names:
            stacked[n] = reduce_adamw(recv[n], W[n], M[n], V[n], l, f"adamw_{n}", stacked[n])

    def on_grads(l, grads):
        names = tuple(grads)
        (handle,), tok = exchange_start([[grads[n] for n in names]], True, f"scatter_start_l{l}_{names[0]}")
        in_flight.append((l, names, handle))
        if len(in_flight) > 2:
            finish_scatter(tok)
        return tok

    start_gather(0, conv_all)
    tokens.clear()
    smalls = [small_params(l) for l in range(depth)]
    loss_vec, dx, small_g = local_step(x[0], loss_target[0], smalls, weight, on_grads)
    while in_flight:
        finish_scatter(dx)
    loss = lax.psum(loss_vec[0, 0], ("x", "y", "c"))

    shapes = [(depth,) + ((DN_CONV, N_DEV * conv_sh) if n == "dn_conv_w" else tuple(W[n].shape[1:])) for n in SMALL]
    packed = _pack([jnp.stack([small_g[l][n] for l in range(depth)]) for n in SMALL])
    gathered = exchange([packed], False, "gather_small_grads")[0]

    def shard_of(n, full):
        return lax.dynamic_slice_in_dim(full, me * conv_sh, conv_sh, axis=2) if n == "dn_conv_w" else full

    conv_pad = lambda t: jnp.tile(t, (1, 1, N_DEV))
    wp = _pack([conv_pad(W[n]) if n == "dn_conv_w" else W[n] for n in SMALL])
    mp = _pack([conv_pad(M[n]) if n == "dn_conv_w" else M[n] for n in SMALL])
    vp = _pack([conv_pad(V[n]) if n == "dn_conv_w" else V[n] for n in SMALL])
    sm = reduce_adamw(gathered, wp[None], mp[None], vp[None], 0, "adamw_small")
    small_out = [dict(zip(SMALL, [shard_of(n, t) for n, t in zip(SMALL, _unpack(o[0], shapes))])) for o in sm]

    outs = []
    for kind in range(4):
        for n in WEIGHTS:
            if n in BIG:
                outs.append(stacked[n][kind])
            else:
                outs.append(small_out[kind][n])
    return (loss, dx[None], *outs)
```

```python
import functools
import math

import jax
import jax.numpy as jnp
import numpy as np
from jax import lax
from jax.experimental import pallas as pl
from jax.experimental.pallas import tpu as pltpu

F32 = jnp.float32
BF16 = jnp.bfloat16

N_DEV = 8
DEPTH = 4
ATTN_HEADS = 8
ATTN_KV_HEADS = 2
HEAD_DIM = 128
WINDOW = 128
ROPE_THETA = 10000.0
DN_HEADS = 4
DN_HEAD_DIM = 128
DN_CONV = 4
DN_CHUNK = 64
S5_GROUPS = 32
S5_GROUP_CH = 16
S5_STATE = 64
ATTN_WIDTH = ATTN_HEADS * HEAD_DIM
ATTN_KV_WIDTH = ATTN_KV_HEADS * HEAD_DIM
DN_WIDTH = DN_HEADS * DN_HEAD_DIM
S5_WIDTH = S5_GROUPS * S5_GROUP_CH
S5_P = S5_GROUPS * S5_STATE
MIX_WIDTH = ATTN_WIDTH + DN_WIDTH + S5_WIDTH
IN_WIDTH = ATTN_WIDTH + 2 * ATTN_KV_WIDTH + 4 * DN_WIDTH + 2 * DN_HEADS + S5_WIDTH
Z_Q, Z_K, Z_V = 0, ATTN_WIDTH, ATTN_WIDTH + ATTN_KV_WIDTH
Z_DN = ATTN_WIDTH + 2 * ATTN_KV_WIDTH
Z_ZG = Z_DN + 3 * DN_WIDTH
Z_S5 = Z_ZG + DN_WIDTH
Z_SM = Z_S5 + S5_WIDTH
Z_ALL = Z_SM + 128
FFN_RES_WEIGHT = 0.5
NORM_EPS = 1e-6
ADAM_LR, ADAM_B1, ADAM_B2, ADAM_EPS, ADAM_WD, ADAM_STEP = 0.001, 0.9, 0.999, 1e-08, 0.01, 10

VMEM_LIMIT = 56 * 1024 * 1024
HI = lax.Precision.HIGHEST
INV_PREC = HI
NEG = -1e30

NN = (((1,), (0,)), ((), ()))
NT = (((1,), (1,)), ((), ()))
TN = (((0,), (0,)), ((), ()))


def _dot(a, b, dims=NN, prec=None):
    return lax.dot_general(a, b, dims, preferred_element_type=F32, precision=prec)


def _tile(n, pref, mult=8):
    if n <= pref:
        return n
    t = (pref // mult) * mult
    while t > mult and n % t:
        t -= mult
    assert n % t == 0, (n, pref)
    return t


def _call(body, name, grid, in_specs, out_specs, out_shape, scratch=(), sem=None):
    if sem is None:
        sem = ("arbitrary",) * len(grid)
    return pl.pallas_call(
        body, name=name, grid=grid, in_specs=in_specs, out_specs=out_specs, out_shape=out_shape,
        scratch_shapes=list(scratch),
        compiler_params=pltpu.CompilerParams(dimension_semantics=sem, vmem_limit_bytes=VMEM_LIMIT))


def _sds(shape, dtype):
    return jax.ShapeDtypeStruct(tuple(shape), dtype)


def _sigmoid(x):
    return 1.0 / (1.0 + jnp.exp(-x))


def _silu_and_grad(a):
    sg = _sigmoid(a)
    return a * sg, sg * (1.0 + a * (1.0 - sg))


def _softplus(x):
    return jnp.maximum(x, 0.0) + jnp.log(1.0 + jnp.exp(-jnp.abs(x)))


def _rms(x, g):
    r = lax.rsqrt(jnp.mean(x * x, axis=-1, keepdims=True) + NORM_EPS)
    return x * r * g


def _rms_bwd(dout, y, g):
    r = lax.rsqrt(jnp.mean(y * y, axis=-1, keepdims=True) + NORM_EPS)
    n = y * r
    dn = dout * g
    dy = r * (dn - n * jnp.mean(dn * n, axis=-1, keepdims=True))
    return dy, jnp.sum(dout * n, axis=0, keepdims=True)


def _rope(x, cos_f, sin_s):
    return x * cos_f + pltpu.roll(x, HEAD_DIM // 2, 1) * sin_s


def _rope_bwd(d, cos_f, sin_s):
    return d * cos_f + pltpu.roll(d * sin_s, HEAD_DIM // 2, 1)


def rmsnorm_fwd(x, g, name):
    s, d = x.shape
    tm = _tile(s, 512)

    def body(x_ref, g_ref, o_ref):
        o_ref[...] = _rms(x_ref[...], g_ref[...]).astype(BF16)

    return _call(body, name, (s // tm,),
                 [pl.BlockSpec((tm, d), lambda i: (i, 0)), pl.BlockSpec((1, d), lambda i: (0, 0))],
                 pl.BlockSpec((tm, d), lambda i: (i, 0)), _sds((s, d), BF16), sem=("parallel",))(x, g)


def norm_bwd(dout, y, g, scale, resid, out_dtype, name):
    s, d = y.shape
    tm = _tile(s, 256)
    has_res = resid is not None

    def body(*refs):
        if has_res:
            do_ref, y_ref, g_ref, r_ref, dy_ref, dg_ref = refs
        else:
            do_ref, y_ref, g_ref, dy_ref, dg_ref = refs
        dy, dg = _rms_bwd(do_ref[...] * scale, y_ref[...], g_ref[...])
        if has_res:
            dy = dy + r_ref[...]
        dy_ref[...] = dy.astype(out_dtype)

        @pl.when(pl.program_id(0) == 0)
        def _():
            dg_ref[...] = jnp.zeros_like(dg_ref)

        dg_ref[...] += dg

    row = pl.BlockSpec((tm, d), lambda i: (i, 0))
    vec = pl.BlockSpec((1, d), lambda i: (0, 0))
    ins = [dout, y, g] + ([resid] if has_res else [])
    return _call(body, name, (s // tm,), [row, row, vec] + ([row] if has_res else []),
                 [row, vec], [_sds((s, d), out_dtype), _sds((1, d), F32)])(*ins)


def ffn_up(h, wg, wu, name, dep=None):
    s, d = h.shape
    nj, _, fs = wg.shape
    tm = _tile(s, 512)

    def body(h_ref, wg_ref, wu_ref, *rest):
        a_ref, b_ref, u_ref = rest[-3:]
        hh = h_ref[...]
        a = _dot(hh, wg_ref[...])
        b = _dot(hh, wu_ref[...])
        a_ref[...] = a.astype(BF16)
        b_ref[...] = b.astype(BF16)
        u_ref[...] = (a * _sigmoid(a) * b).astype(BF16)

    wspec = pl.BlockSpec((None, d, fs), lambda j, i: (j, 0, 0))
    ospec = pl.BlockSpec((None, tm, fs), lambda j, i: (j, i, 0))
    osd = _sds((nj, s, fs), BF16)
    ins, specs = [h, wg, wu], [pl.BlockSpec((tm, d), lambda j, i: (i, 0)), wspec, wspec]
    if dep is not None:
        ins.append(dep)
        specs.append(pl.BlockSpec((8, 128), lambda j, i: (0, 0)))
    return _call(body, name, (nj, s // tm), specs, [ospec, ospec, ospec], [osd, osd, osd], sem=("parallel", "parallel"))(*ins)


def down_norm(u3, w3, x, g, scale, name):
    nj, s, k = u3.shape
    d = w3.shape[2]
    tm = _tile(s, 512)
    jb = 2 if nj % 2 == 0 else 1
    nsteps = nj // jb

    def body(u_ref, w_ref, x_ref, g_ref, y_ref, xn_ref):
        j = pl.program_id(1)
        t = _dot(u_ref[0], w_ref[0])
        for q in range(1, jb):
            t = t + _dot(u_ref[q], w_ref[q])

        @pl.when(j == 0)
        def _():
            y_ref[...] = t

        @pl.when(j > 0)
        def _():
            y_ref[...] += t

        @pl.when(j == nsteps - 1)
        def _():
            xn_ref[...] = x_ref[...] + scale * _rms(y_ref[...], g_ref[...])

    row = pl.BlockSpec((tm, d), lambda i, j: (i, 0))
    return _call(body, name, (s // tm, nsteps),
                 [pl.BlockSpec((jb, tm, k), lambda i, j: (j, i, 0)), pl.BlockSpec((jb, k, d), lambda i, j: (j, 0, 0)),
                  row, pl.BlockSpec((1, d), lambda i, j: (0, 0))],
                 [row, row], [_sds((s, d), F32), _sds((s, d), F32)], sem=("parallel", "arbitrary"))(u3, w3, x, g)


def ffn_down_bwd(dy, wd, a, b, name):
    nj, fs, d = wd.shape
    s = dy.shape[0]
    tm = _tile(s, 512)

    def body(dy_ref, w_ref, a_ref, b_ref, da_ref, db_ref):
        du = _dot(dy_ref[...], w_ref[...], NT)
        aa = a_ref[...].astype(F32)
        bb = b_ref[...].astype(F32)
        sl, dsl = _silu_and_grad(aa)
        da_ref[...] = (du * bb * dsl).astype(BF16)
        db_ref[...] = (du * sl).astype(BF16)

    hspec = pl.BlockSpec((None, tm, fs), lambda j, i: (j, i, 0))
    osd = _sds((nj, s, fs), BF16)
    return _call(body, name, (nj, s // tm),
                 [pl.BlockSpec((tm, d), lambda j, i: (i, 0)), pl.BlockSpec((None, fs, d), lambda j, i: (j, 0, 0)), hspec, hspec],
                 [hspec, hspec], [osd, osd], sem=("parallel", "parallel"))(dy, wd, a, b)


def mm_tn(a3, b3, out_dtype, name, tmm=2048, tn=1408, tk=1024):
    ja, s, m = a3.shape
    jb, _, n = b3.shape
    nj = max(ja, jb)
    tmm, tn, tk = _tile(m, tmm, 128), _tile(n, tn, 128), _tile(s, tk)
    nk = s // tk

    def body(a_ref, b_ref, o_ref, acc_ref):
        k = pl.program_id(3)

        @pl.when(k == 0)
        def _():
            acc_ref[...] = jnp.zeros_like(acc_ref)

        acc_ref[...] += _dot(a_ref[...].astype(BF16), b_ref[...].astype(BF16), TN)

        @pl.when(k == nk - 1)
        def _():
            o_ref[...] = acc_ref[...].astype(out_dtype)

    aj = (lambda j: j) if ja > 1 else (lambda j: 0)
    bj = (lambda j: j) if jb > 1 else (lambda j: 0)
    return _call(body, name, (nj, m // tmm, n // tn, nk),
                 [pl.BlockSpec((None, tk, tmm), lambda j, im, jn, k: (aj(j), k, im)),
                  pl.BlockSpec((None, tk, tn), lambda j, im, jn, k: (bj(j), k, jn))],
                 pl.BlockSpec((None, tmm, tn), lambda j, im, jn, k: (j, im, jn)), _sds((nj, m, n), out_dtype),
                 scratch=[pltpu.VMEM((tmm, tn), F32)],
                 sem=("parallel", "parallel", "parallel", "arbitrary"))(a3, b3)


def mm_nt_acc(pairs, name, tm=512, tn=512, dep=None):
    nj, s, _ = pairs[0][0].shape
    n = pairs[0][1].shape[1]
    tm, tn = _tile(s, tm), _tile(n, tn, 128)
    npair = len(pairs)

    def body(*refs):
        o_ref = refs[-1]
        j = pl.program_id(2)
        t = None
        for p in range(npair):
            c = _dot(refs[2 * p][...].astype(BF16), refs[2 * p + 1][...], NT)
            t = c if t is None else t + c
        if nj == 1:
            o_ref[...] = t
        else:
            @pl.when(j == 0)
            def _():
                o_ref[...] = t

            @pl.when(j > 0)
            def _():
                o_ref[...] += t

    ins, specs = [], []
    for a3, w3 in pairs:
        k = a3.shape[2]
        ins += [a3, w3]
        specs += [pl.BlockSpec((None, tm, k), lambda i, jn, j: (j, i, 0)),
                  pl.BlockSpec((None, tn, k), lambda i, jn, j: (j, jn, 0))]
    if dep is not None:
        ins.append(dep)
        specs.append(pl.BlockSpec((8, 128), lambda i, jn, j: (0, 0)))
    return _call(body, name, (s // tm, n // tn, nj), specs,
                 pl.BlockSpec((tm, tn), lambda i, jn, j: (i, jn)), _sds((s, n), F32),
                 sem=("parallel", "parallel", "arbitrary"))(*ins)


def mm_nn(a, w, name, tm=512, tn=1408, col0=0, kdim=None):
    s = a.shape[0]
    k, n = w.shape
    assert col0 % k == 0
    tm, tn = _tile(s, tm), _tile(n, tn, 128)
    cb = col0 // k

    def body(a_ref, w_ref, o_ref):
        o_ref[...] = _dot(a_ref[...].astype(BF16), w_ref[...])

    return _call(body, name, (n // tn, s // tm),
                 [pl.BlockSpec((tm, k), lambda jn, i: (i, cb)), pl.BlockSpec((k, tn), lambda jn, i: (0, jn))],
                 pl.BlockSpec((tm, tn), lambda jn, i: (i, jn)), _sds((s, n), F32), sem=("parallel", "parallel"))(a, w)


def mm_nn_acc(a3, w3, name, tm=256):
    nj, s, k = a3.shape
    n = w3.shape[2]
    tm = _tile(s, tm)

    def body(a_ref, w_ref, o_ref):
        j = pl.program_id(1)
        t = _dot(a_ref[...].astype(BF16), w_ref[...])

        @pl.when(j == 0)
        def _():
            o_ref[...] = t

        @pl.when(j > 0)
        def _():
            o_ref[...] += t

    return _call(body, name, (s // tm, nj),
                 [pl.BlockSpec((None, tm, k), lambda i, j: (j, i, 0)), pl.BlockSpec((None, k, n), lambda i, j: (j, 0, 0))],
                 pl.BlockSpec((tm, n), lambda i, j: (i, 0)), _sds((s, n), F32), sem=("parallel", "arbitrary"))(a3, w3)


def loss_and_grad(xl, target, name):
    s, d = xl.shape
    tm = _tile(s, 512)

    def body(x_ref, t_ref, l_ref, dx_ref):
        e = x_ref[...] - t_ref[...]
        dx_ref[...] = e * (1.0 / d)

        @pl.when(pl.program_id(0) == 0)
        def _():
            l_ref[...] = jnp.zeros_like(l_ref)

        part = jnp.sum(jnp.sum(e * e, axis=-1, keepdims=True), axis=0, keepdims=True) * (0.5 / d)
        l_ref[...] += jnp.broadcast_to(part, l_ref.shape)

    row = pl.BlockSpec((tm, d), lambda i: (i, 0))
    return _call(body, name, (s // tm,), [row, row], [pl.BlockSpec((1, 128), lambda i: (0, 0)), row],
                 [_sds((1, 128), F32), _sds((s, d), F32)])(xl, target)


def cast_bf16(w4, layer, name):
    _, r, c = w4.shape
    tr = _tile(r, 512)

    def body(w_ref, o_ref):
        o_ref[...] = w_ref[...].astype(BF16)

    return _call(body, name, (r // tr,), [pl.BlockSpec((None, tr, c), lambda i: (layer, i, 0))],
                 pl.BlockSpec((tr, c), lambda i: (i, 0)), _sds((r, c), BF16), sem=("parallel",))(w4)


def _attn_specs(nb):
    w = WINDOW
    prev = lambda i: jnp.maximum(i - 1, 0)
    q = pl.BlockSpec((w, ATTN_WIDTH), lambda i: (i, 0))
    kc = pl.BlockSpec((w, ATTN_KV_WIDTH), lambda i: (i, Z_K // ATTN_KV_WIDTH))
    kp = pl.BlockSpec((w, ATTN_KV_WIDTH), lambda i: (prev(i), Z_K // ATTN_KV_WIDTH))
    vc = pl.BlockSpec((w, ATTN_KV_WIDTH), lambda i: (i, Z_V // ATTN_KV_WIDTH))
    vp = pl.BlockSpec((w, ATTN_KV_WIDTH), lambda i: (prev(i), Z_V // ATTN_KV_WIDTH))
    tc = pl.BlockSpec((w, HEAD_DIM), lambda i: (i, 0))
    tp = pl.BlockSpec((w, HEAD_DIM), lambda i: (prev(i), 0))
    sink = pl.BlockSpec(memory_space=pltpu.SMEM)
    return [q, kc, kp, vc, vp, tc, tc, tp, tp, sink]


def _attn_mask(i):
    w = WINDOW
    qi = lax.broadcasted_iota(jnp.int32, (w, 2 * w), 0) + w
    kj = lax.broadcasted_iota(jnp.int32, (w, 2 * w), 1)
    rel = qi - kj
    band = (rel >= 0) & (rel < w)
    return band & jnp.logical_not((i == 0) & (kj < w))


def _attn_probs(q, kk, sink, mask):
    s = _dot(q, kk, NT) * (HEAD_DIM ** -0.5)
    s = jnp.where(mask, s, NEG)
    m = jnp.maximum(jnp.max(s, axis=-1, keepdims=True), sink)
    p = jnp.exp(s - m)
    es = jnp.exp(sink - m)
    inv = 1.0 / (jnp.sum(p, axis=-1, keepdims=True) + es)
    return p * inv, es * inv


def attn_fwd(z, cos_f, sin_s, sinks, name):
    s = z.shape[0]
    nb = s // WINDOW
    hd = HEAD_DIM
    grp = ATTN_HEADS // ATTN_KV_HEADS

    def body(q_ref, kc_ref, kp_ref, vc_ref, vp_ref, cc_ref, sc_ref, cp_ref, sp_ref, sink_ref, o_ref):
        i = pl.program_id(0)
        mask = _attn_mask(i)
        cc, sc, cp, sp = cc_ref[...], sc_ref[...], cp_ref[...], sp_ref[...]
        for kv in range(ATTN_KV_HEADS):
            ksl = slice(kv * hd, (kv + 1) * hd)
            kk = jnp.concatenate([_rope(kp_ref[:, ksl], cp, sp), _rope(kc_ref[:, ksl], cc, sc)], axis=0).astype(BF16)
            vv = jnp.concatenate([vp_ref[:, ksl], vc_ref[:, ksl]], axis=0).astype(BF16)
            for g in range(grp):
                h = kv * grp + g
                hsl = slice(h * hd, (h + 1) * hd)
                q = _rope(q_ref[:, hsl], cc, sc).astype(BF16)
                pn, _ = _attn_probs(q, kk, sink_ref[h], mask)
                o_ref[:, hsl] = _dot(pn.astype(BF16), vv).astype(BF16)

    return _call(body, name, (nb,), _attn_specs(nb), pl.BlockSpec((WINDOW, ATTN_WIDTH), lambda i: (i, 0)),
                 _sds((s, ATTN_WIDTH), BF16), sem=("parallel",))(z, z, z, z, z, cos_f, sin_s, cos_f, sin_s, sinks)


def attn_bwd(z, cos_f, sin_s, sinks, dy, name):
    s = z.shape[0]
    nb = s // WINDOW
    hd = HEAD_DIM
    grp = ATTN_HEADS // ATTN_KV_HEADS
    scale = HEAD_DIM ** -0.5

    def body(q_ref, kc_ref, kp_ref, vc_ref, vp_ref, cc_ref, sc_ref, cp_ref, sp_ref, sink_ref, dy_ref,
             dq_ref, dkc_ref, dkp_ref, dvc_ref, dvp_ref, ds_ref):
        i = pl.program_id(0)
        mask = _attn_mask(i)
        cc, sc, cp, sp = cc_ref[...], sc_ref[...], cp_ref[...], sp_ref[...]

        @pl.when(i == 0)
        def _():
            ds_ref[...] = jnp.zeros_like(ds_ref)

        for kv in range(ATTN_KV_HEADS):
            ksl = slice(kv * hd, (kv + 1) * hd)
            kk = jnp.concatenate([_rope(kp_ref[:, ksl], cp, sp), _rope(kc_ref[:, ksl], cc, sc)], axis=0).astype(BF16)
            vv = jnp.concatenate([vp_ref[:, ksl], vc_ref[:, ksl]], axis=0).astype(BF16)
            dkk = jnp.zeros((2 * WINDOW, hd), F32)
            dvv = jnp.zeros((2 * WINDOW, hd), F32)
            for g in range(grp):
                h = kv * grp + g
                hsl = slice(h * hd, (h + 1) * hd)
                q = _rope(q_ref[:, hsl], cc, sc).astype(BF16)
                pn, psink = _attn_probs(q, kk, sink_ref[h], mask)
                do = dy_ref[:, hsl].astype(BF16)
                dpn = _dot(do, vv, NT)
                dvv = dvv + _dot(pn.astype(BF16), do, TN)
                tot = jnp.sum(pn * dpn, axis=-1, keepdims=True)
                dsc = (pn * (dpn - tot) * scale).astype(BF16)
                dq_ref[:, hsl] = _rope_bwd(_dot(dsc, kk), cc, sc)
                dkk = dkk + _dot(dsc, q, TN)
                dsink = jnp.sum(-psink * tot, axis=0, keepdims=True)
                ds_ref[h:h + 1, :] += jnp.broadcast_to(dsink, (1, 128))
            dkp_ref[:, ksl] = _rope_bwd(dkk[:WINDOW], cp, sp)
            dkc_ref[:, ksl] = _rope_bwd(dkk[WINDOW:], cc, sc)
            dvp_ref[:, ksl] = dvv[:WINDOW]
            dvc_ref[:, ksl] = dvv[WINDOW:]

    kvo = pl.BlockSpec((WINDOW, ATTN_KV_WIDTH), lambda i: (i, 0))
    kvs = _sds((s, ATTN_KV_WIDTH), F32)
    return _call(body, name, (nb,), _attn_specs(nb) + [pl.BlockSpec((WINDOW, ATTN_WIDTH), lambda i: (i, 0))],
                 [pl.BlockSpec((WINDOW, ATTN_WIDTH), lambda i: (i, 0)), kvo, kvo, kvo, kvo,
                  pl.BlockSpec((ATTN_HEADS, 128), lambda i: (0, 0))],
                 [_sds((s, ATTN_WIDTH), F32), kvs, kvs, kvs, kvs, _sds((ATTN_HEADS, 128), F32)])(
        z, z, z, z, z, cos_f, sin_s, cos_f, sin_s, sinks, dy)


def _shift_down(x, d, row):
    return jnp.where(row >= d, pltpu.roll(x, d, 0), 0.0)


def _shift_up(x, d, row, n):
    return jnp.where(row < n - d, pltpu.roll(x, n - d, 0), 0.0)


def _conv_taps(u, w_ref, row):
    c = w_ref[DN_CONV - 1:DN_CONV, :] * u
    for k in range(DN_CONV - 1):
        c = c + w_ref[k:k + 1, :] * _shift_down(u, DN_CONV - 1 - k, row)
    return c


def dn_pre_fwd(z, conv_w, name):
    s = z.shape[0]
    nblk = 3 * DN_WIDTH // 128
    nqk = 2 * DN_WIDTH // 128

    def body(u_ref, w_ref, o_ref):
        row = lax.broadcasted_iota(jnp.int32, (s, 128), 0)
        c = _conv_taps(u_ref[...], w_ref, row)
        sl = c * _sigmoid(c)
        j = pl.program_id(0)

        @pl.when(j < nqk)
        def _():
            o_ref[...] = sl * lax.rsqrt(jnp.sum(sl * sl, axis=-1, keepdims=True) + NORM_EPS)

        @pl.when(j >= nqk)
        def _():
            o_ref[...] = sl

    return _call(body, name, (nblk,),
                 [pl.BlockSpec((s, 128), lambda j: (0, Z_DN // 128 + j)), pl.BlockSpec((DN_CONV, 128), lambda j: (0, j))],
                 pl.BlockSpec((s, 128), lambda j: (0, j)), _sds((s, 3 * DN_WIDTH), F32), sem=("parallel",))(z, conv_w)


def dn_pre_bwd(z, conv_w, dout, name):
    s = z.shape[0]
    nblk = 3 * DN_WIDTH // 128
    nqk = 2 * DN_WIDTH // 128

    def body(u_ref, w_ref, do_ref, du_ref, dw_ref, ds_ref):
        row = lax.broadcasted_iota(jnp.int32, (s, 128), 0)
        u = u_ref[...]
        c = _conv_taps(u, w_ref, row)
        sl, dsl = _silu_and_grad(c)
        do = do_ref[...]
        j = pl.program_id(0)

        @pl.when(j < nqk)
        def _():
            r = lax.rsqrt(jnp.sum(sl * sl, axis=-1, keepdims=True) + NORM_EPS)
            ds_ref[...] = r * do - sl * (r * r * r) * jnp.sum(do * sl, axis=-1, keepdims=True)

        @pl.when(j >= nqk)
        def _():
            ds_ref[...] = do

        dc = ds_ref[...] * dsl
        du = w_ref[DN_CONV - 1:DN_CONV, :] * dc
        dw_ref[DN_CONV - 1:DN_CONV, :] = jnp.sum(dc * u, axis=0, keepdims=True)
        for k in range(DN_CONV - 1):
            d = DN_CONV - 1 - k
            du = du + w_ref[k:k + 1, :] * _shift_up(dc, d, row, s)
            dw_ref[k:k + 1, :] = jnp.sum(dc * _shift_down(u, d, row), axis=0, keepdims=True)
        du_ref[...] = du

    blk = pl.BlockSpec((s, 128), lambda j: (0, j))
    wsp = pl.BlockSpec((DN_CONV, 128), lambda j: (0, j))
    return _call(body, name, (nblk,), [pl.BlockSpec((s, 128), lambda j: (0, Z_DN // 128 + j)), wsp, blk],
                 [blk, wsp], [_sds((s, 3 * DN_WIDTH), F32), _sds((DN_CONV, 3 * DN_WIDTH), F32)],
                 scratch=[pltpu.VMEM((s, 128), F32)], sem=("parallel",))(z, conv_w, dout)


def _lane_col(x, lane, idx):
    return jnp.sum(jnp.where(lane == idx, x, 0.0), axis=-1, keepdims=True)


def dn_gates_fwd(z, alog_b, dtb_b, name):
    s = z.shape[0]
    tm = _tile(s, 512)

    def body(zs_ref, al_ref, dt_ref, beta_ref, g_ref):
        zs = zs_ref[...]
        lane = lax.broadcasted_iota(jnp.int32, zs.shape, 1)
        for h in range(DN_HEADS):
            b_raw = _lane_col(zs, lane, h)
            a_raw = _lane_col(zs, lane, DN_HEADS + h)
            beta_ref[h] = jnp.broadcast_to(_sigmoid(b_raw), (tm, 128))
            g_ref[h] = -jnp.exp(al_ref[h:h + 1, :]) * _softplus(a_raw + dt_ref[h:h + 1, :])

    osp = pl.BlockSpec((DN_HEADS, tm, 128), lambda i: (0, i, 0))
    psp = pl.BlockSpec((DN_HEADS, 128), lambda i: (0, 0))
    osd = _sds((DN_HEADS, s, 128), F32)
    return _call(body, name, (s // tm,), [pl.BlockSpec((tm, 128), lambda i: (i, Z_SM // 128)), psp, psp],
                 [osp, osp], [osd, osd], sem=("parallel",))(z, alog_b, dtb_b)


def dn_gates_bwd(z, alog_b, dtb_b, dbeta, dg, name):
    s = z.shape[0]
    tm = _tile(s, 512)

    def body(zs_ref, al_ref, dt_ref, dbeta_ref, dg_ref, dz_ref, dal_ref, ddt_ref):
        @pl.when(pl.program_id(0) == 0)
        def _():
            dal_ref[...] = jnp.zeros_like(dal_ref)
            ddt_ref[...] = jnp.zeros_like(ddt_ref)

        zs = zs_ref[...]
        lane = lax.broadcasted_iota(jnp.int32, zs.shape, 1)
        dz = jnp.zeros_like(zs)
        for h in range(DN_HEADS):
            b_raw = _lane_col(zs, lane, h)
            a_raw = _lane_col(zs, lane, DN_HEADS + h)
            dbe = jnp.sum(dbeta_ref[h], axis=-1, keepdims=True)
            dgg = jnp.sum(dg_ref[h], axis=-1, keepdims=True)
            beta = _sigmoid(b_raw)
            ea = jnp.exp(al_ref[h:h + 1, :])
            pre = a_raw + dt_ref[h:h + 1, :]
            da_raw = dgg * (-ea) * _sigmoid(pre)
            dz = dz + jnp.where(lane == h, dbe * beta * (1.0 - beta), 0.0) + jnp.where(lane == DN_HEADS + h, da_raw, 0.0)
            ddt_ref[h:h + 1, :] += jnp.sum(da_raw, axis=0, keepdims=True)
            dal_ref[h:h + 1, :] += jnp.sum(dgg * (-ea) * _softplus(pre), axis=0, keepdims=True)
        dz_ref[...] = dz

    hsp = pl.BlockSpec((DN_HEADS, tm, 128), lambda i: (0, i, 0))
    psp = pl.BlockSpec((DN_HEADS, 128), lambda i: (0, 0))
    return _call(body, name, (s // tm,), [pl.BlockSpec((tm, 128), lambda i: (i, Z_SM // 128)), psp, psp, hsp, hsp],
                 [pl.BlockSpec((tm, 128), lambda i: (i, 0)), psp, psp],
                 [_sds((s, 128), F32), _sds((DN_HEADS, 128), F32), _sds((DN_HEADS, 128), F32)])(z, alog_b, dtb_b, dbeta, dg)


def _dn_intra(q, k, v, gb, bb):
    c = DN_CHUNK
    ri = lax.broadcasted_iota(jnp.int32, (c, c), 0)
    ci = lax.broadcasted_iota(jnp.int32, (c, c), 1)
    causal = ri >= ci
    strict = ri > ci
    gc = _dot(causal.astype(F32), gb, NN, HI)
    e0 = (lax.broadcasted_iota(jnp.int32, (c, 128), 1) == 0).astype(F32)
    grow = _dot(e0, gc, NT, HI)
    decay = jnp.where(causal, jnp.exp(jnp.where(causal, gc[:, :c] - grow, 0.0)), 0.0)
    qs = q * (DN_HEAD_DIM ** -0.5)
    kb = k * bb
    lower = jnp.where(strict, _dot(kb, k, NT) * decay, 0.0)
    t = _unit_lower_inverse(lower)
    eg = jnp.exp(gc)
    u = _dot(t, v * bb)
    w = _dot(t, kb * eg)
    attn = jnp.where(causal, _dot(qs, k, NT) * decay, 0.0)
    glast = _dot((ci == c - 1).astype(F32), gc, NN, HI)
    return u, w, qs * eg, k * jnp.exp(glast - gc), attn, jnp.exp(glast[:8])


@jax.custom_vjp
def _unit_lower_inverse(lower):
    n = lower.shape[0]
    eye = (lax.broadcasted_iota(jnp.int32, (n, n), 0) == lax.broadcasted_iota(jnp.int32, (n, n), 1)).astype(F32)
    t = eye - lower
    p = lower
    for _ in range(5):
        p = _dot(p, p, NN, INV_PREC)
        t = t + _dot(t, p, NN, INV_PREC)
    return t


def _unit_lower_inverse_fwd(lower):
    t = _unit_lower_inverse(lower)
    return t, t


def _unit_lower_inverse_bwd(t, dt):
    return (-_dot(_dot(t, dt, TN, INV_PREC), t, NT, INV_PREC),)


_unit_lower_inverse.defvjp(_unit_lower_inverse_fwd, _unit_lower_inverse_bwd)


def _dn_step(st, qd, kd, u, w, attn, egl):
    v_new = u - _dot(w, st)
    o = _dot(qd, st) + _dot(attn, v_new)
    st_new = st * egl[0:1, :] + _dot(kd, v_new, TN)
    return o, st_new


def _dn_chunk_specs(m=1):
    c = DN_CHUNK
    wide = pl.BlockSpec((m * c, DN_WIDTH), lambda i: (i, 0))
    att = pl.BlockSpec((m * c, DN_HEADS * c), lambda i: (i, 0))
    egl = pl.BlockSpec((m * 8, DN_WIDTH), lambda i: (i, 0))
    return wide, att, egl


def _dn_intra_chunks(nc):
    return 2 if nc % 2 == 0 else 1


def dn_intra_fwd(qkv, gb, bb, name):
    s = qkv.shape[0]
    c, hd = DN_CHUNK, DN_HEAD_DIM
    nc = s // c
    m = _dn_intra_chunks(nc)

    def body(q_ref, k_ref, v_ref, g_ref, b_ref, u_ref, w_ref, qd_ref, kd_ref, at_ref, eg_ref):
        for t in range(m):
            rs = slice(t * c, (t + 1) * c)
            for h in range(DN_HEADS):
                hs = slice(h * hd, (h + 1) * hd)
                u, w, qd, kd, at, eg = _dn_intra(q_ref[rs, hs], k_ref[rs, hs], v_ref[rs, hs], g_ref[h, rs], b_ref[h, rs])
                u_ref[rs, hs], w_ref[rs, hs], qd_ref[rs, hs], kd_ref[rs, hs] = u, w, qd, kd
                at_ref[rs, h * c:(h + 1) * c] = at
                eg_ref[t * 8:(t + 1) * 8, hs] = eg

    wide, att, egl = _dn_chunk_specs(m)
    hsp = pl.BlockSpec((DN_HEADS, m * c, 128), lambda i: (0, i, 0))
    wsd = _sds((s, DN_WIDTH), F32)
    return _call(body, name, (nc // m,),
                 [pl.BlockSpec((m * c, DN_WIDTH), lambda i: (i, 0)), pl.BlockSpec((m * c, DN_WIDTH), lambda i: (i, 1)),
                  pl.BlockSpec((m * c, DN_WIDTH), lambda i: (i, 2)), hsp, hsp],
                 [wide, wide, wide, wide, att, egl],
                 [wsd, wsd, wsd, wsd, _sds((s, DN_HEADS * c), F32), _sds((nc * 8, DN_WIDTH), F32)],
                 sem=("parallel",))(qkv, qkv, qkv, gb, bb)


def dn_intra_bwd(qkv, gb, bb, cts, name):
    s = qkv.shape[0]
    c, hd = DN_CHUNK, DN_HEAD_DIM
    nc = s // c
    m = _dn_intra_chunks(nc)

    def body(q_ref, k_ref, v_ref, g_ref, b_ref, du_ref, dw_ref, dqd_ref, dkd_ref, dat_ref, deg_ref,
             dq_ref, dk_ref, dv_ref, dg_ref, db_ref):
        for t in range(m):
            rs = slice(t * c, (t + 1) * c)
            for h in range(DN_HEADS):
                hs = slice(h * hd, (h + 1) * hd)
                _, vjp = jax.vjp(_dn_intra, q_ref[rs, hs], k_ref[rs, hs], v_ref[rs, hs], g_ref[h, rs], b_ref[h, rs])
                dq, dk, dv, dg, db = vjp((du_ref[rs, hs], dw_ref[rs, hs], dqd_ref[rs, hs], dkd_ref[rs, hs],
                                          dat_ref[rs, h * c:(h + 1) * c], deg_ref[t * 8:(t + 1) * 8, hs]))
                dq_ref[rs, hs], dk_ref[rs, hs], dv_ref[rs, hs] = dq, dk, dv
                dg_ref[h, rs] = dg
                db_ref[h, rs] = db

    wide, att, egl = _dn_chunk_specs(m)
    hsp = pl.BlockSpec((DN_HEADS, m * c, 128), lambda i: (0, i, 0))
    hsd = _sds((DN_HEADS, s, 128), F32)
    wsd = _sds((s, DN_WIDTH), F32)
    return _call(body, name, (nc // m,),
                 [pl.BlockSpec((m * c, DN_WIDTH), lambda i: (i, 0)), pl.BlockSpec((m * c, DN_WIDTH), lambda i: (i, 1)),
                  pl.BlockSpec((m * c, DN_WIDTH), lambda i: (i, 2)), hsp, hsp, wide, wide, wide, wide, att, egl],
                 [wide, wide, wide, hsp, hsp], [wsd, wsd, wsd, hsd, hsd], sem=("parallel",))(qkv, qkv, qkv, gb, bb, *cts)


def dn_scan_fwd(u, w, qd, kd, at, eg, name):
    s = u.shape[0]
    c, hd = DN_CHUNK, DN_HEAD_DIM
    nc = s // c

    def body(u_ref, w_ref, qd_ref, kd_ref, at_ref, eg_ref, o_ref, st_ref, state):
        @pl.when(pl.program_id(0) == 0)
        def _():
            state[...] = jnp.zeros_like(state)

        for h in range(DN_HEADS):
            hs = slice(h * hd, (h + 1) * hd)
            st = state[h]
            st_ref[h] = st
            o, st_new = _dn_step(st, qd_ref[:, hs], kd_ref[:, hs], u_ref[:, hs], w_ref[:, hs],
                                 at_ref[:, h * c:(h + 1) * c], eg_ref[:, hs])
            o_ref[:, hs] = o
            state[h] = st_new

    wide, att, egl = _dn_chunk_specs()
    return _call(body, name, (nc,), [wide, wide, wide, wide, att, egl],
                 [wide, pl.BlockSpec((None, DN_HEADS, hd, hd), lambda i: (i, 0, 0, 0))],
                 [_sds((s, DN_WIDTH), F32), _sds((nc, DN_HEADS, hd, hd), F32)],
                 scratch=[pltpu.VMEM((DN_HEADS, hd, hd), F32)])(u, w, qd, kd, at, eg)


def dn_scan_bwd(u, w, qd, kd, at, eg, states, do, name):
    s = u.shape[0]
    c, hd = DN_CHUNK, DN_HEAD_DIM
    nc = s // c

    def body(u_ref, w_ref, qd_ref, kd_ref, at_ref, eg_ref, st_ref, do_ref,
             du_ref, dw_ref, dqd_ref, dkd_ref, dat_ref, deg_ref, dstate):
        @pl.when(pl.program_id(0) == 0)
        def _():
            dstate[...] = jnp.zeros_like(dstate)

        for h in range(DN_HEADS):
            hs = slice(h * hd, (h + 1) * hd)
            asl = slice(h * c, (h + 1) * c)
            _, vjp = jax.vjp(_dn_step, st_ref[h], qd_ref[:, hs], kd_ref[:, hs], u_ref[:, hs], w_ref[:, hs],
                             at_ref[:, asl], eg_ref[:, hs])
            dst, dqd, dkd, du, dw, dat, deg = vjp((do_ref[:, hs], dstate[h]))
            dstate[h] = dst
            du_ref[:, hs], dw_ref[:, hs], dqd_ref[:, hs], dkd_ref[:, hs] = du, dw, dqd, dkd
            dat_ref[:, asl] = dat
            deg_ref[:, hs] = deg

    rev = lambda i: nc - 1 - i
    wide = pl.BlockSpec((c, DN_WIDTH), lambda i: (rev(i), 0))
    att = pl.BlockSpec((c, DN_HEADS * c), lambda i: (rev(i), 0))
    egl = pl.BlockSpec((8, DN_WIDTH), lambda i: (rev(i), 0))
    wsd = _sds((s, DN_WIDTH), F32)
    return _call(body, name, (nc,),
                 [wide, wide, wide, wide, att, egl, pl.BlockSpec((None, DN_HEADS, hd, hd), lambda i: (rev(i), 0, 0, 0)), wide],
                 [wide, wide, wide, wide, att, egl],
                 [wsd, wsd, wsd, wsd, _sds((s, DN_HEADS * c), F32), _sds((nc * 8, DN_WIDTH), F32)],
                 scratch=[pltpu.VMEM((DN_HEADS, hd, hd), F32)])(u, w, qd, kd, at, eg, states, do)


def _dn_out(o, zg, nw):
    n = o * lax.rsqrt(jnp.mean(o * o, axis=-1, keepdims=True) + NORM_EPS) * nw
    return n * (zg * _sigmoid(zg))


def dn_out_fwd(o, z, nw, name):
    s = o.shape[0]
    tm = _tile(s, 512)
    hd = DN_HEAD_DIM

    def body(o_ref, zg_ref, nw_ref, y_ref):
        for h in range(DN_HEADS):
            hs = slice(h * hd, (h + 1) * hd)
            y_ref[:, hs] = _dn_out(o_ref[:, hs], zg_ref[:, hs], nw_ref[...]).astype(BF16)

    return _call(body, name, (s // tm,),
                 [pl.BlockSpec((tm, DN_WIDTH), lambda i: (i, 0)), pl.BlockSpec((tm, DN_WIDTH), lambda i: (i, Z_ZG // DN_WIDTH)),
                  pl.BlockSpec((1, hd), lambda i: (0, 0))],
                 pl.BlockSpec((tm, DN_WIDTH), lambda i: (i, 0)), _sds((s, DN_WIDTH), BF16), sem=("parallel",))(o, z, nw)


def dn_out_bwd(o, z, nw, dycat, name):
    s = o.shape[0]
    tm = _tile(s, 512)
    hd = DN_HEAD_DIM

    def body(o_ref, zg_ref, nw_ref, dy_ref, do_ref, dzg_ref, dnw_ref):
        @pl.when(pl.program_id(0) == 0)
        def _():
            dnw_ref[...] = jnp.zeros_like(dnw_ref)

        for h in range(DN_HEADS):
            hs = slice(h * hd, (h + 1) * hd)
            _, vjp = jax.vjp(_dn_out, o_ref[:, hs], zg_ref[:, hs], nw_ref[...])
            do, dzg, dnw = vjp(dy_ref[:, hs])
            do_ref[:, hs] = do
            dzg_ref[:, hs] = dzg
            dnw_ref[...] += dnw

    wide = pl.BlockSpec((tm, DN_WIDTH), lambda i: (i, 0))
    wsd = _sds((s, DN_WIDTH), F32)
    return _call(body, name, (s // tm,),
                 [wide, pl.BlockSpec((tm, DN_WIDTH), lambda i: (i, Z_ZG // DN_WIDTH)), pl.BlockSpec((1, hd), lambda i: (0, 0)),
                  pl.BlockSpec((tm, DN_WIDTH), lambda i: (i, ATTN_WIDTH // DN_WIDTH))],
                 [wide, wide, pl.BlockSpec((1, hd), lambda i: (0, 0))], [wsd, wsd, _sds((1, hd), F32)])(o, z, nw, dycat)


def _s5_param_fn(are, aim, ldt, bre, bim):
    dt = jnp.exp(ldt)
    er = jnp.exp(are * dt)
    abr = er * jnp.cos(aim * dt)
    abi = er * jnp.sin(aim * dt)
    den = are * are + aim * aim
    cr = ((abr - 1.0) * are + abi * aim) / den
    ci = (abi * are - (abr - 1.0) * aim) / den
    return abr, abi, cr * bre - ci * bim, cr * bim + ci * bre


def s5_params_fwd(are, aim, ldt, bre, bim, name):
    p, hh = bre.shape

    def body(a_ref, b_ref, c_ref, d_ref, e_ref, o1, o2, o3, o4):
        o1[...], o2[...], o3[...], o4[...] = _s5_param_fn(a_ref[...], b_ref[...], c_ref[...], d_ref[...], e_ref[...])

    col = pl.BlockSpec((p, 1), lambda: (0, 0))
    mat = pl.BlockSpec((p, hh), lambda: (0, 0))
    return _call(body, name, (), [col, col, col, mat, mat], [col, col, mat, mat],
                 [_sds((p, 1), F32), _sds((p, 1), F32), _sds((p, hh), F32), _sds((p, hh), F32)])(are, aim, ldt, bre, bim)


def s5_params_bwd(are, aim, ldt, bre, bim, cts, name):
    p, hh = bre.shape

    def body(a_ref, b_ref, c_ref, d_ref, e_ref, g1, g2, g3, g4, o1, o2, o3, o4, o5):
        _, vjp = jax.vjp(_s5_param_fn, a_ref[...], b_ref[...], c_ref[...], d_ref[...], e_ref[...])
        o1[...], o2[...], o3[...], o4[...], o5[...] = vjp((g1[...], g2[...], g3[...], g4[...]))

    col = pl.BlockSpec((p, 1), lambda: (0, 0))
    mat = pl.BlockSpec((p, hh), lambda: (0, 0))
    csd, msd = _sds((p, 1), F32), _sds((p, hh), F32)
    return _call(body, name, (), [col, col, col, mat, mat, col, col, mat, mat], [col, col, col, mat, mat],
                 [csd, csd, csd, msd, msd])(are, aim, ldt, bre, bim, *cts)


def _cmul(ar, ai, br, bi):
    return ar * br - ai * bi, ar * bi + ai * br


def _s5_scan_block(xr, xi, ar, ai, n, reverse):
    row = lax.broadcasted_iota(jnp.int32, xr.shape, 0)
    d = 1
    while d < n:
        if reverse:
            sr, si = _shift_up(xr, d, row, n), _shift_up(xi, d, row, n)
        else:
            sr, si = _shift_down(xr, d, row), _shift_down(xi, d, row)
        pr, pi = _cmul(ar, ai, sr, si)
        xr, xi = xr + pr, xi + pi
        ar, ai = _cmul(ar, ai, ar, ai)
        d *= 2
    return xr, xi


def s5_scan_fwd(bu, abr, abi, name):
    s = bu.shape[0]
    npb = S5_P // 128

    def body(br_ref, bi_ref, ar_ref, ai_ref, x_ref):
        x_ref[0], x_ref[1] = _s5_scan_block(br_ref[...], bi_ref[...], ar_ref[...], ai_ref[...], s, False)

    re = pl.BlockSpec((s, 128), lambda j: (0, j))
    im = pl.BlockSpec((s, 128), lambda j: (0, npb + j))
    av = pl.BlockSpec((1, 128), lambda j: (0, j))
    return _call(body, name, (npb,), [re, im, av, av], pl.BlockSpec((2, s, 128), lambda j: (0, 0, j)),
                 _sds((2, s, S5_P), F32), sem=("parallel",))(bu, bu, abr, abi)


def s5_scan_bwd(dx, x, abr, abi, name):
    s = dx.shape[0]
    npb = S5_P // 128

    def body(dr_ref, di_ref, x_ref, ar_ref, ai_ref, g_ref, dar_ref, dai_ref):
        ar, ai = ar_ref[...], ai_ref[...]
        gr, gi = _s5_scan_block(dr_ref[...], di_ref[...], ar, -ai, s, True)
        g_ref[0], g_ref[1] = gr, gi
        row = lax.broadcasted_iota(jnp.int32, gr.shape, 0)
        pr, pi = _shift_down(x_ref[0], 1, row), _shift_down(x_ref[1], 1, row)
        dar_ref[...] = jnp.sum(gr * pr + gi * pi, axis=0, keepdims=True)
        dai_ref[...] = jnp.sum(gi * pr - gr * pi, axis=0, keepdims=True)

    re = pl.BlockSpec((s, 128), lambda j: (0, j))
    im = pl.BlockSpec((s, 128), lambda j: (0, npb + j))
    av = pl.BlockSpec((1, 128), lambda j: (0, j))
    planes = pl.BlockSpec((2, s, 128), lambda j: (0, 0, j))
    asd = _sds((1, S5_P), F32)
    return _call(body, name, (npb,), [re, im, planes, av, av], [planes, av, av], [_sds((2, s, S5_P), F32), asd, asd],
                 sem=("parallel",))(dx, dx, x, abr, abi)


def _gelu(y):
    return 0.5 * y * (1.0 + jnp.tanh(math.sqrt(2.0 / math.pi) * (y + 0.044715 * y * y * y)))


def s5_out_fwd(ypre, z, dvec, glu_w, glu_b, name):
    s = ypre.shape[0]
    tm = _tile(s, 512)
    wd = S5_WIDTH

    def body(yp_ref, u_ref, d_ref, w_ref, b_ref, y_ref, o_ref):
        y = yp_ref[...] + d_ref[...] * u_ref[...]
        y_ref[...] = y
        g = _gelu(y)
        t = _dot(g.astype(BF16), w_ref[...]) + b_ref[...]
        o_ref[...] = (g * _sigmoid(t)).astype(BF16)

    row = pl.BlockSpec((tm, wd), lambda i: (i, 0))
    vec = pl.BlockSpec((1, wd), lambda i: (0, 0))
    return _call(body, name, (s // tm,),
                 [row, pl.BlockSpec((tm, wd), lambda i: (i, Z_S5 // wd)), vec, pl.BlockSpec((wd, wd), lambda i: (0, 0)), vec],
                 [row, row], [_sds((s, wd), F32), _sds((s, wd), BF16)], sem=("parallel",))(ypre, z, dvec, glu_w, glu_b)


def s5_out_bwd(y, z, glu_w, glu_b, dycat, name):
    s = y.shape[0]
    tm = _tile(s, 512)
    wd = S5_WIDTH

    def body(y_ref, u_ref, w_ref, b_ref, do_ref, dy_ref, dd_ref, dw_ref, db_ref):
        @pl.when(pl.program_id(0) == 0)
        def _():
            dd_ref[...] = jnp.zeros_like(dd_ref)
            dw_ref[...] = jnp.zeros_like(dw_ref)
            db_ref[...] = jnp.zeros_like(db_ref)

        g, gvjp = jax.vjp(_gelu, y_ref[...])
        gb = g.astype(BF16)
        sg = _sigmoid(_dot(gb, w_ref[...]) + b_ref[...])
        do = do_ref[...]
        dt = do * g * sg * (1.0 - sg)
        dtb = dt.astype(BF16)
        dg = do * sg + _dot(dtb, w_ref[...], NT)
        (dy,) = gvjp(dg)
        dy_ref[...] = dy
        dd_ref[...] += jnp.sum(dy * u_ref[...], axis=0, keepdims=True)
        dw_ref[...] += _dot(gb, dtb, TN)
        db_ref[...] += jnp.sum(dt, axis=0, keepdims=True)

    row = pl.BlockSpec((tm, wd), lambda i: (i, 0))
    vec = pl.BlockSpec((1, wd), lambda i: (0, 0))
    mat = pl.BlockSpec((wd, wd), lambda i: (0, 0))
    return _call(body, name, (s // tm,),
                 [row, pl.BlockSpec((tm, wd), lambda i: (i, Z_S5 // wd)), mat, vec,
                  pl.BlockSpec((tm, wd), lambda i: (i, (ATTN_WIDTH + DN_WIDTH) // wd))],
                 [row, vec, mat, vec], [_sds((s, wd), F32), _sds((1, wd), F32), _sds((wd, wd), F32), _sds((1, wd), F32)])(
        y, z, glu_w, glu_b, dycat)


def assemble_dz(dq, dkc, dkp, dvc, dvp, ddn, dzg, dus, dys, dvec, dzs, name):
    s = dq.shape[0]
    w = WINDOW
    nb = s // w
    nxt = lambda i: jnp.minimum(i + 1, nb - 1)

    def body(dq_ref, dkc_ref, dkp_ref, dvc_ref, dvp_ref, ddn_ref, dzg_ref, dus_ref, dys_ref, dv_ref, dzs_ref, o_ref):
        live = (pl.program_id(0) < nb - 1).astype(F32)
        o_ref[:, Z_Q:Z_K] = dq_ref[...].astype(BF16)
        o_ref[:, Z_K:Z_V] = (dkc_ref[...] + live * dkp_ref[...]).astype(BF16)
        o_ref[:, Z_V:Z_DN] = (dvc_ref[...] + live * dvp_ref[...]).astype(BF16)
        o_ref[:, Z_DN:Z_ZG] = ddn_ref[...].astype(BF16)
        o_ref[:, Z_ZG:Z_S5] = dzg_ref[...].astype(BF16)
        o_ref[:, Z_S5:Z_SM] = (dus_ref[...] + dv_ref[...] * dys_ref[...]).astype(BF16)
        o_ref[:, Z_SM:Z_ALL] = dzs_ref[...].astype(BF16)

    def blk(width, f=lambda i: i):
        return pl.BlockSpec((w, width), lambda i: (f(i), 0))

    return _call(body, name, (nb,),
                 [blk(ATTN_WIDTH), blk(ATTN_KV_WIDTH), blk(ATTN_KV_WIDTH, nxt), blk(ATTN_KV_WIDTH), blk(ATTN_KV_WIDTH, nxt),
                  blk(3 * DN_WIDTH), blk(DN_WIDTH), blk(S5_WIDTH), blk(S5_WIDTH), pl.BlockSpec((1, S5_WIDTH), lambda i: (0, 0)),
                  blk(128)],
                 blk(Z_ALL), _sds((s, Z_ALL), BF16), sem=("parallel",))(dq, dkc, dkp, dvc, dvp, ddn, dzg, dus, dys, dvec, dzs)


def _my_place():
    return lax.axis_index("x"), lax.axis_index("y"), lax.axis_index("c")


def _peer(place, p):
    x, y, c = place
    px = 1 - x if p & 4 else x
    py = 1 - y if p & 2 else y
    pc = 1 - c if p & 1 else c
    return (px, py, pc), 4 * px + 2 * py + pc


def exchange(arrays, scatter, name):
    na = len(arrays)

    def body(*refs):
        srcs, dsts = refs[:na], refs[na:2 * na]
        send_sems, recv_sems, local_sems = refs[2 * na:]
        place = _my_place()
        me = 4 * place[0] + 2 * place[1] + place[2]
        copies = []
        for k in range(na):
            mine = srcs[k].at[me] if scatter else srcs[k]
            loc = pltpu.make_async_copy(mine, dsts[k].at[me], local_sems.at[k])
            loc.start()
            copies.append(loc)
        sends = []
        for p in range(1, N_DEV):
            peer, pid = _peer(place, p)
            for k in range(na):
                src = srcs[k].at[pid] if scatter else srcs[k]
                cp = pltpu.make_async_remote_copy(src_ref=src, dst_ref=dsts[k].at[me], send_sem=send_sems.at[k, p - 1],
                                                  recv_sem=recv_sems.at[k, p - 1], device_id=peer,
                                                  device_id_type=pl.DeviceIdType.MESH)
                cp.start()
                sends.append(cp)
        for p in range(1, N_DEV):
            peer, pid = _peer(place, p)
            for k in range(na):
                src = srcs[k].at[me] if scatter else srcs[k]
                pltpu.make_async_remote_copy(src_ref=src, dst_ref=dsts[k].at[pid], send_sem=send_sems.at[k, p - 1],
                                             recv_sem=recv_sems.at[k, p - 1], device_id=peer,
                                             device_id_type=pl.DeviceIdType.MESH).wait_recv()
        for cp in sends:
            cp.wait_send()
        for cp in copies:
            cp.wait()

    outs = [_sds((N_DEV,) + tuple(a.shape[1:] if scatter else a.shape), a.dtype) for a in arrays]
    anyspec = pl.BlockSpec(memory_space=pl.ANY)
    return pl.pallas_call(
        body, name=name, in_specs=[anyspec] * na, out_specs=[anyspec] * na, out_shape=outs,
        scratch_shapes=[pltpu.SemaphoreType.DMA((na, N_DEV - 1)), pltpu.SemaphoreType.DMA((na, N_DEV - 1)),
                        pltpu.SemaphoreType.DMA((na,))])(*arrays)


_HBM = pl.BlockSpec(memory_space=pltpu.HBM)
_SEM = pl.BlockSpec(memory_space=pltpu.SEMAPHORE)
_DATAFLOW = pltpu.SideEffectType.DATAFLOW_SIDE_EFFECTING


def _split_copies(srcs, lands, send_sems, recv_sems, scatter, arriving):
    place = _my_place()
    me = 4 * place[0] + 2 * place[1] + place[2]
    out = []
    for p in range(1, N_DEV):
        peer, pid = _peer(place, p)
        for k in range(len(srcs)):
            i = k * (N_DEV - 1) + p - 1
            src = srcs[k].at[pid] if scatter else srcs[k]
            dst = lands[k].at[pid] if arriving else lands[k].at[me]
            out.append(pltpu.make_async_remote_copy(src_ref=src, dst_ref=dst, send_sem=send_sems.at[i], recv_sem=recv_sems.at[i],
                                                    device_id=peer, device_id_type=pl.DeviceIdType.MESH))
    return out


def exchange_start(groups, scatter, name):
    arrays = [a for g in groups for a in g]
    na, ng = len(arrays), len(groups)
    first = [sum(len(g) for g in groups[:i]) for i in range(ng)]
    me = 4 * lax.axis_index("x") + 2 * lax.axis_index("y") + lax.axis_index("c")
    lands = []
    for a in arrays:
        own = lax.dynamic_index_in_dim(a, me, 0, keepdims=True) if scatter else a[None]
        shape = (N_DEV,) + tuple(own.shape[1:])
        land = lax.dynamic_update_slice(lax.empty(shape, a.dtype), own, (me,) + (0,) * (len(shape) - 1))
        lands.append(pltpu.with_memory_space_constraint(land, pltpu.HBM))
    srcs = [pltpu.with_memory_space_constraint(a, pltpu.HBM) for a in arrays]

    def body(*refs):
        src_refs, land_refs = refs[:na], refs[na:2 * na]
        sems = refs[2 * na:2 * na + 2 * ng]
        token = refs[-1]
        for i, g in enumerate(groups):
            sl = slice(first[i], first[i] + len(g))
            for send in _split_copies(src_refs[sl], land_refs[sl], sems[2 * i], sems[2 * i + 1], scatter, False):
                send.start()
        token[...] = jnp.zeros_like(token)

    sem_shapes = []
    for g in groups:
        sem_shapes += [pltpu.SemaphoreType.DMA((len(g) * (N_DEV - 1),))] * 2
    outs = pl.pallas_call(
        body, name=name,
        out_shape=(*sem_shapes, *[pltpu.HBM(a.shape, a.dtype) for a in srcs], *[pltpu.HBM(a.shape, a.dtype) for a in lands],
                   _sds((8, 128), F32)),
        in_specs=[_HBM] * (2 * na), out_specs=(*[_SEM] * (2 * ng), *[_HBM] * (2 * na), pl.BlockSpec(memory_space=pltpu.VMEM)),
        input_output_aliases={i: 2 * ng + i for i in range(2 * na)},
        compiler_params=pltpu.CompilerParams(has_side_effects=_DATAFLOW))(*srcs, *lands)
    src_out, land_out = outs[2 * ng:2 * ng + na], outs[2 * ng + na:2 * ng + 2 * na]
    handles = [(outs[2 * i], outs[2 * i + 1], src_out[first[i]:first[i] + len(g)], land_out[first[i]:first[i] + len(g)])
               for i, g in enumerate(groups)]
    return handles, outs[-1]


def exchange_wait(handle, scatter, name, after):
    send_sems, recv_sems, srcs, lands = handle
    na = len(srcs)

    def body(*refs):
        src_refs, land_refs = refs[:na], refs[na:2 * na]
        for send in _split_copies(src_refs, land_refs, refs[2 * na], refs[2 * na + 1], scatter, False):
            send.wait_send()
        for recv in _split_copies(src_refs, land_refs, refs[2 * na], refs[2 * na + 1], scatter, True):
            recv.wait_recv()

    outs = pl.pallas_call(
        body, name=name, out_shape=tuple(pltpu.HBM(a.shape, a.dtype) for a in (*srcs, *lands)),
        in_specs=[_HBM] * (2 * na) + [_SEM, _SEM, pl.BlockSpec(memory_space=pl.ANY)], out_specs=tuple([_HBM] * (2 * na)),
        input_output_aliases={i: i for i in range(2 * na)},
        compiler_params=pltpu.CompilerParams(has_side_effects=_DATAFLOW))(*srcs, *lands, send_sems, recv_sems, after)
    return list(outs[na:])


def _adamw(w, g, m, v):
    m = ADAM_B1 * m + (1.0 - ADAM_B1) * g
    v = ADAM_B2 * v + (1.0 - ADAM_B2) * (g * g)
    m_hat = m / (1.0 - ADAM_B1 ** ADAM_STEP)
    v_hat = v / (1.0 - ADAM_B2 ** ADAM_STEP)
    return -ADAM_LR * (m_hat / (jnp.sqrt(v_hat) + ADAM_EPS) + ADAM_WD * w), m, v


def reduce_adamw(parts, w4, m4, v4, layer, name, stacked=None):
    nl, r, c = w4.shape
    tr = _tile(r, 256)

    def body(p_ref, w_ref, m_ref, v_ref, *rest):
        g_ref, d_ref, nm_ref, nv_ref = rest[-4:]
        g = p_ref[0].astype(F32)
        for d in range(1, N_DEV):
            g = g + p_ref[d].astype(F32)
        g_ref[...] = g
        d_ref[...], nm_ref[...], nv_ref[...] = _adamw(w_ref[...], g, m_ref[...], v_ref[...])

    lay = pl.BlockSpec((None, tr, c), lambda i: (layer, i, 0))
    osd = _sds((nl, r, c), F32)
    ins = [parts, w4, m4, v4] + (list(stacked) if stacked is not None else [])
    specs = [pl.BlockSpec((N_DEV, tr, c), lambda i: (0, i, 0)), lay, lay, lay]
    specs += [pl.BlockSpec(memory_space=pl.ANY)] * (len(ins) - 4)
    return pl.pallas_call(
        body, name=name, grid=(r // tr,), in_specs=specs, out_specs=[lay, lay, lay, lay], out_shape=[osd, osd, osd, osd],
        input_output_aliases={4 + k: k for k in range(len(ins) - 4)},
        compiler_params=pltpu.CompilerParams(dimension_semantics=("parallel",), vmem_limit_bytes=VMEM_LIMIT))(*ins)


_SM_NAT = ATTN_WIDTH + 2 * ATTN_KV_WIDTH + 4 * DN_WIDTH


def _win_to_zall(w):
    pad = jnp.zeros((w.shape[0], 128 - 2 * DN_HEADS), w.dtype)
    return jnp.concatenate([w[:, :_SM_NAT], w[:, _SM_NAT + 2 * DN_HEADS:], w[:, _SM_NAT:_SM_NAT + 2 * DN_HEADS], pad], axis=1)


def _zall_to_win(g):
    return jnp.concatenate([g[:, :Z_ZG + DN_WIDTH], g[:, Z_SM:Z_SM + 2 * DN_HEADS], g[:, Z_S5:Z_SM]], axis=1)


def _block_diag(t):
    g, a, b = t.shape
    eye = jnp.eye(g, dtype=t.dtype)
    return (t[:, :, None, :] * eye[:, None, :, None]).reshape(g * a, g * b)


def _block_diag_extract(m, g):
    a, b = m.shape[0] // g, m.shape[1] // g
    eye = jnp.eye(g, dtype=m.dtype)
    return jnp.sum(m.reshape(g, a, g, b) * eye[:, None, :, None], axis=2)


def _rope_tables(s):
    half = HEAD_DIM // 2
    inv_freq = ROPE_THETA ** (-jnp.arange(half, dtype=F32) / half)
    ang = jnp.arange(s, dtype=F32)[:, None] * inv_freq[None, :]
    cos, sin = jnp.cos(ang), jnp.sin(ang)
    return jnp.concatenate([cos, cos], axis=1), jnp.concatenate([-sin, sin], axis=1)


def _row(v):
    return v.reshape(1, -1)


def _ffn_fwd(x, g_pre, g_post, weight, tag):
    h = rmsnorm_fwd(x, g_pre, f"{tag}_norm")
    wg, wu = weight(f"{tag}_w_gate", h), weight(f"{tag}_w_up", h)
    a, b, u = ffn_up(h, wg, wu, f"{tag}_up", dep=weight("token", None))
    wd = weight(f"{tag}_w_down", u)
    y, xn = down_norm(u, wd, x, g_post, FFN_RES_WEIGHT, f"{tag}_down")
    return xn, (x, h, a, b, u, y, wg, wu, wd)


def _ffn_bwd(dxn, saved, g_pre, g_post, on_grads, tag):
    x, h, a, b, u, y, wg, wu, wd = saved
    dy, dg_post = norm_bwd(dxn, y, g_post, FFN_RES_WEIGHT, None, BF16, f"{tag}_bnorm_post")
    da, db = ffn_down_bwd(dy, wd, a, b, f"{tag}_bdown")
    dwd = mm_tn(u, dy[None], BF16, f"{tag}_dwd")
    dwg = mm_tn(h[None], da, BF16, f"{tag}_dwg")
    dwu = mm_tn(h[None], db, BF16, f"{tag}_dwu")
    tok = on_grads({f"{tag}_w_gate": dwg, f"{tag}_w_up": dwu, f"{tag}_w_down": dwd})
    dh = mm_nt_acc([(da, wg), (db, wu)], f"{tag}_dh", tm=512, tn=2048, dep=tok)
    dx, dg_pre = norm_bwd(dh, x, g_pre, 1.0, dxn, F32, f"{tag}_bnorm_pre")
    return dx, dict(g_pre=dg_pre, g_post=dg_post)


def _s5_layouts(p):
    are, aim = p["s5_a_re"].reshape(S5_P, 1), p["s5_a_im"].reshape(S5_P, 1)
    ldt = jnp.repeat(p["s5_log_dt"], S5_STATE).reshape(S5_P, 1)
    bre, bim = p["s5_b_re"].reshape(S5_P, S5_GROUP_CH), p["s5_b_im"].reshape(S5_P, S5_GROUP_CH)
    return are, aim, ldt, bre, bim


def _mix_fwd(x, p, weight, cos_f, sin_s, tag):
    h = rmsnorm_fwd(x, _row(p["mix_norm_pre"]), f"{tag}_norm")
    w_all, glu_w = weight("w_all", h), weight("s5_glu_w", h)
    z = mm_nn(h, w_all, f"{tag}_win")
    y_attn = attn_fwd(z, cos_f, sin_s, p["attn_sinks"], f"{tag}_attn")
    alog_b = jnp.broadcast_to(p["dn_a_log"][:, None], (DN_HEADS, 128))
    dtb_b = jnp.broadcast_to(p["dn_dt_bias"][:, None], (DN_HEADS, 128))
    conv_w = weight("dn_conv_w", h)
    qkv = dn_pre_fwd(z, conv_w, f"{tag}_dnpre")
    bb, gb = dn_gates_fwd(z, alog_b, dtb_b, f"{tag}_dngate")
    u, w, qd, kd, at, eg = dn_intra_fwd(qkv, gb, bb, f"{tag}_dnintra")
    o, states = dn_scan_fwd(u, w, qd, kd, at, eg, f"{tag}_dnscan")
    y_dn = dn_out_fwd(o, z, _row(p["dn_norm_w"]), f"{tag}_dnout")
    s5cols = _s5_layouts(p)
    abr, abi, bbr, bbi = s5_params_fwd(*s5cols, f"{tag}_s5par")
    tb = lambda t: jnp.transpose(t.reshape(S5_GROUPS, S5_STATE, S5_GROUP_CH), (0, 2, 1))
    b_blk = jnp.concatenate([_block_diag(tb(bbr)), _block_diag(tb(bbi))], axis=1).astype(BF16)
    tc = lambda t: jnp.transpose(t, (0, 2, 1))
    c_blk = jnp.concatenate([_block_diag(tc(p["s5_c_re"])), -_block_diag(tc(p["s5_c_im"]))], axis=0).astype(BF16)
    bu = mm_nn(z, b_blk, f"{tag}_s5bu", tn=1024, col0=Z_S5)
    xs = s5_scan_fwd(bu, abr.reshape(1, S5_P), abi.reshape(1, S5_P), f"{tag}_s5scan")
    ypre = mm_nn_acc(xs, c_blk.reshape(2, S5_P, S5_WIDTH), f"{tag}_s5c")
    y5, y_s5 = s5_out_fwd(ypre, z, _row(p["s5_d"]), glu_w, _row(p["s5_glu_b"]), f"{tag}_s5out")
    ycat = jnp.concatenate([y_attn, y_dn, y_s5], axis=1)
    w_out = weight("w_out", ycat)
    mixed, xn = down_norm(ycat[None], w_out[None], x, _row(p["mix_norm_post"]), 1.0, f"{tag}_wout")
    saved = dict(x=x, h=h, z=z, qkv=qkv, bb=bb, gb=gb, dn=(u, w, qd, kd, at, eg), states=states, o=o, s5cols=s5cols,
                 abr=abr, abi=abi, b_blk=b_blk, c_blk=c_blk, xs=xs, y5=y5, ycat=ycat, mixed=mixed,
                 alog_b=alog_b, dtb_b=dtb_b, w_all=w_all, w_out=w_out, glu_w=glu_w, conv_w=conv_w)
    return xn, saved


def _mix_bwd(dxn, sv, p, on_grads, cos_f, sin_s, tag):
    z = sv["z"]
    w_all, w_out, glu_w = sv["w_all"], sv["w_out"], sv["glu_w"]
    g = {}
    dmixed, g["mix_norm_post"] = norm_bwd(dxn, sv["mixed"], _row(p["mix_norm_post"]), 1.0, None, BF16, f"{tag}_bnorm_post")
    g["w_out"] = mm_tn(sv["ycat"][None], dmixed[None], BF16, f"{tag}_dwout", tn=1024)[0]
    dycat = mm_nt_acc([(dmixed[None], w_out[None])], f"{tag}_dycat")
    dq, dkc, dkp, dvc, dvp, dsink = attn_bwd(z, cos_f, sin_s, p["attn_sinks"], dycat, f"{tag}_battn")
    g["attn_sinks"] = dsink[:, 0]
    do, dzg, dnw = dn_out_bwd(sv["o"], z, _row(p["dn_norm_w"]), dycat, f"{tag}_bdnout")
    g["dn_norm_w"] = dnw[0]
    cts = dn_scan_bwd(*sv["dn"], sv["states"], do, f"{tag}_bdnscan")
    dqn, dkn, dvn, dgb, dbb = dn_intra_bwd(sv["qkv"], sv["gb"], sv["bb"], cts, f"{tag}_bdnintra")
    dzs, dal, ddt = dn_gates_bwd(z, sv["alog_b"], sv["dtb_b"], dbb, dgb, f"{tag}_bdngate")
    g["dn_a_log"], g["dn_dt_bias"] = dal[:, 0], ddt[:, 0]
    ddn, g["dn_conv_w"] = dn_pre_bwd(z, sv["conv_w"], jnp.concatenate([dqn, dkn, dvn], axis=1), f"{tag}_bdnpre")
    dy5, dd, dglu, dglub = s5_out_bwd(sv["y5"], z, glu_w, _row(p["s5_glu_b"]), dycat, f"{tag}_bs5out")
    g["s5_d"], g["s5_glu_w"], g["s5_glu_b"] = dd[0], dglu, dglub[0]
    dxs = mm_nt_acc([(dy5[None], sv["c_blk"][None])], f"{tag}_bs5c", tn=1024)
    dc_blk = mm_tn(sv["xs"], dy5[None], F32, f"{tag}_ds5c", tk=256)
    ex = lambda m: jnp.transpose(_block_diag_extract(m, S5_GROUPS), (0, 2, 1))
    g["s5_c_re"], g["s5_c_im"] = ex(dc_blk[0]), -ex(dc_blk[1])
    dbu, dar, dai = s5_scan_bwd(dxs, sv["xs"], sv["abr"].reshape(1, S5_P), sv["abi"].reshape(1, S5_P), f"{tag}_bs5scan")
    b_planes = jnp.transpose(sv["b_blk"].reshape(S5_WIDTH, 2, S5_P), (1, 0, 2))
    dus = mm_nt_acc([(dbu, b_planes)], f"{tag}_bs5bu")
    u_s5 = z[:, Z_S5:Z_SM]
    db_blk = mm_tn(u_s5[None], dbu, F32, f"{tag}_ds5b", tn=1024)
    exb = lambda m: jnp.transpose(_block_diag_extract(m, S5_GROUPS), (0, 2, 1)).reshape(S5_P, S5_GROUP_CH)
    dcols = s5_params_bwd(*sv["s5cols"], (dar.reshape(S5_P, 1), dai.reshape(S5_P, 1), exb(db_blk[0]), exb(db_blk[1])),
                          f"{tag}_bs5par")
    g["s5_a_re"] = dcols[0].reshape(S5_GROUPS, S5_STATE)
    g["s5_a_im"] = dcols[1].reshape(S5_GROUPS, S5_STATE)
    g["s5_log_dt"] = jnp.sum(dcols[2].reshape(S5_GROUPS, S5_STATE), axis=1)
    g["s5_b_re"] = dcols[3].reshape(S5_GROUPS, S5_STATE, S5_GROUP_CH)
    g["s5_b_im"] = dcols[4].reshape(S5_GROUPS, S5_STATE, S5_GROUP_CH)
    dz = assemble_dz(dq, dkc, dkp, dvc, dvp, ddn, dzg, dus, dy5, _row(p["s5_d"]), dzs, f"{tag}_dz")
    g["w_all"] = mm_tn(sv["h"][None], dz[None], BF16, f"{tag}_dwin")[0]
    dwin = _zall_to_win(g.pop("w_all"))
    d_model = dwin.shape[0]
    tok = on_grads({"w_in": jnp.transpose(dwin.reshape(d_model, N_DEV, IN_WIDTH // N_DEV), (1, 0, 2)),
                    "s5_glu_w": g.pop("s5_glu_w").astype(BF16).reshape(N_DEV, S5_WIDTH // N_DEV, S5_WIDTH),
                    "w_out": g.pop("w_out").reshape(N_DEV, MIX_WIDTH // N_DEV, d_model)})
    dh = mm_nt_acc([(dz[None], w_all[None])], f"{tag}_dh", dep=tok)
    dx, g["mix_norm_pre"] = norm_bwd(dh, sv["x"], _row(p["mix_norm_pre"]), 1.0, dxn, F32, f"{tag}_bnorm_pre")
    return dx, g


BIG = ("ff1_w_gate", "ff1_w_up", "ff1_w_down", "w_in", "s5_glu_w", "w_out", "ff2_w_gate", "ff2_w_up", "ff2_w_down")
SMALL = ("ff1_norm_pre", "ff1_norm_post", "mix_norm_pre", "attn_sinks", "dn_conv_w", "dn_a_log", "dn_dt_bias", "dn_norm_w",
         "s5_a_re", "s5_a_im", "s5_log_dt", "s5_b_re", "s5_b_im", "s5_c_re", "s5_c_im", "s5_d", "s5_glu_b",
         "mix_norm_post", "ff2_norm_pre", "ff2_norm_post")
GATHER_GROUPS = (("ff1_w_gate", "ff1_w_up"), ("ff1_w_down",), ("w_in", "s5_glu_w", "dn_conv_w"), ("w_out",),
                 ("ff2_w_gate", "ff2_w_up"), ("ff2_w_down",))
WEIGHTS = ("ff1_norm_pre", "ff1_w_gate", "ff1_w_up", "ff1_w_down", "ff1_norm_post", "mix_norm_pre", "w_in", "attn_sinks",
           "dn_conv_w", "dn_a_log", "dn_dt_bias", "dn_norm_w", "s5_a_re", "s5_a_im", "s5_log_dt", "s5_b_re", "s5_b_im",
           "s5_c_re", "s5_c_im", "s5_d", "s5_glu_w", "s5_glu_b", "w_out", "mix_norm_post", "ff2_norm_pre", "ff2_w_gate",
           "ff2_w_up", "ff2_w_down", "ff2_norm_post")


def _pack(parts):
    rows = []
    for a in parts:
        n = a.size
        r = -(-n // 1024) * 8
        rows.append(jnp.pad(a.reshape(-1), (0, r * 128 - n)).reshape(r, 128))
    return jnp.concatenate(rows, axis=0)


def _unpack(mat, shapes):
    out, off = [], 0
    for shp in shapes:
        n = int(np.prod(shp))
        r = -(-n // 1024) * 8
        out.append(mat[off:off + r].reshape(-1)[:n].reshape(shp))
        off += r
    return out


def local_step(x, target, smalls, weight, on_grads):
    depth = len(smalls)
    cos_f, sin_s = _rope_tables(x.shape[0])
    xs = x
    saved = []
    for l in range(depth):
        p = smalls[l]
        wl = functools.partial(weight, l)
        xs, s1 = _ffn_fwd(xs, _row(p["ff1_norm_pre"]), _row(p["ff1_norm_post"]), wl, "ff1")
        xs, s2 = _mix_fwd(xs, p, wl, cos_f, sin_s, "mix")
        xs, s3 = _ffn_fwd(xs, _row(p["ff2_norm_pre"]), _row(p["ff2_norm_post"]), wl, "ff2")
        saved.append((s1, s2, s3))

    loss_vec, dx = loss_and_grad(xs, target, "loss")

    small_g = [None] * depth
    for l in reversed(range(depth)):
        p = smalls[l]
        gl = functools.partial(on_grads, l)
        s1, s2, s3 = saved[l]
        dx, g3 = _ffn_bwd(dx, s3, _row(p["ff2_norm_pre"]), _row(p["ff2_norm_post"]), gl, "ff2")
        dx, g2 = _mix_bwd(dx, s2, p, gl, cos_f, sin_s, "mix")
        dx, g1 = _ffn_bwd(dx, s1, _row(p["ff1_norm_pre"]), _row(p["ff1_norm_post"]), gl, "ff1")
        sg = {n: g2[n] for n in SMALL if n in g2}
        sg.update(ff1_norm_pre=g1["g_pre"][0], ff1_norm_post=g1["g_post"][0], ff2_norm_pre=g3["g_pre"][0], ff2_norm_post=g3["g_post"][0],
                  mix_norm_pre=g2["mix_norm_pre"][0], mix_norm_post=g2["mix_norm_post"][0])
        small_g[l] = sg
    return loss_vec, dx, small_g


def kernel(x, ff1_norm_pre, ff1_w_gate, ff1_w_up, ff1_w_down, ff1_norm_post, mix_norm_pre, w_in, attn_sinks, dn_conv_w, dn_a_log, dn_dt_bias, dn_norm_w, s5_a_re, s5_a_im, s5_log_dt, s5_b_re, s5_b_im, s5_c_re, s5_c_im, s5_d, s5_glu_w, s5_glu_b, w_out, mix_norm_post, ff2_norm_pre, ff2_w_gate, ff2_w_up, ff2_w_down, ff2_norm_post, loss_target, m_ff1_norm_pre, m_ff1_w_gate, m_ff1_w_up, m_ff1_w_down, m_ff1_norm_post, m_mix_norm_pre, m_w_in, m_attn_sinks, m_dn_conv_w, m_dn_a_log, m_dn_dt_bias, m_dn_norm_w, m_s5_a_re, m_s5_a_im, m_s5_log_dt, m_s5_b_re, m_s5_b_im, m_s5_c_re, m_s5_c_im, m_s5_d, m_s5_glu_w, m_s5_glu_b, m_w_out, m_mix_norm_post, m_ff2_norm_pre, m_ff2_w_gate, m_ff2_w_up, m_ff2_w_down, m_ff2_norm_post, v_ff1_norm_pre, v_ff1_w_gate, v_ff1_w_up, v_ff1_w_down, v_ff1_norm_post, v_mix_norm_pre, v_w_in, v_attn_sinks, v_dn_conv_w, v_dn_a_log, v_dn_dt_bias, v_dn_norm_w, v_s5_a_re, v_s5_a_im, v_s5_log_dt, v_s5_b_re, v_s5_b_im, v_s5_c_re, v_s5_c_im, v_s5_d, v_s5_glu_w, v_s5_glu_b, v_w_out, v_mix_norm_post, v_ff2_norm_pre, v_ff2_w_gate, v_ff2_w_up, v_ff2_w_down, v_ff2_norm_post):
    args = dict(locals())
    W = {n: args[n] for n in WEIGHTS}
    M = {n: args["m_" + n] for n in WEIGHTS}
    V = {n: args["v_" + n] for n in WEIGHTS}
    depth = ff1_norm_pre.shape[0]
    d_model = x.shape[2]
    me = 4 * lax.axis_index("x") + 2 * lax.axis_index("y") + lax.axis_index("c")

    conv_sh = dn_conv_w.shape[2]

    def small_params(l):
        return {n: W[n][l] for n in SMALL if n != "dn_conv_w"}

    gathered_w, gather_handles, tokens = {}, {}, []
    group_of = {n: i for i, grp in enumerate(GATHER_GROUPS) for n in grp}

    def start_gather(l):
        shards = {n: cast_bf16(W[n], l, f"cast_{n}") for n in BIG}
        shards["dn_conv_w"] = dn_conv_w[l]
        handles, tok = exchange_start([[shards[n] for n in grp] for grp in GATHER_GROUPS], False, f"gather_start_l{l}")
        gather_handles.update({(l, i): h for i, h in enumerate(handles)})
        tokens.append(tok)

    def weight(l, name, after):
        if name == "token":
            return tokens.pop() if tokens else None
        key = "w_in" if name == "w_all" else name
        i = group_of[key]
        if (l, i) in gather_handles:
            got = dict(zip(GATHER_GROUPS[i], exchange_wait(gather_handles.pop((l, i)), False, f"gather_wait_l{l}_g{i}", after)))
            if i == 0 and l + 1 < depth:
                start_gather(l + 1)
            if "w_in" in got:
                got["dn_conv_w"] = jnp.transpose(got["dn_conv_w"], (1, 0, 2)).reshape(DN_CONV, N_DEV * conv_sh)
                got["w_all"] = _win_to_zall(jnp.transpose(got["w_in"], (1, 0, 2)).reshape(d_model, IN_WIDTH))
                got["s5_glu_w"] = got["s5_glu_w"].reshape(S5_WIDTH, S5_WIDTH)
            if "w_out" in got:
                got["w_out"] = got["w_out"].reshape(MIX_WIDTH, d_model)
            gathered_w.update({(l, n): a for n, a in got.items()})
        return gathered_w[(l, name)]

    stacked = {n: None for n in BIG}
    in_flight = []

    def finish_scatter(after):
        l, names, handle = in_flight.pop(0)
        recv = dict(zip(names, exchange_wait(handle, True, f"scatter_wait_l{l}_{names[0]}", after)))
        for n in names:
            stacked[n] = reduce_adamw(recv[n], W[n], M[n], V[n], l, f"adamw_{n}", stacked[n])

    def on_grads(l, grads):
        names = tuple(grads)
        (handle,), tok = exchange_start([[grads[n] for n in names]], True, f"scatter_start_l{l}_{names[0]}")
        in_flight.append((l, names, handle))
        if len(in_flight) > 1:
            finish_scatter(tok)
        return tok

    start_gather(0)
    tokens.clear()
    smalls = [small_params(l) for l in range(depth)]
    loss_vec, dx, small_g = local_step(x[0], loss_target[0], smalls, weight, on_grads)
    shapes = [(depth,) + ((DN_CONV, N_DEV * conv_sh) if n == "dn_conv_w" else tuple(W[n].shape[1:])) for n in SMALL]
    packed = _pack([jnp.stack([small_g[l][n] for l in range(depth)]) for n in SMALL])
    (small_handle,), small_tok = exchange_start([[packed]], False, "gather_small_start")
    while in_flight:
        finish_scatter(small_tok)
    gathered = exchange_wait(small_handle, False, "gather_small_wait", stacked[BIG[0]][0])[0]
    loss = lax.psum(loss_vec[0, 0], ("x", "y", "c"))

    def shard_of(n, full):
        return lax.dynamic_slice_in_dim(full, me * conv_sh, conv_sh, axis=2) if n == "dn_conv_w" else full

    conv_pad = lambda t: jnp.tile(t, (1, 1, N_DEV))
    wp = _pack([conv_pad(W[n]) if n == "dn_conv_w" else W[n] for n in SMALL])
    mp = _pack([conv_pad(M[n]) if n == "dn_conv_w" else M[n] for n in SMALL])
    vp = _pack([conv_pad(V[n]) if n == "dn_conv_w" else V[n] for n in SMALL])
    sm = reduce_adamw(gathered, wp[None], mp[None], vp[None], 0, "adamw_small")
    small_out = [dict(zip(SMALL, [shard_of(n, t) for n, t in zip(SMALL, _unpack(o[0], shapes))])) for o in sm]

    outs = []
    for kind in range(4):
        for n in WEIGHTS:
            if n in BIG:
                outs.append(stacked[n][kind])
            else:
                outs.append(small_out[kind][n])
    return (loss, dx[None], *outs)
```

```python
import functools
import math

import jax
import jax.numpy as jnp
import numpy as np
from jax import lax
from jax.experimental import pallas as pl
from jax.experimental.pallas import tpu as pltpu

F32 = jnp.float32
BF16 = jnp.bfloat16

N_DEV = 8
DEPTH = 4
ATTN_HEADS = 8
ATTN_KV_HEADS = 2
HEAD_DIM = 128
WINDOW = 128
ROPE_THETA = 10000.0
DN_HEADS = 4
DN_HEAD_DIM = 128
DN_CONV = 4
DN_CHUNK = 64
S5_GROUPS = 32
S5_GROUP_CH = 16
S5_STATE = 64
ATTN_WIDTH = ATTN_HEADS * HEAD_DIM
ATTN_KV_WIDTH = ATTN_KV_HEADS * HEAD_DIM
DN_WIDTH = DN_HEADS * DN_HEAD_DIM
S5_WIDTH = S5_GROUPS * S5_GROUP_CH
S5_P = S5_GROUPS * S5_STATE
MIX_WIDTH = ATTN_WIDTH + DN_WIDTH + S5_WIDTH
IN_WIDTH = ATTN_WIDTH + 2 * ATTN_KV_WIDTH + 4 * DN_WIDTH + 2 * DN_HEADS + S5_WIDTH
Z_Q, Z_K, Z_V = 0, ATTN_WIDTH, ATTN_WIDTH + ATTN_KV_WIDTH
Z_DN = ATTN_WIDTH + 2 * ATTN_KV_WIDTH
Z_ZG = Z_DN + 3 * DN_WIDTH
Z_S5 = Z_ZG + DN_WIDTH
Z_SM = Z_S5 + S5_WIDTH
Z_ALL = Z_SM + 128
FFN_RES_WEIGHT = 0.5
NORM_EPS = 1e-6
ADAM_LR, ADAM_B1, ADAM_B2, ADAM_EPS, ADAM_WD, ADAM_STEP = 0.001, 0.9, 0.999, 1e-08, 0.01, 10

VMEM_LIMIT = 56 * 1024 * 1024
NEG = -1e30

NN = (((1,), (0,)), ((), ()))
NT = (((1,), (1,)), ((), ()))
TN = (((0,), (0,)), ((), ()))


def _dot(a, b, dims=NN, prec=None):
    return lax.dot_general(a, b, dims, preferred_element_type=F32, precision=prec)


def _tile(n, pref, mult=8):
    if n <= pref:
        return n
    t = (pref // mult) * mult
    while t > mult and n % t:
        t -= mult
    assert n % t == 0, (n, pref)
    return t


def _call(body, name, grid, in_specs, out_specs, out_shape, scratch=(), sem=None):
    if sem is None:
        sem = ("arbitrary",) * len(grid)
    return pl.pallas_call(
        body, name=name, grid=grid, in_specs=in_specs, out_specs=out_specs, out_shape=out_shape,
        scratch_shapes=list(scratch),
        compiler_params=pltpu.CompilerParams(dimension_semantics=sem, vmem_limit_bytes=VMEM_LIMIT))


def _sds(shape, dtype):
    return jax.ShapeDtypeStruct(tuple(shape), dtype)


def _sigmoid(x):
    return 1.0 / (1.0 + jnp.exp(-x))


def _silu_and_grad(a):
    sg = _sigmoid(a)
    return a * sg, sg * (1.0 + a * (1.0 - sg))


def _softplus(x):
    return jnp.maximum(x, 0.0) + jnp.log(1.0 + jnp.exp(-jnp.abs(x)))


def _rms(x, g):
    r = lax.rsqrt(jnp.mean(x * x, axis=-1, keepdims=True) + NORM_EPS)
    return x * r * g


def _rms_bwd(dout, y, g):
    r = lax.rsqrt(jnp.mean(y * y, axis=-1, keepdims=True) + NORM_EPS)
    n = y * r
    dn = dout * g
    dy = r * (dn - n * jnp.mean(dn * n, axis=-1, keepdims=True))
    return dy, jnp.sum(dout * n, axis=0, keepdims=True)


def _rope(x, cos_f, sin_s):
    return x * cos_f + pltpu.roll(x, HEAD_DIM // 2, 1) * sin_s


def _rope_bwd(d, cos_f, sin_s):
    return d * cos_f + pltpu.roll(d * sin_s, HEAD_DIM // 2, 1)


def rmsnorm_fwd(x, g, name):
    s, d = x.shape
    tm = _tile(s, 512)

    def body(x_ref, g_ref, o_ref):
        o_ref[...] = _rms(x_ref[...], g_ref[...]).astype(BF16)

    return _call(body, name, (s // tm,),
                 [pl.BlockSpec((tm, d), lambda i: (i, 0)), pl.BlockSpec((1, d), lambda i: (0, 0))],
                 pl.BlockSpec((tm, d), lambda i: (i, 0)), _sds((s, d), BF16), sem=("parallel",))(x, g)


def norm_bwd(dout, y, g, scale, resid, out_dtype, name):
    s, d = y.shape
    tm = _tile(s, 256)
    has_res = resid is not None

    def body(*refs):
        if has_res:
            do_ref, y_ref, g_ref, r_ref, dy_ref, dg_ref = refs
        else:
            do_ref, y_ref, g_ref, dy_ref, dg_ref = refs
        dy, dg = _rms_bwd(do_ref[...] * scale, y_ref[...], g_ref[...])
        if has_res:
            dy = dy + r_ref[...]
        dy_ref[...] = dy.astype(out_dtype)

        @pl.when(pl.program_id(0) == 0)
        def _():
            dg_ref[...] = jnp.zeros_like(dg_ref)

        dg_ref[...] += dg

    row = pl.BlockSpec((tm, d), lambda i: (i, 0))
    vec = pl.BlockSpec((1, d), lambda i: (0, 0))
    ins = [dout, y, g] + ([resid] if has_res else [])
    return _call(body, name, (s // tm,), [row, row, vec] + ([row] if has_res else []),
                 [row, vec], [_sds((s, d), out_dtype), _sds((1, d), F32)])(*ins)


def ffn_up(h, wg, wu, name, dep=None):
    s, d = h.shape
    nj, _, fs = wg.shape
    tm = _tile(s, 512)

    def body(h_ref, wg_ref, wu_ref, *rest):
        a_ref, b_ref, u_ref = rest[-3:]
        hh = h_ref[...]
        a = _dot(hh, wg_ref[...])
        b = _dot(hh, wu_ref[...])
        a_ref[...] = a.astype(BF16)
        b_ref[...] = b.astype(BF16)
        u_ref[...] = (a * _sigmoid(a) * b).astype(BF16)

    wspec = pl.BlockSpec((None, d, fs), lambda j, i: (j, 0, 0))
    ospec = pl.BlockSpec((None, tm, fs), lambda j, i: (j, i, 0))
    osd = _sds((nj, s, fs), BF16)
    ins, specs = [h, wg, wu], [pl.BlockSpec((tm, d), lambda j, i: (i, 0)), wspec, wspec]
    if dep is not None:
        ins.append(dep)
        specs.append(pl.BlockSpec((8, 128), lambda j, i: (0, 0)))
    return _call(body, name, (nj, s // tm), specs, [ospec, ospec, ospec], [osd, osd, osd], sem=("parallel", "parallel"))(*ins)


def down_norm(u3, w3, x, g, scale, name):
    nj, s, k = u3.shape
    d = w3.shape[2]
    tm = _tile(s, 512)
    jb = 2 if nj % 2 == 0 else 1
    nsteps = nj // jb

    def body(u_ref, w_ref, x_ref, g_ref, y_ref, xn_ref):
        j = pl.program_id(1)
        t = _dot(u_ref[0], w_ref[0])
        for q in range(1, jb):
            t = t + _dot(u_ref[q], w_ref[q])

        @pl.when(j == 0)
        def _():
            y_ref[...] = t

        @pl.when(j > 0)
        def _():
            y_ref[...] += t

        @pl.when(j == nsteps - 1)
        def _():
            xn_ref[...] = x_ref[...] + scale * _rms(y_ref[...], g_ref[...])

    row = pl.BlockSpec((tm, d), lambda i, j: (i, 0))
    return _call(body, name, (s // tm, nsteps),
                 [pl.BlockSpec((jb, tm, k), lambda i, j: (j, i, 0)), pl.BlockSpec((jb, k, d), lambda i, j: (j, 0, 0)),
                  row, pl.BlockSpec((1, d), lambda i, j: (0, 0))],
                 [row, row], [_sds((s, d), F32), _sds((s, d), F32)], sem=("parallel", "arbitrary"))(u3, w3, x, g)


def ffn_down_bwd(dy, wd, a, b, name):
    nj, fs, d = wd.shape
    s = dy.shape[0]
    tm = _tile(s, 512)

    def body(dy_ref, w_ref, a_ref, b_ref, da_ref, db_ref):
        du = _dot(dy_ref[...], w_ref[...], NT)
        aa = a_ref[...].astype(F32)
        bb = b_ref[...].astype(F32)
        sl, dsl = _silu_and_grad(aa)
        da_ref[...] = (du * bb * dsl).astype(BF16)
        db_ref[...] = (du * sl).astype(BF16)

    hspec = pl.BlockSpec((None, tm, fs), lambda j, i: (j, i, 0))
    osd = _sds((nj, s, fs), BF16)
    return _call(body, name, (nj, s // tm),
                 [pl.BlockSpec((tm, d), lambda j, i: (i, 0)), pl.BlockSpec((None, fs, d), lambda j, i: (j, 0, 0)), hspec, hspec],
                 [hspec, hspec], [osd, osd], sem=("parallel", "parallel"))(dy, wd, a, b)


def mm_tn(a3, b3, out_dtype, name, tmm=2048, tn=1408, tk=1024):
    ja, s, m = a3.shape
    jb, _, n = b3.shape
    nj = max(ja, jb)
    tmm, tn, tk = _tile(m, tmm, 128), _tile(n, tn, 128), _tile(s, tk)
    nk = s // tk

    def body(a_ref, b_ref, o_ref, acc_ref):
        k = pl.program_id(3)

        @pl.when(k == 0)
        def _():
            acc_ref[...] = jnp.zeros_like(acc_ref)

        acc_ref[...] += _dot(a_ref[...].astype(BF16), b_ref[...].astype(BF16), TN)

        @pl.when(k == nk - 1)
        def _():
            o_ref[...] = acc_ref[...].astype(out_dtype)

    aj = (lambda j: j) if ja > 1 else (lambda j: 0)
    bj = (lambda j: j) if jb > 1 else (lambda j: 0)
    return _call(body, name, (nj, m // tmm, n // tn, nk),
                 [pl.BlockSpec((None, tk, tmm), lambda j, im, jn, k: (aj(j), k, im)),
                  pl.BlockSpec((None, tk, tn), lambda j, im, jn, k: (bj(j), k, jn))],
                 pl.BlockSpec((None, tmm, tn), lambda j, im, jn, k: (j, im, jn)), _sds((nj, m, n), out_dtype),
                 scratch=[pltpu.VMEM((tmm, tn), F32)],
                 sem=("parallel", "parallel", "parallel", "arbitrary"))(a3, b3)


def mm_nt_acc(pairs, name, tm=512, tn=512, dep=None):
    nj, s, _ = pairs[0][0].shape
    n = pairs[0][1].shape[1]
    tm, tn = _tile(s, tm), _tile(n, tn, 128)
    npair = len(pairs)

    def body(*refs):
        o_ref = refs[-1]
        j = pl.program_id(2)
        t = None
        for p in range(npair):
            c = _dot(refs[2 * p][...].astype(BF16), refs[2 * p + 1][...], NT)
            t = c if t is None else t + c
        if nj == 1:
            o_ref[...] = t
        else:
            @pl.when(j == 0)
            def _():
                o_ref[...] = t

            @pl.when(j > 0)
            def _():
                o_ref[...] += t

    ins, specs = [], []
    for a3, w3 in pairs:
        k = a3.shape[2]
        ins += [a3, w3]
        specs += [pl.BlockSpec((None, tm, k), lambda i, jn, j: (j, i, 0)),
                  pl.BlockSpec((None, tn, k), lambda i, jn, j: (j, jn, 0))]
    if dep is not None:
        ins.append(dep)
        specs.append(pl.BlockSpec((8, 128), lambda i, jn, j: (0, 0)))
    return _call(body, name, (s // tm, n // tn, nj), specs,
                 pl.BlockSpec((tm, tn), lambda i, jn, j: (i, jn)), _sds((s, n), F32),
                 sem=("parallel", "parallel", "arbitrary"))(*ins)


def mm_nn(a, w, name, tm=512, tn=1408, col0=0, kdim=None):
    s = a.shape[0]
    k, n = w.shape
    assert col0 % k == 0
    tm, tn = _tile(s, tm), _tile(n, tn, 128)
    cb = col0 // k

    def body(a_ref, w_ref, o_ref):
        o_ref[...] = _dot(a_ref[...].astype(BF16), w_ref[...])

    return _call(body, name, (n // tn, s // tm),
                 [pl.BlockSpec((tm, k), lambda jn, i: (i, cb)), pl.BlockSpec((k, tn), lambda jn, i: (0, jn))],
                 pl.BlockSpec((tm, tn), lambda jn, i: (i, jn)), _sds((s, n), F32), sem=("parallel", "parallel"))(a, w)


def mm_nn_acc(a3, w3, name, tm=256):
    nj, s, k = a3.shape
    n = w3.shape[2]
    tm = _tile(s, tm)

    def body(a_ref, w_ref, o_ref):
        j = pl.program_id(1)
        t = _dot(a_ref[...].astype(BF16), w_ref[...])

        @pl.when(j == 0)
        def _():
            o_ref[...] = t

        @pl.when(j > 0)
        def _():
            o_ref[...] += t

    return _call(body, name, (s // tm, nj),
                 [pl.BlockSpec((None, tm, k), lambda i, j: (j, i, 0)), pl.BlockSpec((None, k, n), lambda i, j: (j, 0, 0))],
                 pl.BlockSpec((tm, n), lambda i, j: (i, 0)), _sds((s, n), F32), sem=("parallel", "arbitrary"))(a3, w3)


def loss_and_grad(xl, target, name):
    s, d = xl.shape
    tm = _tile(s, 512)

    def body(x_ref, t_ref, l_ref, dx_ref):
        e = x_ref[...] - t_ref[...]
        dx_ref[...] = e * (1.0 / d)

        @pl.when(pl.program_id(0) == 0)
        def _():
            l_ref[...] = jnp.zeros_like(l_ref)

        part = jnp.sum(jnp.sum(e * e, axis=-1, keepdims=True), axis=0, keepdims=True) * (0.5 / d)
        l_ref[...] += jnp.broadcast_to(part, l_ref.shape)

    row = pl.BlockSpec((tm, d), lambda i: (i, 0))
    return _call(body, name, (s // tm,), [row, row], [pl.BlockSpec((1, 128), lambda i: (0, 0)), row],
                 [_sds((1, 128), F32), _sds((s, d), F32)])(xl, target)


def cast_bf16(w4, layer, name):
    _, r, c = w4.shape
    tr = _tile(r, 512)

    def body(w_ref, o_ref):
        o_ref[...] = w_ref[...].astype(BF16)

    return _call(body, name, (r // tr,), [pl.BlockSpec((None, tr, c), lambda i: (layer, i, 0))],
                 pl.BlockSpec((tr, c), lambda i: (i, 0)), _sds((r, c), BF16), sem=("parallel",))(w4)


def _attn_specs(nb):
    w = WINDOW
    prev = lambda i: jnp.maximum(i - 1, 0)
    q = pl.BlockSpec((w, ATTN_WIDTH), lambda i: (i, 0))
    kc = pl.BlockSpec((w, ATTN_KV_WIDTH), lambda i: (i, Z_K // ATTN_KV_WIDTH))
    kp = pl.BlockSpec((w, ATTN_KV_WIDTH), lambda i: (prev(i), Z_K // ATTN_KV_WIDTH))
    vc = pl.BlockSpec((w, ATTN_KV_WIDTH), lambda i: (i, Z_V // ATTN_KV_WIDTH))
    vp = pl.BlockSpec((w, ATTN_KV_WIDTH), lambda i: (prev(i), Z_V // ATTN_KV_WIDTH))
    tc = pl.BlockSpec((w, HEAD_DIM), lambda i: (i, 0))
    tp = pl.BlockSpec((w, HEAD_DIM), lambda i: (prev(i), 0))
    sink = pl.BlockSpec(memory_space=pltpu.SMEM)
    return [q, kc, kp, vc, vp, tc, tc, tp, tp, sink]


def _attn_mask(i):
    w = WINDOW
    qi = lax.broadcasted_iota(jnp.int32, (w, 2 * w), 0) + w
    kj = lax.broadcasted_iota(jnp.int32, (w, 2 * w), 1)
    rel = qi - kj
    band = (rel >= 0) & (rel < w)
    return band & jnp.logical_not((i == 0) & (kj < w))


def _attn_probs(q, kk, sink, mask):
    s = _dot(q, kk, NT) * (HEAD_DIM ** -0.5)
    s = jnp.where(mask, s, NEG)
    m = jnp.maximum(jnp.max(s, axis=-1, keepdims=True), sink)
    p = jnp.exp(s - m)
    es = jnp.exp(sink - m)
    inv = 1.0 / (jnp.sum(p, axis=-1, keepdims=True) + es)
    return p * inv, es * inv


def attn_fwd(z, cos_f, sin_s, sinks, name):
    s = z.shape[0]
    nb = s // WINDOW
    hd = HEAD_DIM
    grp = ATTN_HEADS // ATTN_KV_HEADS

    def body(q_ref, kc_ref, kp_ref, vc_ref, vp_ref, cc_ref, sc_ref, cp_ref, sp_ref, sink_ref, o_ref):
        i = pl.program_id(0)
        mask = _attn_mask(i)
        cc, sc, cp, sp = cc_ref[...], sc_ref[...], cp_ref[...], sp_ref[...]
        for kv in range(ATTN_KV_HEADS):
            ksl = slice(kv * hd, (kv + 1) * hd)
            kk = jnp.concatenate([_rope(kp_ref[:, ksl], cp, sp), _rope(kc_ref[:, ksl], cc, sc)], axis=0).astype(BF16)
            vv = jnp.concatenate([vp_ref[:, ksl], vc_ref[:, ksl]], axis=0).astype(BF16)
            for g in range(grp):
                h = kv * grp + g
                hsl = slice(h * hd, (h + 1) * hd)
                q = _rope(q_ref[:, hsl], cc, sc).astype(BF16)
                pn, _ = _attn_probs(q, kk, sink_ref[h], mask)
                o_ref[:, hsl] = _dot(pn.astype(BF16), vv).astype(BF16)

    return _call(body, name, (nb,), _attn_specs(nb), pl.BlockSpec((WINDOW, ATTN_WIDTH), lambda i: (i, 0)),
                 _sds((s, ATTN_WIDTH), BF16), sem=("parallel",))(z, z, z, z, z, cos_f, sin_s, cos_f, sin_s, sinks)


def attn_bwd(z, cos_f, sin_s, sinks, dy, name):
    s = z.shape[0]
    nb = s // WINDOW
    hd = HEAD_DIM
    grp = ATTN_HEADS // ATTN_KV_HEADS
    scale = HEAD_DIM ** -0.5

    def body(q_ref, kc_ref, kp_ref, vc_ref, vp_ref, cc_ref, sc_ref, cp_ref, sp_ref, sink_ref, dy_ref,
             dq_ref, dkc_ref, dkp_ref, dvc_ref, dvp_ref, ds_ref):
        i = pl.program_id(0)
        mask = _attn_mask(i)
        cc, sc, cp, sp = cc_ref[...], sc_ref[...], cp_ref[...], sp_ref[...]

        @pl.when(i == 0)
        def _():
            ds_ref[...] = jnp.zeros_like(ds_ref)

        for kv in range(ATTN_KV_HEADS):
            ksl = slice(kv * hd, (kv + 1) * hd)
            kk = jnp.concatenate([_rope(kp_ref[:, ksl], cp, sp), _rope(kc_ref[:, ksl], cc, sc)], axis=0).astype(BF16)
            vv = jnp.concatenate([vp_ref[:, ksl], vc_ref[:, ksl]], axis=0).astype(BF16)
            dkk = jnp.zeros((2 * WINDOW, hd), F32)
            dvv = jnp.zeros((2 * WINDOW, hd), F32)
            for g in range(grp):
                h = kv * grp + g
                hsl = slice(h * hd, (h + 1) * hd)
                q = _rope(q_ref[:, hsl], cc, sc).astype(BF16)
                pn, psink = _attn_probs(q, kk, sink_ref[h], mask)
                do = dy_ref[:, hsl].astype(BF16)
                dpn = _dot(do, vv, NT)
                dvv = dvv + _dot(pn.astype(BF16), do, TN)
                tot = jnp.sum(pn * dpn, axis=-1, keepdims=True)
                dsc = (pn * (dpn - tot) * scale).astype(BF16)
                dq_ref[:, hsl] = _rope_bwd(_dot(dsc, kk), cc, sc)
                dkk = dkk + _dot(dsc, q, TN)
                dsink = jnp.sum(-psink * tot, axis=0, keepdims=True)
                ds_ref[h:h + 1, :] += jnp.broadcast_to(dsink, (1, 128))
            dkp_ref[:, ksl] = _rope_bwd(dkk[:WINDOW], cp, sp)
            dkc_ref[:, ksl] = _rope_bwd(dkk[WINDOW:], cc, sc)
            dvp_ref[:, ksl] = dvv[:WINDOW]
            dvc_ref[:, ksl] = dvv[WINDOW:]

    kvo = pl.BlockSpec((WINDOW, ATTN_KV_WIDTH), lambda i: (i, 0))
    kvs = _sds((s, ATTN_KV_WIDTH), F32)
    return _call(body, name, (nb,), _attn_specs(nb) + [pl.BlockSpec((WINDOW, ATTN_WIDTH), lambda i: (i, 0))],
                 [pl.BlockSpec((WINDOW, ATTN_WIDTH), lambda i: (i, 0)), kvo, kvo, kvo, kvo,
                  pl.BlockSpec((ATTN_HEADS, 128), lambda i: (0, 0))],
                 [_sds((s, ATTN_WIDTH), F32), kvs, kvs, kvs, kvs, _sds((ATTN_HEADS, 128), F32)])(
        z, z, z, z, z, cos_f, sin_s, cos_f, sin_s, sinks, dy)


def _shift_down(x, d, row):
    return jnp.where(row >= d, pltpu.roll(x, d, 0), 0.0)


def _shift_up(x, d, row, n):
    return jnp.where(row < n - d, pltpu.roll(x, n - d, 0), 0.0)


def _conv_taps(u, w_ref, row):
    c = w_ref[DN_CONV - 1:DN_CONV, :] * u
    for k in range(DN_CONV - 1):
        c = c + w_ref[k:k + 1, :] * _shift_down(u, DN_CONV - 1 - k, row)
    return c


def dn_pre_fwd(z, conv_w, name):
    s = z.shape[0]
    nblk = 3 * DN_WIDTH // 128
    nqk = 2 * DN_WIDTH // 128

    def body(u_ref, w_ref, o_ref):
        row = lax.broadcasted_iota(jnp.int32, (s, 128), 0)
        c = _conv_taps(u_ref[...], w_ref, row)
        sl = c * _sigmoid(c)
        j = pl.program_id(0)

        @pl.when(j < nqk)
        def _():
            o_ref[...] = sl * lax.rsqrt(jnp.sum(sl * sl, axis=-1, keepdims=True) + NORM_EPS)

        @pl.when(j >= nqk)
        def _():
            o_ref[...] = sl

    return _call(body, name, (nblk,),
                 [pl.BlockSpec((s, 128), lambda j: (0, Z_DN // 128 + j)), pl.BlockSpec((DN_CONV, 128), lambda j: (0, j))],
                 pl.BlockSpec((s, 128), lambda j: (0, j)), _sds((s, 3 * DN_WIDTH), F32), sem=("parallel",))(z, conv_w)


def dn_pre_bwd(z, conv_w, dout, name):
    s = z.shape[0]
    nblk = 3 * DN_WIDTH // 128
    nqk = 2 * DN_WIDTH // 128

    def body(u_ref, w_ref, do_ref, du_ref, dw_ref, ds_ref):
        row = lax.broadcasted_iota(jnp.int32, (s, 128), 0)
        u = u_ref[...]
        c = _conv_taps(u, w_ref, row)
        sl, dsl = _silu_and_grad(c)
        do = do_ref[...]
        j = pl.program_id(0)

        @pl.when(j < nqk)
        def _():
            r = lax.rsqrt(jnp.sum(sl * sl, axis=-1, keepdims=True) + NORM_EPS)
            ds_ref[...] = r * do - sl * (r * r * r) * jnp.sum(do * sl, axis=-1, keepdims=True)

        @pl.when(j >= nqk)
        def _():
            ds_ref[...] = do

        dc = ds_ref[...] * dsl
        du = w_ref[DN_CONV - 1:DN_CONV, :] * dc
        dw_ref[DN_CONV - 1:DN_CONV, :] = jnp.sum(dc * u, axis=0, keepdims=True)
        for k in range(DN_CONV - 1):
            d = DN_CONV - 1 - k
            du = du + w_ref[k:k + 1, :] * _shift_up(dc, d, row, s)
            dw_ref[k:k + 1, :] = jnp.sum(dc * _shift_down(u, d, row), axis=0, keepdims=True)
        du_ref[...] = du

    blk = pl.BlockSpec((s, 128), lambda j: (0, j))
    wsp = pl.BlockSpec((DN_CONV, 128), lambda j: (0, j))
    return _call(body, name, (nblk,), [pl.BlockSpec((s, 128), lambda j: (0, Z_DN // 128 + j)), wsp, blk],
                 [blk, wsp], [_sds((s, 3 * DN_WIDTH), F32), _sds((DN_CONV, 3 * DN_WIDTH), F32)],
                 scratch=[pltpu.VMEM((s, 128), F32)], sem=("parallel",))(z, conv_w, dout)


def _lane_col(x, lane, idx):
    return jnp.sum(jnp.where(lane == idx, x, 0.0), axis=-1, keepdims=True)


def dn_gates_fwd(z, alog_b, dtb_b, name):
    s = z.shape[0]
    tm = _tile(s, 512)

    def body(zs_ref, al_ref, dt_ref, beta_ref, g_ref):
        zs = zs_ref[...]
        lane = lax.broadcasted_iota(jnp.int32, zs.shape, 1)
        for h in range(DN_HEADS):
            b_raw = _lane_col(zs, lane, h)
            a_raw = _lane_col(zs, lane, DN_HEADS + h)
            beta_ref[h] = jnp.broadcast_to(_sigmoid(b_raw), (tm, 128))
            g_ref[h] = -jnp.exp(al_ref[h:h + 1, :]) * _softplus(a_raw + dt_ref[h:h + 1, :])

    osp = pl.BlockSpec((DN_HEADS, tm, 128), lambda i: (0, i, 0))
    psp = pl.BlockSpec((DN_HEADS, 128), lambda i: (0, 0))
    osd = _sds((DN_HEADS, s, 128), F32)
    return _call(body, name, (s // tm,), [pl.BlockSpec((tm, 128), lambda i: (i, Z_SM // 128)), psp, psp],
                 [osp, osp], [osd, osd], sem=("parallel",))(z, alog_b, dtb_b)


def dn_gates_bwd(z, alog_b, dtb_b, dbeta, dg, name):
    s = z.shape[0]
    tm = _tile(s, 512)

    def body(zs_ref, al_ref, dt_ref, dbeta_ref, dg_ref, dz_ref, dal_ref, ddt_ref):
        @pl.when(pl.program_id(0) == 0)
        def _():
            dal_ref[...] = jnp.zeros_like(dal_ref)
            ddt_ref[...] = jnp.zeros_like(ddt_ref)

        zs = zs_ref[...]
        lane = lax.broadcasted_iota(jnp.int32, zs.shape, 1)
        dz = jnp.zeros_like(zs)
        for h in range(DN_HEADS):
            b_raw = _lane_col(zs, lane, h)
            a_raw = _lane_col(zs, lane, DN_HEADS + h)
            dbe = jnp.sum(dbeta_ref[h], axis=-1, keepdims=True)
            dgg = jnp.sum(dg_ref[h], axis=-1, keepdims=True)
            beta = _sigmoid(b_raw)
            ea = jnp.exp(al_ref[h:h + 1, :])
            pre = a_raw + dt_ref[h:h + 1, :]
            da_raw = dgg * (-ea) * _sigmoid(pre)
            dz = dz + jnp.where(lane == h, dbe * beta * (1.0 - beta), 0.0) + jnp.where(lane == DN_HEADS + h, da_raw, 0.0)
            ddt_ref[h:h + 1, :] += jnp.sum(da_raw, axis=0, keepdims=True)
            dal_ref[h:h + 1, :] += jnp.sum(dgg * (-ea) * _softplus(pre), axis=0, keepdims=True)
        dz_ref[...] = dz

    hsp = pl.BlockSpec((DN_HEADS, tm, 128), lambda i: (0, i, 0))
    psp = pl.BlockSpec((DN_HEADS, 128), lambda i: (0, 0))
    return _call(body, name, (s // tm,), [pl.BlockSpec((tm, 128), lambda i: (i, Z_SM // 128)), psp, psp, hsp, hsp],
                 [pl.BlockSpec((tm, 128), lambda i: (i, 0)), psp, psp],
                 [_sds((s, 128), F32), _sds((DN_HEADS, 128), F32), _sds((DN_HEADS, 128), F32)])(z, alog_b, dtb_b, dbeta, dg)


def _dn_intra(q, k, v, gb, bb):
    c = DN_CHUNK
    pairs = range(len(q))
    ri = lax.broadcasted_iota(jnp.int32, (c, c), 0)
    ci = lax.broadcasted_iota(jnp.int32, (c, c), 1)
    causal = ri >= ci
    strict = ri > ci
    gc = [_chunk_cumsum(gb[i]) for i in pairs]
    grow = [_as_row(gc[i]) for i in pairs]
    decay = [jnp.where(causal, jnp.exp(jnp.where(causal, gc[i][:, :c] - grow[i], 0.0)), 0.0) for i in pairs]
    qs = [q[i] * (DN_HEAD_DIM ** -0.5) for i in pairs]
    kb = [k[i] * bb[i] for i in pairs]
    lower = [jnp.where(strict, _dot(kb[i], k[i], NT) * decay[i], 0.0) for i in pairs]
    t = _unit_lower_inverse(lower)
    eg = [jnp.exp(gc[i]) for i in pairs]
    u = [_dot(t[i], v[i] * bb[i]) for i in pairs]
    w = [_dot(t[i], kb[i] * eg[i]) for i in pairs]
    attn = [jnp.where(causal, _dot(qs[i], k[i], NT) * decay[i], 0.0) for i in pairs]
    glast = [_last_row(gc[i]) for i in pairs]
    return (u, w, [qs[i] * eg[i] for i in pairs], [k[i] * jnp.exp(glast[i] - gc[i]) for i in pairs], attn,
            [jnp.exp(glast[i][:8]) for i in pairs])


def _split3(x):
    x1 = x.astype(BF16)
    r = x - x1.astype(F32)
    x2 = r.astype(BF16)
    return x1, x2, (r - x2.astype(F32)).astype(BF16)


def _dot_split(a, b, dims=NN):
    a1, a2, _ = _split3(a)
    b1, b2, _ = _split3(b)
    return _dot(a1, b1, dims) + (_dot(a1, b2, dims) + _dot(a2, b1, dims))


def _sel_dot(m01, x, dims, m_left):
    m = m01.astype(BF16)
    parts = [_dot(m, xi, dims) if m_left else _dot(xi, m, dims) for xi in _split3(x)]
    return parts[0] + (parts[1] + parts[2])


def _tri_mask(n, upper):
    ri = lax.broadcasted_iota(jnp.int32, (n, n), 0)
    ci = lax.broadcasted_iota(jnp.int32, (n, n), 1)
    return ci >= ri if upper else ri >= ci


@jax.custom_vjp
def _chunk_cumsum(x):
    return _sel_dot(_tri_mask(x.shape[0], False), x, NN, True)


_chunk_cumsum.defvjp(lambda x: (_chunk_cumsum(x), None),
                     lambda _, ct: (_sel_dot(_tri_mask(ct.shape[0], True), ct, NN, True),))


def _lane0(rows):
    return lax.broadcasted_iota(jnp.int32, (rows, 128), 1) == 0


@jax.custom_vjp
def _as_row(x):
    return _sel_dot(_lane0(x.shape[0]), x, NT, True)


_as_row.defvjp(lambda x: (_as_row(x), None),
               lambda _, ct: (_sel_dot(_lane0(ct.shape[0]), ct, TN, False),))


def _last_col_mask(n, transpose):
    idx = lax.broadcasted_iota(jnp.int32, (n, n), 0 if transpose else 1)
    return idx == n - 1


@jax.custom_vjp
def _last_row(x):
    return _sel_dot(_last_col_mask(x.shape[0], False), x, NN, True)


_last_row.defvjp(lambda x: (_last_row(x), None),
                 lambda _, ct: (_sel_dot(_last_col_mask(ct.shape[0], True), ct, NN, True),))


@jax.custom_vjp
def _unit_lower_inverse(lowers):
    n = lowers[0].shape[0]
    eye = (lax.broadcasted_iota(jnp.int32, (n, n), 0) == lax.broadcasted_iota(jnp.int32, (n, n), 1)).astype(F32)
    ts = [eye - l for l in lowers]
    ps = list(lowers)
    for _ in range(5):
        ps = [_dot_split(p, p) for p in ps]
        ts = [t + _dot_split(t, p) for t, p in zip(ts, ps)]
    return ts


def _unit_lower_inverse_fwd(lowers):
    ts = _unit_lower_inverse(lowers)
    return ts, ts


def _unit_lower_inverse_bwd(ts, dts):
    half = [_dot_split(t, dt, TN) for t, dt in zip(ts, dts)]
    return ([-_dot_split(h, t, NT) for h, t in zip(half, ts)],)


_unit_lower_inverse.defvjp(_unit_lower_inverse_fwd, _unit_lower_inverse_bwd)


def _dn_step(st, qd, kd, u, w, attn, egl):
    heads = range(len(st))
    v_new = [u[h] - _dot(w[h], st[h]) for h in heads]
    o = [_dot(qd[h], st[h]) for h in heads]
    o = [o[h] + _dot(attn[h], v_new[h]) for h in heads]
    st_new = [st[h] * egl[h][0:1, :] + _dot(kd[h], v_new[h], TN) for h in heads]
    return o, st_new


def _dn_chunk_specs(m=1):
    c = DN_CHUNK
    wide = pl.BlockSpec((m * c, DN_WIDTH), lambda i: (i, 0))
    att = pl.BlockSpec((m * c, DN_HEADS * c), lambda i: (i, 0))
    egl = pl.BlockSpec((m * 8, DN_WIDTH), lambda i: (i, 0))
    return wide, att, egl


def _dn_intra_chunks(nc):
    return 2 if nc % 2 == 0 else 1


def dn_intra_fwd(qkv, gb, bb, name):
    s = qkv.shape[0]
    c, hd = DN_CHUNK, DN_HEAD_DIM
    nc = s // c
    m = _dn_intra_chunks(nc)

    def body(q_ref, k_ref, v_ref, g_ref, b_ref, u_ref, w_ref, qd_ref, kd_ref, at_ref, eg_ref):
        pairs = [(t, h) for t in range(m) for h in range(DN_HEADS)]
        rs = lambda t: slice(t * c, (t + 1) * c)
        hs = lambda h: slice(h * hd, (h + 1) * hd)
        u, w, qd, kd, at, eg = _dn_intra([q_ref[rs(t), hs(h)] for t, h in pairs], [k_ref[rs(t), hs(h)] for t, h in pairs],
                                         [v_ref[rs(t), hs(h)] for t, h in pairs], [g_ref[h, rs(t)] for t, h in pairs],
                                         [b_ref[h, rs(t)] for t, h in pairs])
        for i, (t, h) in enumerate(pairs):
            u_ref[rs(t), hs(h)], w_ref[rs(t), hs(h)], qd_ref[rs(t), hs(h)], kd_ref[rs(t), hs(h)] = u[i], w[i], qd[i], kd[i]
            at_ref[rs(t), h * c:(h + 1) * c] = at[i]
            eg_ref[t * 8:(t + 1) * 8, hs(h)] = eg[i]

    wide, att, egl = _dn_chunk_specs(m)
    hsp = pl.BlockSpec((DN_HEADS, m * c, 128), lambda i: (0, i, 0))
    wsd = _sds((s, DN_WIDTH), F32)
    return _call(body, name, (nc // m,),
                 [pl.BlockSpec((m * c, DN_WIDTH), lambda i: (i, 0)), pl.BlockSpec((m * c, DN_WIDTH), lambda i: (i, 1)),
                  pl.BlockSpec((m * c, DN_WIDTH), lambda i: (i, 2)), hsp, hsp],
                 [wide, wide, wide, wide, att, egl],
                 [wsd, wsd, wsd, wsd, _sds((s, DN_HEADS * c), F32), _sds((nc * 8, DN_WIDTH), F32)],
                 sem=("parallel",))(qkv, qkv, qkv, gb, bb)


def dn_intra_bwd(qkv, gb, bb, cts, name):
    s = qkv.shape[0]
    c, hd = DN_CHUNK, DN_HEAD_DIM
    nc = s // c
    m = _dn_intra_chunks(nc)

    def body(q_ref, k_ref, v_ref, g_ref, b_ref, du_ref, dw_ref, dqd_ref, dkd_ref, dat_ref, deg_ref,
             dq_ref, dk_ref, dv_ref, dg_ref, db_ref):
        pairs = [(t, h) for t in range(m) for h in range(DN_HEADS)]
        rs = lambda t: slice(t * c, (t + 1) * c)
        hs = lambda h: slice(h * hd, (h + 1) * hd)
        _, vjp = jax.vjp(_dn_intra, [q_ref[rs(t), hs(h)] for t, h in pairs], [k_ref[rs(t), hs(h)] for t, h in pairs],
                         [v_ref[rs(t), hs(h)] for t, h in pairs], [g_ref[h, rs(t)] for t, h in pairs],
                         [b_ref[h, rs(t)] for t, h in pairs])
        dq, dk, dv, dg, db = vjp(([du_ref[rs(t), hs(h)] for t, h in pairs], [dw_ref[rs(t), hs(h)] for t, h in pairs],
                                  [dqd_ref[rs(t), hs(h)] for t, h in pairs], [dkd_ref[rs(t), hs(h)] for t, h in pairs],
                                  [dat_ref[rs(t), h * c:(h + 1) * c] for t, h in pairs],
                                  [deg_ref[t * 8:(t + 1) * 8, hs(h)] for t, h in pairs]))
        for i, (t, h) in enumerate(pairs):
            dq_ref[rs(t), hs(h)], dk_ref[rs(t), hs(h)], dv_ref[rs(t), hs(h)] = dq[i], dk[i], dv[i]
            dg_ref[h, rs(t)] = dg[i]
            db_ref[h, rs(t)] = db[i]

    wide, att, egl = _dn_chunk_specs(m)
    hsp = pl.BlockSpec((DN_HEADS, m * c, 128), lambda i: (0, i, 0))
    hsd = _sds((DN_HEADS, s, 128), F32)
    wsd = _sds((s, DN_WIDTH), F32)
    return _call(body, name, (nc // m,),
                 [pl.BlockSpec((m * c, DN_WIDTH), lambda i: (i, 0)), pl.BlockSpec((m * c, DN_WIDTH), lambda i: (i, 1)),
                  pl.BlockSpec((m * c, DN_WIDTH), lambda i: (i, 2)), hsp, hsp, wide, wide, wide, wide, att, egl],
                 [wide, wide, wide, hsp, hsp], [wsd, wsd, wsd, hsd, hsd], sem=("parallel",))(qkv, qkv, qkv, gb, bb, *cts)


def dn_scan_fwd(u, w, qd, kd, at, eg, name):
    s = u.shape[0]
    c, hd = DN_CHUNK, DN_HEAD_DIM
    nc = s // c

    def body(u_ref, w_ref, qd_ref, kd_ref, at_ref, eg_ref, o_ref, st_ref, state):
        @pl.when(pl.program_id(0) == 0)
        def _():
            state[...] = jnp.zeros_like(state)

        heads = range(DN_HEADS)
        hs = lambda h: slice(h * hd, (h + 1) * hd)
        st = [state[h] for h in heads]
        for h in heads:
            st_ref[h] = st[h]
        o, st_new = _dn_step(st, [qd_ref[:, hs(h)] for h in heads], [kd_ref[:, hs(h)] for h in heads],
                             [u_ref[:, hs(h)] for h in heads], [w_ref[:, hs(h)] for h in heads],
                             [at_ref[:, h * c:(h + 1) * c] for h in heads], [eg_ref[:, hs(h)] for h in heads])
        for h in heads:
            o_ref[:, hs(h)] = o[h]
            state[h] = st_new[h]

    wide, att, egl = _dn_chunk_specs()
    return _call(body, name, (nc,), [wide, wide, wide, wide, att, egl],
                 [wide, pl.BlockSpec((None, DN_HEADS, hd, hd), lambda i: (i, 0, 0, 0))],
                 [_sds((s, DN_WIDTH), F32), _sds((nc, DN_HEADS, hd, hd), F32)],
                 scratch=[pltpu.VMEM((DN_HEADS, hd, hd), F32)])(u, w, qd, kd, at, eg)


def dn_scan_bwd(u, w, qd, kd, at, eg, states, do, name):
    s = u.shape[0]
    c, hd = DN_CHUNK, DN_HEAD_DIM
    nc = s // c

    def body(u_ref, w_ref, qd_ref, kd_ref, at_ref, eg_ref, st_ref, do_ref,
             du_ref, dw_ref, dqd_ref, dkd_ref, dat_ref, deg_ref, dstate):
        @pl.when(pl.program_id(0) == 0)
        def _():
            dstate[...] = jnp.zeros_like(dstate)

        heads = range(DN_HEADS)
        hs = lambda h: slice(h * hd, (h + 1) * hd)
        asl = lambda h: slice(h * c, (h + 1) * c)
        _, vjp = jax.vjp(_dn_step, [st_ref[h] for h in heads], [qd_ref[:, hs(h)] for h in heads],
                         [kd_ref[:, hs(h)] for h in heads], [u_ref[:, hs(h)] for h in heads], [w_ref[:, hs(h)] for h in heads],
                         [at_ref[:, asl(h)] for h in heads], [eg_ref[:, hs(h)] for h in heads])
        dst, dqd, dkd, du, dw, dat, deg = vjp(([do_ref[:, hs(h)] for h in heads], [dstate[h] for h in heads]))
        for h in heads:
            dstate[h] = dst[h]
            du_ref[:, hs(h)], dw_ref[:, hs(h)], dqd_ref[:, hs(h)], dkd_ref[:, hs(h)] = du[h], dw[h], dqd[h], dkd[h]
            dat_ref[:, asl(h)] = dat[h]
            deg_ref[:, hs(h)] = deg[h]

    rev = lambda i: nc - 1 - i
    wide = pl.BlockSpec((c, DN_WIDTH), lambda i: (rev(i), 0))
    att = pl.BlockSpec((c, DN_HEADS * c), lambda i: (rev(i), 0))
    egl = pl.BlockSpec((8, DN_WIDTH), lambda i: (rev(i), 0))
    wsd = _sds((s, DN_WIDTH), F32)
    return _call(body, name, (nc,),
                 [wide, wide, wide, wide, att, egl, pl.BlockSpec((None, DN_HEADS, hd, hd), lambda i: (rev(i), 0, 0, 0)), wide],
                 [wide, wide, wide, wide, att, egl],
                 [wsd, wsd, wsd, wsd, _sds((s, DN_HEADS * c), F32), _sds((nc * 8, DN_WIDTH), F32)],
                 scratch=[pltpu.VMEM((DN_HEADS, hd, hd), F32)])(u, w, qd, kd, at, eg, states, do)


def _dn_out(o, zg, nw):
    n = o * lax.rsqrt(jnp.mean(o * o, axis=-1, keepdims=True) + NORM_EPS) * nw
    return n * (zg * _sigmoid(zg))


def dn_out_fwd(o, z, nw, name):
    s = o.shape[0]
    tm = _tile(s, 512)
    hd = DN_HEAD_DIM

    def body(o_ref, zg_ref, nw_ref, y_ref):
        for h in range(DN_HEADS):
            hs = slice(h * hd, (h + 1) * hd)
            y_ref[:, hs] = _dn_out(o_ref[:, hs], zg_ref[:, hs], nw_ref[...]).astype(BF16)

    return _call(body, name, (s // tm,),
                 [pl.BlockSpec((tm, DN_WIDTH), lambda i: (i, 0)), pl.BlockSpec((tm, DN_WIDTH), lambda i: (i, Z_ZG // DN_WIDTH)),
                  pl.BlockSpec((1, hd), lambda i: (0, 0))],
                 pl.BlockSpec((tm, DN_WIDTH), lambda i: (i, 0)), _sds((s, DN_WIDTH), BF16), sem=("parallel",))(o, z, nw)


def dn_out_bwd(o, z, nw, dycat, name):
    s = o.shape[0]
    tm = _tile(s, 512)
    hd = DN_HEAD_DIM

    def body(o_ref, zg_ref, nw_ref, dy_ref, do_ref, dzg_ref, dnw_ref):
        @pl.when(pl.program_id(0) == 0)
        def _():
            dnw_ref[...] = jnp.zeros_like(dnw_ref)

        for h in range(DN_HEADS):
            hs = slice(h * hd, (h + 1) * hd)
            _, vjp = jax.vjp(_dn_out, o_ref[:, hs], zg_ref[:, hs], nw_ref[...])
            do, dzg, dnw = vjp(dy_ref[:, hs])
            do_ref[:, hs] = do
            dzg_ref[:, hs] = dzg
            dnw_ref[...] += dnw

    wide = pl.BlockSpec((tm, DN_WIDTH), lambda i: (i, 0))
    wsd = _sds((s, DN_WIDTH), F32)
    return _call(body, name, (s // tm,),
                 [wide, pl.BlockSpec((tm, DN_WIDTH), lambda i: (i, Z_ZG // DN_WIDTH)), pl.BlockSpec((1, hd), lambda i: (0, 0)),
                  pl.BlockSpec((tm, DN_WIDTH), lambda i: (i, ATTN_WIDTH // DN_WIDTH))],
                 [wide, wide, pl.BlockSpec((1, hd), lambda i: (0, 0))], [wsd, wsd, _sds((1, hd), F32)])(o, z, nw, dycat)


def _s5_param_fn(are, aim, ldt, bre, bim):
    dt = jnp.exp(ldt)
    er = jnp.exp(are * dt)
    abr = er * jnp.cos(aim * dt)
    abi = er * jnp.sin(aim * dt)
    den = are * are + aim * aim
    cr = ((abr - 1.0) * are + abi * aim) / den
    ci = (abi * are - (abr - 1.0) * aim) / den
    return abr, abi, cr * bre - ci * bim, cr * bim + ci * bre


def s5_params_fwd(are, aim, ldt, bre, bim, name):
    p, hh = bre.shape

    def body(a_ref, b_ref, c_ref, d_ref, e_ref, o1, o2, o3, o4):
        o1[...], o2[...], o3[...], o4[...] = _s5_param_fn(a_ref[...], b_ref[...], c_ref[...], d_ref[...], e_ref[...])

    col = pl.BlockSpec((p, 1), lambda: (0, 0))
    mat = pl.BlockSpec((p, hh), lambda: (0, 0))
    return _call(body, name, (), [col, col, col, mat, mat], [col, col, mat, mat],
                 [_sds((p, 1), F32), _sds((p, 1), F32), _sds((p, hh), F32), _sds((p, hh), F32)])(are, aim, ldt, bre, bim)


def s5_params_bwd(are, aim, ldt, bre, bim, cts, name):
    p, hh = bre.shape

    def body(a_ref, b_ref, c_ref, d_ref, e_ref, g1, g2, g3, g4, o1, o2, o3, o4, o5):
        _, vjp = jax.vjp(_s5_param_fn, a_ref[...], b_ref[...], c_ref[...], d_ref[...], e_ref[...])
        o1[...], o2[...], o3[...], o4[...], o5[...] = vjp((g1[...], g2[...], g3[...], g4[...]))

    col = pl.BlockSpec((p, 1), lambda: (0, 0))
    mat = pl.BlockSpec((p, hh), lambda: (0, 0))
    csd, msd = _sds((p, 1), F32), _sds((p, hh), F32)
    return _call(body, name, (), [col, col, col, mat, mat, col, col, mat, mat], [col, col, col, mat, mat],
                 [csd, csd, csd, msd, msd])(are, aim, ldt, bre, bim, *cts)


def _cmul(ar, ai, br, bi):
    return ar * br - ai * bi, ar * bi + ai * br


def _s5_scan_block(xr, xi, ar, ai, n, reverse):
    row = lax.broadcasted_iota(jnp.int32, xr.shape, 0)
    d = 1
    while d < n:
        if reverse:
            sr, si = _shift_up(xr, d, row, n), _shift_up(xi, d, row, n)
        else:
            sr, si = _shift_down(xr, d, row), _shift_down(xi, d, row)
        pr, pi = _cmul(ar, ai, sr, si)
        xr, xi = xr + pr, xi + pi
        ar, ai = _cmul(ar, ai, ar, ai)
        d *= 2
    return xr, xi


def s5_scan_fwd(bu, abr, abi, name):
    s = bu.shape[0]
    npb = S5_P // 128

    def body(br_ref, bi_ref, ar_ref, ai_ref, x_ref):
        x_ref[0], x_ref[1] = _s5_scan_block(br_ref[...], bi_ref[...], ar_ref[...], ai_ref[...], s, False)

    re = pl.BlockSpec((s, 128), lambda j: (0, j))
    im = pl.BlockSpec((s, 128), lambda j: (0, npb + j))
    av = pl.BlockSpec((1, 128), lambda j: (0, j))
    return _call(body, name, (npb,), [re, im, av, av], pl.BlockSpec((2, s, 128), lambda j: (0, 0, j)),
                 _sds((2, s, S5_P), F32), sem=("parallel",))(bu, bu, abr, abi)


def s5_scan_bwd(dx, x, abr, abi, name):
    s = dx.shape[0]
    npb = S5_P // 128

    def body(dr_ref, di_ref, x_ref, ar_ref, ai_ref, g_ref, dar_ref, dai_ref):
        ar, ai = ar_ref[...], ai_ref[...]
        gr, gi = _s5_scan_block(dr_ref[...], di_ref[...], ar, -ai, s, True)
        g_ref[0], g_ref[1] = gr, gi
        row = lax.broadcasted_iota(jnp.int32, gr.shape, 0)
        pr, pi = _shift_down(x_ref[0], 1, row), _shift_down(x_ref[1], 1, row)
        dar_ref[...] = jnp.sum(gr * pr + gi * pi, axis=0, keepdims=True)
        dai_ref[...] = jnp.sum(gi * pr - gr * pi, axis=0, keepdims=True)

    re = pl.BlockSpec((s, 128), lambda j: (0, j))
    im = pl.BlockSpec((s, 128), lambda j: (0, npb + j))
    av = pl.BlockSpec((1, 128), lambda j: (0, j))
    planes = pl.BlockSpec((2, s, 128), lambda j: (0, 0, j))
    asd = _sds((1, S5_P), F32)
    return _call(body, name, (npb,), [re, im, planes, av, av], [planes, av, av], [_sds((2, s, S5_P), F32), asd, asd],
                 sem=("parallel",))(dx, dx, x, abr, abi)


def _gelu(y):
    return 0.5 * y * (1.0 + jnp.tanh(math.sqrt(2.0 / math.pi) * (y + 0.044715 * y * y * y)))


def s5_out_fwd(ypre, z, dvec, glu_w, glu_b, name):
    s = ypre.shape[0]
    tm = _tile(s, 512)
    wd = S5_WIDTH

    def body(yp_ref, u_ref, d_ref, w_ref, b_ref, y_ref, o_ref):
        y = yp_ref[...] + d_ref[...] * u_ref[...]
        y_ref[...] = y
        g = _gelu(y)
        t = _dot(g.astype(BF16), w_ref[...]) + b_ref[...]
        o_ref[...] = (g * _sigmoid(t)).astype(BF16)

    row = pl.BlockSpec((tm, wd), lambda i: (i, 0))
    vec = pl.BlockSpec((1, wd), lambda i: (0, 0))
    return _call(body, name, (s // tm,),
                 [row, pl.BlockSpec((tm, wd), lambda i: (i, Z_S5 // wd)), vec, pl.BlockSpec((wd, wd), lambda i: (0, 0)), vec],
                 [row, row], [_sds((s, wd), F32), _sds((s, wd), BF16)], sem=("parallel",))(ypre, z, dvec, glu_w, glu_b)


def s5_out_bwd(y, z, glu_w, glu_b, dycat, name):
    s = y.shape[0]
    tm = _tile(s, 512)
    wd = S5_WIDTH

    def body(y_ref, u_ref, w_ref, b_ref, do_ref, dy_ref, dd_ref, dw_ref, db_ref):
        @pl.when(pl.program_id(0) == 0)
        def _():
            dd_ref[...] = jnp.zeros_like(dd_ref)
            dw_ref[...] = jnp.zeros_like(dw_ref)
            db_ref[...] = jnp.zeros_like(db_ref)

        g, gvjp = jax.vjp(_gelu, y_ref[...])
        gb = g.astype(BF16)
        sg = _sigmoid(_dot(gb, w_ref[...]) + b_ref[...])
        do = do_ref[...]
        dt = do * g * sg * (1.0 - sg)
        dtb = dt.astype(BF16)
        dg = do * sg + _dot(dtb, w_ref[...], NT)
        (dy,) = gvjp(dg)
        dy_ref[...] = dy
        dd_ref[...] += jnp.sum(dy * u_ref[...], axis=0, keepdims=True)
        dw_ref[...] += _dot(gb, dtb, TN)
        db_ref[...] += jnp.sum(dt, axis=0, keepdims=True)

    row = pl.BlockSpec((tm, wd), lambda i: (i, 0))
    vec = pl.BlockSpec((1, wd), lambda i: (0, 0))
    mat = pl.BlockSpec((wd, wd), lambda i: (0, 0))
    return _call(body, name, (s // tm,),
                 [row, pl.BlockSpec((tm, wd), lambda i: (i, Z_S5 // wd)), mat, vec,
                  pl.BlockSpec((tm, wd), lambda i: (i, (ATTN_WIDTH + DN_WIDTH) // wd))],
                 [row, vec, mat, vec], [_sds((s, wd), F32), _sds((1, wd), F32), _sds((wd, wd), F32), _sds((1, wd), F32)])(
        y, z, glu_w, glu_b, dycat)


def assemble_dz(dq, dkc, dkp, dvc, dvp, ddn, dzg, dus, dys, dvec, dzs, name):
    s = dq.shape[0]
    w = WINDOW
    nb = s // w
    nxt = lambda i: jnp.minimum(i + 1, nb - 1)

    def body(dq_ref, dkc_ref, dkp_ref, dvc_ref, dvp_ref, ddn_ref, dzg_ref, dus_ref, dys_ref, dv_ref, dzs_ref, o_ref):
        live = (pl.program_id(0) < nb - 1).astype(F32)
        o_ref[:, Z_Q:Z_K] = dq_ref[...].astype(BF16)
        o_ref[:, Z_K:Z_V] = (dkc_ref[...] + live * dkp_ref[...]).astype(BF16)
        o_ref[:, Z_V:Z_DN] = (dvc_ref[...] + live * dvp_ref[...]).astype(BF16)
        o_ref[:, Z_DN:Z_ZG] = ddn_ref[...].astype(BF16)
        o_ref[:, Z_ZG:Z_S5] = dzg_ref[...].astype(BF16)
        o_ref[:, Z_S5:Z_SM] = (dus_ref[...] + dv_ref[...] * dys_ref[...]).astype(BF16)
        o_ref[:, Z_SM:Z_ALL] = dzs_ref[...].astype(BF16)

    def blk(width, f=lambda i: i):
        return pl.BlockSpec((w, width), lambda i: (f(i), 0))

    return _call(body, name, (nb,),
                 [blk(ATTN_WIDTH), blk(ATTN_KV_WIDTH), blk(ATTN_KV_WIDTH, nxt), blk(ATTN_KV_WIDTH), blk(ATTN_KV_WIDTH, nxt),
                  blk(3 * DN_WIDTH), blk(DN_WIDTH), blk(S5_WIDTH), blk(S5_WIDTH), pl.BlockSpec((1, S5_WIDTH), lambda i: (0, 0)),
                  blk(128)],
                 blk(Z_ALL), _sds((s, Z_ALL), BF16), sem=("parallel",))(dq, dkc, dkp, dvc, dvp, ddn, dzg, dus, dys, dvec, dzs)


def _my_place():
    return lax.axis_index("x"), lax.axis_index("y"), lax.axis_index("c")


def _peer(place, p):
    x, y, c = place
    px = 1 - x if p & 4 else x
    py = 1 - y if p & 2 else y
    pc = 1 - c if p & 1 else c
    return (px, py, pc), 4 * px + 2 * py + pc


def exchange(arrays, scatter, name):
    na = len(arrays)

    def body(*refs):
        srcs, dsts = refs[:na], refs[na:2 * na]
        send_sems, recv_sems, local_sems = refs[2 * na:]
        place = _my_place()
        me = 4 * place[0] + 2 * place[1] + place[2]
        copies = []
        for k in range(na):
            mine = srcs[k].at[me] if scatter else srcs[k]
            loc = pltpu.make_async_copy(mine, dsts[k].at[me], local_sems.at[k])
            loc.start()
            copies.append(loc)
        sends = []
        for p in range(1, N_DEV):
            peer, pid = _peer(place, p)
            for k in range(na):
                src = srcs[k].at[pid] if scatter else srcs[k]
                cp = pltpu.make_async_remote_copy(src_ref=src, dst_ref=dsts[k].at[me], send_sem=send_sems.at[k, p - 1],
                                                  recv_sem=recv_sems.at[k, p - 1], device_id=peer,
                                                  device_id_type=pl.DeviceIdType.MESH)
                cp.start()
                sends.append(cp)
        for p in range(1, N_DEV):
            peer, pid = _peer(place, p)
            for k in range(na):
                src = srcs[k].at[me] if scatter else srcs[k]
                pltpu.make_async_remote_copy(src_ref=src, dst_ref=dsts[k].at[pid], send_sem=send_sems.at[k, p - 1],
                                             recv_sem=recv_sems.at[k, p - 1], device_id=peer,
                                             device_id_type=pl.DeviceIdType.MESH).wait_recv()
        for cp in sends:
            cp.wait_send()
        for cp in copies:
            cp.wait()

    outs = [_sds((N_DEV,) + tuple(a.shape[1:] if scatter else a.shape), a.dtype) for a in arrays]
    anyspec = pl.BlockSpec(memory_space=pl.ANY)
    return pl.pallas_call(
        body, name=name, in_specs=[anyspec] * na, out_specs=[anyspec] * na, out_shape=outs,
        scratch_shapes=[pltpu.SemaphoreType.DMA((na, N_DEV - 1)), pltpu.SemaphoreType.DMA((na, N_DEV - 1)),
                        pltpu.SemaphoreType.DMA((na,))])(*arrays)


_HBM = pl.BlockSpec(memory_space=pltpu.HBM)
_SEM = pl.BlockSpec(memory_space=pltpu.SEMAPHORE)
_DATAFLOW = pltpu.SideEffectType.DATAFLOW_SIDE_EFFECTING


def _split_copies(srcs, lands, send_sems, recv_sems, scatter, arriving):
    place = _my_place()
    me = 4 * place[0] + 2 * place[1] + place[2]
    out = []
    for p in range(1, N_DEV):
        peer, pid = _peer(place, p)
        for k in range(len(srcs)):
            i = k * (N_DEV - 1) + p - 1
            src = srcs[k].at[pid] if scatter else srcs[k]
            dst = lands[k].at[pid] if arriving else lands[k].at[me]
            out.append(pltpu.make_async_remote_copy(src_ref=src, dst_ref=dst, send_sem=send_sems.at[i], recv_sem=recv_sems.at[i],
                                                    device_id=peer, device_id_type=pl.DeviceIdType.MESH))
    return out


def exchange_start(groups, scatter, name):
    arrays = [a for g in groups for a in g]
    na, ng = len(arrays), len(groups)
    first = [sum(len(g) for g in groups[:i]) for i in range(ng)]
    me = 4 * lax.axis_index("x") + 2 * lax.axis_index("y") + lax.axis_index("c")
    lands = []
    for a in arrays:
        own = lax.dynamic_index_in_dim(a, me, 0, keepdims=True) if scatter else a[None]
        shape = (N_DEV,) + tuple(own.shape[1:])
        land = lax.dynamic_update_slice(lax.empty(shape, a.dtype), own, (me,) + (0,) * (len(shape) - 1))
        lands.append(pltpu.with_memory_space_constraint(land, pltpu.HBM))
    srcs = [pltpu.with_memory_space_constraint(a, pltpu.HBM) for a in arrays]

    def body(*refs):
        src_refs, land_refs = refs[:na], refs[na:2 * na]
        sems = refs[2 * na:2 * na + 2 * ng]
        token = refs[-1]
        for i, g in enumerate(groups):
            sl = slice(first[i], first[i] + len(g))
            for send in _split_copies(src_refs[sl], land_refs[sl], sems[2 * i], sems[2 * i + 1], scatter, False):
                send.start()
        token[...] = jnp.zeros_like(token)

    sem_shapes = []
    for g in groups:
        sem_shapes += [pltpu.SemaphoreType.DMA((len(g) * (N_DEV - 1),))] * 2
    outs = pl.pallas_call(
        body, name=name,
        out_shape=(*sem_shapes, *[pltpu.HBM(a.shape, a.dtype) for a in srcs], *[pltpu.HBM(a.shape, a.dtype) for a in lands],
                   _sds((8, 128), F32)),
        in_specs=[_HBM] * (2 * na), out_specs=(*[_SEM] * (2 * ng), *[_HBM] * (2 * na), pl.BlockSpec(memory_space=pltpu.VMEM)),
        input_output_aliases={i: 2 * ng + i for i in range(2 * na)},
        compiler_params=pltpu.CompilerParams(has_side_effects=_DATAFLOW))(*srcs, *lands)
    src_out, land_out = outs[2 * ng:2 * ng + na], outs[2 * ng + na:2 * ng + 2 * na]
    handles = [(outs[2 * i], outs[2 * i + 1], src_out[first[i]:first[i] + len(g)], land_out[first[i]:first[i] + len(g)])
               for i, g in enumerate(groups)]
    return handles, outs[-1]


def exchange_wait(handle, scatter, name, after):
    send_sems, recv_sems, srcs, lands = handle
    na = len(srcs)

    def body(*refs):
        src_refs, land_refs = refs[:na], refs[na:2 * na]
        for send in _split_copies(src_refs, land_refs, refs[2 * na], refs[2 * na + 1], scatter, False):
            send.wait_send()
        for recv in _split_copies(src_refs, land_refs, refs[2 * na], refs[2 * na + 1], scatter, True):
            recv.wait_recv()

    outs = pl.pallas_call(
        body, name=name, out_shape=tuple(pltpu.HBM(a.shape, a.dtype) for a in (*srcs, *lands)),
        in_specs=[_HBM] * (2 * na) + [_SEM, _SEM, pl.BlockSpec(memory_space=pl.ANY)], out_specs=tuple([_HBM] * (2 * na)),
        input_output_aliases={i: i for i in range(2 * na)},
        compiler_params=pltpu.CompilerParams(has_side_effects=_DATAFLOW))(*srcs, *lands, send_sems, recv_sems, after)
    return list(outs[na:])


def _adamw(w, g, m, v):
    m = ADAM_B1 * m + (1.0 - ADAM_B1) * g
    v = ADAM_B2 * v + (1.0 - ADAM_B2) * (g * g)
    m_hat = m / (1.0 - ADAM_B1 ** ADAM_STEP)
    v_hat = v / (1.0 - ADAM_B2 ** ADAM_STEP)
    return -ADAM_LR * (m_hat / (jnp.sqrt(v_hat) + ADAM_EPS) + ADAM_WD * w), m, v


def reduce_adamw(parts, w4, m4, v4, layer, name, stacked=None):
    nl, r, c = w4.shape
    tr = _tile(r, 256)

    def body(p_ref, w_ref, m_ref, v_ref, *rest):
        g_ref, d_ref, nm_ref, nv_ref = rest[-4:]
        g = p_ref[0].astype(F32)
        for d in range(1, N_DEV):
            g = g + p_ref[d].astype(F32)
        g_ref[...] = g
        d_ref[...], nm_ref[...], nv_ref[...] = _adamw(w_ref[...], g, m_ref[...], v_ref[...])

    lay = pl.BlockSpec((None, tr, c), lambda i: (layer, i, 0))
    osd = _sds((nl, r, c), F32)
    ins = [parts, w4, m4, v4] + (list(stacked) if stacked is not None else [])
    specs = [pl.BlockSpec((N_DEV, tr, c), lambda i: (0, i, 0)), lay, lay, lay]
    specs += [pl.BlockSpec(memory_space=pl.ANY)] * (len(ins) - 4)
    return pl.pallas_call(
        body, name=name, grid=(r // tr,), in_specs=specs, out_specs=[lay, lay, lay, lay], out_shape=[osd, osd, osd, osd],
        input_output_aliases={4 + k: k for k in range(len(ins) - 4)},
        compiler_params=pltpu.CompilerParams(dimension_semantics=("parallel",), vmem_limit_bytes=VMEM_LIMIT))(*ins)


_SM_NAT = ATTN_WIDTH + 2 * ATTN_KV_WIDTH + 4 * DN_WIDTH


def _win_to_zall(w):
    pad = jnp.zeros((w.shape[0], 128 - 2 * DN_HEADS), w.dtype)
    return jnp.concatenate([w[:, :_SM_NAT], w[:, _SM_NAT + 2 * DN_HEADS:], w[:, _SM_NAT:_SM_NAT + 2 * DN_HEADS], pad], axis=1)


def _zall_to_win(g):
    return jnp.concatenate([g[:, :Z_ZG + DN_WIDTH], g[:, Z_SM:Z_SM + 2 * DN_HEADS], g[:, Z_S5:Z_SM]], axis=1)


def _block_diag(t):
    g, a, b = t.shape
    eye = jnp.eye(g, dtype=t.dtype)
    return (t[:, :, None, :] * eye[:, None, :, None]).reshape(g * a, g * b)


def _block_diag_extract(m, g):
    a, b = m.shape[0] // g, m.shape[1] // g
    eye = jnp.eye(g, dtype=m.dtype)
    return jnp.sum(m.reshape(g, a, g, b) * eye[:, None, :, None], axis=2)


def _rope_tables(s):
    half = HEAD_DIM // 2
    inv_freq = ROPE_THETA ** (-jnp.arange(half, dtype=F32) / half)
    ang = jnp.arange(s, dtype=F32)[:, None] * inv_freq[None, :]
    cos, sin = jnp.cos(ang), jnp.sin(ang)
    return jnp.concatenate([cos, cos], axis=1), jnp.concatenate([-sin, sin], axis=1)


def _row(v):
    return v.reshape(1, -1)


def _ffn_fwd(x, g_pre, g_post, weight, tag):
    h = rmsnorm_fwd(x, g_pre, f"{tag}_norm")
    wg, wu = weight(f"{tag}_w_gate", h), weight(f"{tag}_w_up", h)
    a, b, u = ffn_up(h, wg, wu, f"{tag}_up", dep=weight("token", None))
    wd = weight(f"{tag}_w_down", u)
    y, xn = down_norm(u, wd, x, g_post, FFN_RES_WEIGHT, f"{tag}_down")
    return xn, (x, h, a, b, u, y, wg, wu, wd)


def _ffn_bwd(dxn, saved, g_pre, g_post, on_grads, tag):
    x, h, a, b, u, y, wg, wu, wd = saved
    dy, dg_post = norm_bwd(dxn, y, g_post, FFN_RES_WEIGHT, None, BF16, f"{tag}_bnorm_post")
    da, db = ffn_down_bwd(dy, wd, a, b, f"{tag}_bdown")
    dwd = mm_tn(u, dy[None], BF16, f"{tag}_dwd")
    dwg = mm_tn(h[None], da, BF16, f"{tag}_dwg")
    dwu = mm_tn(h[None], db, BF16, f"{tag}_dwu")
    tok = on_grads({f"{tag}_w_gate": dwg, f"{tag}_w_up": dwu, f"{tag}_w_down": dwd})
    dh = mm_nt_acc([(da, wg), (db, wu)], f"{tag}_dh", tm=512, tn=2048, dep=tok)
    dx, dg_pre = norm_bwd(dh, x, g_pre, 1.0, dxn, F32, f"{tag}_bnorm_pre")
    return dx, dict(g_pre=dg_pre, g_post=dg_post)


def _s5_layouts(p):
    are, aim = p["s5_a_re"].reshape(S5_P, 1), p["s5_a_im"].reshape(S5_P, 1)
    ldt = jnp.repeat(p["s5_log_dt"], S5_STATE).reshape(S5_P, 1)
    bre, bim = p["s5_b_re"].reshape(S5_P, S5_GROUP_CH), p["s5_b_im"].reshape(S5_P, S5_GROUP_CH)
    return are, aim, ldt, bre, bim


def _mix_fwd(x, p, weight, cos_f, sin_s, tag):
    h = rmsnorm_fwd(x, _row(p["mix_norm_pre"]), f"{tag}_norm")
    w_all, glu_w = weight("w_all", h), weight("s5_glu_w", h)
    z = mm_nn(h, w_all, f"{tag}_win")
    y_attn = attn_fwd(z, cos_f, sin_s, p["attn_sinks"], f"{tag}_attn")
    alog_b = jnp.broadcast_to(p["dn_a_log"][:, None], (DN_HEADS, 128))
    dtb_b = jnp.broadcast_to(p["dn_dt_bias"][:, None], (DN_HEADS, 128))
    conv_w = weight("dn_conv_w", h)
    qkv = dn_pre_fwd(z, conv_w, f"{tag}_dnpre")
    bb, gb = dn_gates_fwd(z, alog_b, dtb_b, f"{tag}_dngate")
    u, w, qd, kd, at, eg = dn_intra_fwd(qkv, gb, bb, f"{tag}_dnintra")
    o, states = dn_scan_fwd(u, w, qd, kd, at, eg, f"{tag}_dnscan")
    y_dn = dn_out_fwd(o, z, _row(p["dn_norm_w"]), f"{tag}_dnout")
    s5cols = _s5_layouts(p)
    abr, abi, bbr, bbi = s5_params_fwd(*s5cols, f"{tag}_s5par")
    tb = lambda t: jnp.transpose(t.reshape(S5_GROUPS, S5_STATE, S5_GROUP_CH), (0, 2, 1))
    b_blk = jnp.concatenate([_block_diag(tb(bbr)), _block_diag(tb(bbi))], axis=1).astype(BF16)
    tc = lambda t: jnp.transpose(t, (0, 2, 1))
    c_blk = jnp.concatenate([_block_diag(tc(p["s5_c_re"])), -_block_diag(tc(p["s5_c_im"]))], axis=0).astype(BF16)
    bu = mm_nn(z, b_blk, f"{tag}_s5bu", tn=1024, col0=Z_S5)
    xs = s5_scan_fwd(bu, abr.reshape(1, S5_P), abi.reshape(1, S5_P), f"{tag}_s5scan")
    ypre = mm_nn_acc(xs, c_blk.reshape(2, S5_P, S5_WIDTH), f"{tag}_s5c")
    y5, y_s5 = s5_out_fwd(ypre, z, _row(p["s5_d"]), glu_w, _row(p["s5_glu_b"]), f"{tag}_s5out")
    ycat = jnp.concatenate([y_attn, y_dn, y_s5], axis=1)
    w_out = weight("w_out", ycat)
    mixed, xn = down_norm(ycat[None], w_out[None], x, _row(p["mix_norm_post"]), 1.0, f"{tag}_wout")
    saved = dict(x=x, h=h, z=z, qkv=qkv, bb=bb, gb=gb, dn=(u, w, qd, kd, at, eg), states=states, o=o, s5cols=s5cols,
                 abr=abr, abi=abi, b_blk=b_blk, c_blk=c_blk, xs=xs, y5=y5, ycat=ycat, mixed=mixed,
                 alog_b=alog_b, dtb_b=dtb_b, w_all=w_all, w_out=w_out, glu_w=glu_w, conv_w=conv_w)
    return xn, saved


def _mix_bwd(dxn, sv, p, on_grads, cos_f, sin_s, tag):
    z = sv["z"]
    w_all, w_out, glu_w = sv["w_all"], sv["w_out"], sv["glu_w"]
    g = {}
    dmixed, g["mix_norm_post"] = norm_bwd(dxn, sv["mixed"], _row(p["mix_norm_post"]), 1.0, None, BF16, f"{tag}_bnorm_post")
    g["w_out"] = mm_tn(sv["ycat"][None], dmixed[None], BF16, f"{tag}_dwout", tn=1024)[0]
    dycat = mm_nt_acc([(dmixed[None], w_out[None])], f"{tag}_dycat")
    dq, dkc, dkp, dvc, dvp, dsink = attn_bwd(z, cos_f, sin_s, p["attn_sinks"], dycat, f"{tag}_battn")
    g["attn_sinks"] = dsink[:, 0]
    do, dzg, dnw = dn_out_bwd(sv["o"], z, _row(p["dn_norm_w"]), dycat, f"{tag}_bdnout")
    g["dn_norm_w"] = dnw[0]
    cts = dn_scan_bwd(*sv["dn"], sv["states"], do, f"{tag}_bdnscan")
    dqn, dkn, dvn, dgb, dbb = dn_intra_bwd(sv["qkv"], sv["gb"], sv["bb"], cts, f"{tag}_bdnintra")
    dzs, dal, ddt = dn_gates_bwd(z, sv["alog_b"], sv["dtb_b"], dbb, dgb, f"{tag}_bdngate")
    g["dn_a_log"], g["dn_dt_bias"] = dal[:, 0], ddt[:, 0]
    ddn, g["dn_conv_w"] = dn_pre_bwd(z, sv["conv_w"], jnp.concatenate([dqn, dkn, dvn], axis=1), f"{tag}_bdnpre")
    dy5, dd, dglu, dglub = s5_out_bwd(sv["y5"], z, glu_w, _row(p["s5_glu_b"]), dycat, f"{tag}_bs5out")
    g["s5_d"], g["s5_glu_w"], g["s5_glu_b"] = dd[0], dglu, dglub[0]
    dxs = mm_nt_acc([(dy5[None], sv["c_blk"][None])], f"{tag}_bs5c", tn=1024)
    dc_blk = mm_tn(sv["xs"], dy5[None], F32, f"{tag}_ds5c", tk=256)
    ex = lambda m: jnp.transpose(_block_diag_extract(m, S5_GROUPS), (0, 2, 1))
    g["s5_c_re"], g["s5_c_im"] = ex(dc_blk[0]), -ex(dc_blk[1])
    dbu, dar, dai = s5_scan_bwd(dxs, sv["xs"], sv["abr"].reshape(1, S5_P), sv["abi"].reshape(1, S5_P), f"{tag}_bs5scan")
    b_planes = jnp.transpose(sv["b_blk"].reshape(S5_WIDTH, 2, S5_P), (1, 0, 2))
    dus = mm_nt_acc([(dbu, b_planes)], f"{tag}_bs5bu")
    u_s5 = z[:, Z_S5:Z_SM]
    db_blk = mm_tn(u_s5[None], dbu, F32, f"{tag}_ds5b", tn=1024)
    exb = lambda m: jnp.transpose(_block_diag_extract(m, S5_GROUPS), (0, 2, 1)).reshape(S5_P, S5_GROUP_CH)
    dcols = s5_params_bwd(*sv["s5cols"], (dar.reshape(S5_P, 1), dai.reshape(S5_P, 1), exb(db_blk[0]), exb(db_blk[1])),
                          f"{tag}_bs5par")
    g["s5_a_re"] = dcols[0].reshape(S5_GROUPS, S5_STATE)
    g["s5_a_im"] = dcols[1].reshape(S5_GROUPS, S5_STATE)
    g["s5_log_dt"] = jnp.sum(dcols[2].reshape(S5_GROUPS, S5_STATE), axis=1)
    g["s5_b_re"] = dcols[3].reshape(S5_GROUPS, S5_STATE, S5_GROUP_CH)
    g["s5_b_im"] = dcols[4].reshape(S5_GROUPS, S5_STATE, S5_GROUP_CH)
    dz = assemble_dz(dq, dkc, dkp, dvc, dvp, ddn, dzg, dus, dy5, _row(p["s5_d"]), dzs, f"{tag}_dz")
    g["w_all"] = mm_tn(sv["h"][None], dz[None], BF16, f"{tag}_dwin")[0]
    dwin = _zall_to_win(g.pop("w_all"))
    d_model = dwin.shape[0]
    tok = on_grads({"w_in": jnp.transpose(dwin.reshape(d_model, N_DEV, IN_WIDTH // N_DEV), (1, 0, 2)),
                    "s5_glu_w": g.pop("s5_glu_w").astype(BF16).reshape(N_DEV, S5_WIDTH // N_DEV, S5_WIDTH),
                    "w_out": g.pop("w_out").reshape(N_DEV, MIX_WIDTH // N_DEV, d_model)})
    dh = mm_nt_acc([(dz[None], w_all[None])], f"{tag}_dh", dep=tok)
    dx, g["mix_norm_pre"] = norm_bwd(dh, sv["x"], _row(p["mix_norm_pre"]), 1.0, dxn, F32, f"{tag}_bnorm_pre")
    return dx, g


BIG = ("ff1_w_gate", "ff1_w_up", "ff1_w_down", "w_in", "s5_glu_w", "w_out", "ff2_w_gate", "ff2_w_up", "ff2_w_down")
SMALL = ("ff1_norm_pre", "ff1_norm_post", "mix_norm_pre", "attn_sinks", "dn_conv_w", "dn_a_log", "dn_dt_bias", "dn_norm_w",
         "s5_a_re", "s5_a_im", "s5_log_dt", "s5_b_re", "s5_b_im", "s5_c_re", "s5_c_im", "s5_d", "s5_glu_b",
         "mix_norm_post", "ff2_norm_pre", "ff2_norm_post")
GATHER_GROUPS = (("ff1_w_gate", "ff1_w_up"), ("ff1_w_down",), ("w_in", "s5_glu_w", "dn_conv_w"), ("w_out",),
                 ("ff2_w_gate", "ff2_w_up"), ("ff2_w_down",))
WEIGHTS = ("ff1_norm_pre", "ff1_w_gate", "ff1_w_up", "ff1_w_down", "ff1_norm_post", "mix_norm_pre", "w_in", "attn_sinks",
           "dn_conv_w", "dn_a_log", "dn_dt_bias", "dn_norm_w", "s5_a_re", "s5_a_im", "s5_log_dt", "s5_b_re", "s5_b_im",
           "s5_c_re", "s5_c_im", "s5_d", "s5_glu_w", "s5_glu_b", "w_out", "mix_norm_post", "ff2_norm_pre", "ff2_w_gate",
           "ff2_w_up", "ff2_w_down", "ff2_norm_post")


def _pack(parts):
    rows = []
    for a in parts:
        n = a.size
        r = -(-n // 1024) * 8
        rows.append(jnp.pad(a.reshape(-1), (0, r * 128 - n)).reshape(r, 128))
    return jnp.concatenate(rows, axis=0)


def _unpack(mat, shapes):
    out, off = [], 0
    for shp in shapes:
        n = int(np.prod(shp))
        r = -(-n // 1024) * 8
        out.append(mat[off:off + r].reshape(-1)[:n].reshape(shp))
        off += r
    return out


def local_step(x, target, smalls, weight, on_grads):
    depth = len(smalls)
    cos_f, sin_s = _rope_tables(x.shape[0])
    xs = x
    saved = []
    for l in range(depth):
        p = smalls[l]
        wl = functools.partial(weight, l)
        xs, s1 = _ffn_fwd(xs, _row(p["ff1_norm_pre"]), _row(p["ff1_norm_post"]), wl, "ff1")
        xs, s2 = _mix_fwd(xs, p, wl, cos_f, sin_s, "mix")
        xs, s3 = _ffn_fwd(xs, _row(p["ff2_norm_pre"]), _row(p["ff2_norm_post"]), wl, "ff2")
        saved.append((s1, s2, s3))

    loss_vec, dx = loss_and_grad(xs, target, "loss")

    small_g = [None] * depth
    for l in reversed(range(depth)):
        p = smalls[l]
        gl = functools.partial(on_grads, l)
        s1, s2, s3 = saved[l]
        dx, g3 = _ffn_bwd(dx, s3, _row(p["ff2_norm_pre"]), _row(p["ff2_norm_post"]), gl, "ff2")
        dx, g2 = _mix_bwd(dx, s2, p, gl, cos_f, sin_s, "mix")
        dx, g1 = _ffn_bwd(dx, s1, _row(p["ff1_norm_pre"]), _row(p["ff1_norm_post"]), gl, "ff1")
        sg = {n: g2[n] for n in SMALL if n in g2}
        sg.update(ff1_norm_pre=g1["g_pre"][0], ff1_norm_post=g1["g_post"][0], ff2_norm_pre=g3["g_pre"][0], ff2_norm_post=g3["g_post"][0],
                  mix_norm_pre=g2["mix_norm_pre"][0], mix_norm_post=g2["mix_norm_post"][0])
        small_g[l] = sg
    return loss_vec, dx, small_g


def kernel(x, ff1_norm_pre, ff1_w_gate, ff1_w_up, ff1_w_down, ff1_norm_post, mix_norm_pre, w_in, attn_sinks, dn_conv_w, dn_a_log, dn_dt_bias, dn_norm_w, s5_a_re, s5_a_im, s5_log_dt, s5_b_re, s5_b_im, s5_c_re, s5_c_im, s5_d, s5_glu_w, s5_glu_b, w_out, mix_norm_post, ff2_norm_pre, ff2_w_gate, ff2_w_up, ff2_w_down, ff2_norm_post, loss_target, m_ff1_norm_pre, m_ff1_w_gate, m_ff1_w_up, m_ff1_w_down, m_ff1_norm_post, m_mix_norm_pre, m_w_in, m_attn_sinks, m_dn_conv_w, m_dn_a_log, m_dn_dt_bias, m_dn_norm_w, m_s5_a_re, m_s5_a_im, m_s5_log_dt, m_s5_b_re, m_s5_b_im, m_s5_c_re, m_s5_c_im, m_s5_d, m_s5_glu_w, m_s5_glu_b, m_w_out, m_mix_norm_post, m_ff2_norm_pre, m_ff2_w_gate, m_ff2_w_up, m_ff2_w_down, m_ff2_norm_post, v_ff1_norm_pre, v_ff1_w_gate, v_ff1_w_up, v_ff1_w_down, v_ff1_norm_post, v_mix_norm_pre, v_w_in, v_attn_sinks, v_dn_conv_w, v_dn_a_log, v_dn_dt_bias, v_dn_norm_w, v_s5_a_re, v_s5_a_im, v_s5_log_dt, v_s5_b_re, v_s5_b_im, v_s5_c_re, v_s5_c_im, v_s5_d, v_s5_glu_w, v_s5_glu_b, v_w_out, v_mix_norm_post, v_ff2_norm_pre, v_ff2_w_gate, v_ff2_w_up, v_ff2_w_down, v_ff2_norm_post):
    args = dict(locals())
    W = {n: args[n] for n in WEIGHTS}
    M = {n: args["m_" + n] for n in WEIGHTS}
    V = {n: args["v_" + n] for n in WEIGHTS}
    depth = ff1_norm_pre.shape[0]
    d_model = x.shape[2]
    me = 4 * lax.axis_index("x") + 2 * lax.axis_index("y") + lax.axis_index("c")

    conv_sh = dn_conv_w.shape[2]

    def small_params(l):
        return {n: W[n][l] for n in SMALL if n != "dn_conv_w"}

    gathered_w, gather_handles, tokens = {}, {}, []
    group_of = {n: i for i, grp in enumerate(GATHER_GROUPS) for n in grp}

    def start_gather(l):
        shards = {n: cast_bf16(W[n], l, f"cast_{n}") for n in BIG}
        shards["dn_conv_w"] = dn_conv_w[l]
        handles, tok = exchange_start([[shards[n] for n in grp] for grp in GATHER_GROUPS], False, f"gather_start_l{l}")
        gather_handles.update({(l, i): h for i, h in enumerate(handles)})
        tokens.append(tok)

    def weight(l, name, after):
        if name == "token":
            return tokens.pop() if tokens else None
        key = "w_in" if name == "w_all" else name
        i = group_of[key]
        if (l, i) in gather_handles:
            got = dict(zip(GATHER_GROUPS[i], exchange_wait(gather_handles.pop((l, i)), False, f"gather_wait_l{l}_g{i}", after)))
            if i == 0 and l + 1 < depth:
                start_gather(l + 1)
            if "w_in" in got:
                got["dn_conv_w"] = jnp.transpose(got["dn_conv_w"], (1, 0, 2)).reshape(DN_CONV, N_DEV * conv_sh)
                got["w_all"] = _win_to_zall(jnp.transpose(got["w_in"], (1, 0, 2)).reshape(d_model, IN_WIDTH))
                got["s5_glu_w"] = got["s5_glu_w"].reshape(S5_WIDTH, S5_WIDTH)
            if "w_out" in got:
                got["w_out"] = got["w_out"].reshape(MIX_WIDTH, d_model)
            gathered_w.update({(l, n): a for n, a in got.items()})
        return gathered_w[(l, name)]

    stacked = {n: None for n in BIG}
    in_flight = []

    def finish_scatter(after):
        l, names, handle = in_flight.pop(0)
        recv = dict(zip(names, exchange_wait(handle, True, f"scatter_wait_l{l}_{names[0]}", after)))
        for n in names:
            stacked[n] = reduce_adamw(recv[n], W[n], M[n], V[n], l, f"adamw_{n}", stacked[n])

    def on_grads(l, grads):
        names = tuple(grads)
        (handle,), tok = exchange_start([[grads[n] for n in names]], True, f"scatter_start_l{l}_{names[0]}")
        in_flight.append((l, names, handle))
        if len(in_flight) > 1:
            finish_scatter(tok)
        return tok

    start_gather(0)
    tokens.clear()
    smalls = [small_params(l) for l in range(depth)]
    loss_vec, dx, small_g = local_step(x[0], loss_target[0], smalls, weight, on_grads)
    shapes = [(depth,) + ((DN_CONV, N_DEV * conv_sh) if n == "dn_conv_w" else tuple(W[n].shape[1:])) for n in SMALL]
    packed = _pack([jnp.stack([small_g[l][n] for l in range(depth)]) for n in SMALL])
    (small_handle,), small_tok = exchange_start([[packed]], False, "gather_small_start")
    while in_flight:
        finish_scatter(small_tok)
    gathered = exchange_wait(small_handle, False, "gather_small_wait", stacked[BIG[0]][0])[0]
    loss = lax.psum(loss_vec[0, 0], ("x", "y", "c"))

    def shard_of(n, full):
        return lax.dynamic_slice_in_dim(full, me * conv_sh, conv_sh, axis=2) if n == "dn_conv_w" else full

    conv_pad = lambda t: jnp.tile(t, (1, 1, N_DEV))
    wp = _pack([conv_pad(W[n]) if n == "dn_conv_w" else W[n] for n in SMALL])
    mp = _pack([conv_pad(M[n]) if n == "dn_conv_w" else M[n] for n in SMALL])
    vp = _pack([conv_pad(V[n]) if n == "dn_conv_w" else V[n] for n in SMALL])
    sm = reduce_adamw(gathered, wp[None], mp[None], vp[None], 0, "adamw_small")
    small_out = [dict(zip(SMALL, [shard_of(n, t) for n, t in zip(SMALL, _unpack(o[0], shapes))])) for o in sm]

    outs = []
    for kind in range(4):
        for n in WEIGHTS:
            if n in BIG:
                outs.append(stacked[n][kind])
            else:
                outs.append(small_out[kind][n])
    return (loss, dx[None], *outs)
```

```python
import functools
import math

import jax
import jax.numpy as jnp
import numpy as np
from jax import lax
from jax.experimental import pallas as pl
from jax.experimental.pallas import tpu as pltpu

F32 = jnp.float32
BF16 = jnp.bfloat16

N_DEV = 8
DEPTH = 4
ATTN_HEADS = 8
ATTN_KV_HEADS = 2
HEAD_DIM = 128
WINDOW = 128
ROPE_THETA = 10000.0
DN_HEADS = 4
DN_HEAD_DIM = 128
DN_CONV = 4
DN_CHUNK = 64
S5_GROUPS = 32
S5_GROUP_CH = 16
S5_STATE = 64
ATTN_WIDTH = ATTN_HEADS * HEAD_DIM
ATTN_KV_WIDTH = ATTN_KV_HEADS * HEAD_DIM
DN_WIDTH = DN_HEADS * DN_HEAD_DIM
S5_WIDTH = S5_GROUPS * S5_GROUP_CH
S5_P = S5_GROUPS * S5_STATE
MIX_WIDTH = ATTN_WIDTH + DN_WIDTH + S5_WIDTH
IN_WIDTH = ATTN_WIDTH + 2 * ATTN_KV_WIDTH + 4 * DN_WIDTH + 2 * DN_HEADS + S5_WIDTH
Z_Q, Z_K, Z_V = 0, ATTN_WIDTH, ATTN_WIDTH + ATTN_KV_WIDTH
Z_DN = ATTN_WIDTH + 2 * ATTN_KV_WIDTH
Z_ZG = Z_DN + 3 * DN_WIDTH
Z_S5 = Z_ZG + DN_WIDTH
Z_SM = Z_S5 + S5_WIDTH
Z_ALL = Z_SM + 128
FFN_RES_WEIGHT = 0.5
NORM_EPS = 1e-6
ADAM_LR, ADAM_B1, ADAM_B2, ADAM_EPS, ADAM_WD, ADAM_STEP = 0.001, 0.9, 0.999, 1e-08, 0.01, 10

VMEM_LIMIT = 56 * 1024 * 1024
NEG = -1e30

NN = (((1,), (0,)), ((), ()))
NT = (((1,), (1,)), ((), ()))
TN = (((0,), (0,)), ((), ()))


def _dot(a, b, dims=NN, prec=None):
    return lax.dot_general(a, b, dims, preferred_element_type=F32, precision=prec)


def _tile(n, pref, mult=8):
    if n <= pref:
        return n
    t = (pref // mult) * mult
    while t > mult and n % t:
        t -= mult
    assert n % t == 0, (n, pref)
    return t


def _call(body, name, grid, in_specs, out_specs, out_shape, scratch=(), sem=None):
    if sem is None:
        sem = ("arbitrary",) * len(grid)
    return pl.pallas_call(
        body, name=name, grid=grid, in_specs=in_specs, out_specs=out_specs, out_shape=out_shape,
        scratch_shapes=list(scratch),
        compiler_params=pltpu.CompilerParams(dimension_semantics=sem, vmem_limit_bytes=VMEM_LIMIT))


def _sds(shape, dtype):
    return jax.ShapeDtypeStruct(tuple(shape), dtype)


def _sigmoid(x):
    return 1.0 / (1.0 + jnp.exp(-x))


def _silu_and_grad(a):
    sg = _sigmoid(a)
    return a * sg, sg * (1.0 + a * (1.0 - sg))


def _softplus(x):
    return jnp.maximum(x, 0.0) + jnp.log(1.0 + jnp.exp(-jnp.abs(x)))


def _rms(x, g):
    r = lax.rsqrt(jnp.mean(x * x, axis=-1, keepdims=True) + NORM_EPS)
    return x * r * g


def _rms_bwd(dout, y, g):
    r = lax.rsqrt(jnp.mean(y * y, axis=-1, keepdims=True) + NORM_EPS)
    n = y * r
    dn = dout * g
    dy = r * (dn - n * jnp.mean(dn * n, axis=-1, keepdims=True))
    return dy, jnp.sum(dout * n, axis=0, keepdims=True)


def _rope(x, cos_f, sin_s):
    return x * cos_f + pltpu.roll(x, HEAD_DIM // 2, 1) * sin_s


def _rope_bwd(d, cos_f, sin_s):
    return d * cos_f + pltpu.roll(d * sin_s, HEAD_DIM // 2, 1)


def rmsnorm_fwd(x, g, name):
    s, d = x.shape
    tm = _tile(s, 512)

    def body(x_ref, g_ref, o_ref):
        o_ref[...] = _rms(x_ref[...], g_ref[...]).astype(BF16)

    return _call(body, name, (s // tm,),
                 [pl.BlockSpec((tm, d), lambda i: (i, 0)), pl.BlockSpec((1, d), lambda i: (0, 0))],
                 pl.BlockSpec((tm, d), lambda i: (i, 0)), _sds((s, d), BF16), sem=("parallel",))(x, g)


def norm_bwd(dout, y, g, scale, resid, out_dtype, name):
    s, d = y.shape
    tm = _tile(s, 256)
    has_res = resid is not None

    def body(*refs):
        if has_res:
            do_ref, y_ref, g_ref, r_ref, dy_ref, dg_ref = refs
        else:
            do_ref, y_ref, g_ref, dy_ref, dg_ref = refs
        dy, dg = _rms_bwd(do_ref[...] * scale, y_ref[...], g_ref[...])
        if has_res:
            dy = dy + r_ref[...]
        dy_ref[...] = dy.astype(out_dtype)

        @pl.when(pl.program_id(0) == 0)
        def _():
            dg_ref[...] = jnp.zeros_like(dg_ref)

        dg_ref[...] += dg

    row = pl.BlockSpec((tm, d), lambda i: (i, 0))
    vec = pl.BlockSpec((1, d), lambda i: (0, 0))
    ins = [dout, y, g] + ([resid] if has_res else [])
    return _call(body, name, (s // tm,), [row, row, vec] + ([row] if has_res else []),
                 [row, vec], [_sds((s, d), out_dtype), _sds((1, d), F32)])(*ins)


def ffn_up(h, wg, wu, name, dep=None):
    s, d = h.shape
    nj, _, fs = wg.shape
    tm = _tile(s, 512)

    def body(h_ref, wg_ref, wu_ref, *rest):
        a_ref, b_ref, u_ref = rest[-3:]
        hh = h_ref[...]
        a = _dot(hh, wg_ref[...])
        b = _dot(hh, wu_ref[...])
        a_ref[...] = a.astype(BF16)
        b_ref[...] = b.astype(BF16)
        u_ref[...] = (a * _sigmoid(a) * b).astype(BF16)

    wspec = pl.BlockSpec((None, d, fs), lambda j, i: (j, 0, 0))
    ospec = pl.BlockSpec((None, tm, fs), lambda j, i: (j, i, 0))
    osd = _sds((nj, s, fs), BF16)
    ins, specs = [h, wg, wu], [pl.BlockSpec((tm, d), lambda j, i: (i, 0)), wspec, wspec]
    if dep is not None:
        ins.append(dep)
        specs.append(pl.BlockSpec((8, 128), lambda j, i: (0, 0)))
    return _call(body, name, (nj, s // tm), specs, [ospec, ospec, ospec], [osd, osd, osd], sem=("parallel", "parallel"))(*ins)


def down_norm(u3, w3, x, g, scale, name):
    nj, s, k = u3.shape
    d = w3.shape[2]
    tm = _tile(s, 512)
    jb = 2 if nj % 2 == 0 else 1
    nsteps = nj // jb

    def body(u_ref, w_ref, x_ref, g_ref, y_ref, xn_ref):
        j = pl.program_id(1)
        t = _dot(u_ref[0], w_ref[0])
        for q in range(1, jb):
            t = t + _dot(u_ref[q], w_ref[q])

        @pl.when(j == 0)
        def _():
            y_ref[...] = t

        @pl.when(j > 0)
        def _():
            y_ref[...] += t

        @pl.when(j == nsteps - 1)
        def _():
            xn_ref[...] = x_ref[...] + scale * _rms(y_ref[...], g_ref[...])

    row = pl.BlockSpec((tm, d), lambda i, j: (i, 0))
    return _call(body, name, (s // tm, nsteps),
                 [pl.BlockSpec((jb, tm, k), lambda i, j: (j, i, 0)), pl.BlockSpec((jb, k, d), lambda i, j: (j, 0, 0)),
                  row, pl.BlockSpec((1, d), lambda i, j: (0, 0))],
                 [row, row], [_sds((s, d), F32), _sds((s, d), F32)], sem=("parallel", "arbitrary"))(u3, w3, x, g)


def ffn_down_bwd(dy, wd, a, b, name):
    nj, fs, d = wd.shape
    s = dy.shape[0]
    tm = _tile(s, 512)

    def body(dy_ref, w_ref, a_ref, b_ref, da_ref, db_ref):
        du = _dot(dy_ref[...], w_ref[...], NT)
        aa = a_ref[...].astype(F32)
        bb = b_ref[...].astype(F32)
        sl, dsl = _silu_and_grad(aa)
        da_ref[...] = (du * bb * dsl).astype(BF16)
        db_ref[...] = (du * sl).astype(BF16)

    hspec = pl.BlockSpec((None, tm, fs), lambda j, i: (j, i, 0))
    osd = _sds((nj, s, fs), BF16)
    return _call(body, name, (nj, s // tm),
                 [pl.BlockSpec((tm, d), lambda j, i: (i, 0)), pl.BlockSpec((None, fs, d), lambda j, i: (j, 0, 0)), hspec, hspec],
                 [hspec, hspec], [osd, osd], sem=("parallel", "parallel"))(dy, wd, a, b)


def mm_tn(a3, b3, out_dtype, name, tmm=2048, tn=1408, tk=1024):
    ja, s, m = a3.shape
    jb, _, n = b3.shape
    nj = max(ja, jb)
    tmm, tn, tk = _tile(m, tmm, 128), _tile(n, tn, 128), _tile(s, tk)
    nk = s // tk

    def body(a_ref, b_ref, o_ref, acc_ref):
        k = pl.program_id(3)

        @pl.when(k == 0)
        def _():
            acc_ref[...] = jnp.zeros_like(acc_ref)

        acc_ref[...] += _dot(a_ref[...].astype(BF16), b_ref[...].astype(BF16), TN)

        @pl.when(k == nk - 1)
        def _():
            o_ref[...] = acc_ref[...].astype(out_dtype)

    aj = (lambda j: j) if ja > 1 else (lambda j: 0)
    bj = (lambda j: j) if jb > 1 else (lambda j: 0)
    return _call(body, name, (nj, m // tmm, n // tn, nk),
                 [pl.BlockSpec((None, tk, tmm), lambda j, im, jn, k: (aj(j), k, im)),
                  pl.BlockSpec((None, tk, tn), lambda j, im, jn, k: (bj(j), k, jn))],
                 pl.BlockSpec((None, tmm, tn), lambda j, im, jn, k: (j, im, jn)), _sds((nj, m, n), out_dtype),
                 scratch=[pltpu.VMEM((tmm, tn), F32)],
                 sem=("parallel", "parallel", "parallel", "arbitrary"))(a3, b3)


def mm_nt_acc(pairs, name, tm=512, tn=512, dep=None):
    nj, s, _ = pairs[0][0].shape
    n = pairs[0][1].shape[1]
    tm, tn = _tile(s, tm), _tile(n, tn, 128)
    npair = len(pairs)

    def body(*refs):
        o_ref = refs[-1]
        j = pl.program_id(2)
        t = None
        for p in range(npair):
            c = _dot(refs[2 * p][...].astype(BF16), refs[2 * p + 1][...], NT)
            t = c if t is None else t + c
        if nj == 1:
            o_ref[...] = t
        else:
            @pl.when(j == 0)
            def _():
                o_ref[...] = t

            @pl.when(j > 0)
            def _():
                o_ref[...] += t

    ins, specs = [], []
    for a3, w3 in pairs:
        k = a3.shape[2]
        ins += [a3, w3]
        specs += [pl.BlockSpec((None, tm, k), lambda i, jn, j: (j, i, 0)),
                  pl.BlockSpec((None, tn, k), lambda i, jn, j: (j, jn, 0))]
    if dep is not None:
        ins.append(dep)
        specs.append(pl.BlockSpec((8, 128), lambda i, jn, j: (0, 0)))
    return _call(body, name, (s // tm, n // tn, nj), specs,
                 pl.BlockSpec((tm, tn), lambda i, jn, j: (i, jn)), _sds((s, n), F32),
                 sem=("parallel", "parallel", "arbitrary"))(*ins)


def mm_nn(a, w, name, tm=512, tn=1408, col0=0, kdim=None):
    s = a.shape[0]
    k, n = w.shape
    assert col0 % k == 0
    tm, tn = _tile(s, tm), _tile(n, tn, 128)
    cb = col0 // k

    def body(a_ref, w_ref, o_ref):
        o_ref[...] = _dot(a_ref[...].astype(BF16), w_ref[...])

    return _call(body, name, (n // tn, s // tm),
                 [pl.BlockSpec((tm, k), lambda jn, i: (i, cb)), pl.BlockSpec((k, tn), lambda jn, i: (0, jn))],
                 pl.BlockSpec((tm, tn), lambda jn, i: (i, jn)), _sds((s, n), F32), sem=("parallel", "parallel"))(a, w)


def mm_nn_acc(a3, w3, name, tm=256):
    nj, s, k = a3.shape
    n = w3.shape[2]
    tm = _tile(s, tm)

    def body(a_ref, w_ref, o_ref):
        j = pl.program_id(1)
        t = _dot(a_ref[...].astype(BF16), w_ref[...])

        @pl.when(j == 0)
        def _():
            o_ref[...] = t

        @pl.when(j > 0)
        def _():
            o_ref[...] += t

    return _call(body, name, (s // tm, nj),
                 [pl.BlockSpec((None, tm, k), lambda i, j: (j, i, 0)), pl.BlockSpec((None, k, n), lambda i, j: (j, 0, 0))],
                 pl.BlockSpec((tm, n), lambda i, j: (i, 0)), _sds((s, n), F32), sem=("parallel", "arbitrary"))(a3, w3)


def loss_and_grad(xl, target, name):
    s, d = xl.shape
    tm = _tile(s, 512)

    def body(x_ref, t_ref, l_ref, dx_ref):
        e = x_ref[...] - t_ref[...]
        dx_ref[...] = e * (1.0 / d)

        @pl.when(pl.program_id(0) == 0)
        def _():
            l_ref[...] = jnp.zeros_like(l_ref)

        part = jnp.sum(jnp.sum(e * e, axis=-1, keepdims=True), axis=0, keepdims=True) * (0.5 / d)
        l_ref[...] += jnp.broadcast_to(part, l_ref.shape)

    row = pl.BlockSpec((tm, d), lambda i: (i, 0))
    return _call(body, name, (s // tm,), [row, row], [pl.BlockSpec((1, 128), lambda i: (0, 0)), row],
                 [_sds((1, 128), F32), _sds((s, d), F32)])(xl, target)


def cast_bf16(w4, layer, name):
    _, r, c = w4.shape
    tr = _tile(r, 512)

    def body(w_ref, o_ref):
        o_ref[...] = w_ref[...].astype(BF16)

    return _call(body, name, (r // tr,), [pl.BlockSpec((None, tr, c), lambda i: (layer, i, 0))],
                 pl.BlockSpec((tr, c), lambda i: (i, 0)), _sds((r, c), BF16), sem=("parallel",))(w4)


def _attn_specs(nb):
    w = WINDOW
    prev = lambda i: jnp.maximum(i - 1, 0)
    q = pl.BlockSpec((w, ATTN_WIDTH), lambda i: (i, 0))
    kc = pl.BlockSpec((w, ATTN_KV_WIDTH), lambda i: (i, Z_K // ATTN_KV_WIDTH))
    kp = pl.BlockSpec((w, ATTN_KV_WIDTH), lambda i: (prev(i), Z_K // ATTN_KV_WIDTH))
    vc = pl.BlockSpec((w, ATTN_KV_WIDTH), lambda i: (i, Z_V // ATTN_KV_WIDTH))
    vp = pl.BlockSpec((w, ATTN_KV_WIDTH), lambda i: (prev(i), Z_V // ATTN_KV_WIDTH))
    tc = pl.BlockSpec((w, HEAD_DIM), lambda i: (i, 0))
    tp = pl.BlockSpec((w, HEAD_DIM), lambda i: (prev(i), 0))
    sink = pl.BlockSpec(memory_space=pltpu.SMEM)
    return [q, kc, kp, vc, vp, tc, tc, tp, tp, sink]


def _attn_mask(i):
    w = WINDOW
    qi = lax.broadcasted_iota(jnp.int32, (w, 2 * w), 0) + w
    kj = lax.broadcasted_iota(jnp.int32, (w, 2 * w), 1)
    rel = qi - kj
    band = (rel >= 0) & (rel < w)
    return band & jnp.logical_not((i == 0) & (kj < w))


def _attn_probs(qs, kk, sinks, mask):
    n = range(len(qs))
    s = [_dot(qs[i], kk, NT) * (HEAD_DIM ** -0.5) for i in n]
    s = [jnp.where(mask, s[i], NEG) for i in n]
    m = [jnp.maximum(jnp.max(s[i], axis=-1, keepdims=True), sinks[i]) for i in n]
    p = [jnp.exp(s[i] - m[i]) for i in n]
    es = [jnp.exp(sinks[i] - m[i]) for i in n]
    inv = [1.0 / (jnp.sum(p[i], axis=-1, keepdims=True) + es[i]) for i in n]
    return [p[i] * inv[i] for i in n], [es[i] * inv[i] for i in n]


def attn_fwd(z, cos_f, sin_s, sinks, name):
    s = z.shape[0]
    nb = s // WINDOW
    hd = HEAD_DIM
    grp = ATTN_HEADS // ATTN_KV_HEADS

    def body(q_ref, kc_ref, kp_ref, vc_ref, vp_ref, cc_ref, sc_ref, cp_ref, sp_ref, sink_ref, o_ref):
        i = pl.program_id(0)
        mask = _attn_mask(i)
        cc, sc, cp, sp = cc_ref[...], sc_ref[...], cp_ref[...], sp_ref[...]
        for kv in range(ATTN_KV_HEADS):
            ksl = slice(kv * hd, (kv + 1) * hd)
            kk = jnp.concatenate([_rope(kp_ref[:, ksl], cp, sp), _rope(kc_ref[:, ksl], cc, sc)], axis=0).astype(BF16)
            vv = jnp.concatenate([vp_ref[:, ksl], vc_ref[:, ksl]], axis=0).astype(BF16)
            heads = [kv * grp + g for g in range(grp)]
            hsl = [slice(h * hd, (h + 1) * hd) for h in heads]
            q = [_rope(q_ref[:, sl], cc, sc).astype(BF16) for sl in hsl]
            pn, _ = _attn_probs(q, kk, [sink_ref[h] for h in heads], mask)
            out = [_dot(pn[i].astype(BF16), vv).astype(BF16) for i in range(grp)]
            for i in range(grp):
                o_ref[:, hsl[i]] = out[i]

    return _call(body, name, (nb,), _attn_specs(nb), pl.BlockSpec((WINDOW, ATTN_WIDTH), lambda i: (i, 0)),
                 _sds((s, ATTN_WIDTH), BF16), sem=("parallel",))(z, z, z, z, z, cos_f, sin_s, cos_f, sin_s, sinks)


def attn_bwd(z, cos_f, sin_s, sinks, dy, name):
    s = z.shape[0]
    nb = s // WINDOW
    hd = HEAD_DIM
    grp = ATTN_HEADS // ATTN_KV_HEADS
    scale = HEAD_DIM ** -0.5

    def body(q_ref, kc_ref, kp_ref, vc_ref, vp_ref, cc_ref, sc_ref, cp_ref, sp_ref, sink_ref, dy_ref,
             dq_ref, dkc_ref, dkp_ref, dvc_ref, dvp_ref, ds_ref):
        i = pl.program_id(0)
        mask = _attn_mask(i)
        cc, sc, cp, sp = cc_ref[...], sc_ref[...], cp_ref[...], sp_ref[...]

        @pl.when(i == 0)
        def _():
            ds_ref[...] = jnp.zeros_like(ds_ref)

        for kv in range(ATTN_KV_HEADS):
            ksl = slice(kv * hd, (kv + 1) * hd)
            kk = jnp.concatenate([_rope(kp_ref[:, ksl], cp, sp), _rope(kc_ref[:, ksl], cc, sc)], axis=0).astype(BF16)
            vv = jnp.concatenate([vp_ref[:, ksl], vc_ref[:, ksl]], axis=0).astype(BF16)
            heads = [kv * grp + g for g in range(grp)]
            n = range(grp)
            hsl = [slice(h * hd, (h + 1) * hd) for h in heads]
            q = [_rope(q_ref[:, sl], cc, sc).astype(BF16) for sl in hsl]
            pn, psink = _attn_probs(q, kk, [sink_ref[h] for h in heads], mask)
            do = [dy_ref[:, sl].astype(BF16) for sl in hsl]
            dpn = [_dot(do[i], vv, NT) for i in n]
            dvs = [_dot(pn[i].astype(BF16), do[i], TN) for i in n]
            tot = [jnp.sum(pn[i] * dpn[i], axis=-1, keepdims=True) for i in n]
            dsc = [(pn[i] * (dpn[i] - tot[i]) * scale).astype(BF16) for i in n]
            dqs = [_rope_bwd(_dot(dsc[i], kk), cc, sc) for i in n]
            dks = [_dot(dsc[i], q[i], TN) for i in n]
            dvv = (dvs[0] + dvs[1]) + (dvs[2] + dvs[3]) if grp == 4 else sum(dvs[1:], dvs[0])
            dkk = (dks[0] + dks[1]) + (dks[2] + dks[3]) if grp == 4 else sum(dks[1:], dks[0])
            for i in n:
                dq_ref[:, hsl[i]] = dqs[i]
                dsink = jnp.sum(-psink[i] * tot[i], axis=0, keepdims=True)
                ds_ref[heads[i]:heads[i] + 1, :] += jnp.broadcast_to(dsink, (1, 128))
            dkp_ref[:, ksl] = _rope_bwd(dkk[:WINDOW], cp, sp)
            dkc_ref[:, ksl] = _rope_bwd(dkk[WINDOW:], cc, sc)
            dvp_ref[:, ksl] = dvv[:WINDOW]
            dvc_ref[:, ksl] = dvv[WINDOW:]

    kvo = pl.BlockSpec((WINDOW, ATTN_KV_WIDTH), lambda i: (i, 0))
    kvs = _sds((s, ATTN_KV_WIDTH), F32)
    return _call(body, name, (nb,), _attn_specs(nb) + [pl.BlockSpec((WINDOW, ATTN_WIDTH), lambda i: (i, 0))],
                 [pl.BlockSpec((WINDOW, ATTN_WIDTH), lambda i: (i, 0)), kvo, kvo, kvo, kvo,
                  pl.BlockSpec((ATTN_HEADS, 128), lambda i: (0, 0))],
                 [_sds((s, ATTN_WIDTH), F32), kvs, kvs, kvs, kvs, _sds((ATTN_HEADS, 128), F32)])(
        z, z, z, z, z, cos_f, sin_s, cos_f, sin_s, sinks, dy)


def _shift_down(x, d, row):
    return jnp.where(row >= d, pltpu.roll(x, d, 0), 0.0)


def _shift_up(x, d, row, n):
    return jnp.where(row < n - d, pltpu.roll(x, n - d, 0), 0.0)


def _conv_taps(u, w_ref, row):
    c = w_ref[DN_CONV - 1:DN_CONV, :] * u
    for k in range(DN_CONV - 1):
        c = c + w_ref[k:k + 1, :] * _shift_down(u, DN_CONV - 1 - k, row)
    return c


def dn_pre_fwd(z, conv_w, name):
    s = z.shape[0]
    nblk = 3 * DN_WIDTH // 128
    nqk = 2 * DN_WIDTH // 128

    def body(u_ref, w_ref, o_ref):
        row = lax.broadcasted_iota(jnp.int32, (s, 128), 0)
        c = _conv_taps(u_ref[...], w_ref, row)
        sl = c * _sigmoid(c)
        j = pl.program_id(0)

        @pl.when(j < nqk)
        def _():
            o_ref[...] = sl * lax.rsqrt(jnp.sum(sl * sl, axis=-1, keepdims=True) + NORM_EPS)

        @pl.when(j >= nqk)
        def _():
            o_ref[...] = sl

    return _call(body, name, (nblk,),
                 [pl.BlockSpec((s, 128), lambda j: (0, Z_DN // 128 + j)), pl.BlockSpec((DN_CONV, 128), lambda j: (0, j))],
                 pl.BlockSpec((s, 128), lambda j: (0, j)), _sds((s, 3 * DN_WIDTH), F32), sem=("parallel",))(z, conv_w)


def dn_pre_bwd(z, conv_w, dout, name):
    s = z.shape[0]
    nblk = 3 * DN_WIDTH // 128
    nqk = 2 * DN_WIDTH // 128

    def body(u_ref, w_ref, do_ref, du_ref, dw_ref, ds_ref):
        row = lax.broadcasted_iota(jnp.int32, (s, 128), 0)
        u = u_ref[...]
        c = _conv_taps(u, w_ref, row)
        sl, dsl = _silu_and_grad(c)
        do = do_ref[...]
        j = pl.program_id(0)

        @pl.when(j < nqk)
        def _():
            r = lax.rsqrt(jnp.sum(sl * sl, axis=-1, keepdims=True) + NORM_EPS)
            ds_ref[...] = r * do - sl * (r * r * r) * jnp.sum(do * sl, axis=-1, keepdims=True)

        @pl.when(j >= nqk)
        def _():
            ds_ref[...] = do

        dc = ds_ref[...] * dsl
        du = w_ref[DN_CONV - 1:DN_CONV, :] * dc
        dw_ref[DN_CONV - 1:DN_CONV, :] = jnp.sum(dc * u, axis=0, keepdims=True)
        for k in range(DN_CONV - 1):
            d = DN_CONV - 1 - k
            du = du + w_ref[k:k + 1, :] * _shift_up(dc, d, row, s)
            dw_ref[k:k + 1, :] = jnp.sum(dc * _shift_down(u, d, row), axis=0, keepdims=True)
        du_ref[...] = du

    blk = pl.BlockSpec((s, 128), lambda j: (0, j))
    wsp = pl.BlockSpec((DN_CONV, 128), lambda j: (0, j))
    return _call(body, name, (nblk,), [pl.BlockSpec((s, 128), lambda j: (0, Z_DN // 128 + j)), wsp, blk],
                 [blk, wsp], [_sds((s, 3 * DN_WIDTH), F32), _sds((DN_CONV, 3 * DN_WIDTH), F32)],
                 scratch=[pltpu.VMEM((s, 128), F32)], sem=("parallel",))(z, conv_w, dout)


def _lane_col(x, lane, idx):
    return jnp.sum(jnp.where(lane == idx, x, 0.0), axis=-1, keepdims=True)


def dn_gates_fwd(z, alog_b, dtb_b, name):
    s = z.shape[0]
    tm = _tile(s, 512)

    def body(zs_ref, al_ref, dt_ref, beta_ref, g_ref):
        zs = zs_ref[...]
        lane = lax.broadcasted_iota(jnp.int32, zs.shape, 1)
        for h in range(DN_HEADS):
            b_raw = _lane_col(zs, lane, h)
            a_raw = _lane_col(zs, lane, DN_HEADS + h)
            beta_ref[h] = jnp.broadcast_to(_sigmoid(b_raw), (tm, 128))
            g_ref[h] = -jnp.exp(al_ref[h:h + 1, :]) * _softplus(a_raw + dt_ref[h:h + 1, :])

    osp = pl.BlockSpec((DN_HEADS, tm, 128), lambda i: (0, i, 0))
    psp = pl.BlockSpec((DN_HEADS, 128), lambda i: (0, 0))
    osd = _sds((DN_HEADS, s, 128), F32)
    return _call(body, name, (s // tm,), [pl.BlockSpec((tm, 128), lambda i: (i, Z_SM // 128)), psp, psp],
                 [osp, osp], [osd, osd], sem=("parallel",))(z, alog_b, dtb_b)


def dn_gates_bwd(z, alog_b, dtb_b, dbeta, dg, name):
    s = z.shape[0]
    tm = _tile(s, 512)

    def body(zs_ref, al_ref, dt_ref, dbeta_ref, dg_ref, dz_ref, dal_ref, ddt_ref):
        @pl.when(pl.program_id(0) == 0)
        def _():
            dal_ref[...] = jnp.zeros_like(dal_ref)
            ddt_ref[...] = jnp.zeros_like(ddt_ref)

        zs = zs_ref[...]
        lane = lax.broadcasted_iota(jnp.int32, zs.shape, 1)
        dz = jnp.zeros_like(zs)
        for h in range(DN_HEADS):
            b_raw = _lane_col(zs, lane, h)
            a_raw = _lane_col(zs, lane, DN_HEADS + h)
            dbe = jnp.sum(dbeta_ref[h], axis=-1, keepdims=True)
            dgg = jnp.sum(dg_ref[h], axis=-1, keepdims=True)
            beta = _sigmoid(b_raw)
            ea = jnp.exp(al_ref[h:h + 1, :])
            pre = a_raw + dt_ref[h:h + 1, :]
            da_raw = dgg * (-ea) * _sigmoid(pre)
            dz = dz + jnp.where(lane == h, dbe * beta * (1.0 - beta), 0.0) + jnp.where(lane == DN_HEADS + h, da_raw, 0.0)
            ddt_ref[h:h + 1, :] += jnp.sum(da_raw, axis=0, keepdims=True)
            dal_ref[h:h + 1, :] += jnp.sum(dgg * (-ea) * _softplus(pre), axis=0, keepdims=True)
        dz_ref[...] = dz

    hsp = pl.BlockSpec((DN_HEADS, tm, 128), lambda i: (0, i, 0))
    psp = pl.BlockSpec((DN_HEADS, 128), lambda i: (0, 0))
    return _call(body, name, (s // tm,), [pl.BlockSpec((tm, 128), lambda i: (i, Z_SM // 128)), psp, psp, hsp, hsp],
                 [pl.BlockSpec((tm, 128), lambda i: (i, 0)), psp, psp],
                 [_sds((s, 128), F32), _sds((DN_HEADS, 128), F32), _sds((DN_HEADS, 128), F32)])(z, alog_b, dtb_b, dbeta, dg)


def _dn_intra(q, k, v, gb, bb):
    c = DN_CHUNK
    pairs = range(len(q))
    ri = lax.broadcasted_iota(jnp.int32, (c, c), 0)
    ci = lax.broadcasted_iota(jnp.int32, (c, c), 1)
    causal = ri >= ci
    strict = ri > ci
    gc = [_chunk_cumsum(gb[i]) for i in pairs]
    grow = [_as_row(gc[i]) for i in pairs]
    decay = [jnp.where(causal, jnp.exp(jnp.where(causal, gc[i][:, :c] - grow[i], 0.0)), 0.0) for i in pairs]
    qs = [q[i] * (DN_HEAD_DIM ** -0.5) for i in pairs]
    kb = [k[i] * bb[i] for i in pairs]
    lower = [jnp.where(strict, _dot(kb[i], k[i], NT) * decay[i], 0.0) for i in pairs]
    t = _unit_lower_inverse(lower)
    eg = [jnp.exp(gc[i]) for i in pairs]
    u = [_dot(t[i], v[i] * bb[i]) for i in pairs]
    w = [_dot(t[i], kb[i] * eg[i]) for i in pairs]
    attn = [jnp.where(causal, _dot(qs[i], k[i], NT) * decay[i], 0.0) for i in pairs]
    glast = [_last_row(gc[i]) for i in pairs]
    return (u, w, [qs[i] * eg[i] for i in pairs], [k[i] * jnp.exp(glast[i] - gc[i]) for i in pairs], attn,
            [jnp.exp(glast[i][:8]) for i in pairs])


def _split3(x):
    x1 = x.astype(BF16)
    r = x - x1.astype(F32)
    x2 = r.astype(BF16)
    return x1, x2, (r - x2.astype(F32)).astype(BF16)


def _dot_split(a, b, dims=NN):
    a1, a2, _ = _split3(a)
    b1, b2, _ = _split3(b)
    return _dot(a1, b1, dims) + (_dot(a1, b2, dims) + _dot(a2, b1, dims))


def _sel_dot(m01, x, dims, m_left):
    m = m01.astype(BF16)
    parts = [_dot(m, xi, dims) if m_left else _dot(xi, m, dims) for xi in _split3(x)]
    return parts[0] + (parts[1] + parts[2])


def _tri_mask(n, upper):
    ri = lax.broadcasted_iota(jnp.int32, (n, n), 0)
    ci = lax.broadcasted_iota(jnp.int32, (n, n), 1)
    return ci >= ri if upper else ri >= ci


@jax.custom_vjp
def _chunk_cumsum(x):
    return _sel_dot(_tri_mask(x.shape[0], False), x, NN, True)


_chunk_cumsum.defvjp(lambda x: (_chunk_cumsum(x), None),
                     lambda _, ct: (_sel_dot(_tri_mask(ct.shape[0], True), ct, NN, True),))


def _lane0(rows):
    return lax.broadcasted_iota(jnp.int32, (rows, 128), 1) == 0


@jax.custom_vjp
def _as_row(x):
    return _sel_dot(_lane0(x.shape[0]), x, NT, True)


_as_row.defvjp(lambda x: (_as_row(x), None),
               lambda _, ct: (_sel_dot(_lane0(ct.shape[0]), ct, TN, False),))


def _last_col_mask(n, transpose):
    idx = lax.broadcasted_iota(jnp.int32, (n, n), 0 if transpose else 1)
    return idx == n - 1


@jax.custom_vjp
def _last_row(x):
    return _sel_dot(_last_col_mask(x.shape[0], False), x, NN, True)


_last_row.defvjp(lambda x: (_last_row(x), None),
                 lambda _, ct: (_sel_dot(_last_col_mask(ct.shape[0], True), ct, NN, True),))


@jax.custom_vjp
def _unit_lower_inverse(lowers):
    n = lowers[0].shape[0]
    eye = (lax.broadcasted_iota(jnp.int32, (n, n), 0) == lax.broadcasted_iota(jnp.int32, (n, n), 1)).astype(F32)
    ts = [eye - l for l in lowers]
    ps = list(lowers)
    for _ in range(5):
        ps = [_dot_split(p, p) for p in ps]
        ts = [t + _dot_split(t, p) for t, p in zip(ts, ps)]
    return ts


def _unit_lower_inverse_fwd(lowers):
    ts = _unit_lower_inverse(lowers)
    return ts, ts


def _unit_lower_inverse_bwd(ts, dts):
    half = [_dot_split(t, dt, TN) for t, dt in zip(ts, dts)]
    return ([-_dot_split(h, t, NT) for h, t in zip(half, ts)],)


_unit_lower_inverse.defvjp(_unit_lower_inverse_fwd, _unit_lower_inverse_bwd)


def _dn_step(st, qd, kd, u, w, attn, egl):
    heads = range(len(st))
    v_new = [u[h] - _dot(w[h], st[h]) for h in heads]
    o = [_dot(qd[h], st[h]) for h in heads]
    o = [o[h] + _dot(attn[h], v_new[h]) for h in heads]
    st_new = [st[h] * egl[h][0:1, :] + _dot(kd[h], v_new[h], TN) for h in heads]
    return o, st_new


def _dn_chunk_specs(m=1):
    c = DN_CHUNK
    wide = pl.BlockSpec((m * c, DN_WIDTH), lambda i: (i, 0))
    att = pl.BlockSpec((m * c, DN_HEADS * c), lambda i: (i, 0))
    egl = pl.BlockSpec((m * 8, DN_WIDTH), lambda i: (i, 0))
    return wide, att, egl


def _dn_intra_chunks(nc):
    return 2 if nc % 2 == 0 else 1


def dn_intra_fwd(qkv, gb, bb, name):
    s = qkv.shape[0]
    c, hd = DN_CHUNK, DN_HEAD_DIM
    nc = s // c
    m = _dn_intra_chunks(nc)

    def body(q_ref, k_ref, v_ref, g_ref, b_ref, u_ref, w_ref, qd_ref, kd_ref, at_ref, eg_ref):
        pairs = [(t, h) for t in range(m) for h in range(DN_HEADS)]
        rs = lambda t: slice(t * c, (t + 1) * c)
        hs = lambda h: slice(h * hd, (h + 1) * hd)
        u, w, qd, kd, at, eg = _dn_intra([q_ref[rs(t), hs(h)] for t, h in pairs], [k_ref[rs(t), hs(h)] for t, h in pairs],
                                         [v_ref[rs(t), hs(h)] for t, h in pairs], [g_ref[h, rs(t)] for t, h in pairs],
                                         [b_ref[h, rs(t)] for t, h in pairs])
        for i, (t, h) in enumerate(pairs):
            u_ref[rs(t), hs(h)], w_ref[rs(t), hs(h)], qd_ref[rs(t), hs(h)], kd_ref[rs(t), hs(h)] = u[i], w[i], qd[i], kd[i]
            at_ref[rs(t), h * c:(h + 1) * c] = at[i]
            eg_ref[t * 8:(t + 1) * 8, hs(h)] = eg[i]

    wide, att, egl = _dn_chunk_specs(m)
    hsp = pl.BlockSpec((DN_HEADS, m * c, 128), lambda i: (0, i, 0))
    wsd = _sds((s, DN_WIDTH), F32)
    return _call(body, name, (nc // m,),
                 [pl.BlockSpec((m * c, DN_WIDTH), lambda i: (i, 0)), pl.BlockSpec((m * c, DN_WIDTH), lambda i: (i, 1)),
                  pl.BlockSpec((m * c, DN_WIDTH), lambda i: (i, 2)), hsp, hsp],
                 [wide, wide, wide, wide, att, egl],
                 [wsd, wsd, wsd, wsd, _sds((s, DN_HEADS * c), F32), _sds((nc * 8, DN_WIDTH), F32)],
                 sem=("parallel",))(qkv, qkv, qkv, gb, bb)


def dn_intra_bwd(qkv, gb, bb, cts, name):
    s = qkv.shape[0]
    c, hd = DN_CHUNK, DN_HEAD_DIM
    nc = s // c
    m = _dn_intra_chunks(nc)

    def body(q_ref, k_ref, v_ref, g_ref, b_ref, du_ref, dw_ref, dqd_ref, dkd_ref, dat_ref, deg_ref,
             dq_ref, dk_ref, dv_ref, dg_ref, db_ref):
        pairs = [(t, h) for t in range(m) for h in range(DN_HEADS)]
        rs = lambda t: slice(t * c, (t + 1) * c)
        hs = lambda h: slice(h * hd, (h + 1) * hd)
        _, vjp = jax.vjp(_dn_intra, [q_ref[rs(t), hs(h)] for t, h in pairs], [k_ref[rs(t), hs(h)] for t, h in pairs],
                         [v_ref[rs(t), hs(h)] for t, h in pairs], [g_ref[h, rs(t)] for t, h in pairs],
                         [b_ref[h, rs(t)] for t, h in pairs])
        dq, dk, dv, dg, db = vjp(([du_ref[rs(t), hs(h)] for t, h in pairs], [dw_ref[rs(t), hs(h)] for t, h in pairs],
                                  [dqd_ref[rs(t), hs(h)] for t, h in pairs], [dkd_ref[rs(t), hs(h)] for t, h in pairs],
                                  [dat_ref[rs(t), h * c:(h + 1) * c] for t, h in pairs],
                                  [deg_ref[t * 8:(t + 1) * 8, hs(h)] for t, h in pairs]))
        for i, (t, h) in enumerate(pairs):
            dq_ref[rs(t), hs(h)], dk_ref[rs(t), hs(h)], dv_ref[rs(t), hs(h)] = dq[i], dk[i], dv[i]
            dg_ref[h, rs(t)] = dg[i]
            db_ref[h, rs(t)] = db[i]

    wide, att, egl = _dn_chunk_specs(m)
    hsp = pl.BlockSpec((DN_HEADS, m * c, 128), lambda i: (0, i, 0))
    hsd = _sds((DN_HEADS, s, 128), F32)
    wsd = _sds((s, DN_WIDTH), F32)
    return _call(body, name, (nc // m,),
                 [pl.BlockSpec((m * c, DN_WIDTH), lambda i: (i, 0)), pl.BlockSpec((m * c, DN_WIDTH), lambda i: (i, 1)),
                  pl.BlockSpec((m * c, DN_WIDTH), lambda i: (i, 2)), hsp, hsp, wide, wide, wide, wide, att, egl],
                 [wide, wide, wide, hsp, hsp], [wsd, wsd, wsd, hsd, hsd], sem=("parallel",))(qkv, qkv, qkv, gb, bb, *cts)


def dn_scan_fwd(u, w, qd, kd, at, eg, name):
    s = u.shape[0]
    c, hd = DN_CHUNK, DN_HEAD_DIM
    nc = s // c

    def body(u_ref, w_ref, qd_ref, kd_ref, at_ref, eg_ref, o_ref, st_ref, state):
        @pl.when(pl.program_id(0) == 0)
        def _():
            state[...] = jnp.zeros_like(state)

        heads = range(DN_HEADS)
        hs = lambda h: slice(h * hd, (h + 1) * hd)
        st = [state[h] for h in heads]
        for h in heads:
            st_ref[h] = st[h]
        o, st_new = _dn_step(st, [qd_ref[:, hs(h)] for h in heads], [kd_ref[:, hs(h)] for h in heads],
                             [u_ref[:, hs(h)] for h in heads], [w_ref[:, hs(h)] for h in heads],
                             [at_ref[:, h * c:(h + 1) * c] for h in heads], [eg_ref[:, hs(h)] for h in heads])
        for h in heads:
            o_ref[:, hs(h)] = o[h]
            state[h] = st_new[h]

    wide, att, egl = _dn_chunk_specs()
    return _call(body, name, (nc,), [wide, wide, wide, wide, att, egl],
                 [wide, pl.BlockSpec((None, DN_HEADS, hd, hd), lambda i: (i, 0, 0, 0))],
                 [_sds((s, DN_WIDTH), F32), _sds((nc, DN_HEADS, hd, hd), F32)],
                 scratch=[pltpu.VMEM((DN_HEADS, hd, hd), F32)])(u, w, qd, kd, at, eg)


def dn_scan_bwd(u, w, qd, kd, at, eg, states, do, name):
    s = u.shape[0]
    c, hd = DN_CHUNK, DN_HEAD_DIM
    nc = s // c

    def body(u_ref, w_ref, qd_ref, kd_ref, at_ref, eg_ref, st_ref, do_ref,
             du_ref, dw_ref, dqd_ref, dkd_ref, dat_ref, deg_ref, dstate):
        @pl.when(pl.program_id(0) == 0)
        def _():
            dstate[...] = jnp.zeros_like(dstate)

        heads = range(DN_HEADS)
        hs = lambda h: slice(h * hd, (h + 1) * hd)
        asl = lambda h: slice(h * c, (h + 1) * c)
        _, vjp = jax.vjp(_dn_step, [st_ref[h] for h in heads], [qd_ref[:, hs(h)] for h in heads],
                         [kd_ref[:, hs(h)] for h in heads], [u_ref[:, hs(h)] for h in heads], [w_ref[:, hs(h)] for h in heads],
                         [at_ref[:, asl(h)] for h in heads], [eg_ref[:, hs(h)] for h in heads])
        dst, dqd, dkd, du, dw, dat, deg = vjp(([do_ref[:, hs(h)] for h in heads], [dstate[h] for h in heads]))
        for h in heads:
            dstate[h] = dst[h]
            du_ref[:, hs(h)], dw_ref[:, hs(h)], dqd_ref[:, hs(h)], dkd_ref[:, hs(h)] = du[h], dw[h], dqd[h], dkd[h]
            dat_ref[:, asl(h)] = dat[h]
            deg_ref[:, hs(h)] = deg[h]

    rev = lambda i: nc - 1 - i
    wide = pl.BlockSpec((c, DN_WIDTH), lambda i: (rev(i), 0))
    att = pl.BlockSpec((c, DN_HEADS * c), lambda i: (rev(i), 0))
    egl = pl.BlockSpec((8, DN_WIDTH), lambda i: (rev(i), 0))
    wsd = _sds((s, DN_WIDTH), F32)
    return _call(body, name, (nc,),
                 [wide, wide, wide, wide, att, egl, pl.BlockSpec((None, DN_HEADS, hd, hd), lambda i: (rev(i), 0, 0, 0)), wide],
                 [wide, wide, wide, wide, att, egl],
                 [wsd, wsd, wsd, wsd, _sds((s, DN_HEADS * c), F32), _sds((nc * 8, DN_WIDTH), F32)],
                 scratch=[pltpu.VMEM((DN_HEADS, hd, hd), F32)])(u, w, qd, kd, at, eg, states, do)


def _dn_out(o, zg, nw):
    n = o * lax.rsqrt(jnp.mean(o * o, axis=-1, keepdims=True) + NORM_EPS) * nw
    return n * (zg * _sigmoid(zg))


def dn_out_fwd(o, z, nw, name):
    s = o.shape[0]
    tm = _tile(s, 512)
    hd = DN_HEAD_DIM

    def body(o_ref, zg_ref, nw_ref, y_ref):
        for h in range(DN_HEADS):
            hs = slice(h * hd, (h + 1) * hd)
            y_ref[:, hs] = _dn_out(o_ref[:, hs], zg_ref[:, hs], nw_ref[...]).astype(BF16)

    return _call(body, name, (s // tm,),
                 [pl.BlockSpec((tm, DN_WIDTH), lambda i: (i, 0)), pl.BlockSpec((tm, DN_WIDTH), lambda i: (i, Z_ZG // DN_WIDTH)),
                  pl.BlockSpec((1, hd), lambda i: (0, 0))],
                 pl.BlockSpec((tm, DN_WIDTH), lambda i: (i, 0)), _sds((s, DN_WIDTH), BF16), sem=("parallel",))(o, z, nw)


def dn_out_bwd(o, z, nw, dycat, name):
    s = o.shape[0]
    tm = _tile(s, 512)
    hd = DN_HEAD_DIM

    def body(o_ref, zg_ref, nw_ref, dy_ref, do_ref, dzg_ref, dnw_ref):
        @pl.when(pl.program_id(0) == 0)
        def _():
            dnw_ref[...] = jnp.zeros_like(dnw_ref)

        for h in range(DN_HEADS):
            hs = slice(h * hd, (h + 1) * hd)
            _, vjp = jax.vjp(_dn_out, o_ref[:, hs], zg_ref[:, hs], nw_ref[...])
            do, dzg, dnw = vjp(dy_ref[:, hs])
            do_ref[:, hs] = do
            dzg_ref[:, hs] = dzg
            dnw_ref[...] += dnw

    wide = pl.BlockSpec((tm, DN_WIDTH), lambda i: (i, 0))
    wsd = _sds((s, DN_WIDTH), F32)
    return _call(body, name, (s // tm,),
                 [wide, pl.BlockSpec((tm, DN_WIDTH), lambda i: (i, Z_ZG // DN_WIDTH)), pl.BlockSpec((1, hd), lambda i: (0, 0)),
                  pl.BlockSpec((tm, DN_WIDTH), lambda i: (i, ATTN_WIDTH // DN_WIDTH))],
                 [wide, wide, pl.BlockSpec((1, hd), lambda i: (0, 0))], [wsd, wsd, _sds((1, hd), F32)])(o, z, nw, dycat)


def _s5_param_fn(are, aim, ldt, bre, bim):
    dt = jnp.exp(ldt)
    er = jnp.exp(are * dt)
    abr = er * jnp.cos(aim * dt)
    abi = er * jnp.sin(aim * dt)
    den = are * are + aim * aim
    cr = ((abr - 1.0) * are + abi * aim) / den
    ci = (abi * are - (abr - 1.0) * aim) / den
    return abr, abi, cr * bre - ci * bim, cr * bim + ci * bre


def s5_params_fwd(are, aim, ldt, bre, bim, name):
    p, hh = bre.shape

    def body(a_ref, b_ref, c_ref, d_ref, e_ref, o1, o2, o3, o4):
        o1[...], o2[...], o3[...], o4[...] = _s5_param_fn(a_ref[...], b_ref[...], c_ref[...], d_ref[...], e_ref[...])

    col = pl.BlockSpec((p, 1), lambda: (0, 0))
    mat = pl.BlockSpec((p, hh), lambda: (0, 0))
    return _call(body, name, (), [col, col, col, mat, mat], [col, col, mat, mat],
                 [_sds((p, 1), F32), _sds((p, 1), F32), _sds((p, hh), F32), _sds((p, hh), F32)])(are, aim, ldt, bre, bim)


def s5_params_bwd(are, aim, ldt, bre, bim, cts, name):
    p, hh = bre.shape

    def body(a_ref, b_ref, c_ref, d_ref, e_ref, g1, g2, g3, g4, o1, o2, o3, o4, o5):
        _, vjp = jax.vjp(_s5_param_fn, a_ref[...], b_ref[...], c_ref[...], d_ref[...], e_ref[...])
        o1[...], o2[...], o3[...], o4[...], o5[...] = vjp((g1[...], g2[...], g3[...], g4[...]))

    col = pl.BlockSpec((p, 1), lambda: (0, 0))
    mat = pl.BlockSpec((p, hh), lambda: (0, 0))
    csd, msd = _sds((p, 1), F32), _sds((p, hh), F32)
    return _call(body, name, (), [col, col, col, mat, mat, col, col, mat, mat], [col, col, col, mat, mat],
                 [csd, csd, csd, msd, msd])(are, aim, ldt, bre, bim, *cts)


def _cmul(ar, ai, br, bi):
    return ar * br - ai * bi, ar * bi + ai * br


S5_ROWS = 8
S5_UNROLL = 4


def _s5_tile_scan(xr, xi, ar, ai, reverse):
    row = lax.broadcasted_iota(jnp.int32, xr.shape, 0)
    d = 1
    while d < S5_ROWS:
        if reverse:
            sr, si = _shift_up(xr, d, row, S5_ROWS), _shift_up(xi, d, row, S5_ROWS)
        else:
            sr, si = _shift_down(xr, d, row), _shift_down(xi, d, row)
        pr, pi = _cmul(ar, ai, sr, si)
        xr, xi = xr + pr, xi + pi
        ar, ai = _cmul(ar, ai, ar, ai)
        d *= 2
    return xr, xi


def _s5_carry_powers(ar, ai, width, reverse):
    row = lax.broadcasted_iota(jnp.int32, (S5_ROWS, width), 0)
    at = row == (S5_ROWS - 1 if reverse else 0)
    return _s5_tile_scan(jnp.where(at, ar, 0.0), jnp.where(at, ai, 0.0), ar, ai, reverse)


def s5_scan_fwd(bu, abr, abi, name):
    s = bu.shape[0]
    wd = 256
    npb = S5_P // wd
    step = S5_ROWS * S5_UNROLL
    assert s % step == 0

    def body(br_ref, bi_ref, ar_ref, ai_ref, x_ref):
        ar, ai = ar_ref[...], ai_ref[...]
        pwr, pwi = _s5_carry_powers(ar, ai, wd, False)

        def loop(i, carry):
            cr, ci = carry
            base = pl.multiple_of(i * step, step)
            tiles = []
            for t in range(S5_UNROLL):
                rows = pl.ds(pl.multiple_of(base + t * S5_ROWS, S5_ROWS), S5_ROWS)
                tiles.append(_s5_tile_scan(br_ref[rows, :], bi_ref[rows, :], ar, ai, False))
            for t in range(S5_UNROLL):
                rows = pl.ds(pl.multiple_of(base + t * S5_ROWS, S5_ROWS), S5_ROWS)
                tr, ti = _cmul(pwr, pwi, cr, ci)
                xr, xi = tiles[t][0] + tr, tiles[t][1] + ti
                x_ref[0, rows, :] = xr
                x_ref[1, rows, :] = xi
                cr, ci = xr[S5_ROWS - 1:S5_ROWS, :], xi[S5_ROWS - 1:S5_ROWS, :]
            return cr, ci

        zero = jnp.zeros((1, wd), F32)
        lax.fori_loop(0, s // step, loop, (zero, zero))

    re = pl.BlockSpec((s, wd), lambda j: (0, j))
    im = pl.BlockSpec((s, wd), lambda j: (0, npb + j))
    av = pl.BlockSpec((1, wd), lambda j: (0, j))
    return _call(body, name, (npb,), [re, im, av, av], pl.BlockSpec((2, s, wd), lambda j: (0, 0, j)),
                 _sds((2, s, S5_P), F32), sem=("parallel",))(bu, bu, abr, abi)


def s5_scan_bwd(dx, x, abr, abi, name):
    s = dx.shape[0]
    wd = 128
    npb = S5_P // wd
    step = S5_ROWS * S5_UNROLL
    nsteps = s // step

    def body(dr_ref, di_ref, x_ref, ar_ref, ai_ref, g_ref, dar_ref, dai_ref):
        ar, ai = ar_ref[...], -ai_ref[...]
        pwr, pwi = _s5_carry_powers(ar, ai, wd, True)
        row = lax.broadcasted_iota(jnp.int32, (S5_ROWS, wd), 0)

        def loop(k, carry):
            cr, ci, accr, acci = carry
            base = pl.multiple_of((nsteps - 1 - k) * step, step)
            tiles = [None] * S5_UNROLL
            for t in range(S5_UNROLL):
                rows = pl.ds(pl.multiple_of(base + t * S5_ROWS, S5_ROWS), S5_ROWS)
                tiles[t] = _s5_tile_scan(dr_ref[rows, :], di_ref[rows, :], ar, ai, True)
            before = pl.ds(pl.multiple_of(jnp.maximum(base - S5_ROWS, 0), S5_ROWS), S5_ROWS)
            lr = jnp.where(base > 0, x_ref[0, before, :][S5_ROWS - 1:S5_ROWS, :], 0.0)
            li = jnp.where(base > 0, x_ref[1, before, :][S5_ROWS - 1:S5_ROWS, :], 0.0)
            prev = []
            for t in range(S5_UNROLL):
                rows = pl.ds(pl.multiple_of(base + t * S5_ROWS, S5_ROWS), S5_ROWS)
                xr, xi = x_ref[0, rows, :], x_ref[1, rows, :]
                prev.append((jnp.where(row >= 1, pltpu.roll(xr, 1, 0), lr), jnp.where(row >= 1, pltpu.roll(xi, 1, 0), li)))
                lr, li = xr[S5_ROWS - 1:S5_ROWS, :], xi[S5_ROWS - 1:S5_ROWS, :]
            for t in reversed(range(S5_UNROLL)):
                rows = pl.ds(pl.multiple_of(base + t * S5_ROWS, S5_ROWS), S5_ROWS)
                tr, ti = _cmul(pwr, pwi, cr, ci)
                gr, gi = tiles[t][0] + tr, tiles[t][1] + ti
                g_ref[0, rows, :] = gr
                g_ref[1, rows, :] = gi
                pr, pi = prev[t]
                accr = accr + (gr * pr + gi * pi)
                acci = acci + (gi * pr - gr * pi)
                cr, ci = gr[0:1, :], gi[0:1, :]
            return cr, ci, accr, acci

        zero = jnp.zeros((1, wd), F32)
        zacc = jnp.zeros((S5_ROWS, wd), F32)
        _, _, accr, acci = lax.fori_loop(0, nsteps, loop, (zero, zero, zacc, zacc))
        dar_ref[...] = jnp.sum(accr, axis=0, keepdims=True)
        dai_ref[...] = jnp.sum(acci, axis=0, keepdims=True)

    re = pl.BlockSpec((s, wd), lambda j: (0, j))
    im = pl.BlockSpec((s, wd), lambda j: (0, npb + j))
    av = pl.BlockSpec((1, wd), lambda j: (0, j))
    planes = pl.BlockSpec((2, s, wd), lambda j: (0, 0, j))
    asd = _sds((1, S5_P), F32)
    return _call(body, name, (npb,), [re, im, planes, av, av], [planes, av, av], [_sds((2, s, S5_P), F32), asd, asd],
                 sem=("parallel",))(dx, dx, x, abr, abi)


def _gelu(y):
    return 0.5 * y * (1.0 + jnp.tanh(math.sqrt(2.0 / math.pi) * (y + 0.044715 * y * y * y)))


def s5_out_fwd(ypre, z, dvec, glu_w, glu_b, name):
    s = ypre.shape[0]
    tm = _tile(s, 512)
    wd = S5_WIDTH

    def body(yp_ref, u_ref, d_ref, w_ref, b_ref, y_ref, o_ref):
        y = yp_ref[...] + d_ref[...] * u_ref[...]
        y_ref[...] = y
        g = _gelu(y)
        t = _dot(g.astype(BF16), w_ref[...]) + b_ref[...]
        o_ref[...] = (g * _sigmoid(t)).astype(BF16)

    row = pl.BlockSpec((tm, wd), lambda i: (i, 0))
    vec = pl.BlockSpec((1, wd), lambda i: (0, 0))
    return _call(body, name, (s // tm,),
                 [row, pl.BlockSpec((tm, wd), lambda i: (i, Z_S5 // wd)), vec, pl.BlockSpec((wd, wd), lambda i: (0, 0)), vec],
                 [row, row], [_sds((s, wd), F32), _sds((s, wd), BF16)], sem=("parallel",))(ypre, z, dvec, glu_w, glu_b)


def s5_out_bwd(y, z, glu_w, glu_b, dycat, name):
    s = y.shape[0]
    tm = _tile(s, 512)
    wd = S5_WIDTH

    def body(y_ref, u_ref, w_ref, b_ref, do_ref, dy_ref, dd_ref, dw_ref, db_ref):
        @pl.when(pl.program_id(0) == 0)
        def _():
            dd_ref[...] = jnp.zeros_like(dd_ref)
            dw_ref[...] = jnp.zeros_like(dw_ref)
            db_ref[...] = jnp.zeros_like(db_ref)

        g, gvjp = jax.vjp(_gelu, y_ref[...])
        gb = g.astype(BF16)
        sg = _sigmoid(_dot(gb, w_ref[...]) + b_ref[...])
        do = do_ref[...]
        dt = do * g * sg * (1.0 - sg)
        dtb = dt.astype(BF16)
        dg = do * sg + _dot(dtb, w_ref[...], NT)
        (dy,) = gvjp(dg)
        dy_ref[...] = dy
        dd_ref[...] += jnp.sum(dy * u_ref[...], axis=0, keepdims=True)
        dw_ref[...] += _dot(gb, dtb, TN)
        db_ref[...] += jnp.sum(dt, axis=0, keepdims=True)

    row = pl.BlockSpec((tm, wd), lambda i: (i, 0))
    vec = pl.BlockSpec((1, wd), lambda i: (0, 0))
    mat = pl.BlockSpec((wd, wd), lambda i: (0, 0))
    return _call(body, name, (s // tm,),
                 [row, pl.BlockSpec((tm, wd), lambda i: (i, Z_S5 // wd)), mat, vec,
                  pl.BlockSpec((tm, wd), lambda i: (i, (ATTN_WIDTH + DN_WIDTH) // wd))],
                 [row, vec, mat, vec], [_sds((s, wd), F32), _sds((1, wd), F32), _sds((wd, wd), F32), _sds((1, wd), F32)])(
        y, z, glu_w, glu_b, dycat)


def assemble_dz(dq, dkc, dkp, dvc, dvp, ddn, dzg, dus, dys, dvec, dzs, name):
    s = dq.shape[0]
    w = WINDOW
    nb = s // w
    nxt = lambda i: jnp.minimum(i + 1, nb - 1)

    def body(dq_ref, dkc_ref, dkp_ref, dvc_ref, dvp_ref, ddn_ref, dzg_ref, dus_ref, dys_ref, dv_ref, dzs_ref, o_ref):
        live = (pl.program_id(0) < nb - 1).astype(F32)
        o_ref[:, Z_Q:Z_K] = dq_ref[...].astype(BF16)
        o_ref[:, Z_K:Z_V] = (dkc_ref[...] + live * dkp_ref[...]).astype(BF16)
        o_ref[:, Z_V:Z_DN] = (dvc_ref[...] + live * dvp_ref[...]).astype(BF16)
        o_ref[:, Z_DN:Z_ZG] = ddn_ref[...].astype(BF16)
        o_ref[:, Z_ZG:Z_S5] = dzg_ref[...].astype(BF16)
        o_ref[:, Z_S5:Z_SM] = (dus_ref[...] + dv_ref[...] * dys_ref[...]).astype(BF16)
        o_ref[:, Z_SM:Z_ALL] = dzs_ref[...].astype(BF16)

    def blk(width, f=lambda i: i):
        return pl.BlockSpec((w, width), lambda i: (f(i), 0))

    return _call(body, name, (nb,),
                 [blk(ATTN_WIDTH), blk(ATTN_KV_WIDTH), blk(ATTN_KV_WIDTH, nxt), blk(ATTN_KV_WIDTH), blk(ATTN_KV_WIDTH, nxt),
                  blk(3 * DN_WIDTH), blk(DN_WIDTH), blk(S5_WIDTH), blk(S5_WIDTH), pl.BlockSpec((1, S5_WIDTH), lambda i: (0, 0)),
                  blk(128)],
                 blk(Z_ALL), _sds((s, Z_ALL), BF16), sem=("parallel",))(dq, dkc, dkp, dvc, dvp, ddn, dzg, dus, dys, dvec, dzs)


def _my_place():
    return lax.axis_index("x"), lax.axis_index("y"), lax.axis_index("c")


def _peer(place, p):
    x, y, c = place
    px = 1 - x if p & 4 else x
    py = 1 - y if p & 2 else y
    pc = 1 - c if p & 1 else c
    return (px, py, pc), 4 * px + 2 * py + pc


def exchange(arrays, scatter, name):
    na = len(arrays)

    def body(*refs):
        srcs, dsts = refs[:na], refs[na:2 * na]
        send_sems, recv_sems, local_sems = refs[2 * na:]
        place = _my_place()
        me = 4 * place[0] + 2 * place[1] + place[2]
        copies = []
        for k in range(na):
            mine = srcs[k].at[me] if scatter else srcs[k]
            loc = pltpu.make_async_copy(mine, dsts[k].at[me], local_sems.at[k])
            loc.start()
            copies.append(loc)
        sends = []
        for p in range(1, N_DEV):
            peer, pid = _peer(place, p)
            for k in range(na):
                src = srcs[k].at[pid] if scatter else srcs[k]
                cp = pltpu.make_async_remote_copy(src_ref=src, dst_ref=dsts[k].at[me], send_sem=send_sems.at[k, p - 1],
                                                  recv_sem=recv_sems.at[k, p - 1], device_id=peer,
                                                  device_id_type=pl.DeviceIdType.MESH)
                cp.start()
                sends.append(cp)
        for p in range(1, N_DEV):
            peer, pid = _peer(place, p)
            for k in range(na):
                src = srcs[k].at[me] if scatter else srcs[k]
                pltpu.make_async_remote_copy(src_ref=src, dst_ref=dsts[k].at[pid], send_sem=send_sems.at[k, p - 1],
                                             recv_sem=recv_sems.at[k, p - 1], device_id=peer,
                                             device_id_type=pl.DeviceIdType.MESH).wait_recv()
        for cp in sends:
            cp.wait_send()
        for cp in copies:
            cp.wait()

    outs = [_sds((N_DEV,) + tuple(a.shape[1:] if scatter else a.shape), a.dtype) for a in arrays]
    anyspec = pl.BlockSpec(memory_space=pl.ANY)
    return pl.pallas_call(
        body, name=name, in_specs=[anyspec] * na, out_specs=[anyspec] * na, out_shape=outs,
        scratch_shapes=[pltpu.SemaphoreType.DMA((na, N_DEV - 1)), pltpu.SemaphoreType.DMA((na, N_DEV - 1)),
                        pltpu.SemaphoreType.DMA((na,))])(*arrays)


_HBM = pl.BlockSpec(memory_space=pltpu.HBM)
_SEM = pl.BlockSpec(memory_space=pltpu.SEMAPHORE)
_DATAFLOW = pltpu.SideEffectType.DATAFLOW_SIDE_EFFECTING


def _split_copies(srcs, lands, send_sems, recv_sems, scatter, arriving):
    place = _my_place()
    me = 4 * place[0] + 2 * place[1] + place[2]
    out = []
    for p in range(1, N_DEV):
        peer, pid = _peer(place, p)
        for k in range(len(srcs)):
            i = k * (N_DEV - 1) + p - 1
            src = srcs[k].at[pid] if scatter else srcs[k]
            dst = lands[k].at[pid] if arriving else lands[k].at[me]
            out.append(pltpu.make_async_remote_copy(src_ref=src, dst_ref=dst, send_sem=send_sems.at[i], recv_sem=recv_sems.at[i],
                                                    device_id=peer, device_id_type=pl.DeviceIdType.MESH))
    return out


def exchange_start(groups, scatter, name):
    arrays = [a for g in groups for a in g]
    na, ng = len(arrays), len(groups)
    first = [sum(len(g) for g in groups[:i]) for i in range(ng)]
    me = 4 * lax.axis_index("x") + 2 * lax.axis_index("y") + lax.axis_index("c")
    lands = []
    for a in arrays:
        own = lax.dynamic_index_in_dim(a, me, 0, keepdims=True) if scatter else a[None]
        shape = (N_DEV,) + tuple(own.shape[1:])
        land = lax.dynamic_update_slice(lax.empty(shape, a.dtype), own, (me,) + (0,) * (len(shape) - 1))
        lands.append(pltpu.with_memory_space_constraint(land, pltpu.HBM))
    srcs = [pltpu.with_memory_space_constraint(a, pltpu.HBM) for a in arrays]

    def body(*refs):
        src_refs, land_refs = refs[:na], refs[na:2 * na]
        sems = refs[2 * na:2 * na + 2 * ng]
        token = refs[-1]
        for i, g in enumerate(groups):
            sl = slice(first[i], first[i] + len(g))
            for send in _split_copies(src_refs[sl], land_refs[sl], sems[2 * i], sems[2 * i + 1], scatter, False):
                send.start()
        token[...] = jnp.zeros_like(token)

    sem_shapes = []
    for g in groups:
        sem_shapes += [pltpu.SemaphoreType.DMA((len(g) * (N_DEV - 1),))] * 2
    outs = pl.pallas_call(
        body, name=name,
        out_shape=(*sem_shapes, *[pltpu.HBM(a.shape, a.dtype) for a in srcs], *[pltpu.HBM(a.shape, a.dtype) for a in lands],
                   _sds((8, 128), F32)),
        in_specs=[_HBM] * (2 * na), out_specs=(*[_SEM] * (2 * ng), *[_HBM] * (2 * na), pl.BlockSpec(memory_space=pltpu.VMEM)),
        input_output_aliases={i: 2 * ng + i for i in range(2 * na)},
        compiler_params=pltpu.CompilerParams(has_side_effects=_DATAFLOW))(*srcs, *lands)
    src_out, land_out = outs[2 * ng:2 * ng + na], outs[2 * ng + na:2 * ng + 2 * na]
    handles = [(outs[2 * i], outs[2 * i + 1], src_out[first[i]:first[i] + len(g)], land_out[first[i]:first[i] + len(g)])
               for i, g in enumerate(groups)]
    return handles, outs[-1]


def exchange_wait(handle, scatter, name, after):
    send_sems, recv_sems, srcs, lands = handle
    na = len(srcs)

    def body(*refs):
        src_refs, land_refs = refs[:na], refs[na:2 * na]
        for send in _split_copies(src_refs, land_refs, refs[2 * na], refs[2 * na + 1], scatter, False):
            send.wait_send()
        for recv in _split_copies(src_refs, land_refs, refs[2 * na], refs[2 * na + 1], scatter, True):
            recv.wait_recv()

    outs = pl.pallas_call(
        body, name=name, out_shape=tuple(pltpu.HBM(a.shape, a.dtype) for a in (*srcs, *lands)),
        in_specs=[_HBM] * (2 * na) + [_SEM, _SEM, pl.BlockSpec(memory_space=pl.ANY)], out_specs=tuple([_HBM] * (2 * na)),
        input_output_aliases={i: i for i in range(2 * na)},
        compiler_params=pltpu.CompilerParams(has_side_effects=_DATAFLOW))(*srcs, *lands, send_sems, recv_sems, after)
    return list(outs[na:])


def _adamw(w, g, m, v):
    m = ADAM_B1 * m + (1.0 - ADAM_B1) * g
    v = ADAM_B2 * v + (1.0 - ADAM_B2) * (g * g)
    m_hat = m / (1.0 - ADAM_B1 ** ADAM_STEP)
    v_hat = v / (1.0 - ADAM_B2 ** ADAM_STEP)
    return -ADAM_LR * (m_hat / (jnp.sqrt(v_hat) + ADAM_EPS) + ADAM_WD * w), m, v


def reduce_adamw(parts, w4, m4, v4, layer, name, stacked=None):
    nl, r, c = w4.shape
    tr = _tile(r, 256)

    def body(p_ref, w_ref, m_ref, v_ref, *rest):
        g_ref, d_ref, nm_ref, nv_ref = rest[-4:]
        g = p_ref[0].astype(F32)
        for d in range(1, N_DEV):
            g = g + p_ref[d].astype(F32)
        g_ref[...] = g
        d_ref[...], nm_ref[...], nv_ref[...] = _adamw(w_ref[...], g, m_ref[...], v_ref[...])

    lay = pl.BlockSpec((None, tr, c), lambda i: (layer, i, 0))
    osd = _sds((nl, r, c), F32)
    ins = [parts, w4, m4, v4] + (list(stacked) if stacked is not None else [])
    specs = [pl.BlockSpec((N_DEV, tr, c), lambda i: (0, i, 0)), lay, lay, lay]
    specs += [pl.BlockSpec(memory_space=pl.ANY)] * (len(ins) - 4)
    return pl.pallas_call(
        body, name=name, grid=(r // tr,), in_specs=specs, out_specs=[lay, lay, lay, lay], out_shape=[osd, osd, osd, osd],
        input_output_aliases={4 + k: k for k in range(len(ins) - 4)},
        compiler_params=pltpu.CompilerParams(dimension_semantics=("parallel",), vmem_limit_bytes=VMEM_LIMIT))(*ins)


_SM_NAT = ATTN_WIDTH + 2 * ATTN_KV_WIDTH + 4 * DN_WIDTH


def _win_to_zall(w):
    pad = jnp.zeros((w.shape[0], 128 - 2 * DN_HEADS), w.dtype)
    return jnp.concatenate([w[:, :_SM_NAT], w[:, _SM_NAT + 2 * DN_HEADS:], w[:, _SM_NAT:_SM_NAT + 2 * DN_HEADS], pad], axis=1)


def _zall_to_win(g):
    return jnp.concatenate([g[:, :Z_ZG + DN_WIDTH], g[:, Z_SM:Z_SM + 2 * DN_HEADS], g[:, Z_S5:Z_SM]], axis=1)


def _block_diag(t):
    g, a, b = t.shape
    eye = jnp.eye(g, dtype=t.dtype)
    return (t[:, :, None, :] * eye[:, None, :, None]).reshape(g * a, g * b)


def _block_diag_extract(m, g):
    a, b = m.shape[0] // g, m.shape[1] // g
    eye = jnp.eye(g, dtype=m.dtype)
    return jnp.sum(m.reshape(g, a, g, b) * eye[:, None, :, None], axis=2)


def _rope_tables(s):
    half = HEAD_DIM // 2
    inv_freq = ROPE_THETA ** (-jnp.arange(half, dtype=F32) / half)
    ang = jnp.arange(s, dtype=F32)[:, None] * inv_freq[None, :]
    cos, sin = jnp.cos(ang), jnp.sin(ang)
    return jnp.concatenate([cos, cos], axis=1), jnp.concatenate([-sin, sin], axis=1)


def _row(v):
    return v.reshape(1, -1)


def _ffn_fwd(x, g_pre, g_post, weight, tag):
    h = rmsnorm_fwd(x, g_pre, f"{tag}_norm")
    wg, wu = weight(f"{tag}_w_gate", h), weight(f"{tag}_w_up", h)
    a, b, u = ffn_up(h, wg, wu, f"{tag}_up", dep=weight("token", None))
    wd = weight(f"{tag}_w_down", u)
    y, xn = down_norm(u, wd, x, g_post, FFN_RES_WEIGHT, f"{tag}_down")
    return xn, (x, h, a, b, u, y, wg, wu, wd)


def _ffn_bwd(dxn, saved, g_pre, g_post, on_grads, tag):
    x, h, a, b, u, y, wg, wu, wd = saved
    dy, dg_post = norm_bwd(dxn, y, g_post, FFN_RES_WEIGHT, None, BF16, f"{tag}_bnorm_post")
    da, db = ffn_down_bwd(dy, wd, a, b, f"{tag}_bdown")
    dwd = mm_tn(u, dy[None], BF16, f"{tag}_dwd")
    dwg = mm_tn(h[None], da, BF16, f"{tag}_dwg")
    dwu = mm_tn(h[None], db, BF16, f"{tag}_dwu")
    tok = on_grads({f"{tag}_w_gate": dwg, f"{tag}_w_up": dwu, f"{tag}_w_down": dwd})
    dh = mm_nt_acc([(da, wg), (db, wu)], f"{tag}_dh", tm=512, tn=2048, dep=tok)
    dx, dg_pre = norm_bwd(dh, x, g_pre, 1.0, dxn, F32, f"{tag}_bnorm_pre")
    return dx, dict(g_pre=dg_pre, g_post=dg_post)


def _s5_layouts(p):
    are, aim = p["s5_a_re"].reshape(S5_P, 1), p["s5_a_im"].reshape(S5_P, 1)
    ldt = jnp.repeat(p["s5_log_dt"], S5_STATE).reshape(S5_P, 1)
    bre, bim = p["s5_b_re"].reshape(S5_P, S5_GROUP_CH), p["s5_b_im"].reshape(S5_P, S5_GROUP_CH)
    return are, aim, ldt, bre, bim


def _mix_fwd(x, p, weight, cos_f, sin_s, tag):
    h = rmsnorm_fwd(x, _row(p["mix_norm_pre"]), f"{tag}_norm")
    w_all, glu_w = weight("w_all", h), weight("s5_glu_w", h)
    z = mm_nn(h, w_all, f"{tag}_win")
    y_attn = attn_fwd(z, cos_f, sin_s, p["attn_sinks"], f"{tag}_attn")
    alog_b = jnp.broadcast_to(p["dn_a_log"][:, None], (DN_HEADS, 128))
    dtb_b = jnp.broadcast_to(p["dn_dt_bias"][:, None], (DN_HEADS, 128))
    conv_w = weight("dn_conv_w", h)
    qkv = dn_pre_fwd(z, conv_w, f"{tag}_dnpre")
    bb, gb = dn_gates_fwd(z, alog_b, dtb_b, f"{tag}_dngate")
    u, w, qd, kd, at, eg = dn_intra_fwd(qkv, gb, bb, f"{tag}_dnintra")
    o, states = dn_scan_fwd(u, w, qd, kd, at, eg, f"{tag}_dnscan")
    y_dn = dn_out_fwd(o, z, _row(p["dn_norm_w"]), f"{tag}_dnout")
    s5cols = _s5_layouts(p)
    abr, abi, bbr, bbi = s5_params_fwd(*s5cols, f"{tag}_s5par")
    tb = lambda t: jnp.transpose(t.reshape(S5_GROUPS, S5_STATE, S5_GROUP_CH), (0, 2, 1))
    b_blk = jnp.concatenate([_block_diag(tb(bbr)), _block_diag(tb(bbi))], axis=1).astype(BF16)
    tc = lambda t: jnp.transpose(t, (0, 2, 1))
    c_blk = jnp.concatenate([_block_diag(tc(p["s5_c_re"])), -_block_diag(tc(p["s5_c_im"]))], axis=0).astype(BF16)
    bu = mm_nn(z, b_blk, f"{tag}_s5bu", tn=1024, col0=Z_S5)
    xs = s5_scan_fwd(bu, abr.reshape(1, S5_P), abi.reshape(1, S5_P), f"{tag}_s5scan")
    ypre = mm_nn_acc(xs, c_blk.reshape(2, S5_P, S5_WIDTH), f"{tag}_s5c")
    y5, y_s5 = s5_out_fwd(ypre, z, _row(p["s5_d"]), glu_w, _row(p["s5_glu_b"]), f"{tag}_s5out")
    ycat = jnp.concatenate([y_attn, y_dn, y_s5], axis=1)
    w_out = weight("w_out", ycat)
    mixed, xn = down_norm(ycat[None], w_out[None], x, _row(p["mix_norm_post"]), 1.0, f"{tag}_wout")
    saved = dict(x=x, h=h, z=z, qkv=qkv, bb=bb, gb=gb, dn=(u, w, qd, kd, at, eg), states=states, o=o, s5cols=s5cols,
                 abr=abr, abi=abi, b_blk=b_blk, c_blk=c_blk, xs=xs, y5=y5, ycat=ycat, mixed=mixed,
                 alog_b=alog_b, dtb_b=dtb_b, w_all=w_all, w_out=w_out, glu_w=glu_w, conv_w=conv_w)
    return xn, saved


def _mix_bwd(dxn, sv, p, on_grads, cos_f, sin_s, tag):
    z = sv["z"]
    w_all, w_out, glu_w = sv["w_all"], sv["w_out"], sv["glu_w"]
    g = {}
    dmixed, g["mix_norm_post"] = norm_bwd(dxn, sv["mixed"], _row(p["mix_norm_post"]), 1.0, None, BF16, f"{tag}_bnorm_post")
    g["w_out"] = mm_tn(sv["ycat"][None], dmixed[None], BF16, f"{tag}_dwout", tn=1024)[0]
    dycat = mm_nt_acc([(dmixed[None], w_out[None])], f"{tag}_dycat")
    dq, dkc, dkp, dvc, dvp, dsink = attn_bwd(z, cos_f, sin_s, p["attn_sinks"], dycat, f"{tag}_battn")
    g["attn_sinks"] = dsink[:, 0]
    do, dzg, dnw = dn_out_bwd(sv["o"], z, _row(p["dn_norm_w"]), dycat, f"{tag}_bdnout")
    g["dn_norm_w"] = dnw[0]
    cts = dn_scan_bwd(*sv["dn"], sv["states"], do, f"{tag}_bdnscan")
    dqn, dkn, dvn, dgb, dbb = dn_intra_bwd(sv["qkv"], sv["gb"], sv["bb"], cts, f"{tag}_bdnintra")
    dzs, dal, ddt = dn_gates_bwd(z, sv["alog_b"], sv["dtb_b"], dbb, dgb, f"{tag}_bdngate")
    g["dn_a_log"], g["dn_dt_bias"] = dal[:, 0], ddt[:, 0]
    ddn, g["dn_conv_w"] = dn_pre_bwd(z, sv["conv_w"], jnp.concatenate([dqn, dkn, dvn], axis=1), f"{tag}_bdnpre")
    dy5, dd, dglu, dglub = s5_out_bwd(sv["y5"], z, glu_w, _row(p["s5_glu_b"]), dycat, f"{tag}_bs5out")
    g["s5_d"], g["s5_glu_w"], g["s5_glu_b"] = dd[0], dglu, dglub[0]
    dxs = mm_nt_acc([(dy5[None], sv["c_blk"][None])], f"{tag}_bs5c", tn=1024)
    dc_blk = mm_tn(sv["xs"], dy5[None], F32, f"{tag}_ds5c", tk=256)
    ex = lambda m: jnp.transpose(_block_diag_extract(m, S5_GROUPS), (0, 2, 1))
    g["s5_c_re"], g["s5_c_im"] = ex(dc_blk[0]), -ex(dc_blk[1])
    dbu, dar, dai = s5_scan_bwd(dxs, sv["xs"], sv["abr"].reshape(1, S5_P), sv["abi"].reshape(1, S5_P), f"{tag}_bs5scan")
    b_planes = jnp.transpose(sv["b_blk"].reshape(S5_WIDTH, 2, S5_P), (1, 0, 2))
    dus = mm_nt_acc([(dbu, b_planes)], f"{tag}_bs5bu")
    u_s5 = z[:, Z_S5:Z_SM]
    db_blk = mm_tn(u_s5[None], dbu, F32, f"{tag}_ds5b", tn=1024)
    exb = lambda m: jnp.transpose(_block_diag_extract(m, S5_GROUPS), (0, 2, 1)).reshape(S5_P, S5_GROUP_CH)
    dcols = s5_params_bwd(*sv["s5cols"], (dar.reshape(S5_P, 1), dai.reshape(S5_P, 1), exb(db_blk[0]), exb(db_blk[1])),
                          f"{tag}_bs5par")
    g["s5_a_re"] = dcols[0].reshape(S5_GROUPS, S5_STATE)
    g["s5_a_im"] = dcols[1].reshape(S5_GROUPS, S5_STATE)
    g["s5_log_dt"] = jnp.sum(dcols[2].reshape(S5_GROUPS, S5_STATE), axis=1)
    g["s5_b_re"] = dcols[3].reshape(S5_GROUPS, S5_STATE, S5_GROUP_CH)
    g["s5_b_im"] = dcols[4].reshape(S5_GROUPS, S5_STATE, S5_GROUP_CH)
    dz = assemble_dz(dq, dkc, dkp, dvc, dvp, ddn, dzg, dus, dy5, _row(p["s5_d"]), dzs, f"{tag}_dz")
    g["w_all"] = mm_tn(sv["h"][None], dz[None], BF16, f"{tag}_dwin")[0]
    dwin = _zall_to_win(g.pop("w_all"))
    d_model = dwin.shape[0]
    tok = on_grads({"w_in": jnp.transpose(dwin.reshape(d_model, N_DEV, IN_WIDTH // N_DEV), (1, 0, 2)),
                    "s5_glu_w": g.pop("s5_glu_w").astype(BF16).reshape(N_DEV, S5_WIDTH // N_DEV, S5_WIDTH),
                    "w_out": g.pop("w_out").reshape(N_DEV, MIX_WIDTH // N_DEV, d_model)})
    dh = mm_nt_acc([(dz[None], w_all[None])], f"{tag}_dh", dep=tok)
    dx, g["mix_norm_pre"] = norm_bwd(dh, sv["x"], _row(p["mix_norm_pre"]), 1.0, dxn, F32, f"{tag}_bnorm_pre")
    return dx, g


BIG = ("ff1_w_gate", "ff1_w_up", "ff1_w_down", "w_in", "s5_glu_w", "w_out", "ff2_w_gate", "ff2_w_up", "ff2_w_down")
SMALL = ("ff1_norm_pre", "ff1_norm_post", "mix_norm_pre", "attn_sinks", "dn_conv_w", "dn_a_log", "dn_dt_bias", "dn_norm_w",
         "s5_a_re", "s5_a_im", "s5_log_dt", "s5_b_re", "s5_b_im", "s5_c_re", "s5_c_im", "s5_d", "s5_glu_b",
         "mix_norm_post", "ff2_norm_pre", "ff2_norm_post")
GATHER_GROUPS = (("ff1_w_gate", "ff1_w_up"), ("ff1_w_down",), ("w_in", "s5_glu_w", "dn_conv_w"), ("w_out",),
                 ("ff2_w_gate", "ff2_w_up"), ("ff2_w_down",))
WEIGHTS = ("ff1_norm_pre", "ff1_w_gate", "ff1_w_up", "ff1_w_down", "ff1_norm_post", "mix_norm_pre", "w_in", "attn_sinks",
           "dn_conv_w", "dn_a_log", "dn_dt_bias", "dn_norm_w", "s5_a_re", "s5_a_im", "s5_log_dt", "s5_b_re", "s5_b_im",
           "s5_c_re", "s5_c_im", "s5_d", "s5_glu_w", "s5_glu_b", "w_out", "mix_norm_post", "ff2_norm_pre", "ff2_w_gate",
           "ff2_w_up", "ff2_w_down", "ff2_norm_post")


def _pack(parts):
    rows = []
    for a in parts:
        n = a.size
        r = -(-n // 1024) * 8
        rows.append(jnp.pad(a.reshape(-1), (0, r * 128 - n)).reshape(r, 128))
    return jnp.concatenate(rows, axis=0)


def _unpack(mat, shapes):
    out, off = [], 0
    for shp in shapes:
        n = int(np.prod(shp))
        r = -(-n // 1024) * 8
        out.append(mat[off:off + r].reshape(-1)[:n].reshape(shp))
        off += r
    return out


def local_step(x, target, smalls, weight, on_grads):
    depth = len(smalls)
    cos_f, sin_s = _rope_tables(x.shape[0])
    xs = x
    saved = []
    for l in range(depth):
        p = smalls[l]
        wl = functools.partial(weight, l)
        xs, s1 = _ffn_fwd(xs, _row(p["ff1_norm_pre"]), _row(p["ff1_norm_post"]), wl, "ff1")
        xs, s2 = _mix_fwd(xs, p, wl, cos_f, sin_s, "mix")
        xs, s3 = _ffn_fwd(xs, _row(p["ff2_norm_pre"]), _row(p["ff2_norm_post"]), wl, "ff2")
        saved.append((s1, s2, s3))

    loss_vec, dx = loss_and_grad(xs, target, "loss")

    small_g = [None] * depth
    for l in reversed(range(depth)):
        p = smalls[l]
        gl = functools.partial(on_grads, l)
        s1, s2, s3 = saved[l]
        dx, g3 = _ffn_bwd(dx, s3, _row(p["ff2_norm_pre"]), _row(p["ff2_norm_post"]), gl, "ff2")
        dx, g2 = _mix_bwd(dx, s2, p, gl, cos_f, sin_s, "mix")
        dx, g1 = _ffn_bwd(dx, s1, _row(p["ff1_norm_pre"]), _row(p["ff1_norm_post"]), gl, "ff1")
        sg = {n: g2[n] for n in SMALL if n in g2}
        sg.update(ff1_norm_pre=g1["g_pre"][0], ff1_norm_post=g1["g_post"][0], ff2_norm_pre=g3["g_pre"][0], ff2_norm_post=g3["g_post"][0],
                  mix_norm_pre=g2["mix_norm_pre"][0], mix_norm_post=g2["mix_norm_post"][0])
        small_g[l] = sg
    return loss_vec, dx, small_g


def kernel(x, ff1_norm_pre, ff1_w_gate, ff1_w_up, ff1_w_down, ff1_norm_post, mix_norm_pre, w_in, attn_sinks, dn_conv_w, dn_a_log, dn_dt_bias, dn_norm_w, s5_a_re, s5_a_im, s5_log_dt, s5_b_re, s5_b_im, s5_c_re, s5_c_im, s5_d, s5_glu_w, s5_glu_b, w_out, mix_norm_post, ff2_norm_pre, ff2_w_gate, ff2_w_up, ff2_w_down, ff2_norm_post, loss_target, m_ff1_norm_pre, m_ff1_w_gate, m_ff1_w_up, m_ff1_w_down, m_ff1_norm_post, m_mix_norm_pre, m_w_in, m_attn_sinks, m_dn_conv_w, m_dn_a_log, m_dn_dt_bias, m_dn_norm_w, m_s5_a_re, m_s5_a_im, m_s5_log_dt, m_s5_b_re, m_s5_b_im, m_s5_c_re, m_s5_c_im, m_s5_d, m_s5_glu_w, m_s5_glu_b, m_w_out, m_mix_norm_post, m_ff2_norm_pre, m_ff2_w_gate, m_ff2_w_up, m_ff2_w_down, m_ff2_norm_post, v_ff1_norm_pre, v_ff1_w_gate, v_ff1_w_up, v_ff1_w_down, v_ff1_norm_post, v_mix_norm_pre, v_w_in, v_attn_sinks, v_dn_conv_w, v_dn_a_log, v_dn_dt_bias, v_dn_norm_w, v_s5_a_re, v_s5_a_im, v_s5_log_dt, v_s5_b_re, v_s5_b_im, v_s5_c_re, v_s5_c_im, v_s5_d, v_s5_glu_w, v_s5_glu_b, v_w_out, v_mix_norm_post, v_ff2_norm_pre, v_ff2_w_gate, v_ff2_w_up, v_ff2_w_down, v_ff2_norm_post):
    args = dict(locals())
    W = {n: args[n] for n in WEIGHTS}
    M = {n: args["m_" + n] for n in WEIGHTS}
    V = {n: args["v_" + n] for n in WEIGHTS}
    depth = ff1_norm_pre.shape[0]
    d_model = x.shape[2]
    me = 4 * lax.axis_index("x") + 2 * lax.axis_index("y") + lax.axis_index("c")

    conv_sh = dn_conv_w.shape[2]

    def small_params(l):
        return {n: W[n][l] for n in SMALL if n != "dn_conv_w"}

    gathered_w, gather_handles, tokens = {}, {}, []
    group_of = {n: i for i, grp in enumerate(GATHER_GROUPS) for n in grp}

    def start_gather(l):
        shards = {n: cast_bf16(W[n], l, f"cast_{n}") for n in BIG}
        shards["dn_conv_w"] = dn_conv_w[l]
        handles, tok = exchange_start([[shards[n] for n in grp] for grp in GATHER_GROUPS], False, f"gather_start_l{l}")
        gather_handles.update({(l, i): h for i, h in enumerate(handles)})
        tokens.append(tok)

    def weight(l, name, after):
        if name == "token":
            return tokens.pop() if tokens else None
        key = "w_in" if name == "w_all" else name
        i = group_of[key]
        if (l, i) in gather_handles:
            got = dict(zip(GATHER_GROUPS[i], exchange_wait(gather_handles.pop((l, i)), False, f"gather_wait_l{l}_g{i}", after)))
            if i == 0 and l + 1 < depth:
                start_gather(l + 1)
            if "w_in" in got:
                got["dn_conv_w"] = jnp.transpose(got["dn_conv_w"], (1, 0, 2)).reshape(DN_CONV, N_DEV * conv_sh)
                got["w_all"] = _win_to_zall(jnp.transpose(got["w_in"], (1, 0, 2)).reshape(d_model, IN_WIDTH))
                got["s5_glu_w"] = got["s5_glu_w"].reshape(S5_WIDTH, S5_WIDTH)
            if "w_out" in got:
                got["w_out"] = got["w_out"].reshape(MIX_WIDTH, d_model)
            gathered_w.update({(l, n): a for n, a in got.items()})
        return gathered_w[(l, name)]

    stacked = {n: None for n in BIG}
    in_flight = []

    def finish_scatter(after):
        l, names, handle = in_flight.pop(0)
        recv = dict(zip(names, exchange_wait(handle, True, f"scatter_wait_l{l}_{names[0]}", after)))
        for n in names:
            stacked[n] = reduce_adamw(recv[n], W[n], M[n], V[n], l, f"adamw_{n}", stacked[n])

    def on_grads(l, grads):
        names = tuple(grads)
        (handle,), tok = exchange_start([[grads[n] for n in names]], True, f"scatter_start_l{l}_{names[0]}")
        in_flight.append((l, names, handle))
        if len(in_flight) > 1:
            finish_scatter(tok)
        return tok

    start_gather(0)
    tokens.clear()
    smalls = [small_params(l) for l in range(depth)]
    loss_vec, dx, small_g = local_step(x[0], loss_target[0], smalls, weight, on_grads)
    shapes = [(depth,) + ((DN_CONV, N_DEV * conv_sh) if n == "dn_conv_w" else tuple(W[n].shape[1:])) for n in SMALL]
    packed = _pack([jnp.stack([small_g[l][n] for l in range(depth)]) for n in SMALL])
    (small_handle,), small_tok = exchange_start([[packed]], False, "gather_small_start")
    while in_flight:
        finish_scatter(small_tok)
    gathered = exchange_wait(small_handle, False, "gather_small_wait", stacked[BIG[0]][0])[0]
    loss = lax.psum(loss_vec[0, 0], ("x", "y", "c"))

    def shard_of(n, full):
        return lax.dynamic_slice_in_dim(full, me * conv_sh, conv_sh, axis=2) if n == "dn_conv_w" else full

    conv_pad = lambda t: jnp.tile(t, (1, 1, N_DEV))
    wp = _pack([conv_pad(W[n]) if n == "dn_conv_w" else W[n] for n in SMALL])
    mp = _pack([conv_pad(M[n]) if n == "dn_conv_w" else M[n] for n in SMALL])
    vp = _pack([conv_pad(V[n]) if n == "dn_conv_w" else V[n] for n in SMALL])
    sm = reduce_adamw(gathered, wp[None], mp[None], vp[None], 0, "adamw_small")
    small_out = [dict(zip(SMALL, [shard_of(n, t) for n, t in zip(SMALL, _unpack(o[0], shapes))])) for o in sm]

    outs = []
    for kind in range(4):
        for n in WEIGHTS:
            if n in BIG:
                outs.append(stacked[n][kind])
            else:
                outs.append(small_out[kind][n])
    return (loss, dx[None], *outs)
```

```python
import functools
import math

import jax
import jax.numpy as jnp
import numpy as np
from jax import lax
from jax.experimental import pallas as pl
from jax.experimental.pallas import tpu as pltpu

F32 = jnp.float32
BF16 = jnp.bfloat16

N_DEV = 8
DEPTH = 4
ATTN_HEADS = 8
ATTN_KV_HEADS = 2
HEAD_DIM = 128
WINDOW = 128
ROPE_THETA = 10000.0
DN_HEADS = 4
DN_HEAD_DIM = 128
DN_CONV = 4
DN_CHUNK = 64
S5_GROUPS = 32
S5_GROUP_CH = 16
S5_STATE = 64
ATTN_WIDTH = ATTN_HEADS * HEAD_DIM
ATTN_KV_WIDTH = ATTN_KV_HEADS * HEAD_DIM
DN_WIDTH = DN_HEADS * DN_HEAD_DIM
S5_WIDTH = S5_GROUPS * S5_GROUP_CH
S5_P = S5_GROUPS * S5_STATE
MIX_WIDTH = ATTN_WIDTH + DN_WIDTH + S5_WIDTH
IN_WIDTH = ATTN_WIDTH + 2 * ATTN_KV_WIDTH + 4 * DN_WIDTH + 2 * DN_HEADS + S5_WIDTH
Z_Q, Z_K, Z_V = 0, ATTN_WIDTH, ATTN_WIDTH + ATTN_KV_WIDTH
Z_DN = ATTN_WIDTH + 2 * ATTN_KV_WIDTH
Z_ZG = Z_DN + 3 * DN_WIDTH
Z_S5 = Z_ZG + DN_WIDTH
Z_SM = Z_S5 + S5_WIDTH
Z_ALL = Z_SM + 128
FFN_RES_WEIGHT = 0.5
NORM_EPS = 1e-6
ADAM_LR, ADAM_B1, ADAM_B2, ADAM_EPS, ADAM_WD, ADAM_STEP = 0.001, 0.9, 0.999, 1e-08, 0.01, 10

VMEM_LIMIT = 56 * 1024 * 1024
NEG = -1e30

NN = (((1,), (0,)), ((), ()))
NT = (((1,), (1,)), ((), ()))
TN = (((0,), (0,)), ((), ()))


def _dot(a, b, dims=NN, prec=None):
    return lax.dot_general(a, b, dims, preferred_element_type=F32, precision=prec)


def _tile(n, pref, mult=8):
    if n <= pref:
        return n
    t = (pref // mult) * mult
    while t > mult and n % t:
        t -= mult
    assert n % t == 0, (n, pref)
    return t


def _call(body, name, grid, in_specs, out_specs, out_shape, scratch=(), sem=None):
    if sem is None:
        sem = ("arbitrary",) * len(grid)
    return pl.pallas_call(
        body, name=name, grid=grid, in_specs=in_specs, out_specs=out_specs, out_shape=out_shape,
        scratch_shapes=list(scratch),
        compiler_params=pltpu.CompilerParams(dimension_semantics=sem, vmem_limit_bytes=VMEM_LIMIT))


def _sds(shape, dtype):
    return jax.ShapeDtypeStruct(tuple(shape), dtype)


def _sigmoid(x):
    return 1.0 / (1.0 + jnp.exp(-x))


def _silu_and_grad(a):
    sg = _sigmoid(a)
    return a * sg, sg * (1.0 + a * (1.0 - sg))


def _softplus(x):
    return jnp.maximum(x, 0.0) + jnp.log(1.0 + jnp.exp(-jnp.abs(x)))


def _rms(x, g):
    r = lax.rsqrt(jnp.mean(x * x, axis=-1, keepdims=True) + NORM_EPS)
    return x * r * g


def _rms_bwd(dout, y, g):
    r = lax.rsqrt(jnp.mean(y * y, axis=-1, keepdims=True) + NORM_EPS)
    n = y * r
    dn = dout * g
    dy = r * (dn - n * jnp.mean(dn * n, axis=-1, keepdims=True))
    return dy, jnp.sum(dout * n, axis=0, keepdims=True)


def _rope(x, cos_f, sin_s):
    return x * cos_f + pltpu.roll(x, HEAD_DIM // 2, 1) * sin_s


def _rope_bwd(d, cos_f, sin_s):
    return d * cos_f + pltpu.roll(d * sin_s, HEAD_DIM // 2, 1)


def rmsnorm_fwd(x, g, name):
    s, d = x.shape
    tm = _tile(s, 512)

    def body(x_ref, g_ref, o_ref):
        o_ref[...] = _rms(x_ref[...], g_ref[...]).astype(BF16)

    return _call(body, name, (s // tm,),
                 [pl.BlockSpec((tm, d), lambda i: (i, 0)), pl.BlockSpec((1, d), lambda i: (0, 0))],
                 pl.BlockSpec((tm, d), lambda i: (i, 0)), _sds((s, d), BF16), sem=("parallel",))(x, g)


def norm_bwd(dout, y, g, scale, resid, out_dtype, name):
    s, d = y.shape
    tm = _tile(s, 256)
    has_res = resid is not None

    def body(*refs):
        if has_res:
            do_ref, y_ref, g_ref, r_ref, dy_ref, dg_ref = refs
        else:
            do_ref, y_ref, g_ref, dy_ref, dg_ref = refs
        dy, dg = _rms_bwd(do_ref[...] * scale, y_ref[...], g_ref[...])
        if has_res:
            dy = dy + r_ref[...]
        dy_ref[...] = dy.astype(out_dtype)

        @pl.when(pl.program_id(0) == 0)
        def _():
            dg_ref[...] = jnp.zeros_like(dg_ref)

        dg_ref[...] += dg

    row = pl.BlockSpec((tm, d), lambda i: (i, 0))
    vec = pl.BlockSpec((1, d), lambda i: (0, 0))
    ins = [dout, y, g] + ([resid] if has_res else [])
    return _call(body, name, (s // tm,), [row, row, vec] + ([row] if has_res else []),
                 [row, vec], [_sds((s, d), out_dtype), _sds((1, d), F32)])(*ins)


def ffn_up(h, wg, wu, name, dep=None):
    s, d = h.shape
    nj, _, fs = wg.shape
    tm = _tile(s, 1024)

    def body(h_ref, wg_ref, wu_ref, *rest):
        a_ref, b_ref, u_ref = rest[-3:]
        hh = h_ref[...]
        a = _dot(hh, wg_ref[...])
        b = _dot(hh, wu_ref[...])
        a_ref[...] = a.astype(BF16)
        b_ref[...] = b.astype(BF16)
        u_ref[...] = (a * _sigmoid(a) * b).astype(BF16)

    wspec = pl.BlockSpec((None, d, fs), lambda j, i: (j, 0, 0))
    ospec = pl.BlockSpec((None, tm, fs), lambda j, i: (j, i, 0))
    osd = _sds((nj, s, fs), BF16)
    ins, specs = [h, wg, wu], [pl.BlockSpec((tm, d), lambda j, i: (i, 0)), wspec, wspec]
    if dep is not None:
        ins.append(dep)
        specs.append(pl.BlockSpec((8, 128), lambda j, i: (0, 0)))
    return _call(body, name, (nj, s // tm), specs, [ospec, ospec, ospec], [osd, osd, osd], sem=("parallel", "parallel"))(*ins)


def down_norm(u3, w3, x, g, scale, name):
    nj, s, k = u3.shape
    d = w3.shape[2]
    tm = _tile(s, 512)
    jb = 2 if nj % 2 == 0 else 1
    nsteps = nj // jb

    def body(u_ref, w_ref, x_ref, g_ref, y_ref, xn_ref):
        j = pl.program_id(1)
        t = _dot(u_ref[0], w_ref[0])
        for q in range(1, jb):
            t = t + _dot(u_ref[q], w_ref[q])

        @pl.when(j == 0)
        def _():
            y_ref[...] = t

        @pl.when(j > 0)
        def _():
            y_ref[...] += t

        @pl.when(j == nsteps - 1)
        def _():
            xn_ref[...] = x_ref[...] + scale * _rms(y_ref[...], g_ref[...])

    row = pl.BlockSpec((tm, d), lambda i, j: (i, 0))
    return _call(body, name, (s // tm, nsteps),
                 [pl.BlockSpec((jb, tm, k), lambda i, j: (j, i, 0)), pl.BlockSpec((jb, k, d), lambda i, j: (j, 0, 0)),
                  row, pl.BlockSpec((1, d), lambda i, j: (0, 0))],
                 [row, row], [_sds((s, d), F32), _sds((s, d), F32)], sem=("parallel", "arbitrary"))(u3, w3, x, g)


def ffn_down_bwd(dy, wd, a, b, name):
    nj, fs, d = wd.shape
    s = dy.shape[0]
    tm = _tile(s, 1024)

    def body(dy_ref, w_ref, a_ref, b_ref, da_ref, db_ref):
        du = _dot(dy_ref[...], w_ref[...], NT)
        aa = a_ref[...].astype(F32)
        bb = b_ref[...].astype(F32)
        sl, dsl = _silu_and_grad(aa)
        da_ref[...] = (du * bb * dsl).astype(BF16)
        db_ref[...] = (du * sl).astype(BF16)

    hspec = pl.BlockSpec((None, tm, fs), lambda j, i: (j, i, 0))
    osd = _sds((nj, s, fs), BF16)
    return _call(body, name, (nj, s // tm),
                 [pl.BlockSpec((tm, d), lambda j, i: (i, 0)), pl.BlockSpec((None, fs, d), lambda j, i: (j, 0, 0)), hspec, hspec],
                 [hspec, hspec], [osd, osd], sem=("parallel", "parallel"))(dy, wd, a, b)


def mm_tn(a3, b3, out_dtype, name, tmm=2048, tn=1408, tk=1024):
    ja, s, m = a3.shape
    jb, _, n = b3.shape
    nj = max(ja, jb)
    tmm, tn, tk = _tile(m, tmm, 128), _tile(n, tn, 128), _tile(s, tk)
    nk = s // tk

    def body(a_ref, b_ref, o_ref, acc_ref):
        k = pl.program_id(3)

        @pl.when(k == 0)
        def _():
            acc_ref[...] = jnp.zeros_like(acc_ref)

        acc_ref[...] += _dot(a_ref[...].astype(BF16), b_ref[...].astype(BF16), TN)

        @pl.when(k == nk - 1)
        def _():
            o_ref[...] = acc_ref[...].astype(out_dtype)

    aj = (lambda j: j) if ja > 1 else (lambda j: 0)
    bj = (lambda j: j) if jb > 1 else (lambda j: 0)
    return _call(body, name, (nj, m // tmm, n // tn, nk),
                 [pl.BlockSpec((None, tk, tmm), lambda j, im, jn, k: (aj(j), k, im)),
                  pl.BlockSpec((None, tk, tn), lambda j, im, jn, k: (bj(j), k, jn))],
                 pl.BlockSpec((None, tmm, tn), lambda j, im, jn, k: (j, im, jn)), _sds((nj, m, n), out_dtype),
                 scratch=[pltpu.VMEM((tmm, tn), F32)],
                 sem=("parallel", "parallel", "parallel", "arbitrary"))(a3, b3)


def mm_nt_acc(pairs, name, tm=512, tn=512, dep=None):
    nj, s, _ = pairs[0][0].shape
    n = pairs[0][1].shape[1]
    tm, tn = _tile(s, tm), _tile(n, tn, 128)
    npair = len(pairs)

    def body(*refs):
        o_ref = refs[-1]
        j = pl.program_id(2)
        t = None
        for p in range(npair):
            c = _dot(refs[2 * p][...].astype(BF16), refs[2 * p + 1][...], NT)
            t = c if t is None else t + c
        if nj == 1:
            o_ref[...] = t
        else:
            @pl.when(j == 0)
            def _():
                o_ref[...] = t

            @pl.when(j > 0)
            def _():
                o_ref[...] += t

    ins, specs = [], []
    for a3, w3 in pairs:
        k = a3.shape[2]
        ins += [a3, w3]
        specs += [pl.BlockSpec((None, tm, k), lambda i, jn, j: (j, i, 0)),
                  pl.BlockSpec((None, tn, k), lambda i, jn, j: (j, jn, 0))]
    if dep is not None:
        ins.append(dep)
        specs.append(pl.BlockSpec((8, 128), lambda i, jn, j: (0, 0)))
    return _call(body, name, (s // tm, n // tn, nj), specs,
                 pl.BlockSpec((tm, tn), lambda i, jn, j: (i, jn)), _sds((s, n), F32),
                 sem=("parallel", "parallel", "arbitrary"))(*ins)


def mm_nn(a, w, name, tm=512, tn=1408, col0=0, kdim=None):
    s = a.shape[0]
    k, n = w.shape
    assert col0 % k == 0
    tm, tn = _tile(s, tm), _tile(n, tn, 128)
    cb = col0 // k

    def body(a_ref, w_ref, o_ref):
        o_ref[...] = _dot(a_ref[...].astype(BF16), w_ref[...])

    return _call(body, name, (n // tn, s // tm),
                 [pl.BlockSpec((tm, k), lambda jn, i: (i, cb)), pl.BlockSpec((k, tn), lambda jn, i: (0, jn))],
                 pl.BlockSpec((tm, tn), lambda jn, i: (i, jn)), _sds((s, n), F32), sem=("parallel", "parallel"))(a, w)


def mm_nn_acc(a3, w3, name, tm=256):
    nj, s, k = a3.shape
    n = w3.shape[2]
    tm = _tile(s, tm)

    def body(a_ref, w_ref, o_ref):
        j = pl.program_id(1)
        t = _dot(a_ref[...].astype(BF16), w_ref[...])

        @pl.when(j == 0)
        def _():
            o_ref[...] = t

        @pl.when(j > 0)
        def _():
            o_ref[...] += t

    return _call(body, name, (s // tm, nj),
                 [pl.BlockSpec((None, tm, k), lambda i, j: (j, i, 0)), pl.BlockSpec((None, k, n), lambda i, j: (j, 0, 0))],
                 pl.BlockSpec((tm, n), lambda i, j: (i, 0)), _sds((s, n), F32), sem=("parallel", "arbitrary"))(a3, w3)


def loss_and_grad(xl, target, name):
    s, d = xl.shape
    tm = _tile(s, 512)

    def body(x_ref, t_ref, l_ref, dx_ref):
        e = x_ref[...] - t_ref[...]
        dx_ref[...] = e * (1.0 / d)

        @pl.when(pl.program_id(0) == 0)
        def _():
            l_ref[...] = jnp.zeros_like(l_ref)

        part = jnp.sum(jnp.sum(e * e, axis=-1, keepdims=True), axis=0, keepdims=True) * (0.5 / d)
        l_ref[...] += jnp.broadcast_to(part, l_ref.shape)

    row = pl.BlockSpec((tm, d), lambda i: (i, 0))
    return _call(body, name, (s // tm,), [row, row], [pl.BlockSpec((1, 128), lambda i: (0, 0)), row],
                 [_sds((1, 128), F32), _sds((s, d), F32)])(xl, target)


def cast_bf16(w4, layer, name):
    _, r, c = w4.shape
    tr = _tile(r, 512)

    def body(w_ref, o_ref):
        o_ref[...] = w_ref[...].astype(BF16)

    return _call(body, name, (r // tr,), [pl.BlockSpec((None, tr, c), lambda i: (layer, i, 0))],
                 pl.BlockSpec((tr, c), lambda i: (i, 0)), _sds((r, c), BF16), sem=("parallel",))(w4)


def _attn_specs(nb):
    w = WINDOW
    prev = lambda i: jnp.maximum(i - 1, 0)
    q = pl.BlockSpec((w, ATTN_WIDTH), lambda i: (i, 0))
    kc = pl.BlockSpec((w, ATTN_KV_WIDTH), lambda i: (i, Z_K // ATTN_KV_WIDTH))
    kp = pl.BlockSpec((w, ATTN_KV_WIDTH), lambda i: (prev(i), Z_K // ATTN_KV_WIDTH))
    vc = pl.BlockSpec((w, ATTN_KV_WIDTH), lambda i: (i, Z_V // ATTN_KV_WIDTH))
    vp = pl.BlockSpec((w, ATTN_KV_WIDTH), lambda i: (prev(i), Z_V // ATTN_KV_WIDTH))
    tc = pl.BlockSpec((w, HEAD_DIM), lambda i: (i, 0))
    tp = pl.BlockSpec((w, HEAD_DIM), lambda i: (prev(i), 0))
    sink = pl.BlockSpec(memory_space=pltpu.SMEM)
    return [q, kc, kp, vc, vp, tc, tc, tp, tp, sink]


def _attn_mask(i):
    w = WINDOW
    qi = lax.broadcasted_iota(jnp.int32, (w, 2 * w), 0) + w
    kj = lax.broadcasted_iota(jnp.int32, (w, 2 * w), 1)
    rel = qi - kj
    band = (rel >= 0) & (rel < w)
    return band & jnp.logical_not((i == 0) & (kj < w))


def _attn_probs(qs, kk, sinks, mask):
    n = range(len(qs))
    s = [_dot(qs[i], kk, NT) * (HEAD_DIM ** -0.5) for i in n]
    s = [jnp.where(mask, s[i], NEG) for i in n]
    m = [jnp.maximum(jnp.max(s[i], axis=-1, keepdims=True), sinks[i]) for i in n]
    p = [jnp.exp(s[i] - m[i]) for i in n]
    es = [jnp.exp(sinks[i] - m[i]) for i in n]
    inv = [1.0 / (jnp.sum(p[i], axis=-1, keepdims=True) + es[i]) for i in n]
    return [p[i] * inv[i] for i in n], [es[i] * inv[i] for i in n]


def attn_fwd(z, cos_f, sin_s, sinks, name):
    s = z.shape[0]
    nb = s // WINDOW
    hd = HEAD_DIM
    grp = ATTN_HEADS // ATTN_KV_HEADS

    def body(q_ref, kc_ref, kp_ref, vc_ref, vp_ref, cc_ref, sc_ref, cp_ref, sp_ref, sink_ref, o_ref):
        i = pl.program_id(0)
        mask = _attn_mask(i)
        cc, sc, cp, sp = cc_ref[...], sc_ref[...], cp_ref[...], sp_ref[...]
        for kv in range(ATTN_KV_HEADS):
            ksl = slice(kv * hd, (kv + 1) * hd)
            kk = jnp.concatenate([_rope(kp_ref[:, ksl], cp, sp), _rope(kc_ref[:, ksl], cc, sc)], axis=0).astype(BF16)
            vv = jnp.concatenate([vp_ref[:, ksl], vc_ref[:, ksl]], axis=0).astype(BF16)
            heads = [kv * grp + g for g in range(grp)]
            hsl = [slice(h * hd, (h + 1) * hd) for h in heads]
            q = [_rope(q_ref[:, sl], cc, sc).astype(BF16) for sl in hsl]
            pn, _ = _attn_probs(q, kk, [sink_ref[h] for h in heads], mask)
            out = [_dot(pn[i].astype(BF16), vv).astype(BF16) for i in range(grp)]
            for i in range(grp):
                o_ref[:, hsl[i]] = out[i]

    return _call(body, name, (nb,), _attn_specs(nb), pl.BlockSpec((WINDOW, ATTN_WIDTH), lambda i: (i, 0)),
                 _sds((s, ATTN_WIDTH), BF16), sem=("parallel",))(z, z, z, z, z, cos_f, sin_s, cos_f, sin_s, sinks)


def attn_bwd(z, cos_f, sin_s, sinks, dy, name):
    s = z.shape[0]
    nb = s // WINDOW
    hd = HEAD_DIM
    grp = ATTN_HEADS // ATTN_KV_HEADS
    scale = HEAD_DIM ** -0.5

    def body(q_ref, kc_ref, kp_ref, vc_ref, vp_ref, cc_ref, sc_ref, cp_ref, sp_ref, sink_ref, dy_ref,
             dq_ref, dkc_ref, dkp_ref, dvc_ref, dvp_ref, ds_ref):
        i = pl.program_id(0)
        mask = _attn_mask(i)
        cc, sc, cp, sp = cc_ref[...], sc_ref[...], cp_ref[...], sp_ref[...]

        @pl.when(i == 0)
        def _():
            ds_ref[...] = jnp.zeros_like(ds_ref)

        for kv in range(ATTN_KV_HEADS):
            ksl = slice(kv * hd, (kv + 1) * hd)
            kk = jnp.concatenate([_rope(kp_ref[:, ksl], cp, sp), _rope(kc_ref[:, ksl], cc, sc)], axis=0).astype(BF16)
            vv = jnp.concatenate([vp_ref[:, ksl], vc_ref[:, ksl]], axis=0).astype(BF16)
            heads = [kv * grp + g for g in range(grp)]
            n = range(grp)
            hsl = [slice(h * hd, (h + 1) * hd) for h in heads]
            q = [_rope(q_ref[:, sl], cc, sc).astype(BF16) for sl in hsl]
            pn, psink = _attn_probs(q, kk, [sink_ref[h] for h in heads], mask)
            do = [dy_ref[:, sl].astype(BF16) for sl in hsl]
            dpn = [_dot(do[i], vv, NT) for i in n]
            dvs = [_dot(pn[i].astype(BF16), do[i], TN) for i in n]
            tot = [jnp.sum(pn[i] * dpn[i], axis=-1, keepdims=True) for i in n]
            dsc = [(pn[i] * (dpn[i] - tot[i]) * scale).astype(BF16) for i in n]
            dqs = [_rope_bwd(_dot(dsc[i], kk), cc, sc) for i in n]
            dks = [_dot(dsc[i], q[i], TN) for i in n]
            dvv = (dvs[0] + dvs[1]) + (dvs[2] + dvs[3]) if grp == 4 else sum(dvs[1:], dvs[0])
            dkk = (dks[0] + dks[1]) + (dks[2] + dks[3]) if grp == 4 else sum(dks[1:], dks[0])
            for i in n:
                dq_ref[:, hsl[i]] = dqs[i]
                dsink = jnp.sum(-psink[i] * tot[i], axis=0, keepdims=True)
                ds_ref[heads[i]:heads[i] + 1, :] += jnp.broadcast_to(dsink, (1, 128))
            dkp_ref[:, ksl] = _rope_bwd(dkk[:WINDOW], cp, sp)
            dkc_ref[:, ksl] = _rope_bwd(dkk[WINDOW:], cc, sc)
            dvp_ref[:, ksl] = dvv[:WINDOW]
            dvc_ref[:, ksl] = dvv[WINDOW:]

    kvo = pl.BlockSpec((WINDOW, ATTN_KV_WIDTH), lambda i: (i, 0))
    kvs = _sds((s, ATTN_KV_WIDTH), F32)
    return _call(body, name, (nb,), _attn_specs(nb) + [pl.BlockSpec((WINDOW, ATTN_WIDTH), lambda i: (i, 0))],
                 [pl.BlockSpec((WINDOW, ATTN_WIDTH), lambda i: (i, 0)), kvo, kvo, kvo, kvo,
                  pl.BlockSpec((ATTN_HEADS, 128), lambda i: (0, 0))],
                 [_sds((s, ATTN_WIDTH), F32), kvs, kvs, kvs, kvs, _sds((ATTN_HEADS, 128), F32)])(
        z, z, z, z, z, cos_f, sin_s, cos_f, sin_s, sinks, dy)


def _shift_down(x, d, row):
    return jnp.where(row >= d, pltpu.roll(x, d, 0), 0.0)


def _shift_up(x, d, row, n):
    return jnp.where(row < n - d, pltpu.roll(x, n - d, 0), 0.0)


def _conv_taps(u, w_ref, row):
    c = w_ref[DN_CONV - 1:DN_CONV, :] * u
    for k in range(DN_CONV - 1):
        c = c + w_ref[k:k + 1, :] * _shift_down(u, DN_CONV - 1 - k, row)
    return c


def dn_pre_fwd(z, conv_w, name):
    s = z.shape[0]
    nblk = 3 * DN_WIDTH // 128
    nqk = 2 * DN_WIDTH // 128

    def body(u_ref, w_ref, o_ref):
        row = lax.broadcasted_iota(jnp.int32, (s, 128), 0)
        c = _conv_taps(u_ref[...], w_ref, row)
        sl = c * _sigmoid(c)
        j = pl.program_id(0)

        @pl.when(j < nqk)
        def _():
            o_ref[...] = sl * lax.rsqrt(jnp.sum(sl * sl, axis=-1, keepdims=True) + NORM_EPS)

        @pl.when(j >= nqk)
        def _():
            o_ref[...] = sl

    return _call(body, name, (nblk,),
                 [pl.BlockSpec((s, 128), lambda j: (0, Z_DN // 128 + j)), pl.BlockSpec((DN_CONV, 128), lambda j: (0, j))],
                 pl.BlockSpec((s, 128), lambda j: (0, j)), _sds((s, 3 * DN_WIDTH), F32), sem=("parallel",))(z, conv_w)


def dn_pre_bwd(z, conv_w, dout, name):
    s = z.shape[0]
    nblk = 3 * DN_WIDTH // 128
    nqk = 2 * DN_WIDTH // 128

    def body(u_ref, w_ref, do_ref, du_ref, dw_ref, ds_ref):
        row = lax.broadcasted_iota(jnp.int32, (s, 128), 0)
        u = u_ref[...]
        c = _conv_taps(u, w_ref, row)
        sl, dsl = _silu_and_grad(c)
        do = do_ref[...]
        j = pl.program_id(0)

        @pl.when(j < nqk)
        def _():
            r = lax.rsqrt(jnp.sum(sl * sl, axis=-1, keepdims=True) + NORM_EPS)
            ds_ref[...] = r * do - sl * (r * r * r) * jnp.sum(do * sl, axis=-1, keepdims=True)

        @pl.when(j >= nqk)
        def _():
            ds_ref[...] = do

        dc = ds_ref[...] * dsl
        du = w_ref[DN_CONV - 1:DN_CONV, :] * dc
        dw_ref[DN_CONV - 1:DN_CONV, :] = jnp.sum(dc * u, axis=0, keepdims=True)
        for k in range(DN_CONV - 1):
            d = DN_CONV - 1 - k
            du = du + w_ref[k:k + 1, :] * _shift_up(dc, d, row, s)
            dw_ref[k:k + 1, :] = jnp.sum(dc * _shift_down(u, d, row), axis=0, keepdims=True)
        du_ref[...] = du

    blk = pl.BlockSpec((s, 128), lambda j: (0, j))
    wsp = pl.BlockSpec((DN_CONV, 128), lambda j: (0, j))
    return _call(body, name, (nblk,), [pl.BlockSpec((s, 128), lambda j: (0, Z_DN // 128 + j)), wsp, blk],
                 [blk, wsp], [_sds((s, 3 * DN_WIDTH), F32), _sds((DN_CONV, 3 * DN_WIDTH), F32)],
                 scratch=[pltpu.VMEM((s, 128), F32)], sem=("parallel",))(z, conv_w, dout)


def _lane_col(x, lane, idx):
    return jnp.sum(jnp.where(lane == idx, x, 0.0), axis=-1, keepdims=True)


def dn_gates_fwd(z, alog_b, dtb_b, name):
    s = z.shape[0]
    tm = _tile(s, 512)

    def body(zs_ref, al_ref, dt_ref, beta_ref, g_ref):
        zs = zs_ref[...]
        lane = lax.broadcasted_iota(jnp.int32, zs.shape, 1)
        for h in range(DN_HEADS):
            b_raw = _lane_col(zs, lane, h)
            a_raw = _lane_col(zs, lane, DN_HEADS + h)
            beta_ref[h] = jnp.broadcast_to(_sigmoid(b_raw), (tm, 128))
            g_ref[h] = -jnp.exp(al_ref[h:h + 1, :]) * _softplus(a_raw + dt_ref[h:h + 1, :])

    osp = pl.BlockSpec((DN_HEADS, tm, 128), lambda i: (0, i, 0))
    psp = pl.BlockSpec((DN_HEADS, 128), lambda i: (0, 0))
    osd = _sds((DN_HEADS, s, 128), F32)
    return _call(body, name, (s // tm,), [pl.BlockSpec((tm, 128), lambda i: (i, Z_SM // 128)), psp, psp],
                 [osp, osp], [osd, osd], sem=("parallel",))(z, alog_b, dtb_b)


def dn_gates_bwd(z, alog_b, dtb_b, dbeta, dg, name):
    s = z.shape[0]
    tm = _tile(s, 512)

    def body(zs_ref, al_ref, dt_ref, dbeta_ref, dg_ref, dz_ref, dal_ref, ddt_ref):
        @pl.when(pl.program_id(0) == 0)
        def _():
            dal_ref[...] = jnp.zeros_like(dal_ref)
            ddt_ref[...] = jnp.zeros_like(ddt_ref)

        zs = zs_ref[...]
        lane = lax.broadcasted_iota(jnp.int32, zs.shape, 1)
        dz = jnp.zeros_like(zs)
        for h in range(DN_HEADS):
            b_raw = _lane_col(zs, lane, h)
            a_raw = _lane_col(zs, lane, DN_HEADS + h)
            dbe = jnp.sum(dbeta_ref[h], axis=-1, keepdims=True)
            dgg = jnp.sum(dg_ref[h], axis=-1, keepdims=True)
            beta = _sigmoid(b_raw)
            ea = jnp.exp(al_ref[h:h + 1, :])
            pre = a_raw + dt_ref[h:h + 1, :]
            da_raw = dgg * (-ea) * _sigmoid(pre)
            dz = dz + jnp.where(lane == h, dbe * beta * (1.0 - beta), 0.0) + jnp.where(lane == DN_HEADS + h, da_raw, 0.0)
            ddt_ref[h:h + 1, :] += jnp.sum(da_raw, axis=0, keepdims=True)
            dal_ref[h:h + 1, :] += jnp.sum(dgg * (-ea) * _softplus(pre), axis=0, keepdims=True)
        dz_ref[...] = dz

    hsp = pl.BlockSpec((DN_HEADS, tm, 128), lambda i: (0, i, 0))
    psp = pl.BlockSpec((DN_HEADS, 128), lambda i: (0, 0))
    return _call(body, name, (s // tm,), [pl.BlockSpec((tm, 128), lambda i: (i, Z_SM // 128)), psp, psp, hsp, hsp],
                 [pl.BlockSpec((tm, 128), lambda i: (i, 0)), psp, psp],
                 [_sds((s, 128), F32), _sds((DN_HEADS, 128), F32), _sds((DN_HEADS, 128), F32)])(z, alog_b, dtb_b, dbeta, dg)


def _dn_intra(q, k, v, gb, bb):
    c = DN_CHUNK
    pairs = range(len(q))
    ri = lax.broadcasted_iota(jnp.int32, (c, c), 0)
    ci = lax.broadcasted_iota(jnp.int32, (c, c), 1)
    causal = ri >= ci
    strict = ri > ci
    gc = [_chunk_cumsum(gb[i]) for i in pairs]
    grow = [_as_row(gc[i]) for i in pairs]
    decay = [jnp.where(causal, jnp.exp(jnp.where(causal, gc[i][:, :c] - grow[i], 0.0)), 0.0) for i in pairs]
    qs = [q[i] * (DN_HEAD_DIM ** -0.5) for i in pairs]
    kb = [k[i] * bb[i] for i in pairs]
    lower = [jnp.where(strict, _dot(kb[i], k[i], NT) * decay[i], 0.0) for i in pairs]
    t = _unit_lower_inverse(lower)
    eg = [jnp.exp(gc[i]) for i in pairs]
    u = [_dot(t[i], v[i] * bb[i]) for i in pairs]
    w = [_dot(t[i], kb[i] * eg[i]) for i in pairs]
    attn = [jnp.where(causal, _dot(qs[i], k[i], NT) * decay[i], 0.0) for i in pairs]
    glast = [_last_row(gc[i]) for i in pairs]
    return (u, w, [qs[i] * eg[i] for i in pairs], [k[i] * jnp.exp(glast[i] - gc[i]) for i in pairs], attn,
            [jnp.exp(glast[i][:8]) for i in pairs])


def _split3(x):
    x1 = x.astype(BF16)
    r = x - x1.astype(F32)
    x2 = r.astype(BF16)
    return x1, x2, (r - x2.astype(F32)).astype(BF16)


def _dot_split(a, b, dims=NN):
    a1, a2, _ = _split3(a)
    b1, b2, _ = _split3(b)
    return _dot(a1, b1, dims) + (_dot(a1, b2, dims) + _dot(a2, b1, dims))


def _sel_dot(m01, x, dims, m_left):
    m = m01.astype(BF16)
    parts = [_dot(m, xi, dims) if m_left else _dot(xi, m, dims) for xi in _split3(x)]
    return parts[0] + (parts[1] + parts[2])


def _tri_mask(n, upper):
    ri = lax.broadcasted_iota(jnp.int32, (n, n), 0)
    ci = lax.broadcasted_iota(jnp.int32, (n, n), 1)
    return ci >= ri if upper else ri >= ci


@jax.custom_vjp
def _chunk_cumsum(x):
    return _sel_dot(_tri_mask(x.shape[0], False), x, NN, True)


_chunk_cumsum.defvjp(lambda x: (_chunk_cumsum(x), None),
                     lambda _, ct: (_sel_dot(_tri_mask(ct.shape[0], True), ct, NN, True),))


def _lane0(rows):
    return lax.broadcasted_iota(jnp.int32, (rows, 128), 1) == 0


@jax.custom_vjp
def _as_row(x):
    return _sel_dot(_lane0(x.shape[0]), x, NT, True)


_as_row.defvjp(lambda x: (_as_row(x), None),
               lambda _, ct: (_sel_dot(_lane0(ct.shape[0]), ct, TN, False),))


def _last_col_mask(n, transpose):
    idx = lax.broadcasted_iota(jnp.int32, (n, n), 0 if transpose else 1)
    return idx == n - 1


@jax.custom_vjp
def _last_row(x):
    return _sel_dot(_last_col_mask(x.shape[0], False), x, NN, True)


_last_row.defvjp(lambda x: (_last_row(x), None),
                 lambda _, ct: (_sel_dot(_last_col_mask(ct.shape[0], True), ct, NN, True),))


@jax.custom_vjp
def _unit_lower_inverse(lowers):
    n = lowers[0].shape[0]
    eye = (lax.broadcasted_iota(jnp.int32, (n, n), 0) == lax.broadcasted_iota(jnp.int32, (n, n), 1)).astype(F32)
    ts = [eye - l for l in lowers]
    ps = list(lowers)
    for _ in range(5):
        ps = [_dot_split(p, p) for p in ps]
        ts = [t + _dot_split(t, p) for t, p in zip(ts, ps)]
    return ts


def _unit_lower_inverse_fwd(lowers):
    ts = _unit_lower_inverse(lowers)
    return ts, ts


def _unit_lower_inverse_bwd(ts, dts):
    half = [_dot_split(t, dt, TN) for t, dt in zip(ts, dts)]
    return ([-_dot_split(h, t, NT) for h, t in zip(half, ts)],)


_unit_lower_inverse.defvjp(_unit_lower_inverse_fwd, _unit_lower_inverse_bwd)


def _dn_step(st, qd, kd, u, w, attn, egl):
    heads = range(len(st))
    v_new = [u[h] - _dot(w[h], st[h]) for h in heads]
    o = [_dot(qd[h], st[h]) for h in heads]
    o = [o[h] + _dot(attn[h], v_new[h]) for h in heads]
    st_new = [st[h] * egl[h][0:1, :] + _dot(kd[h], v_new[h], TN) for h in heads]
    return o, st_new


def _dn_chunk_specs(m=1):
    c = DN_CHUNK
    wide = pl.BlockSpec((m * c, DN_WIDTH), lambda i: (i, 0))
    att = pl.BlockSpec((m * c, DN_HEADS * c), lambda i: (i, 0))
    egl = pl.BlockSpec((m * 8, DN_WIDTH), lambda i: (i, 0))
    return wide, att, egl


def _dn_intra_chunks(nc):
    return 4 if nc % 4 == 0 else (2 if nc % 2 == 0 else 1)


def dn_intra_fwd(qkv, gb, bb, name):
    s = qkv.shape[0]
    c, hd = DN_CHUNK, DN_HEAD_DIM
    nc = s // c
    m = _dn_intra_chunks(nc)

    def body(q_ref, k_ref, v_ref, g_ref, b_ref, u_ref, w_ref, qd_ref, kd_ref, at_ref, eg_ref):
        pairs = [(t, h) for t in range(m) for h in range(DN_HEADS)]
        rs = lambda t: slice(t * c, (t + 1) * c)
        hs = lambda h: slice(h * hd, (h + 1) * hd)
        u, w, qd, kd, at, eg = _dn_intra([q_ref[rs(t), hs(h)] for t, h in pairs], [k_ref[rs(t), hs(h)] for t, h in pairs],
                                         [v_ref[rs(t), hs(h)] for t, h in pairs], [g_ref[h, rs(t)] for t, h in pairs],
                                         [b_ref[h, rs(t)] for t, h in pairs])
        for i, (t, h) in enumerate(pairs):
            u_ref[rs(t), hs(h)], w_ref[rs(t), hs(h)], qd_ref[rs(t), hs(h)], kd_ref[rs(t), hs(h)] = u[i], w[i], qd[i], kd[i]
            at_ref[rs(t), h * c:(h + 1) * c] = at[i]
            eg_ref[t * 8:(t + 1) * 8, hs(h)] = eg[i]

    wide, att, egl = _dn_chunk_specs(m)
    hsp = pl.BlockSpec((DN_HEADS, m * c, 128), lambda i: (0, i, 0))
    wsd = _sds((s, DN_WIDTH), F32)
    return _call(body, name, (nc // m,),
                 [pl.BlockSpec((m * c, DN_WIDTH), lambda i: (i, 0)), pl.BlockSpec((m * c, DN_WIDTH), lambda i: (i, 1)),
                  pl.BlockSpec((m * c, DN_WIDTH), lambda i: (i, 2)), hsp, hsp],
                 [wide, wide, wide, wide, att, egl],
                 [wsd, wsd, wsd, wsd, _sds((s, DN_HEADS * c), F32), _sds((nc * 8, DN_WIDTH), F32)],
                 sem=("parallel",))(qkv, qkv, qkv, gb, bb)


def dn_intra_bwd(qkv, gb, bb, cts, name):
    s = qkv.shape[0]
    c, hd = DN_CHUNK, DN_HEAD_DIM
    nc = s // c
    m = _dn_intra_chunks(nc)

    def body(q_ref, k_ref, v_ref, g_ref, b_ref, du_ref, dw_ref, dqd_ref, dkd_ref, dat_ref, deg_ref,
             dq_ref, dk_ref, dv_ref, dg_ref, db_ref):
        pairs = [(t, h) for t in range(m) for h in range(DN_HEADS)]
        rs = lambda t: slice(t * c, (t + 1) * c)
        hs = lambda h: slice(h * hd, (h + 1) * hd)
        _, vjp = jax.vjp(_dn_intra, [q_ref[rs(t), hs(h)] for t, h in pairs], [k_ref[rs(t), hs(h)] for t, h in pairs],
                         [v_ref[rs(t), hs(h)] for t, h in pairs], [g_ref[h, rs(t)] for t, h in pairs],
                         [b_ref[h, rs(t)] for t, h in pairs])
        dq, dk, dv, dg, db = vjp(([du_ref[rs(t), hs(h)] for t, h in pairs], [dw_ref[rs(t), hs(h)] for t, h in pairs],
                                  [dqd_ref[rs(t), hs(h)] for t, h in pairs], [dkd_ref[rs(t), hs(h)] for t, h in pairs],
                                  [dat_ref[rs(t), h * c:(h + 1) * c] for t, h in pairs],
                                  [deg_ref[t * 8:(t + 1) * 8, hs(h)] for t, h in pairs]))
        for i, (t, h) in enumerate(pairs):
            dq_ref[rs(t), hs(h)], dk_ref[rs(t), hs(h)], dv_ref[rs(t), hs(h)] = dq[i], dk[i], dv[i]
            dg_ref[h, rs(t)] = dg[i]
            db_ref[h, rs(t)] = db[i]

    wide, att, egl = _dn_chunk_specs(m)
    hsp = pl.BlockSpec((DN_HEADS, m * c, 128), lambda i: (0, i, 0))
    hsd = _sds((DN_HEADS, s, 128), F32)
    wsd = _sds((s, DN_WIDTH), F32)
    return _call(body, name, (nc // m,),
                 [pl.BlockSpec((m * c, DN_WIDTH), lambda i: (i, 0)), pl.BlockSpec((m * c, DN_WIDTH), lambda i: (i, 1)),
                  pl.BlockSpec((m * c, DN_WIDTH), lambda i: (i, 2)), hsp, hsp, wide, wide, wide, wide, att, egl],
                 [wide, wide, wide, hsp, hsp], [wsd, wsd, wsd, hsd, hsd], sem=("parallel",))(qkv, qkv, qkv, gb, bb, *cts)


def dn_scan_fwd(u, w, qd, kd, at, eg, name):
    s = u.shape[0]
    c, hd = DN_CHUNK, DN_HEAD_DIM
    nc = s // c

    def body(u_ref, w_ref, qd_ref, kd_ref, at_ref, eg_ref, o_ref, st_ref, state):
        @pl.when(pl.program_id(0) == 0)
        def _():
            state[...] = jnp.zeros_like(state)

        heads = range(DN_HEADS)
        hs = lambda h: slice(h * hd, (h + 1) * hd)
        st = [state[h] for h in heads]
        for h in heads:
            st_ref[h] = st[h]
        o, st_new = _dn_step(st, [qd_ref[:, hs(h)] for h in heads], [kd_ref[:, hs(h)] for h in heads],
                             [u_ref[:, hs(h)] for h in heads], [w_ref[:, hs(h)] for h in heads],
                             [at_ref[:, h * c:(h + 1) * c] for h in heads], [eg_ref[:, hs(h)] for h in heads])
        for h in heads:
            o_ref[:, hs(h)] = o[h]
            state[h] = st_new[h]

    wide, att, egl = _dn_chunk_specs()
    return _call(body, name, (nc,), [wide, wide, wide, wide, att, egl],
                 [wide, pl.BlockSpec((None, DN_HEADS, hd, hd), lambda i: (i, 0, 0, 0))],
                 [_sds((s, DN_WIDTH), F32), _sds((nc, DN_HEADS, hd, hd), F32)],
                 scratch=[pltpu.VMEM((DN_HEADS, hd, hd), F32)])(u, w, qd, kd, at, eg)


def dn_scan_bwd(u, w, qd, kd, at, eg, states, do, name):
    s = u.shape[0]
    c, hd = DN_CHUNK, DN_HEAD_DIM
    nc = s // c

    def body(u_ref, w_ref, qd_ref, kd_ref, at_ref, eg_ref, st_ref, do_ref,
             du_ref, dw_ref, dqd_ref, dkd_ref, dat_ref, deg_ref, dstate):
        @pl.when(pl.program_id(0) == 0)
        def _():
            dstate[...] = jnp.zeros_like(dstate)

        heads = range(DN_HEADS)
        hs = lambda h: slice(h * hd, (h + 1) * hd)
        asl = lambda h: slice(h * c, (h + 1) * c)
        _, vjp = jax.vjp(_dn_step, [st_ref[h] for h in heads], [qd_ref[:, hs(h)] for h in heads],
                         [kd_ref[:, hs(h)] for h in heads], [u_ref[:, hs(h)] for h in heads], [w_ref[:, hs(h)] for h in heads],
                         [at_ref[:, asl(h)] for h in heads], [eg_ref[:, hs(h)] for h in heads])
        dst, dqd, dkd, du, dw, dat, deg = vjp(([do_ref[:, hs(h)] for h in heads], [dstate[h] for h in heads]))
        for h in heads:
            dstate[h] = dst[h]
            du_ref[:, hs(h)], dw_ref[:, hs(h)], dqd_ref[:, hs(h)], dkd_ref[:, hs(h)] = du[h], dw[h], dqd[h], dkd[h]
            dat_ref[:, asl(h)] = dat[h]
            deg_ref[:, hs(h)] = deg[h]

    rev = lambda i: nc - 1 - i
    wide = pl.BlockSpec((c, DN_WIDTH), lambda i: (rev(i), 0))
    att = pl.BlockSpec((c, DN_HEADS * c), lambda i: (rev(i), 0))
    egl = pl.BlockSpec((8, DN_WIDTH), lambda i: (rev(i), 0))
    wsd = _sds((s, DN_WIDTH), F32)
    return _call(body, name, (nc,),
                 [wide, wide, wide, wide, att, egl, pl.BlockSpec((None, DN_HEADS, hd, hd), lambda i: (rev(i), 0, 0, 0)), wide],
                 [wide, wide, wide, wide, att, egl],
                 [wsd, wsd, wsd, wsd, _sds((s, DN_HEADS * c), F32), _sds((nc * 8, DN_WIDTH), F32)],
                 scratch=[pltpu.VMEM((DN_HEADS, hd, hd), F32)])(u, w, qd, kd, at, eg, states, do)


def _dn_out(o, zg, nw):
    n = o * lax.rsqrt(jnp.mean(o * o, axis=-1, keepdims=True) + NORM_EPS) * nw
    return n * (zg * _sigmoid(zg))


def dn_out_fwd(o, z, nw, name):
    s = o.shape[0]
    tm = _tile(s, 512)
    hd = DN_HEAD_DIM

    def body(o_ref, zg_ref, nw_ref, y_ref):
        for h in range(DN_HEADS):
            hs = slice(h * hd, (h + 1) * hd)
            y_ref[:, hs] = _dn_out(o_ref[:, hs], zg_ref[:, hs], nw_ref[...]).astype(BF16)

    return _call(body, name, (s // tm,),
                 [pl.BlockSpec((tm, DN_WIDTH), lambda i: (i, 0)), pl.BlockSpec((tm, DN_WIDTH), lambda i: (i, Z_ZG // DN_WIDTH)),
                  pl.BlockSpec((1, hd), lambda i: (0, 0))],
                 pl.BlockSpec((tm, DN_WIDTH), lambda i: (i, 0)), _sds((s, DN_WIDTH), BF16), sem=("parallel",))(o, z, nw)


def dn_out_bwd(o, z, nw, dycat, name):
    s = o.shape[0]
    tm = _tile(s, 512)
    hd = DN_HEAD_DIM

    def body(o_ref, zg_ref, nw_ref, dy_ref, do_ref, dzg_ref, dnw_ref):
        @pl.when(pl.program_id(0) == 0)
        def _():
            dnw_ref[...] = jnp.zeros_like(dnw_ref)

        for h in range(DN_HEADS):
            hs = slice(h * hd, (h + 1) * hd)
            _, vjp = jax.vjp(_dn_out, o_ref[:, hs], zg_ref[:, hs], nw_ref[...])
            do, dzg, dnw = vjp(dy_ref[:, hs])
            do_ref[:, hs] = do
            dzg_ref[:, hs] = dzg
            dnw_ref[...] += dnw

    wide = pl.BlockSpec((tm, DN_WIDTH), lambda i: (i, 0))
    wsd = _sds((s, DN_WIDTH), F32)
    return _call(body, name, (s // tm,),
                 [wide, pl.BlockSpec((tm, DN_WIDTH), lambda i: (i, Z_ZG // DN_WIDTH)), pl.BlockSpec((1, hd), lambda i: (0, 0)),
                  pl.BlockSpec((tm, DN_WIDTH), lambda i: (i, ATTN_WIDTH // DN_WIDTH))],
                 [wide, wide, pl.BlockSpec((1, hd), lambda i: (0, 0))], [wsd, wsd, _sds((1, hd), F32)])(o, z, nw, dycat)


def _s5_param_fn(are, aim, ldt, bre, bim):
    dt = jnp.exp(ldt)
    er = jnp.exp(are * dt)
    abr = er * jnp.cos(aim * dt)
    abi = er * jnp.sin(aim * dt)
    den = are * are + aim * aim
    cr = ((abr - 1.0) * are + abi * aim) / den
    ci = (abi * are - (abr - 1.0) * aim) / den
    return abr, abi, cr * bre - ci * bim, cr * bim + ci * bre


def s5_params_fwd(are, aim, ldt, bre, bim, name):
    p, hh = bre.shape

    def body(a_ref, b_ref, c_ref, d_ref, e_ref, o1, o2, o3, o4):
        o1[...], o2[...], o3[...], o4[...] = _s5_param_fn(a_ref[...], b_ref[...], c_ref[...], d_ref[...], e_ref[...])

    col = pl.BlockSpec((p, 1), lambda: (0, 0))
    mat = pl.BlockSpec((p, hh), lambda: (0, 0))
    return _call(body, name, (), [col, col, col, mat, mat], [col, col, mat, mat],
                 [_sds((p, 1), F32), _sds((p, 1), F32), _sds((p, hh), F32), _sds((p, hh), F32)])(are, aim, ldt, bre, bim)


def s5_params_bwd(are, aim, ldt, bre, bim, cts, name):
    p, hh = bre.shape

    def body(a_ref, b_ref, c_ref, d_ref, e_ref, g1, g2, g3, g4, o1, o2, o3, o4, o5):
        _, vjp = jax.vjp(_s5_param_fn, a_ref[...], b_ref[...], c_ref[...], d_ref[...], e_ref[...])
        o1[...], o2[...], o3[...], o4[...], o5[...] = vjp((g1[...], g2[...], g3[...], g4[...]))

    col = pl.BlockSpec((p, 1), lambda: (0, 0))
    mat = pl.BlockSpec((p, hh), lambda: (0, 0))
    csd, msd = _sds((p, 1), F32), _sds((p, hh), F32)
    return _call(body, name, (), [col, col, col, mat, mat, col, col, mat, mat], [col, col, col, mat, mat],
                 [csd, csd, csd, msd, msd])(are, aim, ldt, bre, bim, *cts)


def _cmul(ar, ai, br, bi):
    return ar * br - ai * bi, ar * bi + ai * br


S5_ROWS = 8
S5_UNROLL = 4


def _s5_tile_scan(xr, xi, ar, ai, reverse):
    row = lax.broadcasted_iota(jnp.int32, xr.shape, 0)
    d = 1
    while d < S5_ROWS:
        if reverse:
            sr, si = _shift_up(xr, d, row, S5_ROWS), _shift_up(xi, d, row, S5_ROWS)
        else:
            sr, si = _shift_down(xr, d, row), _shift_down(xi, d, row)
        pr, pi = _cmul(ar, ai, sr, si)
        xr, xi = xr + pr, xi + pi
        ar, ai = _cmul(ar, ai, ar, ai)
        d *= 2
    return xr, xi


def _s5_carry_powers(ar, ai, width, reverse):
    row = lax.broadcasted_iota(jnp.int32, (S5_ROWS, width), 0)
    at = row == (S5_ROWS - 1 if reverse else 0)
    return _s5_tile_scan(jnp.where(at, ar, 0.0), jnp.where(at, ai, 0.0), ar, ai, reverse)


def s5_scan_fwd(bu, abr, abi, name):
    s = bu.shape[0]
    wd = 256
    npb = S5_P // wd
    step = S5_ROWS * S5_UNROLL
    assert s % step == 0

    def body(br_ref, bi_ref, ar_ref, ai_ref, x_ref):
        ar, ai = ar_ref[...], ai_ref[...]
        pwr, pwi = _s5_carry_powers(ar, ai, wd, False)

        def loop(i, carry):
            cr, ci = carry
            base = pl.multiple_of(i * step, step)
            tiles = []
            for t in range(S5_UNROLL):
                rows = pl.ds(pl.multiple_of(base + t * S5_ROWS, S5_ROWS), S5_ROWS)
                tiles.append(_s5_tile_scan(br_ref[rows, :], bi_ref[rows, :], ar, ai, False))
            for t in range(S5_UNROLL):
                rows = pl.ds(pl.multiple_of(base + t * S5_ROWS, S5_ROWS), S5_ROWS)
                tr, ti = _cmul(pwr, pwi, cr, ci)
                xr, xi = tiles[t][0] + tr, tiles[t][1] + ti
                x_ref[0, rows, :] = xr
                x_ref[1, rows, :] = xi
                cr, ci = xr[S5_ROWS - 1:S5_ROWS, :], xi[S5_ROWS - 1:S5_ROWS, :]
            return cr, ci

        zero = jnp.zeros((1, wd), F32)
        lax.fori_loop(0, s // step, loop, (zero, zero))

    re = pl.BlockSpec((s, wd), lambda j: (0, j))
    im = pl.BlockSpec((s, wd), lambda j: (0, npb + j))
    av = pl.BlockSpec((1, wd), lambda j: (0, j))
    return _call(body, name, (npb,), [re, im, av, av], pl.BlockSpec((2, s, wd), lambda j: (0, 0, j)),
                 _sds((2, s, S5_P), F32), sem=("parallel",))(bu, bu, abr, abi)


def s5_scan_bwd(dx, x, abr, abi, name):
    s = dx.shape[0]
    wd = 128
    npb = S5_P // wd
    step = S5_ROWS * S5_UNROLL
    nsteps = s // step

    def body(dr_ref, di_ref, x_ref, ar_ref, ai_ref, g_ref, dar_ref, dai_ref):
        ar, ai = ar_ref[...], -ai_ref[...]
        pwr, pwi = _s5_carry_powers(ar, ai, wd, True)
        row = lax.broadcasted_iota(jnp.int32, (S5_ROWS, wd), 0)

        def loop(k, carry):
            cr, ci, accr, acci = carry
            base = pl.multiple_of((nsteps - 1 - k) * step, step)
            tiles = [None] * S5_UNROLL
            for t in range(S5_UNROLL):
                rows = pl.ds(pl.multiple_of(base + t * S5_ROWS, S5_ROWS), S5_ROWS)
                tiles[t] = _s5_tile_scan(dr_ref[rows, :], di_ref[rows, :], ar, ai, True)
            before = pl.ds(pl.multiple_of(jnp.maximum(base - S5_ROWS, 0), S5_ROWS), S5_ROWS)
            lr = jnp.where(base > 0, x_ref[0, before, :][S5_ROWS - 1:S5_ROWS, :], 0.0)
            li = jnp.where(base > 0, x_ref[1, before, :][S5_ROWS - 1:S5_ROWS, :], 0.0)
            prev = []
            for t in range(S5_UNROLL):
                rows = pl.ds(pl.multiple_of(base + t * S5_ROWS, S5_ROWS), S5_ROWS)
                xr, xi = x_ref[0, rows, :], x_ref[1, rows, :]
                prev.append((jnp.where(row >= 1, pltpu.roll(xr, 1, 0), lr), jnp.where(row >= 1, pltpu.roll(xi, 1, 0), li)))
                lr, li = xr[S5_ROWS - 1:S5_ROWS, :], xi[S5_ROWS - 1:S5_ROWS, :]
            for t in reversed(range(S5_UNROLL)):
                rows = pl.ds(pl.multiple_of(base + t * S5_ROWS, S5_ROWS), S5_ROWS)
                tr, ti = _cmul(pwr, pwi, cr, ci)
                gr, gi = tiles[t][0] + tr, tiles[t][1] + ti
                g_ref[0, rows, :] = gr
                g_ref[1, rows, :] = gi
                pr, pi = prev[t]
                accr = accr + (gr * pr + gi * pi)
                acci = acci + (gi * pr - gr * pi)
                cr, ci = gr[0:1, :], gi[0:1, :]
            return cr, ci, accr, acci

        zero = jnp.zeros((1, wd), F32)
        zacc = jnp.zeros((S5_ROWS, wd), F32)
        _, _, accr, acci = lax.fori_loop(0, nsteps, loop, (zero, zero, zacc, zacc))
        dar_ref[...] = jnp.sum(accr, axis=0, keepdims=True)
        dai_ref[...] = jnp.sum(acci, axis=0, keepdims=True)

    re = pl.BlockSpec((s, wd), lambda j: (0, j))
    im = pl.BlockSpec((s, wd), lambda j: (0, npb + j))
    av = pl.BlockSpec((1, wd), lambda j: (0, j))
    planes = pl.BlockSpec((2, s, wd), lambda j: (0, 0, j))
    asd = _sds((1, S5_P), F32)
    return _call(body, name, (npb,), [re, im, planes, av, av], [planes, av, av], [_sds((2, s, S5_P), F32), asd, asd],
                 sem=("parallel",))(dx, dx, x, abr, abi)


def _gelu(y):
    return 0.5 * y * (1.0 + jnp.tanh(math.sqrt(2.0 / math.pi) * (y + 0.044715 * y * y * y)))


def s5_out_fwd(ypre, z, dvec, glu_w, glu_b, name):
    s = ypre.shape[0]
    tm = _tile(s, 512)
    wd = S5_WIDTH

    def body(yp_ref, u_ref, d_ref, w_ref, b_ref, y_ref, o_ref):
        y = yp_ref[...] + d_ref[...] * u_ref[...]
        y_ref[...] = y
        g = _gelu(y)
        t = _dot(g.astype(BF16), w_ref[...]) + b_ref[...]
        o_ref[...] = (g * _sigmoid(t)).astype(BF16)

    row = pl.BlockSpec((tm, wd), lambda i: (i, 0))
    vec = pl.BlockSpec((1, wd), lambda i: (0, 0))
    return _call(body, name, (s // tm,),
                 [row, pl.BlockSpec((tm, wd), lambda i: (i, Z_S5 // wd)), vec, pl.BlockSpec((wd, wd), lambda i: (0, 0)), vec],
                 [row, row], [_sds((s, wd), F32), _sds((s, wd), BF16)], sem=("parallel",))(ypre, z, dvec, glu_w, glu_b)


def s5_out_bwd(y, z, glu_w, glu_b, dycat, name):
    s = y.shape[0]
    tm = _tile(s, 512)
    wd = S5_WIDTH

    def body(y_ref, u_ref, w_ref, b_ref, do_ref, dy_ref, dd_ref, dw_ref, db_ref):
        @pl.when(pl.program_id(0) == 0)
        def _():
            dd_ref[...] = jnp.zeros_like(dd_ref)
            dw_ref[...] = jnp.zeros_like(dw_ref)
            db_ref[...] = jnp.zeros_like(db_ref)

        g, gvjp = jax.vjp(_gelu, y_ref[...])
        gb = g.astype(BF16)
        sg = _sigmoid(_dot(gb, w_ref[...]) + b_ref[...])
        do = do_ref[...]
        dt = do * g * sg * (1.0 - sg)
        dtb = dt.astype(BF16)
        dg = do * sg + _dot(dtb, w_ref[...], NT)
        (dy,) = gvjp(dg)
        dy_ref[...] = dy
        dd_ref[...] += jnp.sum(dy * u_ref[...], axis=0, keepdims=True)
        dw_ref[...] += _dot(gb, dtb, TN)
        db_ref[...] += jnp.sum(dt, axis=0, keepdims=True)

    row = pl.BlockSpec((tm, wd), lambda i: (i, 0))
    vec = pl.BlockSpec((1, wd), lambda i: (0, 0))
    mat = pl.BlockSpec((wd, wd), lambda i: (0, 0))
    return _call(body, name, (s // tm,),
                 [row, pl.BlockSpec((tm, wd), lambda i: (i, Z_S5 // wd)), mat, vec,
                  pl.BlockSpec((tm, wd), lambda i: (i, (ATTN_WIDTH + DN_WIDTH) // wd))],
                 [row, vec, mat, vec], [_sds((s, wd), F32), _sds((1, wd), F32), _sds((wd, wd), F32), _sds((1, wd), F32)])(
        y, z, glu_w, glu_b, dycat)


def assemble_dz(dq, dkc, dkp, dvc, dvp, ddn, dzg, dus, dys, dvec, dzs, name):
    s = dq.shape[0]
    w = WINDOW
    nb = s // w
    nxt = lambda i: jnp.minimum(i + 1, nb - 1)

    def body(dq_ref, dkc_ref, dkp_ref, dvc_ref, dvp_ref, ddn_ref, dzg_ref, dus_ref, dys_ref, dv_ref, dzs_ref, o_ref):
        live = (pl.program_id(0) < nb - 1).astype(F32)
        o_ref[:, Z_Q:Z_K] = dq_ref[...].astype(BF16)
        o_ref[:, Z_K:Z_V] = (dkc_ref[...] + live * dkp_ref[...]).astype(BF16)
        o_ref[:, Z_V:Z_DN] = (dvc_ref[...] + live * dvp_ref[...]).astype(BF16)
        o_ref[:, Z_DN:Z_ZG] = ddn_ref[...].astype(BF16)
        o_ref[:, Z_ZG:Z_S5] = dzg_ref[...].astype(BF16)
        o_ref[:, Z_S5:Z_SM] = (dus_ref[...] + dv_ref[...] * dys_ref[...]).astype(BF16)
        o_ref[:, Z_SM:Z_ALL] = dzs_ref[...].astype(BF16)

    def blk(width, f=lambda i: i):
        return pl.BlockSpec((w, width), lambda i: (f(i), 0))

    return _call(body, name, (nb,),
                 [blk(ATTN_WIDTH), blk(ATTN_KV_WIDTH), blk(ATTN_KV_WIDTH, nxt), blk(ATTN_KV_WIDTH), blk(ATTN_KV_WIDTH, nxt),
                  blk(3 * DN_WIDTH), blk(DN_WIDTH), blk(S5_WIDTH), blk(S5_WIDTH), pl.BlockSpec((1, S5_WIDTH), lambda i: (0, 0)),
                  blk(128)],
                 blk(Z_ALL), _sds((s, Z_ALL), BF16), sem=("parallel",))(dq, dkc, dkp, dvc, dvp, ddn, dzg, dus, dys, dvec, dzs)


def _my_place():
    return lax.axis_index("x"), lax.axis_index("y"), lax.axis_index("c")


def _peer(place, p):
    x, y, c = place
    px = 1 - x if p & 4 else x
    py = 1 - y if p & 2 else y
    pc = 1 - c if p & 1 else c
    return (px, py, pc), 4 * px + 2 * py + pc


def exchange(arrays, scatter, name):
    na = len(arrays)

    def body(*refs):
        srcs, dsts = refs[:na], refs[na:2 * na]
        send_sems, recv_sems, local_sems = refs[2 * na:]
        place = _my_place()
        me = 4 * place[0] + 2 * place[1] + place[2]
        copies = []
        for k in range(na):
            mine = srcs[k].at[me] if scatter else srcs[k]
            loc = pltpu.make_async_copy(mine, dsts[k].at[me], local_sems.at[k])
            loc.start()
            copies.append(loc)
        sends = []
        for p in range(1, N_DEV):
            peer, pid = _peer(place, p)
            for k in range(na):
                src = srcs[k].at[pid] if scatter else srcs[k]
                cp = pltpu.make_async_remote_copy(src_ref=src, dst_ref=dsts[k].at[me], send_sem=send_sems.at[k, p - 1],
                                                  recv_sem=recv_sems.at[k, p - 1], device_id=peer,
                                                  device_id_type=pl.DeviceIdType.MESH)
                cp.start()
                sends.append(cp)
        for p in range(1, N_DEV):
            peer, pid = _peer(place, p)
            for k in range(na):
                src = srcs[k].at[me] if scatter else srcs[k]
                pltpu.make_async_remote_copy(src_ref=src, dst_ref=dsts[k].at[pid], send_sem=send_sems.at[k, p - 1],
                                             recv_sem=recv_sems.at[k, p - 1], device_id=peer,
                                             device_id_type=pl.DeviceIdType.MESH).wait_recv()
        for cp in sends:
            cp.wait_send()
        for cp in copies:
            cp.wait()

    outs = [_sds((N_DEV,) + tuple(a.shape[1:] if scatter else a.shape), a.dtype) for a in arrays]
    anyspec = pl.BlockSpec(memory_space=pl.ANY)
    return pl.pallas_call(
        body, name=name, in_specs=[anyspec] * na, out_specs=[anyspec] * na, out_shape=outs,
        scratch_shapes=[pltpu.SemaphoreType.DMA((na, N_DEV - 1)), pltpu.SemaphoreType.DMA((na, N_DEV - 1)),
                        pltpu.SemaphoreType.DMA((na,))])(*arrays)


_HBM = pl.BlockSpec(memory_space=pltpu.HBM)
_SEM = pl.BlockSpec(memory_space=pltpu.SEMAPHORE)
_DATAFLOW = pltpu.SideEffectType.DATAFLOW_SIDE_EFFECTING


def _split_copies(srcs, lands, send_sems, recv_sems, scatter, arriving):
    place = _my_place()
    me = 4 * place[0] + 2 * place[1] + place[2]
    out = []
    for p in range(1, N_DEV):
        peer, pid = _peer(place, p)
        for k in range(len(srcs)):
            i = k * (N_DEV - 1) + p - 1
            src = srcs[k].at[pid] if scatter else srcs[k]
            dst = lands[k].at[pid] if arriving else lands[k].at[me]
            out.append(pltpu.make_async_remote_copy(src_ref=src, dst_ref=dst, send_sem=send_sems.at[i], recv_sem=recv_sems.at[i],
                                                    device_id=peer, device_id_type=pl.DeviceIdType.MESH))
    return out


def exchange_start(groups, scatter, name):
    arrays = [a for g in groups for a in g]
    na, ng = len(arrays), len(groups)
    first = [sum(len(g) for g in groups[:i]) for i in range(ng)]
    me = 4 * lax.axis_index("x") + 2 * lax.axis_index("y") + lax.axis_index("c")
    lands = []
    for a in arrays:
        own = lax.dynamic_index_in_dim(a, me, 0, keepdims=True) if scatter else a[None]
        shape = (N_DEV,) + tuple(own.shape[1:])
        land = lax.dynamic_update_slice(lax.empty(shape, a.dtype), own, (me,) + (0,) * (len(shape) - 1))
        lands.append(pltpu.with_memory_space_constraint(land, pltpu.HBM))
    srcs = [pltpu.with_memory_space_constraint(a, pltpu.HBM) for a in arrays]

    def body(*refs):
        src_refs, land_refs = refs[:na], refs[na:2 * na]
        sems = refs[2 * na:2 * na + 2 * ng]
        token = refs[-1]
        for i, g in enumerate(groups):
            sl = slice(first[i], first[i] + len(g))
            for send in _split_copies(src_refs[sl], land_refs[sl], sems[2 * i], sems[2 * i + 1], scatter, False):
                send.start()
        token[...] = jnp.zeros_like(token)

    sem_shapes = []
    for g in groups:
        sem_shapes += [pltpu.SemaphoreType.DMA((len(g) * (N_DEV - 1),))] * 2
    outs = pl.pallas_call(
        body, name=name,
        out_shape=(*sem_shapes, *[pltpu.HBM(a.shape, a.dtype) for a in srcs], *[pltpu.HBM(a.shape, a.dtype) for a in lands],
                   _sds((8, 128), F32)),
        in_specs=[_HBM] * (2 * na), out_specs=(*[_SEM] * (2 * ng), *[_HBM] * (2 * na), pl.BlockSpec(memory_space=pltpu.VMEM)),
        input_output_aliases={i: 2 * ng + i for i in range(2 * na)},
        compiler_params=pltpu.CompilerParams(has_side_effects=_DATAFLOW))(*srcs, *lands)
    src_out, land_out = outs[2 * ng:2 * ng + na], outs[2 * ng + na:2 * ng + 2 * na]
    handles = [(outs[2 * i], outs[2 * i + 1], src_out[first[i]:first[i] + len(g)], land_out[first[i]:first[i] + len(g)])
               for i, g in enumerate(groups)]
    return handles, outs[-1]


def exchange_wait(handle, scatter, name, after):
    send_sems, recv_sems, srcs, lands = handle
    na = len(srcs)

    def body(*refs):
        src_refs, land_refs = refs[:na], refs[na:2 * na]
        for send in _split_copies(src_refs, land_refs, refs[2 * na], refs[2 * na + 1], scatter, False):
            send.wait_send()
        for recv in _split_copies(src_refs, land_refs, refs[2 * na], refs[2 * na + 1], scatter, True):
            recv.wait_recv()

    outs = pl.pallas_call(
        body, name=name, out_shape=tuple(pltpu.HBM(a.shape, a.dtype) for a in (*srcs, *lands)),
        in_specs=[_HBM] * (2 * na) + [_SEM, _SEM, pl.BlockSpec(memory_space=pl.ANY)], out_specs=tuple([_HBM] * (2 * na)),
        input_output_aliases={i: i for i in range(2 * na)},
        compiler_params=pltpu.CompilerParams(has_side_effects=_DATAFLOW))(*srcs, *lands, send_sems, recv_sems, after)
    return list(outs[na:])


def _adamw(w, g, m, v):
    m = ADAM_B1 * m + (1.0 - ADAM_B1) * g
    v = ADAM_B2 * v + (1.0 - ADAM_B2) * (g * g)
    m_hat = m / (1.0 - ADAM_B1 ** ADAM_STEP)
    v_hat = v / (1.0 - ADAM_B2 ** ADAM_STEP)
    return -ADAM_LR * (m_hat / (jnp.sqrt(v_hat) + ADAM_EPS) + ADAM_WD * w), m, v


def reduce_adamw(parts, w4, m4, v4, layer, name, stacked=None):
    nl, r, c = w4.shape
    tr = _tile(r, 256)

    def body(p_ref, w_ref, m_ref, v_ref, *rest):
        g_ref, d_ref, nm_ref, nv_ref = rest[-4:]
        g = p_ref[0].astype(F32)
        for d in range(1, N_DEV):
            g = g + p_ref[d].astype(F32)
        g_ref[...] = g
        d_ref[...], nm_ref[...], nv_ref[...] = _adamw(w_ref[...], g, m_ref[...], v_ref[...])

    lay = pl.BlockSpec((None, tr, c), lambda i: (layer, i, 0))
    osd = _sds((nl, r, c), F32)
    ins = [parts, w4, m4, v4] + (list(stacked) if stacked is not None else [])
    specs = [pl.BlockSpec((N_DEV, tr, c), lambda i: (0, i, 0)), lay, lay, lay]
    specs += [pl.BlockSpec(memory_space=pl.ANY)] * (len(ins) - 4)
    return pl.pallas_call(
        body, name=name, grid=(r // tr,), in_specs=specs, out_specs=[lay, lay, lay, lay], out_shape=[osd, osd, osd, osd],
        input_output_aliases={4 + k: k for k in range(len(ins) - 4)},
        compiler_params=pltpu.CompilerParams(dimension_semantics=("parallel",), vmem_limit_bytes=VMEM_LIMIT))(*ins)


_SM_NAT = ATTN_WIDTH + 2 * ATTN_KV_WIDTH + 4 * DN_WIDTH


def _win_to_zall(w):
    pad = jnp.zeros((w.shape[0], 128 - 2 * DN_HEADS), w.dtype)
    return jnp.concatenate([w[:, :_SM_NAT], w[:, _SM_NAT + 2 * DN_HEADS:], w[:, _SM_NAT:_SM_NAT + 2 * DN_HEADS], pad], axis=1)


def _zall_to_win(g):
    return jnp.concatenate([g[:, :Z_ZG + DN_WIDTH], g[:, Z_SM:Z_SM + 2 * DN_HEADS], g[:, Z_S5:Z_SM]], axis=1)


def _block_diag(t):
    g, a, b = t.shape
    eye = jnp.eye(g, dtype=t.dtype)
    return (t[:, :, None, :] * eye[:, None, :, None]).reshape(g * a, g * b)


def _block_diag_extract(m, g):
    a, b = m.shape[0] // g, m.shape[1] // g
    eye = jnp.eye(g, dtype=m.dtype)
    return jnp.sum(m.reshape(g, a, g, b) * eye[:, None, :, None], axis=2)


def _rope_tables(s):
    half = HEAD_DIM // 2
    inv_freq = ROPE_THETA ** (-jnp.arange(half, dtype=F32) / half)
    ang = jnp.arange(s, dtype=F32)[:, None] * inv_freq[None, :]
    cos, sin = jnp.cos(ang), jnp.sin(ang)
    return jnp.concatenate([cos, cos], axis=1), jnp.concatenate([-sin, sin], axis=1)


def _row(v):
    return v.reshape(1, -1)


def _ffn_fwd(x, g_pre, g_post, weight, tag):
    h = rmsnorm_fwd(x, g_pre, f"{tag}_norm")
    wg, wu = weight(f"{tag}_w_gate", h), weight(f"{tag}_w_up", h)
    a, b, u = ffn_up(h, wg, wu, f"{tag}_up", dep=weight("token", None))
    wd = weight(f"{tag}_w_down", u)
    y, xn = down_norm(u, wd, x, g_post, FFN_RES_WEIGHT, f"{tag}_down")
    return xn, (x, h, a, b, u, y, wg, wu, wd)


def _ffn_bwd(dxn, saved, g_pre, g_post, on_grads, tag):
    x, h, a, b, u, y, wg, wu, wd = saved
    dy, dg_post = norm_bwd(dxn, y, g_post, FFN_RES_WEIGHT, None, BF16, f"{tag}_bnorm_post")
    da, db = ffn_down_bwd(dy, wd, a, b, f"{tag}_bdown")
    dwd = mm_tn(u, dy[None], BF16, f"{tag}_dwd", tk=2048)
    dwg = mm_tn(h[None], da, BF16, f"{tag}_dwg", tk=2048)
    dwu = mm_tn(h[None], db, BF16, f"{tag}_dwu", tk=2048)
    tok = on_grads({f"{tag}_w_gate": dwg, f"{tag}_w_up": dwu, f"{tag}_w_down": dwd})
    dh = mm_nt_acc([(da, wg), (db, wu)], f"{tag}_dh", tm=512, tn=2048, dep=tok)
    dx, dg_pre = norm_bwd(dh, x, g_pre, 1.0, dxn, F32, f"{tag}_bnorm_pre")
    return dx, dict(g_pre=dg_pre, g_post=dg_post)


def _s5_layouts(p):
    are, aim = p["s5_a_re"].reshape(S5_P, 1), p["s5_a_im"].reshape(S5_P, 1)
    ldt = jnp.repeat(p["s5_log_dt"], S5_STATE).reshape(S5_P, 1)
    bre, bim = p["s5_b_re"].reshape(S5_P, S5_GROUP_CH), p["s5_b_im"].reshape(S5_P, S5_GROUP_CH)
    return are, aim, ldt, bre, bim


def _mix_fwd(x, p, weight, cos_f, sin_s, tag):
    h = rmsnorm_fwd(x, _row(p["mix_norm_pre"]), f"{tag}_norm")
    w_all, glu_w = weight("w_all", h), weight("s5_glu_w", h)
    z = mm_nn(h, w_all, f"{tag}_win")
    y_attn = attn_fwd(z, cos_f, sin_s, p["attn_sinks"], f"{tag}_attn")
    alog_b = jnp.broadcast_to(p["dn_a_log"][:, None], (DN_HEADS, 128))
    dtb_b = jnp.broadcast_to(p["dn_dt_bias"][:, None], (DN_HEADS, 128))
    conv_w = weight("dn_conv_w", h)
    qkv = dn_pre_fwd(z, conv_w, f"{tag}_dnpre")
    bb, gb = dn_gates_fwd(z, alog_b, dtb_b, f"{tag}_dngate")
    u, w, qd, kd, at, eg = dn_intra_fwd(qkv, gb, bb, f"{tag}_dnintra")
    o, states = dn_scan_fwd(u, w, qd, kd, at, eg, f"{tag}_dnscan")
    y_dn = dn_out_fwd(o, z, _row(p["dn_norm_w"]), f"{tag}_dnout")
    s5cols = _s5_layouts(p)
    abr, abi, bbr, bbi = s5_params_fwd(*s5cols, f"{tag}_s5par")
    tb = lambda t: jnp.transpose(t.reshape(S5_GROUPS, S5_STATE, S5_GROUP_CH), (0, 2, 1))
    b_blk = jnp.concatenate([_block_diag(tb(bbr)), _block_diag(tb(bbi))], axis=1).astype(BF16)
    tc = lambda t: jnp.transpose(t, (0, 2, 1))
    c_blk = jnp.concatenate([_block_diag(tc(p["s5_c_re"])), -_block_diag(tc(p["s5_c_im"]))], axis=0).astype(BF16)
    bu = mm_nn(z, b_blk, f"{tag}_s5bu", tn=1024, col0=Z_S5)
    xs = s5_scan_fwd(bu, abr.reshape(1, S5_P), abi.reshape(1, S5_P), f"{tag}_s5scan")
    ypre = mm_nn_acc(xs, c_blk.reshape(2, S5_P, S5_WIDTH), f"{tag}_s5c")
    y5, y_s5 = s5_out_fwd(ypre, z, _row(p["s5_d"]), glu_w, _row(p["s5_glu_b"]), f"{tag}_s5out")
    ycat = jnp.concatenate([y_attn, y_dn, y_s5], axis=1)
    w_out = weight("w_out", ycat)
    mixed, xn = down_norm(ycat[None], w_out[None], x, _row(p["mix_norm_post"]), 1.0, f"{tag}_wout")
    saved = dict(x=x, h=h, z=z, qkv=qkv, bb=bb, gb=gb, dn=(u, w, qd, kd, at, eg), states=states, o=o, s5cols=s5cols,
                 abr=abr, abi=abi, b_blk=b_blk, c_blk=c_blk, xs=xs, y5=y5, ycat=ycat, mixed=mixed,
                 alog_b=alog_b, dtb_b=dtb_b, w_all=w_all, w_out=w_out, glu_w=glu_w, conv_w=conv_w)
    return xn, saved


def _mix_bwd(dxn, sv, p, on_grads, cos_f, sin_s, tag):
    z = sv["z"]
    w_all, w_out, glu_w = sv["w_all"], sv["w_out"], sv["glu_w"]
    g = {}
    dmixed, g["mix_norm_post"] = norm_bwd(dxn, sv["mixed"], _row(p["mix_norm_post"]), 1.0, None, BF16, f"{tag}_bnorm_post")
    g["w_out"] = mm_tn(sv["ycat"][None], dmixed[None], BF16, f"{tag}_dwout", tn=1024)[0]
    dycat = mm_nt_acc([(dmixed[None], w_out[None])], f"{tag}_dycat")
    dq, dkc, dkp, dvc, dvp, dsink = attn_bwd(z, cos_f, sin_s, p["attn_sinks"], dycat, f"{tag}_battn")
    g["attn_sinks"] = dsink[:, 0]
    do, dzg, dnw = dn_out_bwd(sv["o"], z, _row(p["dn_norm_w"]), dycat, f"{tag}_bdnout")
    g["dn_norm_w"] = dnw[0]
    cts = dn_scan_bwd(*sv["dn"], sv["states"], do, f"{tag}_bdnscan")
    dqn, dkn, dvn, dgb, dbb = dn_intra_bwd(sv["qkv"], sv["gb"], sv["bb"], cts, f"{tag}_bdnintra")
    dzs, dal, ddt = dn_gates_bwd(z, sv["alog_b"], sv["dtb_b"], dbb, dgb, f"{tag}_bdngate")
    g["dn_a_log"], g["dn_dt_bias"] = dal[:, 0], ddt[:, 0]
    ddn, g["dn_conv_w"] = dn_pre_bwd(z, sv["conv_w"], jnp.concatenate([dqn, dkn, dvn], axis=1), f"{tag}_bdnpre")
    dy5, dd, dglu, dglub = s5_out_bwd(sv["y5"], z, glu_w, _row(p["s5_glu_b"]), dycat, f"{tag}_bs5out")
    g["s5_d"], g["s5_glu_w"], g["s5_glu_b"] = dd[0], dglu, dglub[0]
    dxs = mm_nt_acc([(dy5[None], sv["c_blk"][None])], f"{tag}_bs5c", tn=1024)
    dc_blk = mm_tn(sv["xs"], dy5[None], F32, f"{tag}_ds5c", tk=256)
    ex = lambda m: jnp.transpose(_block_diag_extract(m, S5_GROUPS), (0, 2, 1))
    g["s5_c_re"], g["s5_c_im"] = ex(dc_blk[0]), -ex(dc_blk[1])
    dbu, dar, dai = s5_scan_bwd(dxs, sv["xs"], sv["abr"].reshape(1, S5_P), sv["abi"].reshape(1, S5_P), f"{tag}_bs5scan")
    b_planes = jnp.transpose(sv["b_blk"].reshape(S5_WIDTH, 2, S5_P), (1, 0, 2))
    dus = mm_nt_acc([(dbu, b_planes)], f"{tag}_bs5bu")
    u_s5 = z[:, Z_S5:Z_SM]
    db_blk = mm_tn(u_s5[None], dbu, F32, f"{tag}_ds5b", tn=1024)
    exb = lambda m: jnp.transpose(_block_diag_extract(m, S5_GROUPS), (0, 2, 1)).reshape(S5_P, S5_GROUP_CH)
    dcols = s5_params_bwd(*sv["s5cols"], (dar.reshape(S5_P, 1), dai.reshape(S5_P, 1), exb(db_blk[0]), exb(db_blk[1])),
                          f"{tag}_bs5par")
    g["s5_a_re"] = dcols[0].reshape(S5_GROUPS, S5_STATE)
    g["s5_a_im"] = dcols[1].reshape(S5_GROUPS, S5_STATE)
    g["s5_log_dt"] = jnp.sum(dcols[2].reshape(S5_GROUPS, S5_STATE), axis=1)
    g["s5_b_re"] = dcols[3].reshape(S5_GROUPS, S5_STATE, S5_GROUP_CH)
    g["s5_b_im"] = dcols[4].reshape(S5_GROUPS, S5_STATE, S5_GROUP_CH)
    dz = assemble_dz(dq, dkc, dkp, dvc, dvp, ddn, dzg, dus, dy5, _row(p["s5_d"]), dzs, f"{tag}_dz")
    g["w_all"] = mm_tn(sv["h"][None], dz[None], BF16, f"{tag}_dwin")[0]
    dwin = _zall_to_win(g.pop("w_all"))
    d_model = dwin.shape[0]
    tok = on_grads({"w_in": jnp.transpose(dwin.reshape(d_model, N_DEV, IN_WIDTH // N_DEV), (1, 0, 2)),
                    "s5_glu_w": g.pop("s5_glu_w").astype(BF16).reshape(N_DEV, S5_WIDTH // N_DEV, S5_WIDTH),
                    "w_out": g.pop("w_out").reshape(N_DEV, MIX_WIDTH // N_DEV, d_model)})
    dh = mm_nt_acc([(dz[None], w_all[None])], f"{tag}_dh", dep=tok)
    dx, g["mix_norm_pre"] = norm_bwd(dh, sv["x"], _row(p["mix_norm_pre"]), 1.0, dxn, F32, f"{tag}_bnorm_pre")
    return dx, g


BIG = ("ff1_w_gate", "ff1_w_up", "ff1_w_down", "w_in", "s5_glu_w", "w_out", "ff2_w_gate", "ff2_w_up", "ff2_w_down")
SMALL = ("ff1_norm_pre", "ff1_norm_post", "mix_norm_pre", "attn_sinks", "dn_conv_w", "dn_a_log", "dn_dt_bias", "dn_norm_w",
         "s5_a_re", "s5_a_im", "s5_log_dt", "s5_b_re", "s5_b_im", "s5_c_re", "s5_c_im", "s5_d", "s5_glu_b",
         "mix_norm_post", "ff2_norm_pre", "ff2_norm_post")
GATHER_GROUPS = (("ff1_w_gate", "ff1_w_up"), ("ff1_w_down",), ("w_in", "s5_glu_w", "dn_conv_w"), ("w_out",),
                 ("ff2_w_gate", "ff2_w_up"), ("ff2_w_down",))
WEIGHTS = ("ff1_norm_pre", "ff1_w_gate", "ff1_w_up", "ff1_w_down", "ff1_norm_post", "mix_norm_pre", "w_in", "attn_sinks",
           "dn_conv_w", "dn_a_log", "dn_dt_bias", "dn_norm_w", "s5_a_re", "s5_a_im", "s5_log_dt", "s5_b_re", "s5_b_im",
           "s5_c_re", "s5_c_im", "s5_d", "s5_glu_w", "s5_glu_b", "w_out", "mix_norm_post", "ff2_norm_pre", "ff2_w_gate",
           "ff2_w_up", "ff2_w_down", "ff2_norm_post")


def _pack(parts):
    rows = []
    for a in parts:
        n = a.size
        r = -(-n // 1024) * 8
        rows.append(jnp.pad(a.reshape(-1), (0, r * 128 - n)).reshape(r, 128))
    return jnp.concatenate(rows, axis=0)


def _unpack(mat, shapes):
    out, off = [], 0
    for shp in shapes:
        n = int(np.prod(shp))
        r = -(-n // 1024) * 8
        out.append(mat[off:off + r].reshape(-1)[:n].reshape(shp))
        off += r
    return out


def local_step(x, target, smalls, weight, on_grads):
    depth = len(smalls)
    cos_f, sin_s = _rope_tables(x.shape[0])
    xs = x
    saved = []
    for l in range(depth):
        p = smalls[l]
        wl = functools.partial(weight, l)
        xs, s1 = _ffn_fwd(xs, _row(p["ff1_norm_pre"]), _row(p["ff1_norm_post"]), wl, "ff1")
        xs, s2 = _mix_fwd(xs, p, wl, cos_f, sin_s, "mix")
        xs, s3 = _ffn_fwd(xs, _row(p["ff2_norm_pre"]), _row(p["ff2_norm_post"]), wl, "ff2")
        saved.append((s1, s2, s3))

    loss_vec, dx = loss_and_grad(xs, target, "loss")

    small_g = [None] * depth
    for l in reversed(range(depth)):
        p = smalls[l]
        gl = functools.partial(on_grads, l)
        s1, s2, s3 = saved[l]
        dx, g3 = _ffn_bwd(dx, s3, _row(p["ff2_norm_pre"]), _row(p["ff2_norm_post"]), gl, "ff2")
        dx, g2 = _mix_bwd(dx, s2, p, gl, cos_f, sin_s, "mix")
        dx, g1 = _ffn_bwd(dx, s1, _row(p["ff1_norm_pre"]), _row(p["ff1_norm_post"]), gl, "ff1")
        sg = {n: g2[n] for n in SMALL if n in g2}
        sg.update(ff1_norm_pre=g1["g_pre"][0], ff1_norm_post=g1["g_post"][0], ff2_norm_pre=g3["g_pre"][0], ff2_norm_post=g3["g_post"][0],
                  mix_norm_pre=g2["mix_norm_pre"][0], mix_norm_post=g2["mix_norm_post"][0])
        small_g[l] = sg
    return loss_vec, dx, small_g


def kernel(x, ff1_norm_pre, ff1_w_gate, ff1_w_up, ff1_w_down, ff1_norm_post, mix_norm_pre, w_in, attn_sinks, dn_conv_w, dn_a_log, dn_dt_bias, dn_norm_w, s5_a_re, s5_a_im, s5_log_dt, s5_b_re, s5_b_im, s5_c_re, s5_c_im, s5_d, s5_glu_w, s5_glu_b, w_out, mix_norm_post, ff2_norm_pre, ff2_w_gate, ff2_w_up, ff2_w_down, ff2_norm_post, loss_target, m_ff1_norm_pre, m_ff1_w_gate, m_ff1_w_up, m_ff1_w_down, m_ff1_norm_post, m_mix_norm_pre, m_w_in, m_attn_sinks, m_dn_conv_w, m_dn_a_log, m_dn_dt_bias, m_dn_norm_w, m_s5_a_re, m_s5_a_im, m_s5_log_dt, m_s5_b_re, m_s5_b_im, m_s5_c_re, m_s5_c_im, m_s5_d, m_s5_glu_w, m_s5_glu_b, m_w_out, m_mix_norm_post, m_ff2_norm_pre, m_ff2_w_gate, m_ff2_w_up, m_ff2_w_down, m_ff2_norm_post, v_ff1_norm_pre, v_ff1_w_gate, v_ff1_w_up, v_ff1_w_down, v_ff1_norm_post, v_mix_norm_pre, v_w_in, v_attn_sinks, v_dn_conv_w, v_dn_a_log, v_dn_dt_bias, v_dn_norm_w, v_s5_a_re, v_s5_a_im, v_s5_log_dt, v_s5_b_re, v_s5_b_im, v_s5_c_re, v_s5_c_im, v_s5_d, v_s5_glu_w, v_s5_glu_b, v_w_out, v_mix_norm_post, v_ff2_norm_pre, v_ff2_w_gate, v_ff2_w_up, v_ff2_w_down, v_ff2_norm_post):
    args = dict(locals())
    W = {n: args[n] for n in WEIGHTS}
    M = {n: args["m_" + n] for n in WEIGHTS}
    V = {n: args["v_" + n] for n in WEIGHTS}
    depth = ff1_norm_pre.shape[0]
    d_model = x.shape[2]
    me = 4 * lax.axis_index("x") + 2 * lax.axis_index("y") + lax.axis_index("c")

    conv_sh = dn_conv_w.shape[2]

    def small_params(l):
        return {n: W[n][l] for n in SMALL if n != "dn_conv_w"}

    gathered_w, gather_handles, tokens = {}, {}, []
    group_of = {n: i for i, grp in enumerate(GATHER_GROUPS) for n in grp}

    def start_gather(l):
        shards = {n: cast_bf16(W[n], l, f"cast_{n}") for n in BIG}
        shards["dn_conv_w"] = dn_conv_w[l]
        handles, tok = exchange_start([[shards[n] for n in grp] for grp in GATHER_GROUPS], False, f"gather_start_l{l}")
        gather_handles.update({(l, i): h for i, h in enumerate(handles)})
        tokens.append(tok)

    def weight(l, name, after):
        if name == "token":
            return tokens.pop() if tokens else None
        key = "w_in" if name == "w_all" else name
        i = group_of[key]
        if (l, i) in gather_handles:
            got = dict(zip(GATHER_GROUPS[i], exchange_wait(gather_handles.pop((l, i)), False, f"gather_wait_l{l}_g{i}", after)))
            if i == 0 and l + 1 < depth:
                start_gather(l + 1)
            if "w_in" in got:
                got["dn_conv_w"] = jnp.transpose(got["dn_conv_w"], (1, 0, 2)).reshape(DN_CONV, N_DEV * conv_sh)
                got["w_all"] = _win_to_zall(jnp.transpose(got["w_in"], (1, 0, 2)).reshape(d_model, IN_WIDTH))
                got["s5_glu_w"] = got["s5_glu_w"].reshape(S5_WIDTH, S5_WIDTH)
            if "w_out" in got:
                got["w_out"] = got["w_out"].reshape(MIX_WIDTH, d_model)
            gathered_w.update({(l, n): a for n, a in got.items()})
        return gathered_w[(l, name)]

    stacked = {n: None for n in BIG}
    in_flight = []

    def finish_scatter(after):
        l, names, handle = in_flight.pop(0)
        recv = dict(zip(names, exchange_wait(handle, True, f"scatter_wait_l{l}_{names[0]}", after)))
        for n in names:
            stacked[n] = reduce_adamw(recv[n], W[n], M[n], V[n], l, f"adamw_{n}", stacked[n])

    def on_grads(l, grads):
        names = tuple(grads)
        (handle,), tok = exchange_start([[grads[n] for n in names]], True, f"scatter_start_l{l}_{names[0]}")
        in_flight.append((l, names, handle))
        if len(in_flight) > 1:
            finish_scatter(tok)
        return tok

    start_gather(0)
    tokens.clear()
    smalls = [small_params(l) for l in range(depth)]
    loss_vec, dx, small_g = local_step(x[0], loss_target[0], smalls, weight, on_grads)
    shapes = [(depth,) + ((DN_CONV, N_DEV * conv_sh) if n == "dn_conv_w" else tuple(W[n].shape[1:])) for n in SMALL]
    packed = _pack([jnp.stack([small_g[l][n] for l in range(depth)]) for n in SMALL])
    (small_handle,), small_tok = exchange_start([[packed]], False, "gather_small_start")
    while in_flight:
        finish_scatter(small_tok)
    gathered = exchange_wait(small_handle, False, "gather_small_wait", stacked[BIG[0]][0])[0]
    loss = lax.psum(loss_vec[0, 0], ("x", "y", "c"))

    def shard_of(n, full):
        return lax.dynamic_slice_in_dim(full, me * conv_sh, conv_sh, axis=2) if n == "dn_conv_w" else full

    conv_pad = lambda t: jnp.tile(t, (1, 1, N_DEV))
    wp = _pack([conv_pad(W[n]) if n == "dn_conv_w" else W[n] for n in SMALL])
    mp = _pack([conv_pad(M[n]) if n == "dn_conv_w" else M[n] for n in SMALL])
    vp = _pack([conv_pad(V[n]) if n == "dn_conv_w" else V[n] for n in SMALL])
    sm = reduce_adamw(gathered, wp[None], mp[None], vp[None], 0, "adamw_small")
    small_out = [dict(zip(SMALL, [shard_of(n, t) for n, t in zip(SMALL, _unpack(o[0], shapes))])) for o in sm]

    outs = []
    for kind in range(4):
        for n in WEIGHTS:
            if n in BIG:
                outs.append(stacked[n][kind])
            else:
                outs.append(small_out[kind][n])
    return (loss, dx[None], *outs)
```

```python
import functools
import math

import jax
import jax.numpy as jnp
import numpy as np
from jax import lax
from jax.experimental import pallas as pl
from jax.experimental.pallas import tpu as pltpu

F32 = jnp.float32
BF16 = jnp.bfloat16

N_DEV = 8
DEPTH = 4
ATTN_HEADS = 8
ATTN_KV_HEADS = 2
HEAD_DIM = 128
WINDOW = 128
ROPE_THETA = 10000.0
DN_HEADS = 4
DN_HEAD_DIM = 128
DN_CONV = 4
DN_CHUNK = 64
S5_GROUPS = 32
S5_GROUP_CH = 16
S5_STATE = 64
ATTN_WIDTH = ATTN_HEADS * HEAD_DIM
ATTN_KV_WIDTH = ATTN_KV_HEADS * HEAD_DIM
DN_WIDTH = DN_HEADS * DN_HEAD_DIM
S5_WIDTH = S5_GROUPS * S5_GROUP_CH
S5_P = S5_GROUPS * S5_STATE
MIX_WIDTH = ATTN_WIDTH + DN_WIDTH + S5_WIDTH
IN_WIDTH = ATTN_WIDTH + 2 * ATTN_KV_WIDTH + 4 * DN_WIDTH + 2 * DN_HEADS + S5_WIDTH
Z_Q, Z_K, Z_V = 0, ATTN_WIDTH, ATTN_WIDTH + ATTN_KV_WIDTH
Z_DN = ATTN_WIDTH + 2 * ATTN_KV_WIDTH
Z_ZG = Z_DN + 3 * DN_WIDTH
Z_S5 = Z_ZG + DN_WIDTH
Z_SM = Z_S5 + S5_WIDTH
Z_ALL = Z_SM + 128
FFN_RES_WEIGHT = 0.5
NORM_EPS = 1e-6
ADAM_LR, ADAM_B1, ADAM_B2, ADAM_EPS, ADAM_WD, ADAM_STEP = 0.001, 0.9, 0.999, 1e-08, 0.01, 10

VMEM_LIMIT = 56 * 1024 * 1024
NEG = -1e30

NN = (((1,), (0,)), ((), ()))
NT = (((1,), (1,)), ((), ()))
TN = (((0,), (0,)), ((), ()))


def _dot(a, b, dims=NN, prec=None):
    return lax.dot_general(a, b, dims, preferred_element_type=F32, precision=prec)


def _tile(n, pref, mult=8):
    if n <= pref:
        return n
    t = (pref // mult) * mult
    while t > mult and n % t:
        t -= mult
    assert n % t == 0, (n, pref)
    return t


def _call(body, name, grid, in_specs, out_specs, out_shape, scratch=(), sem=None):
    if sem is None:
        sem = ("arbitrary",) * len(grid)
    return pl.pallas_call(
        body, name=name, grid=grid, in_specs=in_specs, out_specs=out_specs, out_shape=out_shape,
        scratch_shapes=list(scratch),
        compiler_params=pltpu.CompilerParams(dimension_semantics=sem, vmem_limit_bytes=VMEM_LIMIT))


def _sds(shape, dtype):
    return jax.ShapeDtypeStruct(tuple(shape), dtype)


def _sigmoid(x):
    return 1.0 / (1.0 + jnp.exp(-x))


def _silu_and_grad(a):
    sg = _sigmoid(a)
    return a * sg, sg * (1.0 + a * (1.0 - sg))


def _softplus(x):
    return jnp.maximum(x, 0.0) + jnp.log(1.0 + jnp.exp(-jnp.abs(x)))


def _rms(x, g):
    r = lax.rsqrt(jnp.mean(x * x, axis=-1, keepdims=True) + NORM_EPS)
    return x * r * g


def _rms_bwd(dout, y, g):
    r = lax.rsqrt(jnp.mean(y * y, axis=-1, keepdims=True) + NORM_EPS)
    n = y * r
    dn = dout * g
    dy = r * (dn - n * jnp.mean(dn * n, axis=-1, keepdims=True))
    return dy, jnp.sum(dout * n, axis=0, keepdims=True)


def _rope(x, cos_f, sin_s):
    return x * cos_f + pltpu.roll(x, HEAD_DIM // 2, 1) * sin_s


def _rope_bwd(d, cos_f, sin_s):
    return d * cos_f + pltpu.roll(d * sin_s, HEAD_DIM // 2, 1)


def rmsnorm_fwd(x, g, name):
    s, d = x.shape
    tm = _tile(s, 512)

    def body(x_ref, g_ref, o_ref):
        o_ref[...] = _rms(x_ref[...], g_ref[...]).astype(BF16)

    return _call(body, name, (s // tm,),
                 [pl.BlockSpec((tm, d), lambda i: (i, 0)), pl.BlockSpec((1, d), lambda i: (0, 0))],
                 pl.BlockSpec((tm, d), lambda i: (i, 0)), _sds((s, d), BF16), sem=("parallel",))(x, g)


def norm_bwd(dout, y, g, scale, resid, out_dtype, name):
    s, d = y.shape
    tm = _tile(s, 256)
    has_res = resid is not None

    def body(*refs):
        if has_res:
            do_ref, y_ref, g_ref, r_ref, dy_ref, dg_ref = refs
        else:
            do_ref, y_ref, g_ref, dy_ref, dg_ref = refs
        dy, dg = _rms_bwd(do_ref[...] * scale, y_ref[...], g_ref[...])
        if has_res:
            dy = dy + r_ref[...]
        dy_ref[...] = dy.astype(out_dtype)

        @pl.when(pl.program_id(0) == 0)
        def _():
            dg_ref[...] = jnp.zeros_like(dg_ref)

        dg_ref[...] += dg

    row = pl.BlockSpec((tm, d), lambda i: (i, 0))
    vec = pl.BlockSpec((1, d), lambda i: (0, 0))
    ins = [dout, y, g] + ([resid] if has_res else [])
    return _call(body, name, (s // tm,), [row, row, vec] + ([row] if has_res else []),
                 [row, vec], [_sds((s, d), out_dtype), _sds((1, d), F32)])(*ins)


def ffn_up(h, wg, wu, name, dep=None):
    s, d = h.shape
    nj, _, fs = wg.shape
    tm = _tile(s, 1024)

    def body(h_ref, wg_ref, wu_ref, *rest):
        a_ref, b_ref, u_ref = rest[-3:]
        hh = h_ref[...]
        a = _dot(hh, wg_ref[...])
        b = _dot(hh, wu_ref[...])
        a_ref[...] = a.astype(BF16)
        b_ref[...] = b.astype(BF16)
        u_ref[...] = (a * _sigmoid(a) * b).astype(BF16)

    wspec = pl.BlockSpec((None, d, fs), lambda j, i: (j, 0, 0))
    ospec = pl.BlockSpec((None, tm, fs), lambda j, i: (j, i, 0))
    osd = _sds((nj, s, fs), BF16)
    ins, specs = [h, wg, wu], [pl.BlockSpec((tm, d), lambda j, i: (i, 0)), wspec, wspec]
    if dep is not None:
        ins.append(dep)
        specs.append(pl.BlockSpec((8, 128), lambda j, i: (0, 0)))
    return _call(body, name, (nj, s // tm), specs, [ospec, ospec, ospec], [osd, osd, osd], sem=("parallel", "parallel"))(*ins)


def down_norm(u3, w3, x, g, scale, name):
    nj, s, k = u3.shape
    d = w3.shape[2]
    tm = _tile(s, 512)
    jb = 2 if nj % 2 == 0 else 1
    nsteps = nj // jb

    def body(u_ref, w_ref, x_ref, g_ref, y_ref, xn_ref):
        j = pl.program_id(1)
        t = _dot(u_ref[0], w_ref[0])
        for q in range(1, jb):
            t = t + _dot(u_ref[q], w_ref[q])

        @pl.when(j == 0)
        def _():
            y_ref[...] = t

        @pl.when(j > 0)
        def _():
            y_ref[...] += t

        @pl.when(j == nsteps - 1)
        def _():
            xn_ref[...] = x_ref[...] + scale * _rms(y_ref[...], g_ref[...])

    row = pl.BlockSpec((tm, d), lambda i, j: (i, 0))
    return _call(body, name, (s // tm, nsteps),
                 [pl.BlockSpec((jb, tm, k), lambda i, j: (j, i, 0)), pl.BlockSpec((jb, k, d), lambda i, j: (j, 0, 0)),
                  row, pl.BlockSpec((1, d), lambda i, j: (0, 0))],
                 [row, row], [_sds((s, d), F32), _sds((s, d), F32)], sem=("parallel", "arbitrary"))(u3, w3, x, g)


def ffn_down_bwd(dy, wd, a, b, name):
    nj, fs, d = wd.shape
    s = dy.shape[0]
    tm = _tile(s, 1024)

    def body(dy_ref, w_ref, a_ref, b_ref, da_ref, db_ref):
        du = _dot(dy_ref[...], w_ref[...], NT)
        aa = a_ref[...].astype(F32)
        bb = b_ref[...].astype(F32)
        sl, dsl = _silu_and_grad(aa)
        da_ref[...] = (du * bb * dsl).astype(BF16)
        db_ref[...] = (du * sl).astype(BF16)

    hspec = pl.BlockSpec((None, tm, fs), lambda j, i: (j, i, 0))
    osd = _sds((nj, s, fs), BF16)
    return _call(body, name, (nj, s // tm),
                 [pl.BlockSpec((tm, d), lambda j, i: (i, 0)), pl.BlockSpec((None, fs, d), lambda j, i: (j, 0, 0)), hspec, hspec],
                 [hspec, hspec], [osd, osd], sem=("parallel", "parallel"))(dy, wd, a, b)


def mm_tn(a3, b3, out_dtype, name, tmm=2048, tn=1408, tk=1024):
    ja, s, m = a3.shape
    jb, _, n = b3.shape
    nj = max(ja, jb)
    tmm, tn, tk = _tile(m, tmm, 128), _tile(n, tn, 128), _tile(s, tk)
    nk = s // tk

    def body(a_ref, b_ref, o_ref, acc_ref):
        k = pl.program_id(3)

        @pl.when(k == 0)
        def _():
            acc_ref[...] = jnp.zeros_like(acc_ref)

        acc_ref[...] += _dot(a_ref[...].astype(BF16), b_ref[...].astype(BF16), TN)

        @pl.when(k == nk - 1)
        def _():
            o_ref[...] = acc_ref[...].astype(out_dtype)

    aj = (lambda j: j) if ja > 1 else (lambda j: 0)
    bj = (lambda j: j) if jb > 1 else (lambda j: 0)
    return _call(body, name, (nj, m // tmm, n // tn, nk),
                 [pl.BlockSpec((None, tk, tmm), lambda j, im, jn, k: (aj(j), k, im)),
                  pl.BlockSpec((None, tk, tn), lambda j, im, jn, k: (bj(j), k, jn))],
                 pl.BlockSpec((None, tmm, tn), lambda j, im, jn, k: (j, im, jn)), _sds((nj, m, n), out_dtype),
                 scratch=[pltpu.VMEM((tmm, tn), F32)],
                 sem=("parallel", "parallel", "parallel", "arbitrary"))(a3, b3)


def mm_nt_acc(pairs, name, tm=512, tn=512, dep=None):
    nj, s, _ = pairs[0][0].shape
    n = pairs[0][1].shape[1]
    tm, tn = _tile(s, tm), _tile(n, tn, 128)
    npair = len(pairs)

    def body(*refs):
        o_ref = refs[-1]
        j = pl.program_id(2)
        t = None
        for p in range(npair):
            c = _dot(refs[2 * p][...].astype(BF16), refs[2 * p + 1][...], NT)
            t = c if t is None else t + c
        if nj == 1:
            o_ref[...] = t
        else:
            @pl.when(j == 0)
            def _():
                o_ref[...] = t

            @pl.when(j > 0)
            def _():
                o_ref[...] += t

    ins, specs = [], []
    for a3, w3 in pairs:
        k = a3.shape[2]
        ins += [a3, w3]
        specs += [pl.BlockSpec((None, tm, k), lambda i, jn, j: (j, i, 0)),
                  pl.BlockSpec((None, tn, k), lambda i, jn, j: (j, jn, 0))]
    if dep is not None:
        ins.append(dep)
        specs.append(pl.BlockSpec((8, 128), lambda i, jn, j: (0, 0)))
    return _call(body, name, (s // tm, n // tn, nj), specs,
                 pl.BlockSpec((tm, tn), lambda i, jn, j: (i, jn)), _sds((s, n), F32),
                 sem=("parallel", "parallel", "arbitrary"))(*ins)


def mm_nn(a, w, name, tm=512, tn=1408, col0=0, kdim=None):
    s = a.shape[0]
    k, n = w.shape
    assert col0 % k == 0
    tm, tn = _tile(s, tm), _tile(n, tn, 128)
    cb = col0 // k

    def body(a_ref, w_ref, o_ref):
        o_ref[...] = _dot(a_ref[...].astype(BF16), w_ref[...])

    return _call(body, name, (n // tn, s // tm),
                 [pl.BlockSpec((tm, k), lambda jn, i: (i, cb)), pl.BlockSpec((k, tn), lambda jn, i: (0, jn))],
                 pl.BlockSpec((tm, tn), lambda jn, i: (i, jn)), _sds((s, n), F32), sem=("parallel", "parallel"))(a, w)


def mm_nn_acc(a3, w3, name, tm=256):
    nj, s, k = a3.shape
    n = w3.shape[2]
    tm = _tile(s, tm)

    def body(a_ref, w_ref, o_ref):
        j = pl.program_id(1)
        t = _dot(a_ref[...].astype(BF16), w_ref[...])

        @pl.when(j == 0)
        def _():
            o_ref[...] = t

        @pl.when(j > 0)
        def _():
            o_ref[...] += t

    return _call(body, name, (s // tm, nj),
                 [pl.BlockSpec((None, tm, k), lambda i, j: (j, i, 0)), pl.BlockSpec((None, k, n), lambda i, j: (j, 0, 0))],
                 pl.BlockSpec((tm, n), lambda i, j: (i, 0)), _sds((s, n), F32), sem=("parallel", "arbitrary"))(a3, w3)


def loss_and_grad(xl, target, name):
    s, d = xl.shape
    tm = _tile(s, 512)

    def body(x_ref, t_ref, l_ref, dx_ref):
        e = x_ref[...] - t_ref[...]
        dx_ref[...] = e * (1.0 / d)

        @pl.when(pl.program_id(0) == 0)
        def _():
            l_ref[...] = jnp.zeros_like(l_ref)

        part = jnp.sum(jnp.sum(e * e, axis=-1, keepdims=True), axis=0, keepdims=True) * (0.5 / d)
        l_ref[...] += jnp.broadcast_to(part, l_ref.shape)

    row = pl.BlockSpec((tm, d), lambda i: (i, 0))
    return _call(body, name, (s // tm,), [row, row], [pl.BlockSpec((1, 128), lambda i: (0, 0)), row],
                 [_sds((1, 128), F32), _sds((s, d), F32)])(xl, target)


def cast_bf16(w4, layer, name):
    _, r, c = w4.shape
    tr = _tile(r, 512)

    def body(w_ref, o_ref):
        o_ref[...] = w_ref[...].astype(BF16)

    return _call(body, name, (r // tr,), [pl.BlockSpec((None, tr, c), lambda i: (layer, i, 0))],
                 pl.BlockSpec((tr, c), lambda i: (i, 0)), _sds((r, c), BF16), sem=("parallel",))(w4)


def _attn_specs(nb):
    w = WINDOW
    prev = lambda i: jnp.maximum(i - 1, 0)
    q = pl.BlockSpec((w, ATTN_WIDTH), lambda i: (i, 0))
    kc = pl.BlockSpec((w, ATTN_KV_WIDTH), lambda i: (i, Z_K // ATTN_KV_WIDTH))
    kp = pl.BlockSpec((w, ATTN_KV_WIDTH), lambda i: (prev(i), Z_K // ATTN_KV_WIDTH))
    vc = pl.BlockSpec((w, ATTN_KV_WIDTH), lambda i: (i, Z_V // ATTN_KV_WIDTH))
    vp = pl.BlockSpec((w, ATTN_KV_WIDTH), lambda i: (prev(i), Z_V // ATTN_KV_WIDTH))
    tc = pl.BlockSpec((w, HEAD_DIM), lambda i: (i, 0))
    tp = pl.BlockSpec((w, HEAD_DIM), lambda i: (prev(i), 0))
    sink = pl.BlockSpec(memory_space=pltpu.SMEM)
    return [q, kc, kp, vc, vp, tc, tc, tp, tp, sink]


def _attn_mask(i):
    w = WINDOW
    qi = lax.broadcasted_iota(jnp.int32, (w, 2 * w), 0) + w
    kj = lax.broadcasted_iota(jnp.int32, (w, 2 * w), 1)
    rel = qi - kj
    band = (rel >= 0) & (rel < w)
    return band & jnp.logical_not((i == 0) & (kj < w))


def _attn_probs(qs, kk, sinks, mask):
    n = range(len(qs))
    s = [_dot(qs[i], kk, NT) * (HEAD_DIM ** -0.5) for i in n]
    s = [jnp.where(mask, s[i], NEG) for i in n]
    m = [jnp.maximum(jnp.max(s[i], axis=-1, keepdims=True), sinks[i]) for i in n]
    p = [jnp.exp(s[i] - m[i]) for i in n]
    es = [jnp.exp(sinks[i] - m[i]) for i in n]
    inv = [1.0 / (jnp.sum(p[i], axis=-1, keepdims=True) + es[i]) for i in n]
    return [p[i] * inv[i] for i in n], [es[i] * inv[i] for i in n]


def attn_fwd(z, cos_f, sin_s, sinks, name):
    s = z.shape[0]
    nb = s // WINDOW
    hd = HEAD_DIM
    grp = ATTN_HEADS // ATTN_KV_HEADS

    def body(q_ref, kc_ref, kp_ref, vc_ref, vp_ref, cc_ref, sc_ref, cp_ref, sp_ref, sink_ref, o_ref):
        i = pl.program_id(0)
        mask = _attn_mask(i)
        cc, sc, cp, sp = cc_ref[...], sc_ref[...], cp_ref[...], sp_ref[...]
        for kv in range(ATTN_KV_HEADS):
            ksl = slice(kv * hd, (kv + 1) * hd)
            kk = jnp.concatenate([_rope(kp_ref[:, ksl], cp, sp), _rope(kc_ref[:, ksl], cc, sc)], axis=0).astype(BF16)
            vv = jnp.concatenate([vp_ref[:, ksl], vc_ref[:, ksl]], axis=0).astype(BF16)
            heads = [kv * grp + g for g in range(grp)]
            hsl = [slice(h * hd, (h + 1) * hd) for h in heads]
            q = [_rope(q_ref[:, sl], cc, sc).astype(BF16) for sl in hsl]
            pn, _ = _attn_probs(q, kk, [sink_ref[h] for h in heads], mask)
            out = [_dot(pn[i].astype(BF16), vv).astype(BF16) for i in range(grp)]
            for i in range(grp):
                o_ref[:, hsl[i]] = out[i]

    return _call(body, name, (nb,), _attn_specs(nb), pl.BlockSpec((WINDOW, ATTN_WIDTH), lambda i: (i, 0)),
                 _sds((s, ATTN_WIDTH), BF16), sem=("parallel",))(z, z, z, z, z, cos_f, sin_s, cos_f, sin_s, sinks)


def attn_bwd(z, cos_f, sin_s, sinks, dy, name):
    s = z.shape[0]
    nb = s // WINDOW
    hd = HEAD_DIM
    grp = ATTN_HEADS // ATTN_KV_HEADS
    scale = HEAD_DIM ** -0.5

    def body(q_ref, kc_ref, kp_ref, vc_ref, vp_ref, cc_ref, sc_ref, cp_ref, sp_ref, sink_ref, dy_ref,
             dq_ref, dkc_ref, dkp_ref, dvc_ref, dvp_ref, ds_ref):
        i = pl.program_id(0)
        mask = _attn_mask(i)
        cc, sc, cp, sp = cc_ref[...], sc_ref[...], cp_ref[...], sp_ref[...]

        @pl.when(i == 0)
        def _():
            ds_ref[...] = jnp.zeros_like(ds_ref)

        for kv in range(ATTN_KV_HEADS):
            ksl = slice(kv * hd, (kv + 1) * hd)
            kk = jnp.concatenate([_rope(kp_ref[:, ksl], cp, sp), _rope(kc_ref[:, ksl], cc, sc)], axis=0).astype(BF16)
            vv = jnp.concatenate([vp_ref[:, ksl], vc_ref[:, ksl]], axis=0).astype(BF16)
            heads = [kv * grp + g for g in range(grp)]
            n = range(grp)
            hsl = [slice(h * hd, (h + 1) * hd) for h in heads]
            q = [_rope(q_ref[:, sl], cc, sc).astype(BF16) for sl in hsl]
            pn, psink = _attn_probs(q, kk, [sink_ref[h] for h in heads], mask)
            do = [dy_ref[:, sl].astype(BF16) for sl in hsl]
            dpn = [_dot(do[i], vv, NT) for i in n]
            dvs = [_dot(pn[i].astype(BF16), do[i], TN) for i in n]
            tot = [jnp.sum(pn[i] * dpn[i], axis=-1, keepdims=True) for i in n]
            dsc = [(pn[i] * (dpn[i] - tot[i]) * scale).astype(BF16) for i in n]
            dqs = [_rope_bwd(_dot(dsc[i], kk), cc, sc) for i in n]
            dks = [_dot(dsc[i], q[i], TN) for i in n]
            dvv = (dvs[0] + dvs[1]) + (dvs[2] + dvs[3]) if grp == 4 else sum(dvs[1:], dvs[0])
            dkk = (dks[0] + dks[1]) + (dks[2] + dks[3]) if grp == 4 else sum(dks[1:], dks[0])
            for i in n:
                dq_ref[:, hsl[i]] = dqs[i]
                dsink = jnp.sum(-psink[i] * tot[i], axis=0, keepdims=True)
                ds_ref[heads[i]:heads[i] + 1, :] += jnp.broadcast_to(dsink, (1, 128))
            dkp_ref[:, ksl] = _rope_bwd(dkk[:WINDOW], cp, sp)
            dkc_ref[:, ksl] = _rope_bwd(dkk[WINDOW:], cc, sc)
            dvp_ref[:, ksl] = dvv[:WINDOW]
            dvc_ref[:, ksl] = dvv[WINDOW:]

    kvo = pl.BlockSpec((WINDOW, ATTN_KV_WIDTH), lambda i: (i, 0))
    kvs = _sds((s, ATTN_KV_WIDTH), F32)
    return _call(body, name, (nb,), _attn_specs(nb) + [pl.BlockSpec((WINDOW, ATTN_WIDTH), lambda i: (i, 0))],
                 [pl.BlockSpec((WINDOW, ATTN_WIDTH), lambda i: (i, 0)), kvo, kvo, kvo, kvo,
                  pl.BlockSpec((ATTN_HEADS, 128), lambda i: (0, 0))],
                 [_sds((s, ATTN_WIDTH), F32), kvs, kvs, kvs, kvs, _sds((ATTN_HEADS, 128), F32)])(
        z, z, z, z, z, cos_f, sin_s, cos_f, sin_s, sinks, dy)


def _shift_down(x, d, row):
    return jnp.where(row >= d, pltpu.roll(x, d, 0), 0.0)


def _shift_up(x, d, row, n):
    return jnp.where(row < n - d, pltpu.roll(x, n - d, 0), 0.0)


def _conv_taps(u, w_ref, row):
    c = w_ref[DN_CONV - 1:DN_CONV, :] * u
    for k in range(DN_CONV - 1):
        c = c + w_ref[k:k + 1, :] * _shift_down(u, DN_CONV - 1 - k, row)
    return c


def dn_pre_fwd(z, conv_w, name):
    s = z.shape[0]
    nblk = 3 * DN_WIDTH // 128
    nqk = 2 * DN_WIDTH // 128

    def body(u_ref, w_ref, o_ref):
        row = lax.broadcasted_iota(jnp.int32, (s, 128), 0)
        c = _conv_taps(u_ref[...], w_ref, row)
        sl = c * _sigmoid(c)
        j = pl.program_id(0)

        @pl.when(j < nqk)
        def _():
            o_ref[...] = sl * lax.rsqrt(jnp.sum(sl * sl, axis=-1, keepdims=True) + NORM_EPS)

        @pl.when(j >= nqk)
        def _():
            o_ref[...] = sl

    return _call(body, name, (nblk,),
                 [pl.BlockSpec((s, 128), lambda j: (0, Z_DN // 128 + j)), pl.BlockSpec((DN_CONV, 128), lambda j: (0, j))],
                 pl.BlockSpec((s, 128), lambda j: (0, j)), _sds((s, 3 * DN_WIDTH), F32), sem=("parallel",))(z, conv_w)


def dn_pre_bwd(z, conv_w, dout, name):
    s = z.shape[0]
    nblk = 3 * DN_WIDTH // 128
    nqk = 2 * DN_WIDTH // 128

    def body(u_ref, w_ref, do_ref, du_ref, dw_ref, ds_ref):
        row = lax.broadcasted_iota(jnp.int32, (s, 128), 0)
        u = u_ref[...]
        c = _conv_taps(u, w_ref, row)
        sl, dsl = _silu_and_grad(c)
        do = do_ref[...]
        j = pl.program_id(0)

        @pl.when(j < nqk)
        def _():
            r = lax.rsqrt(jnp.sum(sl * sl, axis=-1, keepdims=True) + NORM_EPS)
            ds_ref[...] = r * do - sl * (r * r * r) * jnp.sum(do * sl, axis=-1, keepdims=True)

        @pl.when(j >= nqk)
        def _():
            ds_ref[...] = do

        dc = ds_ref[...] * dsl
        du = w_ref[DN_CONV - 1:DN_CONV, :] * dc
        dw_ref[DN_CONV - 1:DN_CONV, :] = jnp.sum(dc * u, axis=0, keepdims=True)
        for k in range(DN_CONV - 1):
            d = DN_CONV - 1 - k
            du = du + w_ref[k:k + 1, :] * _shift_up(dc, d, row, s)
            dw_ref[k:k + 1, :] = jnp.sum(dc * _shift_down(u, d, row), axis=0, keepdims=True)
        du_ref[...] = du

    blk = pl.BlockSpec((s, 128), lambda j: (0, j))
    wsp = pl.BlockSpec((DN_CONV, 128), lambda j: (0, j))
    return _call(body, name, (nblk,), [pl.BlockSpec((s, 128), lambda j: (0, Z_DN // 128 + j)), wsp, blk],
                 [blk, wsp], [_sds((s, 3 * DN_WIDTH), F32), _sds((DN_CONV, 3 * DN_WIDTH), F32)],
                 scratch=[pltpu.VMEM((s, 128), F32)], sem=("parallel",))(z, conv_w, dout)


def _lane_col(x, lane, idx):
    return jnp.sum(jnp.where(lane == idx, x, 0.0), axis=-1, keepdims=True)


def dn_gates_fwd(z, alog_b, dtb_b, name):
    s = z.shape[0]
    tm = _tile(s, 512)

    def body(zs_ref, al_ref, dt_ref, beta_ref, g_ref):
        zs = zs_ref[...]
        lane = lax.broadcasted_iota(jnp.int32, zs.shape, 1)
        for h in range(DN_HEADS):
            b_raw = _lane_col(zs, lane, h)
            a_raw = _lane_col(zs, lane, DN_HEADS + h)
            beta_ref[h] = jnp.broadcast_to(_sigmoid(b_raw), (tm, 128))
            g_ref[h] = -jnp.exp(al_ref[h:h + 1, :]) * _softplus(a_raw + dt_ref[h:h + 1, :])

    osp = pl.BlockSpec((DN_HEADS, tm, 128), lambda i: (0, i, 0))
    psp = pl.BlockSpec((DN_HEADS, 128), lambda i: (0, 0))
    osd = _sds((DN_HEADS, s, 128), F32)
    return _call(body, name, (s // tm,), [pl.BlockSpec((tm, 128), lambda i: (i, Z_SM // 128)), psp, psp],
                 [osp, osp], [osd, osd], sem=("parallel",))(z, alog_b, dtb_b)


def dn_gates_bwd(z, alog_b, dtb_b, dbeta, dg, name):
    s = z.shape[0]
    tm = _tile(s, 512)

    def body(zs_ref, al_ref, dt_ref, dbeta_ref, dg_ref, dz_ref, dal_ref, ddt_ref):
        @pl.when(pl.program_id(0) == 0)
        def _():
            dal_ref[...] = jnp.zeros_like(dal_ref)
            ddt_ref[...] = jnp.zeros_like(ddt_ref)

        zs = zs_ref[...]
        lane = lax.broadcasted_iota(jnp.int32, zs.shape, 1)
        dz = jnp.zeros_like(zs)
        for h in range(DN_HEADS):
            b_raw = _lane_col(zs, lane, h)
            a_raw = _lane_col(zs, lane, DN_HEADS + h)
            dbe = jnp.sum(dbeta_ref[h], axis=-1, keepdims=True)
            dgg = jnp.sum(dg_ref[h], axis=-1, keepdims=True)
            beta = _sigmoid(b_raw)
            ea = jnp.exp(al_ref[h:h + 1, :])
            pre = a_raw + dt_ref[h:h + 1, :]
            da_raw = dgg * (-ea) * _sigmoid(pre)
            dz = dz + jnp.where(lane == h, dbe * beta * (1.0 - beta), 0.0) + jnp.where(lane == DN_HEADS + h, da_raw, 0.0)
            ddt_ref[h:h + 1, :] += jnp.sum(da_raw, axis=0, keepdims=True)
            dal_ref[h:h + 1, :] += jnp.sum(dgg * (-ea) * _softplus(pre), axis=0, keepdims=True)
        dz_ref[...] = dz

    hsp = pl.BlockSpec((DN_HEADS, tm, 128), lambda i: (0, i, 0))
    psp = pl.BlockSpec((DN_HEADS, 128), lambda i: (0, 0))
    return _call(body, name, (s // tm,), [pl.BlockSpec((tm, 128), lambda i: (i, Z_SM // 128)), psp, psp, hsp, hsp],
                 [pl.BlockSpec((tm, 128), lambda i: (i, 0)), psp, psp],
                 [_sds((s, 128), F32), _sds((DN_HEADS, 128), F32), _sds((DN_HEADS, 128), F32)])(z, alog_b, dtb_b, dbeta, dg)


def _dn_intra(q, k, v, gb, bb):
    c = DN_CHUNK
    pairs = range(len(q))
    ri = lax.broadcasted_iota(jnp.int32, (c, c), 0)
    ci = lax.broadcasted_iota(jnp.int32, (c, c), 1)
    causal = ri >= ci
    strict = ri > ci
    gc = [_chunk_cumsum(gb[i]) for i in pairs]
    grow = [_as_row(gc[i]) for i in pairs]
    decay = [jnp.where(causal, jnp.exp(jnp.where(causal, gc[i][:, :c] - grow[i], 0.0)), 0.0) for i in pairs]
    qs = [q[i] * (DN_HEAD_DIM ** -0.5) for i in pairs]
    kb = [k[i] * bb[i] for i in pairs]
    lower = [jnp.where(strict, _dot(kb[i], k[i], NT) * decay[i], 0.0) for i in pairs]
    t = _unit_lower_inverse(lower)
    eg = [jnp.exp(gc[i]) for i in pairs]
    u = [_dot(t[i], v[i] * bb[i]) for i in pairs]
    w = [_dot(t[i], kb[i] * eg[i]) for i in pairs]
    attn = [jnp.where(causal, _dot(qs[i], k[i], NT) * decay[i], 0.0) for i in pairs]
    glast = [_last_row(gc[i]) for i in pairs]
    return (u, w, [qs[i] * eg[i] for i in pairs], [k[i] * jnp.exp(glast[i] - gc[i]) for i in pairs], attn,
            [jnp.exp(glast[i][:8]) for i in pairs])


def _split3(x):
    x1 = x.astype(BF16)
    r = x - x1.astype(F32)
    x2 = r.astype(BF16)
    return x1, x2, (r - x2.astype(F32)).astype(BF16)


def _dot_split(a, b, dims=NN):
    a1, a2, _ = _split3(a)
    b1, b2, _ = _split3(b)
    return _dot(a1, b1, dims) + (_dot(a1, b2, dims) + _dot(a2, b1, dims))


def _sel_dot(m01, x, dims, m_left):
    m = m01.astype(BF16)
    parts = [_dot(m, xi, dims) if m_left else _dot(xi, m, dims) for xi in _split3(x)]
    return parts[0] + (parts[1] + parts[2])


def _tri_mask(n, upper):
    ri = lax.broadcasted_iota(jnp.int32, (n, n), 0)
    ci = lax.broadcasted_iota(jnp.int32, (n, n), 1)
    return ci >= ri if upper else ri >= ci


@jax.custom_vjp
def _chunk_cumsum(x):
    return _sel_dot(_tri_mask(x.shape[0], False), x, NN, True)


_chunk_cumsum.defvjp(lambda x: (_chunk_cumsum(x), None),
                     lambda _, ct: (_sel_dot(_tri_mask(ct.shape[0], True), ct, NN, True),))


def _lane0(rows):
    return lax.broadcasted_iota(jnp.int32, (rows, 128), 1) == 0


@jax.custom_vjp
def _as_row(x):
    return _sel_dot(_lane0(x.shape[0]), x, NT, True)


_as_row.defvjp(lambda x: (_as_row(x), None),
               lambda _, ct: (_sel_dot(_lane0(ct.shape[0]), ct, TN, False),))


def _last_col_mask(n, transpose):
    idx = lax.broadcasted_iota(jnp.int32, (n, n), 0 if transpose else 1)
    return idx == n - 1


@jax.custom_vjp
def _last_row(x):
    return _sel_dot(_last_col_mask(x.shape[0], False), x, NN, True)


_last_row.defvjp(lambda x: (_last_row(x), None),
                 lambda _, ct: (_sel_dot(_last_col_mask(ct.shape[0], True), ct, NN, True),))


@jax.custom_vjp
def _unit_lower_inverse(lowers):
    n = lowers[0].shape[0]
    eye = (lax.broadcasted_iota(jnp.int32, (n, n), 0) == lax.broadcasted_iota(jnp.int32, (n, n), 1)).astype(F32)
    ts = [eye - l for l in lowers]
    ps = list(lowers)
    for _ in range(5):
        ps = [_dot_split(p, p) for p in ps]
        ts = [t + _dot_split(t, p) for t, p in zip(ts, ps)]
    return ts


def _unit_lower_inverse_fwd(lowers):
    ts = _unit_lower_inverse(lowers)
    return ts, ts


def _unit_lower_inverse_bwd(ts, dts):
    half = [_dot_split(t, dt, TN) for t, dt in zip(ts, dts)]
    return ([-_dot_split(h, t, NT) for h, t in zip(half, ts)],)


_unit_lower_inverse.defvjp(_unit_lower_inverse_fwd, _unit_lower_inverse_bwd)


def _dn_step(st, qd, kd, u, w, attn, egl):
    heads = range(len(st))
    v_new = [u[h] - _dot(w[h], st[h]) for h in heads]
    o = [_dot(qd[h], st[h]) for h in heads]
    o = [o[h] + _dot(attn[h], v_new[h]) for h in heads]
    st_new = [st[h] * egl[h][0:1, :] + _dot(kd[h], v_new[h], TN) for h in heads]
    return o, st_new


def _dn_chunk_specs(m=1):
    c = DN_CHUNK
    wide = pl.BlockSpec((m * c, DN_WIDTH), lambda i: (i, 0))
    att = pl.BlockSpec((m * c, DN_HEADS * c), lambda i: (i, 0))
    egl = pl.BlockSpec((m * 8, DN_WIDTH), lambda i: (i, 0))
    return wide, att, egl


def _dn_intra_chunks(nc):
    return 4 if nc % 4 == 0 else (2 if nc % 2 == 0 else 1)


def dn_intra_fwd(qkv, gb, bb, name):
    s = qkv.shape[0]
    c, hd = DN_CHUNK, DN_HEAD_DIM
    nc = s // c
    m = _dn_intra_chunks(nc)

    def body(q_ref, k_ref, v_ref, g_ref, b_ref, u_ref, w_ref, qd_ref, kd_ref, at_ref, eg_ref):
        pairs = [(t, h) for t in range(m) for h in range(DN_HEADS)]
        rs = lambda t: slice(t * c, (t + 1) * c)
        hs = lambda h: slice(h * hd, (h + 1) * hd)
        u, w, qd, kd, at, eg = _dn_intra([q_ref[rs(t), hs(h)] for t, h in pairs], [k_ref[rs(t), hs(h)] for t, h in pairs],
                                         [v_ref[rs(t), hs(h)] for t, h in pairs], [g_ref[h, rs(t)] for t, h in pairs],
                                         [b_ref[h, rs(t)] for t, h in pairs])
        for i, (t, h) in enumerate(pairs):
            u_ref[rs(t), hs(h)], w_ref[rs(t), hs(h)], qd_ref[rs(t), hs(h)], kd_ref[rs(t), hs(h)] = u[i], w[i], qd[i], kd[i]
            at_ref[rs(t), h * c:(h + 1) * c] = at[i]
            eg_ref[t * 8:(t + 1) * 8, hs(h)] = eg[i]

    wide, att, egl = _dn_chunk_specs(m)
    hsp = pl.BlockSpec((DN_HEADS, m * c, 128), lambda i: (0, i, 0))
    wsd = _sds((s, DN_WIDTH), F32)
    return _call(body, name, (nc // m,),
                 [pl.BlockSpec((m * c, DN_WIDTH), lambda i: (i, 0)), pl.BlockSpec((m * c, DN_WIDTH), lambda i: (i, 1)),
                  pl.BlockSpec((m * c, DN_WIDTH), lambda i: (i, 2)), hsp, hsp],
                 [wide, wide, wide, wide, att, egl],
                 [wsd, wsd, wsd, wsd, _sds((s, DN_HEADS * c), F32), _sds((nc * 8, DN_WIDTH), F32)],
                 sem=("parallel",))(qkv, qkv, qkv, gb, bb)


def dn_intra_bwd(qkv, gb, bb, cts, name):
    s = qkv.shape[0]
    c, hd = DN_CHUNK, DN_HEAD_DIM
    nc = s // c
    m = _dn_intra_chunks(nc)

    def body(q_ref, k_ref, v_ref, g_ref, b_ref, du_ref, dw_ref, dqd_ref, dkd_ref, dat_ref, deg_ref,
             dq_ref, dk_ref, dv_ref, dg_ref, db_ref):
        pairs = [(t, h) for t in range(m) for h in range(DN_HEADS)]
        rs = lambda t: slice(t * c, (t + 1) * c)
        hs = lambda h: slice(h * hd, (h + 1) * hd)
        _, vjp = jax.vjp(_dn_intra, [q_ref[rs(t), hs(h)] for t, h in pairs], [k_ref[rs(t), hs(h)] for t, h in pairs],
                         [v_ref[rs(t), hs(h)] for t, h in pairs], [g_ref[h, rs(t)] for t, h in pairs],
                         [b_ref[h, rs(t)] for t, h in pairs])
        dq, dk, dv, dg, db = vjp(([du_ref[rs(t), hs(h)] for t, h in pairs], [dw_ref[rs(t), hs(h)] for t, h in pairs],
                                  [dqd_ref[rs(t), hs(h)] for t, h in pairs], [dkd_ref[rs(t), hs(h)] for t, h in pairs],
                                  [dat_ref[rs(t), h * c:(h + 1) * c] for t, h in pairs],
                                  [deg_ref[t * 8:(t + 1) * 8, hs(h)] for t, h in pairs]))
        for i, (t, h) in enumerate(pairs):
            dq_ref[rs(t), hs(h)], dk_ref[rs(t), hs(h)], dv_ref[rs(t), hs(h)] = dq[i], dk[i], dv[i]
            dg_ref[h, rs(t)] = dg[i]
            db_ref[h, rs(t)] = db[i]

    wide, att, egl = _dn_chunk_specs(m)
    hsp = pl.BlockSpec((DN_HEADS, m * c, 128), lambda i: (0, i, 0))
    hsd = _sds((DN_HEADS, s, 128), F32)
    wsd = _sds((s, DN_WIDTH), F32)
    return _call(body, name, (nc // m,),
                 [pl.BlockSpec((m * c, DN_WIDTH), lambda i: (i, 0)), pl.BlockSpec((m * c, DN_WIDTH), lambda i: (i, 1)),
                  pl.BlockSpec((m * c, DN_WIDTH), lambda i: (i, 2)), hsp, hsp, wide, wide, wide, wide, att, egl],
                 [wide, wide, wide, hsp, hsp], [wsd, wsd, wsd, hsd, hsd], sem=("parallel",))(qkv, qkv, qkv, gb, bb, *cts)


def dn_scan_fwd(u, w, qd, kd, at, eg, name):
    s = u.shape[0]
    c, hd = DN_CHUNK, DN_HEAD_DIM
    nc = s // c

    def body(u_ref, w_ref, qd_ref, kd_ref, at_ref, eg_ref, o_ref, st_ref, state):
        @pl.when(pl.program_id(0) == 0)
        def _():
            state[...] = jnp.zeros_like(state)

        heads = range(DN_HEADS)
        hs = lambda h: slice(h * hd, (h + 1) * hd)
        st = [state[h] for h in heads]
        for h in heads:
            st_ref[h] = st[h]
        o, st_new = _dn_step(st, [qd_ref[:, hs(h)] for h in heads], [kd_ref[:, hs(h)] for h in heads],
                             [u_ref[:, hs(h)] for h in heads], [w_ref[:, hs(h)] for h in heads],
                             [at_ref[:, h * c:(h + 1) * c] for h in heads], [eg_ref[:, hs(h)] for h in heads])
        for h in heads:
            o_ref[:, hs(h)] = o[h]
            state[h] = st_new[h]

    wide, att, egl = _dn_chunk_specs()
    return _call(body, name, (nc,), [wide, wide, wide, wide, att, egl],
                 [wide, pl.BlockSpec((None, DN_HEADS, hd, hd), lambda i: (i, 0, 0, 0))],
                 [_sds((s, DN_WIDTH), F32), _sds((nc, DN_HEADS, hd, hd), F32)],
                 scratch=[pltpu.VMEM((DN_HEADS, hd, hd), F32)])(u, w, qd, kd, at, eg)


def dn_scan_bwd(u, w, qd, kd, at, eg, states, do, name):
    s = u.shape[0]
    c, hd = DN_CHUNK, DN_HEAD_DIM
    nc = s // c

    def body(u_ref, w_ref, qd_ref, kd_ref, at_ref, eg_ref, st_ref, do_ref,
             du_ref, dw_ref, dqd_ref, dkd_ref, dat_ref, deg_ref, dstate):
        @pl.when(pl.program_id(0) == 0)
        def _():
            dstate[...] = jnp.zeros_like(dstate)

        heads = range(DN_HEADS)
        hs = lambda h: slice(h * hd, (h + 1) * hd)
        asl = lambda h: slice(h * c, (h + 1) * c)
        _, vjp = jax.vjp(_dn_step, [st_ref[h] for h in heads], [qd_ref[:, hs(h)] for h in heads],
                         [kd_ref[:, hs(h)] for h in heads], [u_ref[:, hs(h)] for h in heads], [w_ref[:, hs(h)] for h in heads],
                         [at_ref[:, asl(h)] for h in heads], [eg_ref[:, hs(h)] for h in heads])
        dst, dqd, dkd, du, dw, dat, deg = vjp(([do_ref[:, hs(h)] for h in heads], [dstate[h] for h in heads]))
        for h in heads:
            dstate[h] = dst[h]
            du_ref[:, hs(h)], dw_ref[:, hs(h)], dqd_ref[:, hs(h)], dkd_ref[:, hs(h)] = du[h], dw[h], dqd[h], dkd[h]
            dat_ref[:, asl(h)] = dat[h]
            deg_ref[:, hs(h)] = deg[h]

    rev = lambda i: nc - 1 - i
    wide = pl.BlockSpec((c, DN_WIDTH), lambda i: (rev(i), 0))
    att = pl.BlockSpec((c, DN_HEADS * c), lambda i: (rev(i), 0))
    egl = pl.BlockSpec((8, DN_WIDTH), lambda i: (rev(i), 0))
    wsd = _sds((s, DN_WIDTH), F32)
    return _call(body, name, (nc,),
                 [wide, wide, wide, wide, att, egl, pl.BlockSpec((None, DN_HEADS, hd, hd), lambda i: (rev(i), 0, 0, 0)), wide],
                 [wide, wide, wide, wide, att, egl],
                 [wsd, wsd, wsd, wsd, _sds((s, DN_HEADS * c), F32), _sds((nc * 8, DN_WIDTH), F32)],
                 scratch=[pltpu.VMEM((DN_HEADS, hd, hd), F32)])(u, w, qd, kd, at, eg, states, do)


def _dn_out(o, zg, nw):
    n = o * lax.rsqrt(jnp.mean(o * o, axis=-1, keepdims=True) + NORM_EPS) * nw
    return n * (zg * _sigmoid(zg))


def dn_out_fwd(o, z, nw, name):
    s = o.shape[0]
    tm = _tile(s, 512)
    hd = DN_HEAD_DIM

    def body(o_ref, zg_ref, nw_ref, y_ref):
        for h in range(DN_HEADS):
            hs = slice(h * hd, (h + 1) * hd)
            y_ref[:, hs] = _dn_out(o_ref[:, hs], zg_ref[:, hs], nw_ref[...]).astype(BF16)

    return _call(body, name, (s // tm,),
                 [pl.BlockSpec((tm, DN_WIDTH), lambda i: (i, 0)), pl.BlockSpec((tm, DN_WIDTH), lambda i: (i, Z_ZG // DN_WIDTH)),
                  pl.BlockSpec((1, hd), lambda i: (0, 0))],
                 pl.BlockSpec((tm, DN_WIDTH), lambda i: (i, 0)), _sds((s, DN_WIDTH), BF16), sem=("parallel",))(o, z, nw)


def dn_out_bwd(o, z, nw, dycat, name):
    s = o.shape[0]
    tm = _tile(s, 512)
    hd = DN_HEAD_DIM

    def body(o_ref, zg_ref, nw_ref, dy_ref, do_ref, dzg_ref, dnw_ref):
        @pl.when(pl.program_id(0) == 0)
        def _():
            dnw_ref[...] = jnp.zeros_like(dnw_ref)

        for h in range(DN_HEADS):
            hs = slice(h * hd, (h + 1) * hd)
            _, vjp = jax.vjp(_dn_out, o_ref[:, hs], zg_ref[:, hs], nw_ref[...])
            do, dzg, dnw = vjp(dy_ref[:, hs])
            do_ref[:, hs] = do
            dzg_ref[:, hs] = dzg
            dnw_ref[...] += dnw

    wide = pl.BlockSpec((tm, DN_WIDTH), lambda i: (i, 0))
    wsd = _sds((s, DN_WIDTH), F32)
    return _call(body, name, (s // tm,),
                 [wide, pl.BlockSpec((tm, DN_WIDTH), lambda i: (i, Z_ZG // DN_WIDTH)), pl.BlockSpec((1, hd), lambda i: (0, 0)),
                  pl.BlockSpec((tm, DN_WIDTH), lambda i: (i, ATTN_WIDTH // DN_WIDTH))],
                 [wide, wide, pl.BlockSpec((1, hd), lambda i: (0, 0))], [wsd, wsd, _sds((1, hd), F32)])(o, z, nw, dycat)


def _s5_param_fn(are, aim, ldt, bre, bim):
    dt = jnp.exp(ldt)
    er = jnp.exp(are * dt)
    abr = er * jnp.cos(aim * dt)
    abi = er * jnp.sin(aim * dt)
    den = are * are + aim * aim
    cr = ((abr - 1.0) * are + abi * aim) / den
    ci = (abi * are - (abr - 1.0) * aim) / den
    return abr, abi, cr * bre - ci * bim, cr * bim + ci * bre


def s5_params_fwd(are, aim, ldt, bre, bim, name):
    p, hh = bre.shape

    def body(a_ref, b_ref, c_ref, d_ref, e_ref, o1, o2, o3, o4):
        o1[...], o2[...], o3[...], o4[...] = _s5_param_fn(a_ref[...], b_ref[...], c_ref[...], d_ref[...], e_ref[...])

    col = pl.BlockSpec((p, 1), lambda: (0, 0))
    mat = pl.BlockSpec((p, hh), lambda: (0, 0))
    return _call(body, name, (), [col, col, col, mat, mat], [col, col, mat, mat],
                 [_sds((p, 1), F32), _sds((p, 1), F32), _sds((p, hh), F32), _sds((p, hh), F32)])(are, aim, ldt, bre, bim)


def s5_params_bwd(are, aim, ldt, bre, bim, cts, name):
    p, hh = bre.shape

    def body(a_ref, b_ref, c_ref, d_ref, e_ref, g1, g2, g3, g4, o1, o2, o3, o4, o5):
        _, vjp = jax.vjp(_s5_param_fn, a_ref[...], b_ref[...], c_ref[...], d_ref[...], e_ref[...])
        o1[...], o2[...], o3[...], o4[...], o5[...] = vjp((g1[...], g2[...], g3[...], g4[...]))

    col = pl.BlockSpec((p, 1), lambda: (0, 0))
    mat = pl.BlockSpec((p, hh), lambda: (0, 0))
    csd, msd = _sds((p, 1), F32), _sds((p, hh), F32)
    return _call(body, name, (), [col, col, col, mat, mat, col, col, mat, mat], [col, col, col, mat, mat],
                 [csd, csd, csd, msd, msd])(are, aim, ldt, bre, bim, *cts)


def _cmul(ar, ai, br, bi):
    return ar * br - ai * bi, ar * bi + ai * br


S5_ROWS = 8
S5_UNROLL = 4


def _s5_tile_scan(xr, xi, ar, ai, reverse):
    row = lax.broadcasted_iota(jnp.int32, xr.shape, 0)
    d = 1
    while d < S5_ROWS:
        if reverse:
            sr, si = _shift_up(xr, d, row, S5_ROWS), _shift_up(xi, d, row, S5_ROWS)
        else:
            sr, si = _shift_down(xr, d, row), _shift_down(xi, d, row)
        pr, pi = _cmul(ar, ai, sr, si)
        xr, xi = xr + pr, xi + pi
        ar, ai = _cmul(ar, ai, ar, ai)
        d *= 2
    return xr, xi


def _s5_carry_powers(ar, ai, width, reverse):
    row = lax.broadcasted_iota(jnp.int32, (S5_ROWS, width), 0)
    at = row == (S5_ROWS - 1 if reverse else 0)
    return _s5_tile_scan(jnp.where(at, ar, 0.0), jnp.where(at, ai, 0.0), ar, ai, reverse)


def s5_scan_fwd(bu, abr, abi, name):
    s = bu.shape[0]
    wd = 256
    npb = S5_P // wd
    step = S5_ROWS * S5_UNROLL
    assert s % step == 0

    def body(br_ref, bi_ref, ar_ref, ai_ref, x_ref):
        ar, ai = ar_ref[...], ai_ref[...]
        pwr, pwi = _s5_carry_powers(ar, ai, wd, False)

        def loop(i, carry):
            cr, ci = carry
            base = pl.multiple_of(i * step, step)
            tiles = []
            for t in range(S5_UNROLL):
                rows = pl.ds(pl.multiple_of(base + t * S5_ROWS, S5_ROWS), S5_ROWS)
                tiles.append(_s5_tile_scan(br_ref[rows, :], bi_ref[rows, :], ar, ai, False))
            for t in range(S5_UNROLL):
                rows = pl.ds(pl.multiple_of(base + t * S5_ROWS, S5_ROWS), S5_ROWS)
                tr, ti = _cmul(pwr, pwi, cr, ci)
                xr, xi = tiles[t][0] + tr, tiles[t][1] + ti
                x_ref[0, rows, :] = xr
                x_ref[1, rows, :] = xi
                cr, ci = xr[S5_ROWS - 1:S5_ROWS, :], xi[S5_ROWS - 1:S5_ROWS, :]
            return cr, ci

        zero = jnp.zeros((1, wd), F32)
        lax.fori_loop(0, s // step, loop, (zero, zero))

    re = pl.BlockSpec((s, wd), lambda j: (0, j))
    im = pl.BlockSpec((s, wd), lambda j: (0, npb + j))
    av = pl.BlockSpec((1, wd), lambda j: (0, j))
    return _call(body, name, (npb,), [re, im, av, av], pl.BlockSpec((2, s, wd), lambda j: (0, 0, j)),
                 _sds((2, s, S5_P), F32), sem=("parallel",))(bu, bu, abr, abi)


def s5_scan_bwd(dx, x, abr, abi, name):
    s = dx.shape[0]
    wd = 128
    npb = S5_P // wd
    step = S5_ROWS * S5_UNROLL
    nsteps = s // step

    def body(dr_ref, di_ref, x_ref, ar_ref, ai_ref, g_ref, dar_ref, dai_ref):
        ar, ai = ar_ref[...], -ai_ref[...]
        pwr, pwi = _s5_carry_powers(ar, ai, wd, True)
        row = lax.broadcasted_iota(jnp.int32, (S5_ROWS, wd), 0)

        def loop(k, carry):
            cr, ci, accr, acci = carry
            base = pl.multiple_of((nsteps - 1 - k) * step, step)
            tiles = [None] * S5_UNROLL
            for t in range(S5_UNROLL):
                rows = pl.ds(pl.multiple_of(base + t * S5_ROWS, S5_ROWS), S5_ROWS)
                tiles[t] = _s5_tile_scan(dr_ref[rows, :], di_ref[rows, :], ar, ai, True)
            before = pl.ds(pl.multiple_of(jnp.maximum(base - S5_ROWS, 0), S5_ROWS), S5_ROWS)
            lr = jnp.where(base > 0, x_ref[0, before, :][S5_ROWS - 1:S5_ROWS, :], 0.0)
            li = jnp.where(base > 0, x_ref[1, before, :][S5_ROWS - 1:S5_ROWS, :], 0.0)
            prev = []
            for t in range(S5_UNROLL):
                rows = pl.ds(pl.multiple_of(base + t * S5_ROWS, S5_ROWS), S5_ROWS)
                xr, xi = x_ref[0, rows, :], x_ref[1, rows, :]
                prev.append((jnp.where(row >= 1, pltpu.roll(xr, 1, 0), lr), jnp.where(row >= 1, pltpu.roll(xi, 1, 0), li)))
                lr, li = xr[S5_ROWS - 1:S5_ROWS, :], xi[S5_ROWS - 1:S5_ROWS, :]
            for t in reversed(range(S5_UNROLL)):
                rows = pl.ds(pl.multiple_of(base + t * S5_ROWS, S5_ROWS), S5_ROWS)
                tr, ti = _cmul(pwr, pwi, cr, ci)
                gr, gi = tiles[t][0] + tr, tiles[t][1] + ti
                g_ref[0, rows, :] = gr
                g_ref[1, rows, :] = gi
                pr, pi = prev[t]
                accr = accr + (gr * pr + gi * pi)
                acci = acci + (gi * pr - gr * pi)
                cr, ci = gr[0:1, :], gi[0:1, :]
            return cr, ci, accr, acci

        zero = jnp.zeros((1, wd), F32)
        zacc = jnp.zeros((S5_ROWS, wd), F32)
        _, _, accr, acci = lax.fori_loop(0, nsteps, loop, (zero, zero, zacc, zacc))
        dar_ref[...] = jnp.sum(accr, axis=0, keepdims=True)
        dai_ref[...] = jnp.sum(acci, axis=0, keepdims=True)

    re = pl.BlockSpec((s, wd), lambda j: (0, j))
    im = pl.BlockSpec((s, wd), lambda j: (0, npb + j))
    av = pl.BlockSpec((1, wd), lambda j: (0, j))
    planes = pl.BlockSpec((2, s, wd), lambda j: (0, 0, j))
    asd = _sds((1, S5_P), F32)
    return _call(body, name, (npb,), [re, im, planes, av, av], [planes, av, av], [_sds((2, s, S5_P), F32), asd, asd],
                 sem=("parallel",))(dx, dx, x, abr, abi)


def _gelu(y):
    return 0.5 * y * (1.0 + jnp.tanh(math.sqrt(2.0 / math.pi) * (y + 0.044715 * y * y * y)))


def s5_out_fwd(ypre, z, dvec, glu_w, glu_b, name):
    s = ypre.shape[0]
    tm = _tile(s, 512)
    wd = S5_WIDTH

    def body(yp_ref, u_ref, d_ref, w_ref, b_ref, y_ref, o_ref):
        y = yp_ref[...] + d_ref[...] * u_ref[...]
        y_ref[...] = y
        g = _gelu(y)
        t = _dot(g.astype(BF16), w_ref[...]) + b_ref[...]
        o_ref[...] = (g * _sigmoid(t)).astype(BF16)

    row = pl.BlockSpec((tm, wd), lambda i: (i, 0))
    vec = pl.BlockSpec((1, wd), lambda i: (0, 0))
    return _call(body, name, (s // tm,),
                 [row, pl.BlockSpec((tm, wd), lambda i: (i, Z_S5 // wd)), vec, pl.BlockSpec((wd, wd), lambda i: (0, 0)), vec],
                 [row, row], [_sds((s, wd), F32), _sds((s, wd), BF16)], sem=("parallel",))(ypre, z, dvec, glu_w, glu_b)


def s5_out_bwd(y, z, glu_w, glu_b, dycat, name):
    s = y.shape[0]
    tm = _tile(s, 512)
    wd = S5_WIDTH

    def body(y_ref, u_ref, w_ref, b_ref, do_ref, dy_ref, dd_ref, dw_ref, db_ref):
        @pl.when(pl.program_id(0) == 0)
        def _():
            dd_ref[...] = jnp.zeros_like(dd_ref)
            dw_ref[...] = jnp.zeros_like(dw_ref)
            db_ref[...] = jnp.zeros_like(db_ref)

        g, gvjp = jax.vjp(_gelu, y_ref[...])
        gb = g.astype(BF16)
        sg = _sigmoid(_dot(gb, w_ref[...]) + b_ref[...])
        do = do_ref[...]
        dt = do * g * sg * (1.0 - sg)
        dtb = dt.astype(BF16)
        dg = do * sg + _dot(dtb, w_ref[...], NT)
        (dy,) = gvjp(dg)
        dy_ref[...] = dy
        dd_ref[...] += jnp.sum(dy * u_ref[...], axis=0, keepdims=True)
        dw_ref[...] += _dot(gb, dtb, TN)
        db_ref[...] += jnp.sum(dt, axis=0, keepdims=True)

    row = pl.BlockSpec((tm, wd), lambda i: (i, 0))
    vec = pl.BlockSpec((1, wd), lambda i: (0, 0))
    mat = pl.BlockSpec((wd, wd), lambda i: (0, 0))
    return _call(body, name, (s // tm,),
                 [row, pl.BlockSpec((tm, wd), lambda i: (i, Z_S5 // wd)), mat, vec,
                  pl.BlockSpec((tm, wd), lambda i: (i, (ATTN_WIDTH + DN_WIDTH) // wd))],
                 [row, vec, mat, vec], [_sds((s, wd), F32), _sds((1, wd), F32), _sds((wd, wd), F32), _sds((1, wd), F32)])(
        y, z, glu_w, glu_b, dycat)


def assemble_dz(dq, dkc, dkp, dvc, dvp, ddn, dzg, dus, dys, dvec, dzs, name):
    s = dq.shape[0]
    w = WINDOW
    nb = s // w
    nxt = lambda i: jnp.minimum(i + 1, nb - 1)

    def body(dq_ref, dkc_ref, dkp_ref, dvc_ref, dvp_ref, ddn_ref, dzg_ref, dus_ref, dys_ref, dv_ref, dzs_ref, o_ref):
        live = (pl.program_id(0) < nb - 1).astype(F32)
        o_ref[:, Z_Q:Z_K] = dq_ref[...].astype(BF16)
        o_ref[:, Z_K:Z_V] = (dkc_ref[...] + live * dkp_ref[...]).astype(BF16)
        o_ref[:, Z_V:Z_DN] = (dvc_ref[...] + live * dvp_ref[...]).astype(BF16)
        o_ref[:, Z_DN:Z_ZG] = ddn_ref[...].astype(BF16)
        o_ref[:, Z_ZG:Z_S5] = dzg_ref[...].astype(BF16)
        o_ref[:, Z_S5:Z_SM] = (dus_ref[...] + dv_ref[...] * dys_ref[...]).astype(BF16)
        o_ref[:, Z_SM:Z_ALL] = dzs_ref[...].astype(BF16)

    def blk(width, f=lambda i: i):
        return pl.BlockSpec((w, width), lambda i: (f(i), 0))

    return _call(body, name, (nb,),
                 [blk(ATTN_WIDTH), blk(ATTN_KV_WIDTH), blk(ATTN_KV_WIDTH, nxt), blk(ATTN_KV_WIDTH), blk(ATTN_KV_WIDTH, nxt),
                  blk(3 * DN_WIDTH), blk(DN_WIDTH), blk(S5_WIDTH), blk(S5_WIDTH), pl.BlockSpec((1, S5_WIDTH), lambda i: (0, 0)),
                  blk(128)],
                 blk(Z_ALL), _sds((s, Z_ALL), BF16), sem=("parallel",))(dq, dkc, dkp, dvc, dvp, ddn, dzg, dus, dys, dvec, dzs)


def _my_place():
    return lax.axis_index("x"), lax.axis_index("y"), lax.axis_index("c")


def _peer(place, p):
    x, y, c = place
    px = 1 - x if p & 4 else x
    py = 1 - y if p & 2 else y
    pc = 1 - c if p & 1 else c
    return (px, py, pc), 4 * px + 2 * py + pc


def exchange(arrays, scatter, name):
    na = len(arrays)

    def body(*refs):
        srcs, dsts = refs[:na], refs[na:2 * na]
        send_sems, recv_sems, local_sems = refs[2 * na:]
        place = _my_place()
        me = 4 * place[0] + 2 * place[1] + place[2]
        copies = []
        for k in range(na):
            mine = srcs[k].at[me] if scatter else srcs[k]
            loc = pltpu.make_async_copy(mine, dsts[k].at[me], local_sems.at[k])
            loc.start()
            copies.append(loc)
        sends = []
        for p in range(1, N_DEV):
            peer, pid = _peer(place, p)
            for k in range(na):
                src = srcs[k].at[pid] if scatter else srcs[k]
                cp = pltpu.make_async_remote_copy(src_ref=src, dst_ref=dsts[k].at[me], send_sem=send_sems.at[k, p - 1],
                                                  recv_sem=recv_sems.at[k, p - 1], device_id=peer,
                                                  device_id_type=pl.DeviceIdType.MESH)
                cp.start()
                sends.append(cp)
        for p in range(1, N_DEV):
            peer, pid = _peer(place, p)
            for k in range(na):
                src = srcs[k].at[me] if scatter else srcs[k]
                pltpu.make_async_remote_copy(src_ref=src, dst_ref=dsts[k].at[pid], send_sem=send_sems.at[k, p - 1],
                                             recv_sem=recv_sems.at[k, p - 1], device_id=peer,
                                             device_id_type=pl.DeviceIdType.MESH).wait_recv()
        for cp in sends:
            cp.wait_send()
        for cp in copies:
            cp.wait()

    outs = [_sds((N_DEV,) + tuple(a.shape[1:] if scatter else a.shape), a.dtype) for a in arrays]
    anyspec = pl.BlockSpec(memory_space=pl.ANY)
    return pl.pallas_call(
        body, name=name, in_specs=[anyspec] * na, out_specs=[anyspec] * na, out_shape=outs,
        scratch_shapes=[pltpu.SemaphoreType.DMA((na, N_DEV - 1)), pltpu.SemaphoreType.DMA((na, N_DEV - 1)),
                        pltpu.SemaphoreType.DMA((na,))])(*arrays)


_HBM = pl.BlockSpec(memory_space=pltpu.HBM)
_SEM = pl.BlockSpec(memory_space=pltpu.SEMAPHORE)
_DATAFLOW = pltpu.SideEffectType.DATAFLOW_SIDE_EFFECTING


def _split_copies(srcs, lands, send_sems, recv_sems, scatter, arriving):
    place = _my_place()
    me = 4 * place[0] + 2 * place[1] + place[2]
    out = []
    for p in range(1, N_DEV):
        peer, pid = _peer(place, p)
        for k in range(len(srcs)):
            i = k * (N_DEV - 1) + p - 1
            src = srcs[k].at[pid] if scatter else srcs[k]
            dst = lands[k].at[pid] if arriving else lands[k].at[me]
            out.append(pltpu.make_async_remote_copy(src_ref=src, dst_ref=dst, send_sem=send_sems.at[i], recv_sem=recv_sems.at[i],
                                                    device_id=peer, device_id_type=pl.DeviceIdType.MESH))
    return out


def exchange_start(groups, scatter, name):
    arrays = [a for g in groups for a in g]
    na, ng = len(arrays), len(groups)
    first = [sum(len(g) for g in groups[:i]) for i in range(ng)]
    me = 4 * lax.axis_index("x") + 2 * lax.axis_index("y") + lax.axis_index("c")
    lands = []
    for a in arrays:
        own = lax.dynamic_index_in_dim(a, me, 0, keepdims=True) if scatter else a[None]
        shape = (N_DEV,) + tuple(own.shape[1:])
        land = lax.dynamic_update_slice(lax.empty(shape, a.dtype), own, (me,) + (0,) * (len(shape) - 1))
        lands.append(pltpu.with_memory_space_constraint(land, pltpu.HBM))
    srcs = [pltpu.with_memory_space_constraint(a, pltpu.HBM) for a in arrays]

    def body(*refs):
        src_refs, land_refs = refs[:na], refs[na:2 * na]
        sems = refs[2 * na:2 * na + 2 * ng]
        token = refs[-1]
        for i, g in enumerate(groups):
            sl = slice(first[i], first[i] + len(g))
            for send in _split_copies(src_refs[sl], land_refs[sl], sems[2 * i], sems[2 * i + 1], scatter, False):
                send.start()
        token[...] = jnp.zeros_like(token)

    sem_shapes = []
    for g in groups:
        sem_shapes += [pltpu.SemaphoreType.DMA((len(g) * (N_DEV - 1),))] * 2
    outs = pl.pallas_call(
        body, name=name,
        out_shape=(*sem_shapes, *[pltpu.HBM(a.shape, a.dtype) for a in srcs], *[pltpu.HBM(a.shape, a.dtype) for a in lands],
                   _sds((8, 128), F32)),
        in_specs=[_HBM] * (2 * na), out_specs=(*[_SEM] * (2 * ng), *[_HBM] * (2 * na), pl.BlockSpec(memory_space=pltpu.VMEM)),
        input_output_aliases={i: 2 * ng + i for i in range(2 * na)},
        compiler_params=pltpu.CompilerParams(has_side_effects=_DATAFLOW))(*srcs, *lands)
    src_out, land_out = outs[2 * ng:2 * ng + na], outs[2 * ng + na:2 * ng + 2 * na]
    handles = [(outs[2 * i], outs[2 * i + 1], src_out[first[i]:first[i] + len(g)], land_out[first[i]:first[i] + len(g)])
               for i, g in enumerate(groups)]
    return handles, outs[-1]


def exchange_wait(handle, scatter, name, after):
    send_sems, recv_sems, srcs, lands = handle
    na = len(srcs)

    def body(*refs):
        src_refs, land_refs = refs[:na], refs[na:2 * na]
        for send in _split_copies(src_refs, land_refs, refs[2 * na], refs[2 * na + 1], scatter, False):
            send.wait_send()
        for recv in _split_copies(src_refs, land_refs, refs[2 * na], refs[2 * na + 1], scatter, True):
            recv.wait_recv()

    outs = pl.pallas_call(
        body, name=name, out_shape=tuple(pltpu.HBM(a.shape, a.dtype) for a in (*srcs, *lands)),
        in_specs=[_HBM] * (2 * na) + [_SEM, _SEM, pl.BlockSpec(memory_space=pl.ANY)], out_specs=tuple([_HBM] * (2 * na)),
        input_output_aliases={i: i for i in range(2 * na)},
        compiler_params=pltpu.CompilerParams(has_side_effects=_DATAFLOW))(*srcs, *lands, send_sems, recv_sems, after)
    return list(outs[na:])


def _adamw(w, g, m, v):
    m = ADAM_B1 * m + (1.0 - ADAM_B1) * g
    v = ADAM_B2 * v + (1.0 - ADAM_B2) * (g * g)
    m_hat = m / (1.0 - ADAM_B1 ** ADAM_STEP)
    v_hat = v / (1.0 - ADAM_B2 ** ADAM_STEP)
    return -ADAM_LR * (m_hat / (jnp.sqrt(v_hat) + ADAM_EPS) + ADAM_WD * w), m, v


def reduce_adamw(parts, w4, m4, v4, layer, name, stacked=None):
    nl, r, c = w4.shape
    tr = _tile(r, 256)

    def body(p_ref, w_ref, m_ref, v_ref, *rest):
        g_ref, d_ref, nm_ref, nv_ref = rest[-4:]
        g = p_ref[0].astype(F32)
        for d in range(1, N_DEV):
            g = g + p_ref[d].astype(F32)
        g_ref[...] = g
        d_ref[...], nm_ref[...], nv_ref[...] = _adamw(w_ref[...], g, m_ref[...], v_ref[...])

    lay = pl.BlockSpec((None, tr, c), lambda i: (layer, i, 0))
    osd = _sds((nl, r, c), F32)
    ins = [parts, w4, m4, v4] + (list(stacked) if stacked is not None else [])
    specs = [pl.BlockSpec((N_DEV, tr, c), lambda i: (0, i, 0)), lay, lay, lay]
    specs += [pl.BlockSpec(memory_space=pl.ANY)] * (len(ins) - 4)
    return pl.pallas_call(
        body, name=name, grid=(r // tr,), in_specs=specs, out_specs=[lay, lay, lay, lay], out_shape=[osd, osd, osd, osd],
        input_output_aliases={4 + k: k for k in range(len(ins) - 4)},
        compiler_params=pltpu.CompilerParams(dimension_semantics=("parallel",), vmem_limit_bytes=VMEM_LIMIT))(*ins)


_SM_NAT = ATTN_WIDTH + 2 * ATTN_KV_WIDTH + 4 * DN_WIDTH


def _win_to_zall(w):
    pad = jnp.zeros((w.shape[0], 128 - 2 * DN_HEADS), w.dtype)
    return jnp.concatenate([w[:, :_SM_NAT], w[:, _SM_NAT + 2 * DN_HEADS:], w[:, _SM_NAT:_SM_NAT + 2 * DN_HEADS], pad], axis=1)


def _zall_to_win(g):
    return jnp.concatenate([g[:, :Z_ZG + DN_WIDTH], g[:, Z_SM:Z_SM + 2 * DN_HEADS], g[:, Z_S5:Z_SM]], axis=1)


def _block_diag(t):
    g, a, b = t.shape
    eye = jnp.eye(g, dtype=t.dtype)
    return (t[:, :, None, :] * eye[:, None, :, None]).reshape(g * a, g * b)


def _block_diag_extract(m, g):
    a, b = m.shape[0] // g, m.shape[1] // g
    eye = jnp.eye(g, dtype=m.dtype)
    return jnp.sum(m.reshape(g, a, g, b) * eye[:, None, :, None], axis=2)


def _rope_tables(s):
    half = HEAD_DIM // 2
    inv_freq = ROPE_THETA ** (-jnp.arange(half, dtype=F32) / half)
    ang = jnp.arange(s, dtype=F32)[:, None] * inv_freq[None, :]
    cos, sin = jnp.cos(ang), jnp.sin(ang)
    return jnp.concatenate([cos, cos], axis=1), jnp.concatenate([-sin, sin], axis=1)


def _row(v):
    return v.reshape(1, -1)


def _ffn_fwd(x, g_pre, g_post, weight, tag):
    h = rmsnorm_fwd(x, g_pre, f"{tag}_norm")
    wg, wu = weight(f"{tag}_w_gate", h), weight(f"{tag}_w_up", h)
    a, b, u = ffn_up(h, wg, wu, f"{tag}_up", dep=weight("token", None))
    wd = weight(f"{tag}_w_down", u)
    y, xn = down_norm(u, wd, x, g_post, FFN_RES_WEIGHT, f"{tag}_down")
    return xn, (x, h, a, b, u, y, wg, wu, wd)


def _ffn_bwd(dxn, saved, g_pre, g_post, on_grads, tag):
    x, h, a, b, u, y, wg, wu, wd = saved
    dy, dg_post = norm_bwd(dxn, y, g_post, FFN_RES_WEIGHT, None, BF16, f"{tag}_bnorm_post")
    da, db = ffn_down_bwd(dy, wd, a, b, f"{tag}_bdown")
    dwd = mm_tn(u, dy[None], BF16, f"{tag}_dwd", tn=2048, tk=2048)
    dwg = mm_tn(h[None], da, BF16, f"{tag}_dwg", tk=2048)
    dwu = mm_tn(h[None], db, BF16, f"{tag}_dwu", tk=2048)
    tok = on_grads({f"{tag}_w_gate": dwg, f"{tag}_w_up": dwu, f"{tag}_w_down": dwd})
    dh = mm_nt_acc([(da, wg), (db, wu)], f"{tag}_dh", tm=512, tn=2048, dep=tok)
    dx, dg_pre = norm_bwd(dh, x, g_pre, 1.0, dxn, F32, f"{tag}_bnorm_pre")
    return dx, dict(g_pre=dg_pre, g_post=dg_post)


def _s5_layouts(p):
    are, aim = p["s5_a_re"].reshape(S5_P, 1), p["s5_a_im"].reshape(S5_P, 1)
    ldt = jnp.repeat(p["s5_log_dt"], S5_STATE).reshape(S5_P, 1)
    bre, bim = p["s5_b_re"].reshape(S5_P, S5_GROUP_CH), p["s5_b_im"].reshape(S5_P, S5_GROUP_CH)
    return are, aim, ldt, bre, bim


def _mix_fwd(x, p, weight, cos_f, sin_s, tag):
    h = rmsnorm_fwd(x, _row(p["mix_norm_pre"]), f"{tag}_norm")
    w_all, glu_w = weight("w_all", h), weight("s5_glu_w", h)
    z = mm_nn(h, w_all, f"{tag}_win", tm=1024)
    y_attn = attn_fwd(z, cos_f, sin_s, p["attn_sinks"], f"{tag}_attn")
    alog_b = jnp.broadcast_to(p["dn_a_log"][:, None], (DN_HEADS, 128))
    dtb_b = jnp.broadcast_to(p["dn_dt_bias"][:, None], (DN_HEADS, 128))
    conv_w = weight("dn_conv_w", h)
    qkv = dn_pre_fwd(z, conv_w, f"{tag}_dnpre")
    bb, gb = dn_gates_fwd(z, alog_b, dtb_b, f"{tag}_dngate")
    u, w, qd, kd, at, eg = dn_intra_fwd(qkv, gb, bb, f"{tag}_dnintra")
    o, states = dn_scan_fwd(u, w, qd, kd, at, eg, f"{tag}_dnscan")
    y_dn = dn_out_fwd(o, z, _row(p["dn_norm_w"]), f"{tag}_dnout")
    s5cols = _s5_layouts(p)
    abr, abi, bbr, bbi = s5_params_fwd(*s5cols, f"{tag}_s5par")
    tb = lambda t: jnp.transpose(t.reshape(S5_GROUPS, S5_STATE, S5_GROUP_CH), (0, 2, 1))
    b_blk = jnp.concatenate([_block_diag(tb(bbr)), _block_diag(tb(bbi))], axis=1).astype(BF16)
    tc = lambda t: jnp.transpose(t, (0, 2, 1))
    c_blk = jnp.concatenate([_block_diag(tc(p["s5_c_re"])), -_block_diag(tc(p["s5_c_im"]))], axis=0).astype(BF16)
    bu = mm_nn(z, b_blk, f"{tag}_s5bu", tn=1024, col0=Z_S5)
    xs = s5_scan_fwd(bu, abr.reshape(1, S5_P), abi.reshape(1, S5_P), f"{tag}_s5scan")
    ypre = mm_nn_acc(xs, c_blk.reshape(2, S5_P, S5_WIDTH), f"{tag}_s5c")
    y5, y_s5 = s5_out_fwd(ypre, z, _row(p["s5_d"]), glu_w, _row(p["s5_glu_b"]), f"{tag}_s5out")
    ycat = jnp.concatenate([y_attn, y_dn, y_s5], axis=1)
    w_out = weight("w_out", ycat)
    mixed, xn = down_norm(ycat[None], w_out[None], x, _row(p["mix_norm_post"]), 1.0, f"{tag}_wout")
    saved = dict(x=x, h=h, z=z, qkv=qkv, bb=bb, gb=gb, dn=(u, w, qd, kd, at, eg), states=states, o=o, s5cols=s5cols,
                 abr=abr, abi=abi, b_blk=b_blk, c_blk=c_blk, xs=xs, y5=y5, ycat=ycat, mixed=mixed,
                 alog_b=alog_b, dtb_b=dtb_b, w_all=w_all, w_out=w_out, glu_w=glu_w, conv_w=conv_w)
    return xn, saved


def _mix_bwd(dxn, sv, p, on_grads, cos_f, sin_s, tag):
    z = sv["z"]
    w_all, w_out, glu_w = sv["w_all"], sv["w_out"], sv["glu_w"]
    g = {}
    dmixed, g["mix_norm_post"] = norm_bwd(dxn, sv["mixed"], _row(p["mix_norm_post"]), 1.0, None, BF16, f"{tag}_bnorm_post")
    g["w_out"] = mm_tn(sv["ycat"][None], dmixed[None], BF16, f"{tag}_dwout", tn=1024)[0]
    dycat = mm_nt_acc([(dmixed[None], w_out[None])], f"{tag}_dycat", tn=1024)
    dq, dkc, dkp, dvc, dvp, dsink = attn_bwd(z, cos_f, sin_s, p["attn_sinks"], dycat, f"{tag}_battn")
    g["attn_sinks"] = dsink[:, 0]
    do, dzg, dnw = dn_out_bwd(sv["o"], z, _row(p["dn_norm_w"]), dycat, f"{tag}_bdnout")
    g["dn_norm_w"] = dnw[0]
    cts = dn_scan_bwd(*sv["dn"], sv["states"], do, f"{tag}_bdnscan")
    dqn, dkn, dvn, dgb, dbb = dn_intra_bwd(sv["qkv"], sv["gb"], sv["bb"], cts, f"{tag}_bdnintra")
    dzs, dal, ddt = dn_gates_bwd(z, sv["alog_b"], sv["dtb_b"], dbb, dgb, f"{tag}_bdngate")
    g["dn_a_log"], g["dn_dt_bias"] = dal[:, 0], ddt[:, 0]
    ddn, g["dn_conv_w"] = dn_pre_bwd(z, sv["conv_w"], jnp.concatenate([dqn, dkn, dvn], axis=1), f"{tag}_bdnpre")
    dy5, dd, dglu, dglub = s5_out_bwd(sv["y5"], z, glu_w, _row(p["s5_glu_b"]), dycat, f"{tag}_bs5out")
    g["s5_d"], g["s5_glu_w"], g["s5_glu_b"] = dd[0], dglu, dglub[0]
    dxs = mm_nt_acc([(dy5[None], sv["c_blk"][None])], f"{tag}_bs5c", tn=1024)
    dc_blk = mm_tn(sv["xs"], dy5[None], F32, f"{tag}_ds5c", tk=256)
    ex = lambda m: jnp.transpose(_block_diag_extract(m, S5_GROUPS), (0, 2, 1))
    g["s5_c_re"], g["s5_c_im"] = ex(dc_blk[0]), -ex(dc_blk[1])
    dbu, dar, dai = s5_scan_bwd(dxs, sv["xs"], sv["abr"].reshape(1, S5_P), sv["abi"].reshape(1, S5_P), f"{tag}_bs5scan")
    b_planes = jnp.transpose(sv["b_blk"].reshape(S5_WIDTH, 2, S5_P), (1, 0, 2))
    dus = mm_nt_acc([(dbu, b_planes)], f"{tag}_bs5bu")
    u_s5 = z[:, Z_S5:Z_SM]
    db_blk = mm_tn(u_s5[None], dbu, F32, f"{tag}_ds5b", tn=1024)
    exb = lambda m: jnp.transpose(_block_diag_extract(m, S5_GROUPS), (0, 2, 1)).reshape(S5_P, S5_GROUP_CH)
    dcols = s5_params_bwd(*sv["s5cols"], (dar.reshape(S5_P, 1), dai.reshape(S5_P, 1), exb(db_blk[0]), exb(db_blk[1])),
                          f"{tag}_bs5par")
    g["s5_a_re"] = dcols[0].reshape(S5_GROUPS, S5_STATE)
    g["s5_a_im"] = dcols[1].reshape(S5_GROUPS, S5_STATE)
    g["s5_log_dt"] = jnp.sum(dcols[2].reshape(S5_GROUPS, S5_STATE), axis=1)
    g["s5_b_re"] = dcols[3].reshape(S5_GROUPS, S5_STATE, S5_GROUP_CH)
    g["s5_b_im"] = dcols[4].reshape(S5_GROUPS, S5_STATE, S5_GROUP_CH)
    dz = assemble_dz(dq, dkc, dkp, dvc, dvp, ddn, dzg, dus, dy5, _row(p["s5_d"]), dzs, f"{tag}_dz")
    g["w_all"] = mm_tn(sv["h"][None], dz[None], BF16, f"{tag}_dwin")[0]
    dwin = _zall_to_win(g.pop("w_all"))
    d_model = dwin.shape[0]
    tok = on_grads({"w_in": jnp.transpose(dwin.reshape(d_model, N_DEV, IN_WIDTH // N_DEV), (1, 0, 2)),
                    "s5_glu_w": g.pop("s5_glu_w").astype(BF16).reshape(N_DEV, S5_WIDTH // N_DEV, S5_WIDTH),
                    "w_out": g.pop("w_out").reshape(N_DEV, MIX_WIDTH // N_DEV, d_model)})
    dh = mm_nt_acc([(dz[None], w_all[None])], f"{tag}_dh", tn=1024, dep=tok)
    dx, g["mix_norm_pre"] = norm_bwd(dh, sv["x"], _row(p["mix_norm_pre"]), 1.0, dxn, F32, f"{tag}_bnorm_pre")
    return dx, g


BIG = ("ff1_w_gate", "ff1_w_up", "ff1_w_down", "w_in", "s5_glu_w", "w_out", "ff2_w_gate", "ff2_w_up", "ff2_w_down")
SMALL = ("ff1_norm_pre", "ff1_norm_post", "mix_norm_pre", "attn_sinks", "dn_conv_w", "dn_a_log", "dn_dt_bias", "dn_norm_w",
         "s5_a_re", "s5_a_im", "s5_log_dt", "s5_b_re", "s5_b_im", "s5_c_re", "s5_c_im", "s5_d", "s5_glu_b",
         "mix_norm_post", "ff2_norm_pre", "ff2_norm_post")
GATHER_GROUPS = (("ff1_w_gate", "ff1_w_up"), ("ff1_w_down",), ("w_in", "s5_glu_w", "dn_conv_w"), ("w_out",),
                 ("ff2_w_gate", "ff2_w_up"), ("ff2_w_down",))
WEIGHTS = ("ff1_norm_pre", "ff1_w_gate", "ff1_w_up", "ff1_w_down", "ff1_norm_post", "mix_norm_pre", "w_in", "attn_sinks",
           "dn_conv_w", "dn_a_log", "dn_dt_bias", "dn_norm_w", "s5_a_re", "s5_a_im", "s5_log_dt", "s5_b_re", "s5_b_im",
           "s5_c_re", "s5_c_im", "s5_d", "s5_glu_w", "s5_glu_b", "w_out", "mix_norm_post", "ff2_norm_pre", "ff2_w_gate",
           "ff2_w_up", "ff2_w_down", "ff2_norm_post")


def _pack(parts):
    rows = []
    for a in parts:
        n = a.size
        r = -(-n // 1024) * 8
        rows.append(jnp.pad(a.reshape(-1), (0, r * 128 - n)).reshape(r, 128))
    return jnp.concatenate(rows, axis=0)


def _unpack(mat, shapes):
    out, off = [], 0
    for shp in shapes:
        n = int(np.prod(shp))
        r = -(-n // 1024) * 8
        out.append(mat[off:off + r].reshape(-1)[:n].reshape(shp))
        off += r
    return out


def local_step(x, target, smalls, weight, on_grads):
    depth = len(smalls)
    cos_f, sin_s = _rope_tables(x.shape[0])
    xs = x
    saved = []
    for l in range(depth):
        p = smalls[l]
        wl = functools.partial(weight, l)
        xs, s1 = _ffn_fwd(xs, _row(p["ff1_norm_pre"]), _row(p["ff1_norm_post"]), wl, "ff1")
        xs, s2 = _mix_fwd(xs, p, wl, cos_f, sin_s, "mix")
        xs, s3 = _ffn_fwd(xs, _row(p["ff2_norm_pre"]), _row(p["ff2_norm_post"]), wl, "ff2")
        saved.append((s1, s2, s3))

    loss_vec, dx = loss_and_grad(xs, target, "loss")

    small_g = [None] * depth
    for l in reversed(range(depth)):
        p = smalls[l]
        gl = functools.partial(on_grads, l)
        s1, s2, s3 = saved[l]
        dx, g3 = _ffn_bwd(dx, s3, _row(p["ff2_norm_pre"]), _row(p["ff2_norm_post"]), gl, "ff2")
        dx, g2 = _mix_bwd(dx, s2, p, gl, cos_f, sin_s, "mix")
        dx, g1 = _ffn_bwd(dx, s1, _row(p["ff1_norm_pre"]), _row(p["ff1_norm_post"]), gl, "ff1")
        sg = {n: g2[n] for n in SMALL if n in g2}
        sg.update(ff1_norm_pre=g1["g_pre"][0], ff1_norm_post=g1["g_post"][0], ff2_norm_pre=g3["g_pre"][0], ff2_norm_post=g3["g_post"][0],
                  mix_norm_pre=g2["mix_norm_pre"][0], mix_norm_post=g2["mix_norm_post"][0])
        small_g[l] = sg
    return loss_vec, dx, small_g


def kernel(x, ff1_norm_pre, ff1_w_gate, ff1_w_up, ff1_w_down, ff1_norm_post, mix_norm_pre, w_in, attn_sinks, dn_conv_w, dn_a_log, dn_dt_bias, dn_norm_w, s5_a_re, s5_a_im, s5_log_dt, s5_b_re, s5_b_im, s5_c_re, s5_c_im, s5_d, s5_glu_w, s5_glu_b, w_out, mix_norm_post, ff2_norm_pre, ff2_w_gate, ff2_w_up, ff2_w_down, ff2_norm_post, loss_target, m_ff1_norm_pre, m_ff1_w_gate, m_ff1_w_up, m_ff1_w_down, m_ff1_norm_post, m_mix_norm_pre, m_w_in, m_attn_sinks, m_dn_conv_w, m_dn_a_log, m_dn_dt_bias, m_dn_norm_w, m_s5_a_re, m_s5_a_im, m_s5_log_dt, m_s5_b_re, m_s5_b_im, m_s5_c_re, m_s5_c_im, m_s5_d, m_s5_glu_w, m_s5_glu_b, m_w_out, m_mix_norm_post, m_ff2_norm_pre, m_ff2_w_gate, m_ff2_w_up, m_ff2_w_down, m_ff2_norm_post, v_ff1_norm_pre, v_ff1_w_gate, v_ff1_w_up, v_ff1_w_down, v_ff1_norm_post, v_mix_norm_pre, v_w_in, v_attn_sinks, v_dn_conv_w, v_dn_a_log, v_dn_dt_bias, v_dn_norm_w, v_s5_a_re, v_s5_a_im, v_s5_log_dt, v_s5_b_re, v_s5_b_im, v_s5_c_re, v_s5_c_im, v_s5_d, v_s5_glu_w, v_s5_glu_b, v_w_out, v_mix_norm_post, v_ff2_norm_pre, v_ff2_w_gate, v_ff2_w_up, v_ff2_w_down, v_ff2_norm_post):
    args = dict(locals())
    W = {n: args[n] for n in WEIGHTS}
    M = {n: args["m_" + n] for n in WEIGHTS}
    V = {n: args["v_" + n] for n in WEIGHTS}
    depth = ff1_norm_pre.shape[0]
    d_model = x.shape[2]
    me = 4 * lax.axis_index("x") + 2 * lax.axis_index("y") + lax.axis_index("c")

    conv_sh = dn_conv_w.shape[2]

    def small_params(l):
        return {n: W[n][l] for n in SMALL if n != "dn_conv_w"}

    gathered_w, gather_handles, tokens = {}, {}, []
    group_of = {n: i for i, grp in enumerate(GATHER_GROUPS) for n in grp}

    def start_gather(l):
        shards = {n: cast_bf16(W[n], l, f"cast_{n}") for n in BIG}
        shards["dn_conv_w"] = dn_conv_w[l]
        handles, tok = exchange_start([[shards[n] for n in grp] for grp in GATHER_GROUPS], False, f"gather_start_l{l}")
        gather_handles.update({(l, i): h for i, h in enumerate(handles)})
        tokens.append(tok)

    def weight(l, name, after):
        if name == "token":
            return tokens.pop() if tokens else None
        key = "w_in" if name == "w_all" else name
        i = group_of[key]
        if (l, i) in gather_handles:
            got = dict(zip(GATHER_GROUPS[i], exchange_wait(gather_handles.pop((l, i)), False, f"gather_wait_l{l}_g{i}", after)))
            if i == 0 and l + 1 < depth:
                start_gather(l + 1)
            if "w_in" in got:
                got["dn_conv_w"] = jnp.transpose(got["dn_conv_w"], (1, 0, 2)).reshape(DN_CONV, N_DEV * conv_sh)
                got["w_all"] = _win_to_zall(jnp.transpose(got["w_in"], (1, 0, 2)).reshape(d_model, IN_WIDTH))
                got["s5_glu_w"] = got["s5_glu_w"].reshape(S5_WIDTH, S5_WIDTH)
            if "w_out" in got:
                got["w_out"] = got["w_out"].reshape(MIX_WIDTH, d_model)
            gathered_w.update({(l, n): a for n, a in got.items()})
        return gathered_w[(l, name)]

    stacked = {n: None for n in BIG}
    in_flight = []

    def finish_scatter(after):
        l, names, handle = in_flight.pop(0)
        recv = dict(zip(names, exchange_wait(handle, True, f"scatter_wait_l{l}_{names[0]}", after)))
        for n in names:
            stacked[n] = reduce_adamw(recv[n], W[n], M[n], V[n], l, f"adamw_{n}", stacked[n])

    def on_grads(l, grads):
        names = tuple(grads)
        (handle,), tok = exchange_start([[grads[n] for n in names]], True, f"scatter_start_l{l}_{names[0]}")
        in_flight.append((l, names, handle))
        if len(in_flight) > 1:
            finish_scatter(tok)
        return tok

    start_gather(0)
    tokens.clear()
    smalls = [small_params(l) for l in range(depth)]
    loss_vec, dx, small_g = local_step(x[0], loss_target[0], smalls, weight, on_grads)
    shapes = [(depth,) + ((DN_CONV, N_DEV * conv_sh) if n == "dn_conv_w" else tuple(W[n].shape[1:])) for n in SMALL]
    packed = _pack([jnp.stack([small_g[l][n] for l in range(depth)]) for n in SMALL])
    (small_handle,), small_tok = exchange_start([[packed]], False, "gather_small_start")
    while in_flight:
        finish_scatter(small_tok)
    gathered = exchange_wait(small_handle, False, "gather_small_wait", stacked[BIG[0]][0])[0]
    loss = lax.psum(loss_vec[0, 0], ("x", "y", "c"))

    def shard_of(n, full):
        return lax.dynamic_slice_in_dim(full, me * conv_sh, conv_sh, axis=2) if n == "dn_conv_w" else full

    conv_pad = lambda t: jnp.tile(t, (1, 1, N_DEV))
    wp = _pack([conv_pad(W[n]) if n == "dn_conv_w" else W[n] for n in SMALL])
    mp = _pack([conv_pad(M[n]) if n == "dn_conv_w" else M[n] for n in SMALL])
    vp = _pack([conv_pad(V[n]) if n == "dn_conv_w" else V[n] for n in SMALL])
    sm = reduce_adamw(gathered, wp[None], mp[None], vp[None], 0, "adamw_small")
    small_out = [dict(zip(SMALL, [shard_of(n, t) for n, t in zip(SMALL, _unpack(o[0], shapes))])) for o in sm]

    outs = []
    for kind in range(4):
        for n in WEIGHTS:
            if n in BIG:
                outs.append(stacked[n][kind])
            else:
                outs.append(small_out[kind][n])
    return (loss, dx[None], *outs)
```

```python
import functools
import math

import jax
import jax.numpy as jnp
import numpy as np
from jax import lax
from jax.experimental import pallas as pl
from jax.experimental.pallas import tpu as pltpu

F32 = jnp.float32
BF16 = jnp.bfloat16

N_DEV = 8
DEPTH = 4
ATTN_HEADS = 8
ATTN_KV_HEADS = 2
HEAD_DIM = 128
WINDOW = 128
ROPE_THETA = 10000.0
DN_HEADS = 4
DN_HEAD_DIM = 128
DN_CONV = 4
DN_CHUNK = 64
S5_GROUPS = 32
S5_GROUP_CH = 16
S5_STATE = 64
ATTN_WIDTH = ATTN_HEADS * HEAD_DIM
ATTN_KV_WIDTH = ATTN_KV_HEADS * HEAD_DIM
DN_WIDTH = DN_HEADS * DN_HEAD_DIM
S5_WIDTH = S5_GROUPS * S5_GROUP_CH
S5_P = S5_GROUPS * S5_STATE
MIX_WIDTH = ATTN_WIDTH + DN_WIDTH + S5_WIDTH
IN_WIDTH = ATTN_WIDTH + 2 * ATTN_KV_WIDTH + 4 * DN_WIDTH + 2 * DN_HEADS + S5_WIDTH
Z_Q, Z_K, Z_V = 0, ATTN_WIDTH, ATTN_WIDTH + ATTN_KV_WIDTH
Z_DN = ATTN_WIDTH + 2 * ATTN_KV_WIDTH
Z_ZG = Z_DN + 3 * DN_WIDTH
Z_S5 = Z_ZG + DN_WIDTH
Z_SM = Z_S5 + S5_WIDTH
Z_ALL = Z_SM + 128
FFN_RES_WEIGHT = 0.5
NORM_EPS = 1e-6
ADAM_LR, ADAM_B1, ADAM_B2, ADAM_EPS, ADAM_WD, ADAM_STEP = 0.001, 0.9, 0.999, 1e-08, 0.01, 10

VMEM_LIMIT = 56 * 1024 * 1024
NEG = -1e30

NN = (((1,), (0,)), ((), ()))
NT = (((1,), (1,)), ((), ()))
TN = (((0,), (0,)), ((), ()))


def _dot(a, b, dims=NN, prec=None):
    return lax.dot_general(a, b, dims, preferred_element_type=F32, precision=prec)


def _tile(n, pref, mult=8):
    if n <= pref:
        return n
    t = (pref // mult) * mult
    while t > mult and n % t:
        t -= mult
    assert n % t == 0, (n, pref)
    return t


def _call(body, name, grid, in_specs, out_specs, out_shape, scratch=(), sem=None):
    if sem is None:
        sem = ("arbitrary",) * len(grid)
    return pl.pallas_call(
        body, name=name, grid=grid, in_specs=in_specs, out_specs=out_specs, out_shape=out_shape,
        scratch_shapes=list(scratch),
        compiler_params=pltpu.CompilerParams(dimension_semantics=sem, vmem_limit_bytes=VMEM_LIMIT))


def _sds(shape, dtype):
    return jax.ShapeDtypeStruct(tuple(shape), dtype)


def _sigmoid(x):
    return 1.0 / (1.0 + jnp.exp(-x))


def _silu_and_grad(a):
    sg = _sigmoid(a)
    return a * sg, sg * (1.0 + a * (1.0 - sg))


def _softplus(x):
    return jnp.maximum(x, 0.0) + jnp.log(1.0 + jnp.exp(-jnp.abs(x)))


def _rms(x, g):
    r = lax.rsqrt(jnp.mean(x * x, axis=-1, keepdims=True) + NORM_EPS)
    return x * r * g


def _rms_bwd(dout, y, g):
    r = lax.rsqrt(jnp.mean(y * y, axis=-1, keepdims=True) + NORM_EPS)
    n = y * r
    dn = dout * g
    dy = r * (dn - n * jnp.mean(dn * n, axis=-1, keepdims=True))
    return dy, jnp.sum(dout * n, axis=0, keepdims=True)


def _rope(x, cos_f, sin_s):
    return x * cos_f + pltpu.roll(x, HEAD_DIM // 2, 1) * sin_s


def _rope_bwd(d, cos_f, sin_s):
    return d * cos_f + pltpu.roll(d * sin_s, HEAD_DIM // 2, 1)


def rmsnorm_fwd(x, g, name):
    s, d = x.shape
    tm = _tile(s, 512)

    def body(x_ref, g_ref, o_ref):
        o_ref[...] = _rms(x_ref[...], g_ref[...]).astype(BF16)

    return _call(body, name, (s // tm,),
                 [pl.BlockSpec((tm, d), lambda i: (i, 0)), pl.BlockSpec((1, d), lambda i: (0, 0))],
                 pl.BlockSpec((tm, d), lambda i: (i, 0)), _sds((s, d), BF16), sem=("parallel",))(x, g)


def norm_bwd(dout, y, g, scale, resid, out_dtype, name):
    s, d = y.shape
    tm = _tile(s, 256)
    has_res = resid is not None

    def body(*refs):
        if has_res:
            do_ref, y_ref, g_ref, r_ref, dy_ref, dg_ref = refs
        else:
            do_ref, y_ref, g_ref, dy_ref, dg_ref = refs
        dy, dg = _rms_bwd(do_ref[...] * scale, y_ref[...], g_ref[...])
        if has_res:
            dy = dy + r_ref[...]
        dy_ref[...] = dy.astype(out_dtype)

        @pl.when(pl.program_id(0) == 0)
        def _():
            dg_ref[...] = jnp.zeros_like(dg_ref)

        dg_ref[...] += dg

    row = pl.BlockSpec((tm, d), lambda i: (i, 0))
    vec = pl.BlockSpec((1, d), lambda i: (0, 0))
    ins = [dout, y, g] + ([resid] if has_res else [])
    return _call(body, name, (s // tm,), [row, row, vec] + ([row] if has_res else []),
                 [row, vec], [_sds((s, d), out_dtype), _sds((1, d), F32)])(*ins)


def ffn_up(h, wg, wu, name, dep=None):
    s, d = h.shape
    nj, _, fs = wg.shape
    tm = _tile(s, 1024)

    def body(h_ref, wg_ref, wu_ref, *rest):
        a_ref, b_ref, u_ref = rest[-3:]
        hh = h_ref[...]
        a = _dot(hh, wg_ref[...])
        b = _dot(hh, wu_ref[...])
        a_ref[...] = a.astype(BF16)
        b_ref[...] = b.astype(BF16)
        u_ref[...] = (a * _sigmoid(a) * b).astype(BF16)

    wspec = pl.BlockSpec((None, d, fs), lambda j, i: (j, 0, 0))
    ospec = pl.BlockSpec((None, tm, fs), lambda j, i: (j, i, 0))
    osd = _sds((nj, s, fs), BF16)
    ins, specs = [h, wg, wu], [pl.BlockSpec((tm, d), lambda j, i: (i, 0)), wspec, wspec]
    if dep is not None:
        ins.append(dep)
        specs.append(pl.BlockSpec((8, 128), lambda j, i: (0, 0)))
    return _call(body, name, (nj, s // tm), specs, [ospec, ospec, ospec], [osd, osd, osd], sem=("parallel", "parallel"))(*ins)


def down_norm(u3, w3, x, g, scale, name):
    nj, s, k = u3.shape
    d = w3.shape[2]
    tm = _tile(s, 512)
    jb = 2 if nj % 2 == 0 else 1
    nsteps = nj // jb

    def body(u_ref, w_ref, x_ref, g_ref, y_ref, xn_ref):
        j = pl.program_id(1)
        t = _dot(u_ref[0], w_ref[0])
        for q in range(1, jb):
            t = t + _dot(u_ref[q], w_ref[q])

        @pl.when(j == 0)
        def _():
            y_ref[...] = t

        @pl.when(j > 0)
        def _():
            y_ref[...] += t

        @pl.when(j == nsteps - 1)
        def _():
            xn_ref[...] = x_ref[...] + scale * _rms(y_ref[...], g_ref[...])

    row = pl.BlockSpec((tm, d), lambda i, j: (i, 0))
    return _call(body, name, (s // tm, nsteps),
                 [pl.BlockSpec((jb, tm, k), lambda i, j: (j, i, 0)), pl.BlockSpec((jb, k, d), lambda i, j: (j, 0, 0)),
                  row, pl.BlockSpec((1, d), lambda i, j: (0, 0))],
                 [row, row], [_sds((s, d), F32), _sds((s, d), F32)], sem=("parallel", "arbitrary"))(u3, w3, x, g)


def ffn_down_bwd(dy, wd, a, b, name):
    nj, fs, d = wd.shape
    s = dy.shape[0]
    tm = _tile(s, 1024)

    def body(dy_ref, w_ref, a_ref, b_ref, da_ref, db_ref):
        du = _dot(dy_ref[...], w_ref[...], NT)
        aa = a_ref[...].astype(F32)
        bb = b_ref[...].astype(F32)
        sl, dsl = _silu_and_grad(aa)
        da_ref[...] = (du * bb * dsl).astype(BF16)
        db_ref[...] = (du * sl).astype(BF16)

    hspec = pl.BlockSpec((None, tm, fs), lambda j, i: (j, i, 0))
    osd = _sds((nj, s, fs), BF16)
    return _call(body, name, (nj, s // tm),
                 [pl.BlockSpec((tm, d), lambda j, i: (i, 0)), pl.BlockSpec((None, fs, d), lambda j, i: (j, 0, 0)), hspec, hspec],
                 [hspec, hspec], [osd, osd], sem=("parallel", "parallel"))(dy, wd, a, b)


def mm_tn(a3, b3, out_dtype, name, tmm=2048, tn=1408, tk=1024):
    ja, s, m = a3.shape
    jb, _, n = b3.shape
    nj = max(ja, jb)
    tmm, tn, tk = _tile(m, tmm, 128), _tile(n, tn, 128), _tile(s, tk)
    nk = s // tk

    def body(a_ref, b_ref, o_ref, acc_ref):
        k = pl.program_id(3)

        @pl.when(k == 0)
        def _():
            acc_ref[...] = jnp.zeros_like(acc_ref)

        acc_ref[...] += _dot(a_ref[...].astype(BF16), b_ref[...].astype(BF16), TN)

        @pl.when(k == nk - 1)
        def _():
            o_ref[...] = acc_ref[...].astype(out_dtype)

    aj = (lambda j: j) if ja > 1 else (lambda j: 0)
    bj = (lambda j: j) if jb > 1 else (lambda j: 0)
    return _call(body, name, (nj, m // tmm, n // tn, nk),
                 [pl.BlockSpec((None, tk, tmm), lambda j, im, jn, k: (aj(j), k, im)),
                  pl.BlockSpec((None, tk, tn), lambda j, im, jn, k: (bj(j), k, jn))],
                 pl.BlockSpec((None, tmm, tn), lambda j, im, jn, k: (j, im, jn)), _sds((nj, m, n), out_dtype),
                 scratch=[pltpu.VMEM((tmm, tn), F32)],
                 sem=("parallel", "parallel", "parallel", "arbitrary"))(a3, b3)


def mm_nt_acc(pairs, name, tm=512, tn=512, dep=None):
    nj, s, _ = pairs[0][0].shape
    n = pairs[0][1].shape[1]
    tm, tn = _tile(s, tm), _tile(n, tn, 128)
    npair = len(pairs)

    def body(*refs):
        o_ref = refs[-1]
        j = pl.program_id(2)
        t = None
        for p in range(npair):
            c = _dot(refs[2 * p][...].astype(BF16), refs[2 * p + 1][...], NT)
            t = c if t is None else t + c
        if nj == 1:
            o_ref[...] = t
        else:
            @pl.when(j == 0)
            def _():
                o_ref[...] = t

            @pl.when(j > 0)
            def _():
                o_ref[...] += t

    ins, specs = [], []
    for a3, w3 in pairs:
        k = a3.shape[2]
        ins += [a3, w3]
        specs += [pl.BlockSpec((None, tm, k), lambda i, jn, j: (j, i, 0)),
                  pl.BlockSpec((None, tn, k), lambda i, jn, j: (j, jn, 0))]
    if dep is not None:
        ins.append(dep)
        specs.append(pl.BlockSpec((8, 128), lambda i, jn, j: (0, 0)))
    return _call(body, name, (s // tm, n // tn, nj), specs,
                 pl.BlockSpec((tm, tn), lambda i, jn, j: (i, jn)), _sds((s, n), F32),
                 sem=("parallel", "parallel", "arbitrary"))(*ins)


def mm_nn(a, w, name, tm=512, tn=1408, col0=0, kdim=None):
    s = a.shape[0]
    k, n = w.shape
    assert col0 % k == 0
    tm, tn = _tile(s, tm), _tile(n, tn, 128)
    cb = col0 // k

    def body(a_ref, w_ref, o_ref):
        o_ref[...] = _dot(a_ref[...].astype(BF16), w_ref[...])

    return _call(body, name, (n // tn, s // tm),
                 [pl.BlockSpec((tm, k), lambda jn, i: (i, cb)), pl.BlockSpec((k, tn), lambda jn, i: (0, jn))],
                 pl.BlockSpec((tm, tn), lambda jn, i: (i, jn)), _sds((s, n), F32), sem=("parallel", "parallel"))(a, w)


def mm_nn_acc(a3, w3, name, tm=256):
    nj, s, k = a3.shape
    n = w3.shape[2]
    tm = _tile(s, tm)

    def body(a_ref, w_ref, o_ref):
        j = pl.program_id(1)
        t = _dot(a_ref[...].astype(BF16), w_ref[...])

        @pl.when(j == 0)
        def _():
            o_ref[...] = t

        @pl.when(j > 0)
        def _():
            o_ref[...] += t

    return _call(body, name, (s // tm, nj),
                 [pl.BlockSpec((None, tm, k), lambda i, j: (j, i, 0)), pl.BlockSpec((None, k, n), lambda i, j: (j, 0, 0))],
                 pl.BlockSpec((tm, n), lambda i, j: (i, 0)), _sds((s, n), F32), sem=("parallel", "arbitrary"))(a3, w3)


def loss_and_grad(xl, target, name):
    s, d = xl.shape
    tm = _tile(s, 512)

    def body(x_ref, t_ref, l_ref, dx_ref):
        e = x_ref[...] - t_ref[...]
        dx_ref[...] = e * (1.0 / d)

        @pl.when(pl.program_id(0) == 0)
        def _():
            l_ref[...] = jnp.zeros_like(l_ref)

        part = jnp.sum(jnp.sum(e * e, axis=-1, keepdims=True), axis=0, keepdims=True) * (0.5 / d)
        l_ref[...] += jnp.broadcast_to(part, l_ref.shape)

    row = pl.BlockSpec((tm, d), lambda i: (i, 0))
    return _call(body, name, (s // tm,), [row, row], [pl.BlockSpec((1, 128), lambda i: (0, 0)), row],
                 [_sds((1, 128), F32), _sds((s, d), F32)])(xl, target)


def cast_bf16(w4, layer, name):
    _, r, c = w4.shape
    tr = _tile(r, 512)

    def body(w_ref, o_ref):
        o_ref[...] = w_ref[...].astype(BF16)

    return _call(body, name, (r // tr,), [pl.BlockSpec((None, tr, c), lambda i: (layer, i, 0))],
                 pl.BlockSpec((tr, c), lambda i: (i, 0)), _sds((r, c), BF16), sem=("parallel",))(w4)


def _attn_specs(nb):
    w = WINDOW
    prev = lambda i: jnp.maximum(i - 1, 0)
    q = pl.BlockSpec((w, ATTN_WIDTH), lambda i: (i, 0))
    kc = pl.BlockSpec((w, ATTN_KV_WIDTH), lambda i: (i, Z_K // ATTN_KV_WIDTH))
    kp = pl.BlockSpec((w, ATTN_KV_WIDTH), lambda i: (prev(i), Z_K // ATTN_KV_WIDTH))
    vc = pl.BlockSpec((w, ATTN_KV_WIDTH), lambda i: (i, Z_V // ATTN_KV_WIDTH))
    vp = pl.BlockSpec((w, ATTN_KV_WIDTH), lambda i: (prev(i), Z_V // ATTN_KV_WIDTH))
    tc = pl.BlockSpec((w, HEAD_DIM), lambda i: (i, 0))
    tp = pl.BlockSpec((w, HEAD_DIM), lambda i: (prev(i), 0))
    sink = pl.BlockSpec(memory_space=pltpu.SMEM)
    return [q, kc, kp, vc, vp, tc, tc, tp, tp, sink]


def _attn_mask(i):
    w = WINDOW
    qi = lax.broadcasted_iota(jnp.int32, (w, 2 * w), 0) + w
    kj = lax.broadcasted_iota(jnp.int32, (w, 2 * w), 1)
    rel = qi - kj
    band = (rel >= 0) & (rel < w)
    return band & jnp.logical_not((i == 0) & (kj < w))


def _attn_probs(qs, kk, sinks, mask):
    n = range(len(qs))
    s = [_dot(qs[i], kk, NT) * (HEAD_DIM ** -0.5) for i in n]
    s = [jnp.where(mask, s[i], NEG) for i in n]
    m = [jnp.maximum(jnp.max(s[i], axis=-1, keepdims=True), sinks[i]) for i in n]
    p = [jnp.exp(s[i] - m[i]) for i in n]
    es = [jnp.exp(sinks[i] - m[i]) for i in n]
    inv = [1.0 / (jnp.sum(p[i], axis=-1, keepdims=True) + es[i]) for i in n]
    return [p[i] * inv[i] for i in n], [es[i] * inv[i] for i in n]


def attn_fwd(z, cos_f, sin_s, sinks, name):
    s = z.shape[0]
    nb = s // WINDOW
    hd = HEAD_DIM
    grp = ATTN_HEADS // ATTN_KV_HEADS

    def body(q_ref, kc_ref, kp_ref, vc_ref, vp_ref, cc_ref, sc_ref, cp_ref, sp_ref, sink_ref, o_ref):
        i = pl.program_id(0)
        mask = _attn_mask(i)
        cc, sc, cp, sp = cc_ref[...], sc_ref[...], cp_ref[...], sp_ref[...]
        for kv in range(ATTN_KV_HEADS):
            ksl = slice(kv * hd, (kv + 1) * hd)
            kk = jnp.concatenate([_rope(kp_ref[:, ksl], cp, sp), _rope(kc_ref[:, ksl], cc, sc)], axis=0).astype(BF16)
            vv = jnp.concatenate([vp_ref[:, ksl], vc_ref[:, ksl]], axis=0).astype(BF16)
            heads = [kv * grp + g for g in range(grp)]
            hsl = [slice(h * hd, (h + 1) * hd) for h in heads]
            q = [_rope(q_ref[:, sl], cc, sc).astype(BF16) for sl in hsl]
            pn, _ = _attn_probs(q, kk, [sink_ref[h] for h in heads], mask)
            out = [_dot(pn[i].astype(BF16), vv).astype(BF16) for i in range(grp)]
            for i in range(grp):
                o_ref[:, hsl[i]] = out[i]

    return _call(body, name, (nb,), _attn_specs(nb), pl.BlockSpec((WINDOW, ATTN_WIDTH), lambda i: (i, 0)),
                 _sds((s, ATTN_WIDTH), BF16), sem=("parallel",))(z, z, z, z, z, cos_f, sin_s, cos_f, sin_s, sinks)


def attn_bwd(z, cos_f, sin_s, sinks, dy, name):
    s = z.shape[0]
    nb = s // WINDOW
    hd = HEAD_DIM
    grp = ATTN_HEADS // ATTN_KV_HEADS
    scale = HEAD_DIM ** -0.5

    def body(q_ref, kc_ref, kp_ref, vc_ref, vp_ref, cc_ref, sc_ref, cp_ref, sp_ref, sink_ref, dy_ref,
             dq_ref, dkc_ref, dkp_ref, dvc_ref, dvp_ref, ds_ref):
        i = pl.program_id(0)
        mask = _attn_mask(i)
        cc, sc, cp, sp = cc_ref[...], sc_ref[...], cp_ref[...], sp_ref[...]

        @pl.when(i == 0)
        def _():
            ds_ref[...] = jnp.zeros_like(ds_ref)

        for kv in range(ATTN_KV_HEADS):
            ksl = slice(kv * hd, (kv + 1) * hd)
            kk = jnp.concatenate([_rope(kp_ref[:, ksl], cp, sp), _rope(kc_ref[:, ksl], cc, sc)], axis=0).astype(BF16)
            vv = jnp.concatenate([vp_ref[:, ksl], vc_ref[:, ksl]], axis=0).astype(BF16)
            heads = [kv * grp + g for g in range(grp)]
            n = range(grp)
            hsl = [slice(h * hd, (h + 1) * hd) for h in heads]
            q = [_rope(q_ref[:, sl], cc, sc).astype(BF16) for sl in hsl]
            pn, psink = _attn_probs(q, kk, [sink_ref[h] for h in heads], mask)
            do = [dy_ref[:, sl].astype(BF16) for sl in hsl]
            dpn = [_dot(do[i], vv, NT) for i in n]
            dvs = [_dot(pn[i].astype(BF16), do[i], TN) for i in n]
            tot = [jnp.sum(pn[i] * dpn[i], axis=-1, keepdims=True) for i in n]
            dsc = [(pn[i] * (dpn[i] - tot[i]) * scale).astype(BF16) for i in n]
            dqs = [_rope_bwd(_dot(dsc[i], kk), cc, sc) for i in n]
            dks = [_dot(dsc[i], q[i], TN) for i in n]
            dvv = (dvs[0] + dvs[1]) + (dvs[2] + dvs[3]) if grp == 4 else sum(dvs[1:], dvs[0])
            dkk = (dks[0] + dks[1]) + (dks[2] + dks[3]) if grp == 4 else sum(dks[1:], dks[0])
            for i in n:
                dq_ref[:, hsl[i]] = dqs[i]
                dsink = jnp.sum(-psink[i] * tot[i], axis=0, keepdims=True)
                ds_ref[heads[i]:heads[i] + 1, :] += jnp.broadcast_to(dsink, (1, 128))
            dkp_ref[:, ksl] = _rope_bwd(dkk[:WINDOW], cp, sp)
            dkc_ref[:, ksl] = _rope_bwd(dkk[WINDOW:], cc, sc)
            dvp_ref[:, ksl] = dvv[:WINDOW]
            dvc_ref[:, ksl] = dvv[WINDOW:]

    kvo = pl.BlockSpec((WINDOW, ATTN_KV_WIDTH), lambda i: (i, 0))
    kvs = _sds((s, ATTN_KV_WIDTH), F32)
    return _call(body, name, (nb,), _attn_specs(nb) + [pl.BlockSpec((WINDOW, ATTN_WIDTH), lambda i: (i, 0))],
                 [pl.BlockSpec((WINDOW, ATTN_WIDTH), lambda i: (i, 0)), kvo, kvo, kvo, kvo,
                  pl.BlockSpec((ATTN_HEADS, 128), lambda i: (0, 0))],
                 [_sds((s, ATTN_WIDTH), F32), kvs, kvs, kvs, kvs, _sds((ATTN_HEADS, 128), F32)])(
        z, z, z, z, z, cos_f, sin_s, cos_f, sin_s, sinks, dy)


def _shift_down(x, d, row):
    return jnp.where(row >= d, pltpu.roll(x, d, 0), 0.0)


def _shift_up(x, d, row, n):
    return jnp.where(row < n - d, pltpu.roll(x, n - d, 0), 0.0)


def _conv_taps(u, w_ref, row):
    c = w_ref[DN_CONV - 1:DN_CONV, :] * u
    for k in range(DN_CONV - 1):
        c = c + w_ref[k:k + 1, :] * _shift_down(u, DN_CONV - 1 - k, row)
    return c


def dn_pre_fwd(z, conv_w, name):
    s = z.shape[0]
    nblk = 3 * DN_WIDTH // 128
    nqk = 2 * DN_WIDTH // 128

    def body(u_ref, w_ref, o_ref):
        row = lax.broadcasted_iota(jnp.int32, (s, 128), 0)
        c = _conv_taps(u_ref[...], w_ref, row)
        sl = c * _sigmoid(c)
        j = pl.program_id(0)

        @pl.when(j < nqk)
        def _():
            o_ref[...] = sl * lax.rsqrt(jnp.sum(sl * sl, axis=-1, keepdims=True) + NORM_EPS)

        @pl.when(j >= nqk)
        def _():
            o_ref[...] = sl

    return _call(body, name, (nblk,),
                 [pl.BlockSpec((s, 128), lambda j: (0, Z_DN // 128 + j)), pl.BlockSpec((DN_CONV, 128), lambda j: (0, j))],
                 pl.BlockSpec((s, 128), lambda j: (0, j)), _sds((s, 3 * DN_WIDTH), F32), sem=("parallel",))(z, conv_w)


def dn_pre_bwd(z, conv_w, dout, name):
    s = z.shape[0]
    nblk = 3 * DN_WIDTH // 128
    nqk = 2 * DN_WIDTH // 128

    def body(u_ref, w_ref, do_ref, du_ref, dw_ref, ds_ref):
        row = lax.broadcasted_iota(jnp.int32, (s, 128), 0)
        u = u_ref[...]
        c = _conv_taps(u, w_ref, row)
        sl, dsl = _silu_and_grad(c)
        do = do_ref[...]
        j = pl.program_id(0)

        @pl.when(j < nqk)
        def _():
            r = lax.rsqrt(jnp.sum(sl * sl, axis=-1, keepdims=True) + NORM_EPS)
            ds_ref[...] = r * do - sl * (r * r * r) * jnp.sum(do * sl, axis=-1, keepdims=True)

        @pl.when(j >= nqk)
        def _():
            ds_ref[...] = do

        dc = ds_ref[...] * dsl
        du = w_ref[DN_CONV - 1:DN_CONV, :] * dc
        dw_ref[DN_CONV - 1:DN_CONV, :] = jnp.sum(dc * u, axis=0, keepdims=True)
        for k in range(DN_CONV - 1):
            d = DN_CONV - 1 - k
            du = du + w_ref[k:k + 1, :] * _shift_up(dc, d, row, s)
            dw_ref[k:k + 1, :] = jnp.sum(dc * _shift_down(u, d, row), axis=0, keepdims=True)
        du_ref[...] = du

    blk = pl.BlockSpec((s, 128), lambda j: (0, j))
    wsp = pl.BlockSpec((DN_CONV, 128), lambda j: (0, j))
    return _call(body, name, (nblk,), [pl.BlockSpec((s, 128), lambda j: (0, Z_DN // 128 + j)), wsp, blk],
                 [blk, wsp], [_sds((s, 3 * DN_WIDTH), F32), _sds((DN_CONV, 3 * DN_WIDTH), F32)],
                 scratch=[pltpu.VMEM((s, 128), F32)], sem=("parallel",))(z, conv_w, dout)


def _lane_col(x, lane, idx):
    return jnp.sum(jnp.where(lane == idx, x, 0.0), axis=-1, keepdims=True)


def dn_gates_fwd(z, alog_b, dtb_b, name):
    s = z.shape[0]
    tm = _tile(s, 512)

    def body(zs_ref, al_ref, dt_ref, beta_ref, g_ref):
        zs = zs_ref[...]
        lane = lax.broadcasted_iota(jnp.int32, zs.shape, 1)
        for h in range(DN_HEADS):
            b_raw = _lane_col(zs, lane, h)
            a_raw = _lane_col(zs, lane, DN_HEADS + h)
            beta_ref[h] = jnp.broadcast_to(_sigmoid(b_raw), (tm, 128))
            g_ref[h] = -jnp.exp(al_ref[h:h + 1, :]) * _softplus(a_raw + dt_ref[h:h + 1, :])

    osp = pl.BlockSpec((DN_HEADS, tm, 128), lambda i: (0, i, 0))
    psp = pl.BlockSpec((DN_HEADS, 128), lambda i: (0, 0))
    osd = _sds((DN_HEADS, s, 128), F32)
    return _call(body, name, (s // tm,), [pl.BlockSpec((tm, 128), lambda i: (i, Z_SM // 128)), psp, psp],
                 [osp, osp], [osd, osd], sem=("parallel",))(z, alog_b, dtb_b)


def dn_gates_bwd(z, alog_b, dtb_b, dbeta, dg, name):
    s = z.shape[0]
    tm = _tile(s, 512)

    def body(zs_ref, al_ref, dt_ref, dbeta_ref, dg_ref, dz_ref, dal_ref, ddt_ref):
        @pl.when(pl.program_id(0) == 0)
        def _():
            dal_ref[...] = jnp.zeros_like(dal_ref)
            ddt_ref[...] = jnp.zeros_like(ddt_ref)

        zs = zs_ref[...]
        lane = lax.broadcasted_iota(jnp.int32, zs.shape, 1)
        dz = jnp.zeros_like(zs)
        for h in range(DN_HEADS):
            b_raw = _lane_col(zs, lane, h)
            a_raw = _lane_col(zs, lane, DN_HEADS + h)
            dbe = jnp.sum(dbeta_ref[h], axis=-1, keepdims=True)
            dgg = jnp.sum(dg_ref[h], axis=-1, keepdims=True)
            beta = _sigmoid(b_raw)
            ea = jnp.exp(al_ref[h:h + 1, :])
            pre = a_raw + dt_ref[h:h + 1, :]
            da_raw = dgg * (-ea) * _sigmoid(pre)
            dz = dz + jnp.where(lane == h, dbe * beta * (1.0 - beta), 0.0) + jnp.where(lane == DN_HEADS + h, da_raw, 0.0)
            ddt_ref[h:h + 1, :] += jnp.sum(da_raw, axis=0, keepdims=True)
            dal_ref[h:h + 1, :] += jnp.sum(dgg * (-ea) * _softplus(pre), axis=0, keepdims=True)
        dz_ref[...] = dz

    hsp = pl.BlockSpec((DN_HEADS, tm, 128), lambda i: (0, i, 0))
    psp = pl.BlockSpec((DN_HEADS, 128), lambda i: (0, 0))
    return _call(body, name, (s // tm,), [pl.BlockSpec((tm, 128), lambda i: (i, Z_SM // 128)), psp, psp, hsp, hsp],
                 [pl.BlockSpec((tm, 128), lambda i: (i, 0)), psp, psp],
                 [_sds((s, 128), F32), _sds((DN_HEADS, 128), F32), _sds((DN_HEADS, 128), F32)])(z, alog_b, dtb_b, dbeta, dg)


def _dn_intra(q, k, v, gb, bb):
    c = DN_CHUNK
    pairs = range(len(q))
    ri = lax.broadcasted_iota(jnp.int32, (c, c), 0)
    ci = lax.broadcasted_iota(jnp.int32, (c, c), 1)
    causal = ri >= ci
    strict = ri > ci
    gc = [_chunk_cumsum(gb[i]) for i in pairs]
    grow = [_as_row(gc[i]) for i in pairs]
    decay = [jnp.where(causal, jnp.exp(jnp.where(causal, gc[i][:, :c] - grow[i], 0.0)), 0.0) for i in pairs]
    qs = [q[i] * (DN_HEAD_DIM ** -0.5) for i in pairs]
    kb = [k[i] * bb[i] for i in pairs]
    lower = [jnp.where(strict, _dot(kb[i], k[i], NT) * decay[i], 0.0) for i in pairs]
    t = _unit_lower_inverse(lower)
    eg = [jnp.exp(gc[i]) for i in pairs]
    u = [_dot(t[i], v[i] * bb[i]) for i in pairs]
    w = [_dot(t[i], kb[i] * eg[i]) for i in pairs]
    attn = [jnp.where(causal, _dot(qs[i], k[i], NT) * decay[i], 0.0) for i in pairs]
    glast = [_last_row(gc[i]) for i in pairs]
    return (u, w, [qs[i] * eg[i] for i in pairs], [k[i] * jnp.exp(glast[i] - gc[i]) for i in pairs], attn,
            [jnp.exp(glast[i][:8]) for i in pairs])


def _split3(x):
    x1 = x.astype(BF16)
    r = x - x1.astype(F32)
    x2 = r.astype(BF16)
    return x1, x2, (r - x2.astype(F32)).astype(BF16)


def _dot_split(a, b, dims=NN):
    a1, a2, _ = _split3(a)
    b1, b2, _ = _split3(b)
    return _dot(a1, b1, dims) + (_dot(a1, b2, dims) + _dot(a2, b1, dims))


def _sel_dot(m01, x, dims, m_left):
    m = m01.astype(BF16)
    parts = [_dot(m, xi, dims) if m_left else _dot(xi, m, dims) for xi in _split3(x)]
    return parts[0] + (parts[1] + parts[2])


def _tri_mask(n, upper):
    ri = lax.broadcasted_iota(jnp.int32, (n, n), 0)
    ci = lax.broadcasted_iota(jnp.int32, (n, n), 1)
    return ci >= ri if upper else ri >= ci


@jax.custom_vjp
def _chunk_cumsum(x):
    return _sel_dot(_tri_mask(x.shape[0], False), x, NN, True)


_chunk_cumsum.defvjp(lambda x: (_chunk_cumsum(x), None),
                     lambda _, ct: (_sel_dot(_tri_mask(ct.shape[0], True), ct, NN, True),))


def _lane0(rows):
    return lax.broadcasted_iota(jnp.int32, (rows, 128), 1) == 0


@jax.custom_vjp
def _as_row(x):
    return _sel_dot(_lane0(x.shape[0]), x, NT, True)


_as_row.defvjp(lambda x: (_as_row(x), None),
               lambda _, ct: (_sel_dot(_lane0(ct.shape[0]), ct, TN, False),))


def _last_col_mask(n, transpose):
    idx = lax.broadcasted_iota(jnp.int32, (n, n), 0 if transpose else 1)
    return idx == n - 1


@jax.custom_vjp
def _last_row(x):
    return _sel_dot(_last_col_mask(x.shape[0], False), x, NN, True)


_last_row.defvjp(lambda x: (_last_row(x), None),
                 lambda _, ct: (_sel_dot(_last_col_mask(ct.shape[0], True), ct, NN, True),))


@jax.custom_vjp
def _unit_lower_inverse(lowers):
    n = lowers[0].shape[0]
    eye = (lax.broadcasted_iota(jnp.int32, (n, n), 0) == lax.broadcasted_iota(jnp.int32, (n, n), 1)).astype(F32)
    ts = [eye - l for l in lowers]
    ps = list(lowers)
    for _ in range(5):
        ps = [_dot_split(p, p) for p in ps]
        ts = [t + _dot_split(t, p) for t, p in zip(ts, ps)]
    return ts


def _unit_lower_inverse_fwd(lowers):
    ts = _unit_lower_inverse(lowers)
    return ts, ts


def _unit_lower_inverse_bwd(ts, dts):
    half = [_dot_split(t, dt, TN) for t, dt in zip(ts, dts)]
    return ([-_dot_split(h, t, NT) for h, t in zip(half, ts)],)


_unit_lower_inverse.defvjp(_unit_lower_inverse_fwd, _unit_lower_inverse_bwd)


def _dn_step(st, qd, kd, u, w, attn, egl):
    heads = range(len(st))
    v_new = [u[h] - _dot(w[h], st[h]) for h in heads]
    o = [_dot(qd[h], st[h]) for h in heads]
    o = [o[h] + _dot(attn[h], v_new[h]) for h in heads]
    st_new = [st[h] * egl[h][0:1, :] + _dot(kd[h], v_new[h], TN) for h in heads]
    return o, st_new


def _dn_chunk_specs(m=1):
    c = DN_CHUNK
    wide = pl.BlockSpec((m * c, DN_WIDTH), lambda i: (i, 0))
    att = pl.BlockSpec((m * c, DN_HEADS * c), lambda i: (i, 0))
    egl = pl.BlockSpec((m * 8, DN_WIDTH), lambda i: (i, 0))
    return wide, att, egl


def _dn_intra_chunks(nc):
    return 4 if nc % 4 == 0 else (2 if nc % 2 == 0 else 1)


def dn_intra_fwd(qkv, gb, bb, name):
    s = qkv.shape[0]
    c, hd = DN_CHUNK, DN_HEAD_DIM
    nc = s // c
    m = _dn_intra_chunks(nc)

    def body(q_ref, k_ref, v_ref, g_ref, b_ref, u_ref, w_ref, qd_ref, kd_ref, at_ref, eg_ref):
        pairs = [(t, h) for t in range(m) for h in range(DN_HEADS)]
        rs = lambda t: slice(t * c, (t + 1) * c)
        hs = lambda h: slice(h * hd, (h + 1) * hd)
        u, w, qd, kd, at, eg = _dn_intra([q_ref[rs(t), hs(h)] for t, h in pairs], [k_ref[rs(t), hs(h)] for t, h in pairs],
                                         [v_ref[rs(t), hs(h)] for t, h in pairs], [g_ref[h, rs(t)] for t, h in pairs],
                                         [b_ref[h, rs(t)] for t, h in pairs])
        for i, (t, h) in enumerate(pairs):
            u_ref[rs(t), hs(h)], w_ref[rs(t), hs(h)], qd_ref[rs(t), hs(h)], kd_ref[rs(t), hs(h)] = u[i], w[i], qd[i], kd[i]
            at_ref[rs(t), h * c:(h + 1) * c] = at[i]
            eg_ref[t * 8:(t + 1) * 8, hs(h)] = eg[i]

    wide, att, egl = _dn_chunk_specs(m)
    hsp = pl.BlockSpec((DN_HEADS, m * c, 128), lambda i: (0, i, 0))
    wsd = _sds((s, DN_WIDTH), F32)
    return _call(body, name, (nc // m,),
                 [pl.BlockSpec((m * c, DN_WIDTH), lambda i: (i, 0)), pl.BlockSpec((m * c, DN_WIDTH), lambda i: (i, 1)),
                  pl.BlockSpec((m * c, DN_WIDTH), lambda i: (i, 2)), hsp, hsp],
                 [wide, wide, wide, wide, att, egl],
                 [wsd, wsd, wsd, wsd, _sds((s, DN_HEADS * c), F32), _sds((nc * 8, DN_WIDTH), F32)],
                 sem=("parallel",))(qkv, qkv, qkv, gb, bb)


def dn_intra_bwd(qkv, gb, bb, cts, name):
    s = qkv.shape[0]
    c, hd = DN_CHUNK, DN_HEAD_DIM
    nc = s // c
    m = _dn_intra_chunks(nc)

    def body(q_ref, k_ref, v_ref, g_ref, b_ref, du_ref, dw_ref, dqd_ref, dkd_ref, dat_ref, deg_ref,
             dq_ref, dk_ref, dv_ref, dg_ref, db_ref):
        pairs = [(t, h) for t in range(m) for h in range(DN_HEADS)]
        rs = lambda t: slice(t * c, (t + 1) * c)
        hs = lambda h: slice(h * hd, (h + 1) * hd)
        _, vjp = jax.vjp(_dn_intra, [q_ref[rs(t), hs(h)] for t, h in pairs], [k_ref[rs(t), hs(h)] for t, h in pairs],
                         [v_ref[rs(t), hs(h)] for t, h in pairs], [g_ref[h, rs(t)] for t, h in pairs],
                         [b_ref[h, rs(t)] for t, h in pairs])
        dq, dk, dv, dg, db = vjp(([du_ref[rs(t), hs(h)] for t, h in pairs], [dw_ref[rs(t), hs(h)] for t, h in pairs],
                                  [dqd_ref[rs(t), hs(h)] for t, h in pairs], [dkd_ref[rs(t), hs(h)] for t, h in pairs],
                                  [dat_ref[rs(t), h * c:(h + 1) * c] for t, h in pairs],
                                  [deg_ref[t * 8:(t + 1) * 8, hs(h)] for t, h in pairs]))
        for i, (t, h) in enumerate(pairs):
            dq_ref[rs(t), hs(h)], dk_ref[rs(t), hs(h)], dv_ref[rs(t), hs(h)] = dq[i], dk[i], dv[i]
            dg_ref[h, rs(t)] = dg[i]
            db_ref[h, rs(t)] = db[i]

    wide, att, egl = _dn_chunk_specs(m)
    hsp = pl.BlockSpec((DN_HEADS, m * c, 128), lambda i: (0, i, 0))
    hsd = _sds((DN_HEADS, s, 128), F32)
    wsd = _sds((s, DN_WIDTH), F32)
    return _call(body, name, (nc // m,),
                 [pl.BlockSpec((m * c, DN_WIDTH), lambda i: (i, 0)), pl.BlockSpec((m * c, DN_WIDTH), lambda i: (i, 1)),
                  pl.BlockSpec((m * c, DN_WIDTH), lambda i: (i, 2)), hsp, hsp, wide, wide, wide, wide, att, egl],
                 [wide, wide, wide, hsp, hsp], [wsd, wsd, wsd, hsd, hsd], sem=("parallel",))(qkv, qkv, qkv, gb, bb, *cts)


def dn_scan_fwd(u, w, qd, kd, at, eg, name):
    s = u.shape[0]
    c, hd = DN_CHUNK, DN_HEAD_DIM
    nc = s // c

    def body(u_ref, w_ref, qd_ref, kd_ref, at_ref, eg_ref, o_ref, st_ref, state):
        @pl.when(pl.program_id(0) == 0)
        def _():
            state[...] = jnp.zeros_like(state)

        heads = range(DN_HEADS)
        hs = lambda h: slice(h * hd, (h + 1) * hd)
        st = [state[h] for h in heads]
        for h in heads:
            st_ref[h] = st[h]
        o, st_new = _dn_step(st, [qd_ref[:, hs(h)] for h in heads], [kd_ref[:, hs(h)] for h in heads],
                             [u_ref[:, hs(h)] for h in heads], [w_ref[:, hs(h)] for h in heads],
                             [at_ref[:, h * c:(h + 1) * c] for h in heads], [eg_ref[:, hs(h)] for h in heads])
        for h in heads:
            o_ref[:, hs(h)] = o[h]
            state[h] = st_new[h]

    wide, att, egl = _dn_chunk_specs()
    return _call(body, name, (nc,), [wide, wide, wide, wide, att, egl],
                 [wide, pl.BlockSpec((None, DN_HEADS, hd, hd), lambda i: (i, 0, 0, 0))],
                 [_sds((s, DN_WIDTH), F32), _sds((nc, DN_HEADS, hd, hd), F32)],
                 scratch=[pltpu.VMEM((DN_HEADS, hd, hd), F32)])(u, w, qd, kd, at, eg)


def dn_scan_bwd(u, w, qd, kd, at, eg, states, do, name):
    s = u.shape[0]
    c, hd = DN_CHUNK, DN_HEAD_DIM
    nc = s // c

    def body(u_ref, w_ref, qd_ref, kd_ref, at_ref, eg_ref, st_ref, do_ref,
             du_ref, dw_ref, dqd_ref, dkd_ref, dat_ref, deg_ref, dstate):
        @pl.when(pl.program_id(0) == 0)
        def _():
            dstate[...] = jnp.zeros_like(dstate)

        heads = range(DN_HEADS)
        hs = lambda h: slice(h * hd, (h + 1) * hd)
        asl = lambda h: slice(h * c, (h + 1) * c)
        _, vjp = jax.vjp(_dn_step, [st_ref[h] for h in heads], [qd_ref[:, hs(h)] for h in heads],
                         [kd_ref[:, hs(h)] for h in heads], [u_ref[:, hs(h)] for h in heads], [w_ref[:, hs(h)] for h in heads],
                         [at_ref[:, asl(h)] for h in heads], [eg_ref[:, hs(h)] for h in heads])
        dst, dqd, dkd, du, dw, dat, deg = vjp(([do_ref[:, hs(h)] for h in heads], [dstate[h] for h in heads]))
        for h in heads:
            dstate[h] = dst[h]
            du_ref[:, hs(h)], dw_ref[:, hs(h)], dqd_ref[:, hs(h)], dkd_ref[:, hs(h)] = du[h], dw[h], dqd[h], dkd[h]
            dat_ref[:, asl(h)] = dat[h]
            deg_ref[:, hs(h)] = deg[h]

    rev = lambda i: nc - 1 - i
    wide = pl.BlockSpec((c, DN_WIDTH), lambda i: (rev(i), 0))
    att = pl.BlockSpec((c, DN_HEADS * c), lambda i: (rev(i), 0))
    egl = pl.BlockSpec((8, DN_WIDTH), lambda i: (rev(i), 0))
    wsd = _sds((s, DN_WIDTH), F32)
    return _call(body, name, (nc,),
                 [wide, wide, wide, wide, att, egl, pl.BlockSpec((None, DN_HEADS, hd, hd), lambda i: (rev(i), 0, 0, 0)), wide],
                 [wide, wide, wide, wide, att, egl],
                 [wsd, wsd, wsd, wsd, _sds((s, DN_HEADS * c), F32), _sds((nc * 8, DN_WIDTH), F32)],
                 scratch=[pltpu.VMEM((DN_HEADS, hd, hd), F32)])(u, w, qd, kd, at, eg, states, do)


def _dn_out(o, zg, nw):
    n = o * lax.rsqrt(jnp.mean(o * o, axis=-1, keepdims=True) + NORM_EPS) * nw
    return n * (zg * _sigmoid(zg))


def dn_out_fwd(o, z, nw, name):
    s = o.shape[0]
    tm = _tile(s, 512)
    hd = DN_HEAD_DIM

    def body(o_ref, zg_ref, nw_ref, y_ref):
        for h in range(DN_HEADS):
            hs = slice(h * hd, (h + 1) * hd)
            y_ref[:, hs] = _dn_out(o_ref[:, hs], zg_ref[:, hs], nw_ref[...]).astype(BF16)

    return _call(body, name, (s // tm,),
                 [pl.BlockSpec((tm, DN_WIDTH), lambda i: (i, 0)), pl.BlockSpec((tm, DN_WIDTH), lambda i: (i, Z_ZG // DN_WIDTH)),
                  pl.BlockSpec((1, hd), lambda i: (0, 0))],
                 pl.BlockSpec((tm, DN_WIDTH), lambda i: (i, 0)), _sds((s, DN_WIDTH), BF16), sem=("parallel",))(o, z, nw)


def dn_out_bwd(o, z, nw, dycat, name):
    s = o.shape[0]
    tm = _tile(s, 512)
    hd = DN_HEAD_DIM

    def body(o_ref, zg_ref, nw_ref, dy_ref, do_ref, dzg_ref, dnw_ref):
        @pl.when(pl.program_id(0) == 0)
        def _():
            dnw_ref[...] = jnp.zeros_like(dnw_ref)

        for h in range(DN_HEADS):
            hs = slice(h * hd, (h + 1) * hd)
            _, vjp = jax.vjp(_dn_out, o_ref[:, hs], zg_ref[:, hs], nw_ref[...])
            do, dzg, dnw = vjp(dy_ref[:, hs])
            do_ref[:, hs] = do
            dzg_ref[:, hs] = dzg
            dnw_ref[...] += dnw

    wide = pl.BlockSpec((tm, DN_WIDTH), lambda i: (i, 0))
    wsd = _sds((s, DN_WIDTH), F32)
    return _call(body, name, (s // tm,),
                 [wide, pl.BlockSpec((tm, DN_WIDTH), lambda i: (i, Z_ZG // DN_WIDTH)), pl.BlockSpec((1, hd), lambda i: (0, 0)),
                  pl.BlockSpec((tm, DN_WIDTH), lambda i: (i, ATTN_WIDTH // DN_WIDTH))],
                 [wide, wide, pl.BlockSpec((1, hd), lambda i: (0, 0))], [wsd, wsd, _sds((1, hd), F32)])(o, z, nw, dycat)


def _s5_param_fn(are, aim, ldt, bre, bim):
    dt = jnp.exp(ldt)
    er = jnp.exp(are * dt)
    abr = er * jnp.cos(aim * dt)
    abi = er * jnp.sin(aim * dt)
    den = are * are + aim * aim
    cr = ((abr - 1.0) * are + abi * aim) / den
    ci = (abi * are - (abr - 1.0) * aim) / den
    return abr, abi, cr * bre - ci * bim, cr * bim + ci * bre


def s5_params_fwd(are, aim, ldt, bre, bim, name):
    p, hh = bre.shape

    def body(a_ref, b_ref, c_ref, d_ref, e_ref, o1, o2, o3, o4):
        o1[...], o2[...], o3[...], o4[...] = _s5_param_fn(a_ref[...], b_ref[...], c_ref[...], d_ref[...], e_ref[...])

    col = pl.BlockSpec((p, 1), lambda: (0, 0))
    mat = pl.BlockSpec((p, hh), lambda: (0, 0))
    return _call(body, name, (), [col, col, col, mat, mat], [col, col, mat, mat],
                 [_sds((p, 1), F32), _sds((p, 1), F32), _sds((p, hh), F32), _sds((p, hh), F32)])(are, aim, ldt, bre, bim)


def s5_params_bwd(are, aim, ldt, bre, bim, cts, name):
    p, hh = bre.shape

    def body(a_ref, b_ref, c_ref, d_ref, e_ref, g1, g2, g3, g4, o1, o2, o3, o4, o5):
        _, vjp = jax.vjp(_s5_param_fn, a_ref[...], b_ref[...], c_ref[...], d_ref[...], e_ref[...])
        o1[...], o2[...], o3[...], o4[...], o5[...] = vjp((g1[...], g2[...], g3[...], g4[...]))

    col = pl.BlockSpec((p, 1), lambda: (0, 0))
    mat = pl.BlockSpec((p, hh), lambda: (0, 0))
    csd, msd = _sds((p, 1), F32), _sds((p, hh), F32)
    return _call(body, name, (), [col, col, col, mat, mat, col, col, mat, mat], [col, col, col, mat, mat],
                 [csd, csd, csd, msd, msd])(are, aim, ldt, bre, bim, *cts)


def _cmul(ar, ai, br, bi):
    return ar * br - ai * bi, ar * bi + ai * br


S5_ROWS = 8
S5_UNROLL = 4


def _s5_tile_scan(xr, xi, ar, ai, reverse):
    row = lax.broadcasted_iota(jnp.int32, xr.shape, 0)
    d = 1
    while d < S5_ROWS:
        if reverse:
            sr, si = _shift_up(xr, d, row, S5_ROWS), _shift_up(xi, d, row, S5_ROWS)
        else:
            sr, si = _shift_down(xr, d, row), _shift_down(xi, d, row)
        pr, pi = _cmul(ar, ai, sr, si)
        xr, xi = xr + pr, xi + pi
        ar, ai = _cmul(ar, ai, ar, ai)
        d *= 2
    return xr, xi


def _s5_carry_powers(ar, ai, width, reverse):
    row = lax.broadcasted_iota(jnp.int32, (S5_ROWS, width), 0)
    at = row == (S5_ROWS - 1 if reverse else 0)
    return _s5_tile_scan(jnp.where(at, ar, 0.0), jnp.where(at, ai, 0.0), ar, ai, reverse)


def s5_scan_fwd(bu, abr, abi, name):
    s = bu.shape[0]
    wd = 256
    npb = S5_P // wd
    step = S5_ROWS * S5_UNROLL
    assert s % step == 0

    def body(br_ref, bi_ref, ar_ref, ai_ref, x_ref):
        ar, ai = ar_ref[...], ai_ref[...]
        pwr, pwi = _s5_carry_powers(ar, ai, wd, False)

        def loop(i, carry):
            cr, ci = carry
            base = pl.multiple_of(i * step, step)
            tiles = []
            for t in range(S5_UNROLL):
                rows = pl.ds(pl.multiple_of(base + t * S5_ROWS, S5_ROWS), S5_ROWS)
                tiles.append(_s5_tile_scan(br_ref[rows, :], bi_ref[rows, :], ar, ai, False))
            for t in range(S5_UNROLL):
                rows = pl.ds(pl.multiple_of(base + t * S5_ROWS, S5_ROWS), S5_ROWS)
                tr, ti = _cmul(pwr, pwi, cr, ci)
                xr, xi = tiles[t][0] + tr, tiles[t][1] + ti
                x_ref[0, rows, :] = xr
                x_ref[1, rows, :] = xi
                cr, ci = xr[S5_ROWS - 1:S5_ROWS, :], xi[S5_ROWS - 1:S5_ROWS, :]
            return cr, ci

        zero = jnp.zeros((1, wd), F32)
        lax.fori_loop(0, s // step, loop, (zero, zero))

    re = pl.BlockSpec((s, wd), lambda j: (0, j))
    im = pl.BlockSpec((s, wd), lambda j: (0, npb + j))
    av = pl.BlockSpec((1, wd), lambda j: (0, j))
    return _call(body, name, (npb,), [re, im, av, av], pl.BlockSpec((2, s, wd), lambda j: (0, 0, j)),
                 _sds((2, s, S5_P), F32), sem=("parallel",))(bu, bu, abr, abi)


def s5_scan_bwd(dx, x, abr, abi, name):
    s = dx.shape[0]
    wd = 128
    npb = S5_P // wd
    step = S5_ROWS * S5_UNROLL
    nsteps = s // step

    def body(dr_ref, di_ref, x_ref, ar_ref, ai_ref, g_ref, dar_ref, dai_ref):
        ar, ai = ar_ref[...], -ai_ref[...]
        pwr, pwi = _s5_carry_powers(ar, ai, wd, True)
        row = lax.broadcasted_iota(jnp.int32, (S5_ROWS, wd), 0)

        def loop(k, carry):
            cr, ci, accr, acci = carry
            base = pl.multiple_of((nsteps - 1 - k) * step, step)
            tiles = [None] * S5_UNROLL
            for t in range(S5_UNROLL):
                rows = pl.ds(pl.multiple_of(base + t * S5_ROWS, S5_ROWS), S5_ROWS)
                tiles[t] = _s5_tile_scan(dr_ref[rows, :], di_ref[rows, :], ar, ai, True)
            before = pl.ds(pl.multiple_of(jnp.maximum(base - S5_ROWS, 0), S5_ROWS), S5_ROWS)
            lr = jnp.where(base > 0, x_ref[0, before, :][S5_ROWS - 1:S5_ROWS, :], 0.0)
            li = jnp.where(base > 0, x_ref[1, before, :][S5_ROWS - 1:S5_ROWS, :], 0.0)
            prev = []
            for t in range(S5_UNROLL):
                rows = pl.ds(pl.multiple_of(base + t * S5_ROWS, S5_ROWS), S5_ROWS)
                xr, xi = x_ref[0, rows, :], x_ref[1, rows, :]
                prev.append((jnp.where(row >= 1, pltpu.roll(xr, 1, 0), lr), jnp.where(row >= 1, pltpu.roll(xi, 1, 0), li)))
                lr, li = xr[S5_ROWS - 1:S5_ROWS, :], xi[S5_ROWS - 1:S5_ROWS, :]
            for t in reversed(range(S5_UNROLL)):
                rows = pl.ds(pl.multiple_of(base + t * S5_ROWS, S5_ROWS), S5_ROWS)
                tr, ti = _cmul(pwr, pwi, cr, ci)
                gr, gi = tiles[t][0] + tr, tiles[t][1] + ti
                g_ref[0, rows, :] = gr
                g_ref[1, rows, :] = gi
                pr, pi = prev[t]
                accr = accr + (gr * pr + gi * pi)
                acci = acci + (gi * pr - gr * pi)
                cr, ci = gr[0:1, :], gi[0:1, :]
            return cr, ci, accr, acci

        zero = jnp.zeros((1, wd), F32)
        zacc = jnp.zeros((S5_ROWS, wd), F32)
        _, _, accr, acci = lax.fori_loop(0, nsteps, loop, (zero, zero, zacc, zacc))
        dar_ref[...] = jnp.sum(accr, axis=0, keepdims=True)
        dai_ref[...] = jnp.sum(acci, axis=0, keepdims=True)

    re = pl.BlockSpec((s, wd), lambda j: (0, j))
    im = pl.BlockSpec((s, wd), lambda j: (0, npb + j))
    av = pl.BlockSpec((1, wd), lambda j: (0, j))
    planes = pl.BlockSpec((2, s, wd), lambda j: (0, 0, j))
    asd = _sds((1, S5_P), F32)
    return _call(body, name, (npb,), [re, im, planes, av, av], [planes, av, av], [_sds((2, s, S5_P), F32), asd, asd],
                 sem=("parallel",))(dx, dx, x, abr, abi)


def _gelu(y):
    return 0.5 * y * (1.0 + jnp.tanh(math.sqrt(2.0 / math.pi) * (y + 0.044715 * y * y * y)))


def s5_out_fwd(ypre, z, dvec, glu_w, glu_b, name):
    s = ypre.shape[0]
    tm = _tile(s, 512)
    wd = S5_WIDTH

    def body(yp_ref, u_ref, d_ref, w_ref, b_ref, y_ref, o_ref):
        y = yp_ref[...] + d_ref[...] * u_ref[...]
        y_ref[...] = y
        g = _gelu(y)
        t = _dot(g.astype(BF16), w_ref[...]) + b_ref[...]
        o_ref[...] = (g * _sigmoid(t)).astype(BF16)

    row = pl.BlockSpec((tm, wd), lambda i: (i, 0))
    vec = pl.BlockSpec((1, wd), lambda i: (0, 0))
    return _call(body, name, (s // tm,),
                 [row, pl.BlockSpec((tm, wd), lambda i: (i, Z_S5 // wd)), vec, pl.BlockSpec((wd, wd), lambda i: (0, 0)), vec],
                 [row, row], [_sds((s, wd), F32), _sds((s, wd), BF16)], sem=("parallel",))(ypre, z, dvec, glu_w, glu_b)


def s5_out_bwd(y, z, glu_w, glu_b, dycat, name):
    s = y.shape[0]
    tm = _tile(s, 512)
    wd = S5_WIDTH

    def body(y_ref, u_ref, w_ref, b_ref, do_ref, dy_ref, dd_ref, dw_ref, db_ref):
        @pl.when(pl.program_id(0) == 0)
        def _():
            dd_ref[...] = jnp.zeros_like(dd_ref)
            dw_ref[...] = jnp.zeros_like(dw_ref)
            db_ref[...] = jnp.zeros_like(db_ref)

        g, gvjp = jax.vjp(_gelu, y_ref[...])
        gb = g.astype(BF16)
        sg = _sigmoid(_dot(gb, w_ref[...]) + b_ref[...])
        do = do_ref[...]
        dt = do * g * sg * (1.0 - sg)
        dtb = dt.astype(BF16)
        dg = do * sg + _dot(dtb, w_ref[...], NT)
        (dy,) = gvjp(dg)
        dy_ref[...] = dy
        dd_ref[...] += jnp.sum(dy * u_ref[...], axis=0, keepdims=True)
        dw_ref[...] += _dot(gb, dtb, TN)
        db_ref[...] += jnp.sum(dt, axis=0, keepdims=True)

    row = pl.BlockSpec((tm, wd), lambda i: (i, 0))
    vec = pl.BlockSpec((1, wd), lambda i: (0, 0))
    mat = pl.BlockSpec((wd, wd), lambda i: (0, 0))
    return _call(body, name, (s // tm,),
                 [row, pl.BlockSpec((tm, wd), lambda i: (i, Z_S5 // wd)), mat, vec,
                  pl.BlockSpec((tm, wd), lambda i: (i, (ATTN_WIDTH + DN_WIDTH) // wd))],
                 [row, vec, mat, vec], [_sds((s, wd), F32), _sds((1, wd), F32), _sds((wd, wd), F32), _sds((1, wd), F32)])(
        y, z, glu_w, glu_b, dycat)


def assemble_dz(dq, dkc, dkp, dvc, dvp, ddn, dzg, dus, dys, dvec, dzs, name):
    s = dq.shape[0]
    w = WINDOW
    nb = s // w
    nxt = lambda i: jnp.minimum(i + 1, nb - 1)

    def body(dq_ref, dkc_ref, dkp_ref, dvc_ref, dvp_ref, ddn_ref, dzg_ref, dus_ref, dys_ref, dv_ref, dzs_ref, o_ref):
        live = (pl.program_id(0) < nb - 1).astype(F32)
        o_ref[:, Z_Q:Z_K] = dq_ref[...].astype(BF16)
        o_ref[:, Z_K:Z_V] = (dkc_ref[...] + live * dkp_ref[...]).astype(BF16)
        o_ref[:, Z_V:Z_DN] = (dvc_ref[...] + live * dvp_ref[...]).astype(BF16)
        o_ref[:, Z_DN:Z_ZG] = ddn_ref[...].astype(BF16)
        o_ref[:, Z_ZG:Z_S5] = dzg_ref[...].astype(BF16)
        o_ref[:, Z_S5:Z_SM] = (dus_ref[...] + dv_ref[...] * dys_ref[...]).astype(BF16)
        o_ref[:, Z_SM:Z_ALL] = dzs_ref[...].astype(BF16)

    def blk(width, f=lambda i: i):
        return pl.BlockSpec((w, width), lambda i: (f(i), 0))

    return _call(body, name, (nb,),
                 [blk(ATTN_WIDTH), blk(ATTN_KV_WIDTH), blk(ATTN_KV_WIDTH, nxt), blk(ATTN_KV_WIDTH), blk(ATTN_KV_WIDTH, nxt),
                  blk(3 * DN_WIDTH), blk(DN_WIDTH), blk(S5_WIDTH), blk(S5_WIDTH), pl.BlockSpec((1, S5_WIDTH), lambda i: (0, 0)),
                  blk(128)],
                 blk(Z_ALL), _sds((s, Z_ALL), BF16), sem=("parallel",))(dq, dkc, dkp, dvc, dvp, ddn, dzg, dus, dys, dvec, dzs)


def _my_place():
    return lax.axis_index("x"), lax.axis_index("y"), lax.axis_index("c")


def _peer(place, p):
    x, y, c = place
    px = 1 - x if p & 4 else x
    py = 1 - y if p & 2 else y
    pc = 1 - c if p & 1 else c
    return (px, py, pc), 4 * px + 2 * py + pc


def exchange(arrays, scatter, name):
    na = len(arrays)

    def body(*refs):
        srcs, dsts = refs[:na], refs[na:2 * na]
        send_sems, recv_sems, local_sems = refs[2 * na:]
        place = _my_place()
        me = 4 * place[0] + 2 * place[1] + place[2]
        copies = []
        for k in range(na):
            mine = srcs[k].at[me] if scatter else srcs[k]
            loc = pltpu.make_async_copy(mine, dsts[k].at[me], local_sems.at[k])
            loc.start()
            copies.append(loc)
        sends = []
        for p in range(1, N_DEV):
            peer, pid = _peer(place, p)
            for k in range(na):
                src = srcs[k].at[pid] if scatter else srcs[k]
                cp = pltpu.make_async_remote_copy(src_ref=src, dst_ref=dsts[k].at[me], send_sem=send_sems.at[k, p - 1],
                                                  recv_sem=recv_sems.at[k, p - 1], device_id=peer,
                                                  device_id_type=pl.DeviceIdType.MESH)
                cp.start()
                sends.append(cp)
        for p in range(1, N_DEV):
            peer, pid = _peer(place, p)
            for k in range(na):
                src = srcs[k].at[me] if scatter else srcs[k]
                pltpu.make_async_remote_copy(src_ref=src, dst_ref=dsts[k].at[pid], send_sem=send_sems.at[k, p - 1],
                                             recv_sem=recv_sems.at[k, p - 1], device_id=peer,
                                             device_id_type=pl.DeviceIdType.MESH).wait_recv()
        for cp in sends:
            cp.wait_send()
        for cp in copies:
            cp.wait()

    outs = [_sds((N_DEV,) + tuple(a.shape[1:] if scatter else a.shape), a.dtype) for a in arrays]
    anyspec = pl.BlockSpec(memory_space=pl.ANY)
    return pl.pallas_call(
        body, name=name, in_specs=[anyspec] * na, out_specs=[anyspec] * na, out_shape=outs,
        scratch_shapes=[pltpu.SemaphoreType.DMA((na, N_DEV - 1)), pltpu.SemaphoreType.DMA((na, N_DEV - 1)),
                        pltpu.SemaphoreType.DMA((na,))])(*arrays)


_HBM = pl.BlockSpec(memory_space=pltpu.HBM)
_SEM = pl.BlockSpec(memory_space=pltpu.SEMAPHORE)
_DATAFLOW = pltpu.SideEffectType.DATAFLOW_SIDE_EFFECTING


def _split_copies(srcs, lands, send_sems, recv_sems, scatter, arriving):
    place = _my_place()
    me = 4 * place[0] + 2 * place[1] + place[2]
    out = []
    for p in range(1, N_DEV):
        peer, pid = _peer(place, p)
        for k in range(len(srcs)):
            i = k * (N_DEV - 1) + p - 1
            src = srcs[k].at[pid] if scatter else srcs[k]
            dst = lands[k].at[pid] if arriving else lands[k].at[me]
            out.append(pltpu.make_async_remote_copy(src_ref=src, dst_ref=dst, send_sem=send_sems.at[i], recv_sem=recv_sems.at[i],
                                                    device_id=peer, device_id_type=pl.DeviceIdType.MESH))
    return out


def exchange_start(groups, scatter, name):
    arrays = [a for g in groups for a in g]
    na, ng = len(arrays), len(groups)
    first = [sum(len(g) for g in groups[:i]) for i in range(ng)]
    me = 4 * lax.axis_index("x") + 2 * lax.axis_index("y") + lax.axis_index("c")
    lands = []
    for a in arrays:
        own = lax.dynamic_index_in_dim(a, me, 0, keepdims=True) if scatter else a[None]
        shape = (N_DEV,) + tuple(own.shape[1:])
        land = lax.dynamic_update_slice(lax.empty(shape, a.dtype), own, (me,) + (0,) * (len(shape) - 1))
        lands.append(pltpu.with_memory_space_constraint(land, pltpu.HBM))
    srcs = [pltpu.with_memory_space_constraint(a, pltpu.HBM) for a in arrays]

    def body(*refs):
        src_refs, land_refs = refs[:na], refs[na:2 * na]
        sems = refs[2 * na:2 * na + 2 * ng]
        token = refs[-1]
        for i, g in enumerate(groups):
            sl = slice(first[i], first[i] + len(g))
            for send in _split_copies(src_refs[sl], land_refs[sl], sems[2 * i], sems[2 * i + 1], scatter, False):
                send.start()
        token[...] = jnp.zeros_like(token)

    sem_shapes = []
    for g in groups:
        sem_shapes += [pltpu.SemaphoreType.DMA((len(g) * (N_DEV - 1),))] * 2
    outs = pl.pallas_call(
        body, name=name,
        out_shape=(*sem_shapes, *[pltpu.HBM(a.shape, a.dtype) for a in srcs], *[pltpu.HBM(a.shape, a.dtype) for a in lands],
                   _sds((8, 128), F32)),
        in_specs=[_HBM] * (2 * na), out_specs=(*[_SEM] * (2 * ng), *[_HBM] * (2 * na), pl.BlockSpec(memory_space=pltpu.VMEM)),
        input_output_aliases={i: 2 * ng + i for i in range(2 * na)},
        compiler_params=pltpu.CompilerParams(has_side_effects=_DATAFLOW))(*srcs, *lands)
    src_out, land_out = outs[2 * ng:2 * ng + na], outs[2 * ng + na:2 * ng + 2 * na]
    handles = [(outs[2 * i], outs[2 * i + 1], src_out[first[i]:first[i] + len(g)], land_out[first[i]:first[i] + len(g)])
               for i, g in enumerate(groups)]
    return handles, outs[-1]


def exchange_wait(handle, scatter, name, after):
    send_sems, recv_sems, srcs, lands = handle
    na = len(srcs)

    def body(*refs):
        src_refs, land_refs = refs[:na], refs[na:2 * na]
        for send in _split_copies(src_refs, land_refs, refs[2 * na], refs[2 * na + 1], scatter, False):
            send.wait_send()
        for recv in _split_copies(src_refs, land_refs, refs[2 * na], refs[2 * na + 1], scatter, True):
            recv.wait_recv()

    outs = pl.pallas_call(
        body, name=name, out_shape=tuple(pltpu.HBM(a.shape, a.dtype) for a in (*srcs, *lands)),
        in_specs=[_HBM] * (2 * na) + [_SEM, _SEM, pl.BlockSpec(memory_space=pl.ANY)], out_specs=tuple([_HBM] * (2 * na)),
        input_output_aliases={i: i for i in range(2 * na)},
        compiler_params=pltpu.CompilerParams(has_side_effects=_DATAFLOW))(*srcs, *lands, send_sems, recv_sems, after)
    return list(outs[na:])


def _adamw(w, g, m, v):
    m = ADAM_B1 * m + (1.0 - ADAM_B1) * g
    v = ADAM_B2 * v + (1.0 - ADAM_B2) * (g * g)
    m_hat = m / (1.0 - ADAM_B1 ** ADAM_STEP)
    v_hat = v / (1.0 - ADAM_B2 ** ADAM_STEP)
    return -ADAM_LR * (m_hat / (jnp.sqrt(v_hat) + ADAM_EPS) + ADAM_WD * w), m, v


def reduce_adamw(parts, w4, m4, v4, layer, name, stacked=None):
    nl, r, c = w4.shape
    tr = _tile(r, 256)

    def body(p_ref, w_ref, m_ref, v_ref, *rest):
        g_ref, d_ref, nm_ref, nv_ref = rest[-4:]
        g = p_ref[0].astype(F32)
        for d in range(1, N_DEV):
            g = g + p_ref[d].astype(F32)
        g_ref[...] = g
        d_ref[...], nm_ref[...], nv_ref[...] = _adamw(w_ref[...], g, m_ref[...], v_ref[...])

    lay = pl.BlockSpec((None, tr, c), lambda i: (layer, i, 0))
    osd = _sds((nl, r, c), F32)
    ins = [parts, w4, m4, v4] + (list(stacked) if stacked is not None else [])
    specs = [pl.BlockSpec((N_DEV, tr, c), lambda i: (0, i, 0)), lay, lay, lay]
    specs += [pl.BlockSpec(memory_space=pl.ANY)] * (len(ins) - 4)
    return pl.pallas_call(
        body, name=name, grid=(r // tr,), in_specs=specs, out_specs=[lay, lay, lay, lay], out_shape=[osd, osd, osd, osd],
        input_output_aliases={4 + k: k for k in range(len(ins) - 4)},
        compiler_params=pltpu.CompilerParams(dimension_semantics=("parallel",), vmem_limit_bytes=VMEM_LIMIT))(*ins)


_SM_NAT = ATTN_WIDTH + 2 * ATTN_KV_WIDTH + 4 * DN_WIDTH


def _win_to_zall(w):
    pad = jnp.zeros((w.shape[0], 128 - 2 * DN_HEADS), w.dtype)
    return jnp.concatenate([w[:, :_SM_NAT], w[:, _SM_NAT + 2 * DN_HEADS:], w[:, _SM_NAT:_SM_NAT + 2 * DN_HEADS], pad], axis=1)


def _zall_to_win(g):
    return jnp.concatenate([g[:, :Z_ZG + DN_WIDTH], g[:, Z_SM:Z_SM + 2 * DN_HEADS], g[:, Z_S5:Z_SM]], axis=1)


def _block_diag(t):
    g, a, b = t.shape
    eye = jnp.eye(g, dtype=t.dtype)
    return (t[:, :, None, :] * eye[:, None, :, None]).reshape(g * a, g * b)


def _block_diag_extract(m, g):
    a, b = m.shape[0] // g, m.shape[1] // g
    eye = jnp.eye(g, dtype=m.dtype)
    return jnp.sum(m.reshape(g, a, g, b) * eye[:, None, :, None], axis=2)


def _rope_tables(s):
    half = HEAD_DIM // 2
    inv_freq = ROPE_THETA ** (-jnp.arange(half, dtype=F32) / half)
    ang = jnp.arange(s, dtype=F32)[:, None] * inv_freq[None, :]
    cos, sin = jnp.cos(ang), jnp.sin(ang)
    return jnp.concatenate([cos, cos], axis=1), jnp.concatenate([-sin, sin], axis=1)


def _row(v):
    return v.reshape(1, -1)


def _ffn_fwd(x, g_pre, g_post, weight, tag):
    h = rmsnorm_fwd(x, g_pre, f"{tag}_norm")
    wg, wu = weight(f"{tag}_w_gate", h), weight(f"{tag}_w_up", h)
    a, b, u = ffn_up(h, wg, wu, f"{tag}_up", dep=weight("token", None))
    wd = weight(f"{tag}_w_down", u)
    y, xn = down_norm(u, wd, x, g_post, FFN_RES_WEIGHT, f"{tag}_down")
    return xn, (x, h, a, b, u, y, wg, wu, wd)


def _ffn_bwd(dxn, saved, g_pre, g_post, on_grads, tag):
    x, h, a, b, u, y, wg, wu, wd = saved
    dy, dg_post = norm_bwd(dxn, y, g_post, FFN_RES_WEIGHT, None, BF16, f"{tag}_bnorm_post")
    da, db = ffn_down_bwd(dy, wd, a, b, f"{tag}_bdown")
    dwd = mm_tn(u, dy[None], BF16, f"{tag}_dwd", tn=2048, tk=2048)
    on_grads({f"{tag}_w_down": dwd})
    dwg = mm_tn(h[None], da, BF16, f"{tag}_dwg", tk=2048)
    dwu = mm_tn(h[None], db, BF16, f"{tag}_dwu", tk=2048)
    tok = on_grads({f"{tag}_w_gate": dwg, f"{tag}_w_up": dwu})
    dh = mm_nt_acc([(da, wg), (db, wu)], f"{tag}_dh", tm=512, tn=2048, dep=tok)
    dx, dg_pre = norm_bwd(dh, x, g_pre, 1.0, dxn, F32, f"{tag}_bnorm_pre")
    return dx, dict(g_pre=dg_pre, g_post=dg_post)


def _s5_layouts(p):
    are, aim = p["s5_a_re"].reshape(S5_P, 1), p["s5_a_im"].reshape(S5_P, 1)
    ldt = jnp.repeat(p["s5_log_dt"], S5_STATE).reshape(S5_P, 1)
    bre, bim = p["s5_b_re"].reshape(S5_P, S5_GROUP_CH), p["s5_b_im"].reshape(S5_P, S5_GROUP_CH)
    return are, aim, ldt, bre, bim


def _mix_fwd(x, p, weight, cos_f, sin_s, tag):
    h = rmsnorm_fwd(x, _row(p["mix_norm_pre"]), f"{tag}_norm")
    w_all, glu_w = weight("w_all", h), weight("s5_glu_w", h)
    z = mm_nn(h, w_all, f"{tag}_win", tm=1024)
    y_attn = attn_fwd(z, cos_f, sin_s, p["attn_sinks"], f"{tag}_attn")
    alog_b = jnp.broadcast_to(p["dn_a_log"][:, None], (DN_HEADS, 128))
    dtb_b = jnp.broadcast_to(p["dn_dt_bias"][:, None], (DN_HEADS, 128))
    conv_w = weight("dn_conv_w", h)
    qkv = dn_pre_fwd(z, conv_w, f"{tag}_dnpre")
    bb, gb = dn_gates_fwd(z, alog_b, dtb_b, f"{tag}_dngate")
    u, w, qd, kd, at, eg = dn_intra_fwd(qkv, gb, bb, f"{tag}_dnintra")
    o, states = dn_scan_fwd(u, w, qd, kd, at, eg, f"{tag}_dnscan")
    y_dn = dn_out_fwd(o, z, _row(p["dn_norm_w"]), f"{tag}_dnout")
    s5cols = _s5_layouts(p)
    abr, abi, bbr, bbi = s5_params_fwd(*s5cols, f"{tag}_s5par")
    tb = lambda t: jnp.transpose(t.reshape(S5_GROUPS, S5_STATE, S5_GROUP_CH), (0, 2, 1))
    b_blk = jnp.concatenate([_block_diag(tb(bbr)), _block_diag(tb(bbi))], axis=1).astype(BF16)
    tc = lambda t: jnp.transpose(t, (0, 2, 1))
    c_blk = jnp.concatenate([_block_diag(tc(p["s5_c_re"])), -_block_diag(tc(p["s5_c_im"]))], axis=0).astype(BF16)
    bu = mm_nn(z, b_blk, f"{tag}_s5bu", tn=1024, col0=Z_S5)
    xs = s5_scan_fwd(bu, abr.reshape(1, S5_P), abi.reshape(1, S5_P), f"{tag}_s5scan")
    ypre = mm_nn_acc(xs, c_blk.reshape(2, S5_P, S5_WIDTH), f"{tag}_s5c")
    y5, y_s5 = s5_out_fwd(ypre, z, _row(p["s5_d"]), glu_w, _row(p["s5_glu_b"]), f"{tag}_s5out")
    ycat = jnp.concatenate([y_attn, y_dn, y_s5], axis=1)
    w_out = weight("w_out", ycat)
    mixed, xn = down_norm(ycat[None], w_out[None], x, _row(p["mix_norm_post"]), 1.0, f"{tag}_wout")
    saved = dict(x=x, h=h, z=z, qkv=qkv, bb=bb, gb=gb, dn=(u, w, qd, kd, at, eg), states=states, o=o, s5cols=s5cols,
                 abr=abr, abi=abi, b_blk=b_blk, c_blk=c_blk, xs=xs, y5=y5, ycat=ycat, mixed=mixed,
                 alog_b=alog_b, dtb_b=dtb_b, w_all=w_all, w_out=w_out, glu_w=glu_w, conv_w=conv_w)
    return xn, saved


def _mix_bwd(dxn, sv, p, on_grads, cos_f, sin_s, tag):
    z = sv["z"]
    w_all, w_out, glu_w = sv["w_all"], sv["w_out"], sv["glu_w"]
    g = {}
    dmixed, g["mix_norm_post"] = norm_bwd(dxn, sv["mixed"], _row(p["mix_norm_post"]), 1.0, None, BF16, f"{tag}_bnorm_post")
    g["w_out"] = mm_tn(sv["ycat"][None], dmixed[None], BF16, f"{tag}_dwout", tn=1024)[0]
    dycat = mm_nt_acc([(dmixed[None], w_out[None])], f"{tag}_dycat", tn=1024)
    dq, dkc, dkp, dvc, dvp, dsink = attn_bwd(z, cos_f, sin_s, p["attn_sinks"], dycat, f"{tag}_battn")
    g["attn_sinks"] = dsink[:, 0]
    do, dzg, dnw = dn_out_bwd(sv["o"], z, _row(p["dn_norm_w"]), dycat, f"{tag}_bdnout")
    g["dn_norm_w"] = dnw[0]
    cts = dn_scan_bwd(*sv["dn"], sv["states"], do, f"{tag}_bdnscan")
    dqn, dkn, dvn, dgb, dbb = dn_intra_bwd(sv["qkv"], sv["gb"], sv["bb"], cts, f"{tag}_bdnintra")
    dzs, dal, ddt = dn_gates_bwd(z, sv["alog_b"], sv["dtb_b"], dbb, dgb, f"{tag}_bdngate")
    g["dn_a_log"], g["dn_dt_bias"] = dal[:, 0], ddt[:, 0]
    ddn, g["dn_conv_w"] = dn_pre_bwd(z, sv["conv_w"], jnp.concatenate([dqn, dkn, dvn], axis=1), f"{tag}_bdnpre")
    dy5, dd, dglu, dglub = s5_out_bwd(sv["y5"], z, glu_w, _row(p["s5_glu_b"]), dycat, f"{tag}_bs5out")
    g["s5_d"], g["s5_glu_w"], g["s5_glu_b"] = dd[0], dglu, dglub[0]
    dxs = mm_nt_acc([(dy5[None], sv["c_blk"][None])], f"{tag}_bs5c", tn=1024)
    dc_blk = mm_tn(sv["xs"], dy5[None], F32, f"{tag}_ds5c", tk=256)
    ex = lambda m: jnp.transpose(_block_diag_extract(m, S5_GROUPS), (0, 2, 1))
    g["s5_c_re"], g["s5_c_im"] = ex(dc_blk[0]), -ex(dc_blk[1])
    dbu, dar, dai = s5_scan_bwd(dxs, sv["xs"], sv["abr"].reshape(1, S5_P), sv["abi"].reshape(1, S5_P), f"{tag}_bs5scan")
    b_planes = jnp.transpose(sv["b_blk"].reshape(S5_WIDTH, 2, S5_P), (1, 0, 2))
    dus = mm_nt_acc([(dbu, b_planes)], f"{tag}_bs5bu")
    u_s5 = z[:, Z_S5:Z_SM]
    db_blk = mm_tn(u_s5[None], dbu, F32, f"{tag}_ds5b", tn=1024)
    exb = lambda m: jnp.transpose(_block_diag_extract(m, S5_GROUPS), (0, 2, 1)).reshape(S5_P, S5_GROUP_CH)
    dcols = s5_params_bwd(*sv["s5cols"], (dar.reshape(S5_P, 1), dai.reshape(S5_P, 1), exb(db_blk[0]), exb(db_blk[1])),
                          f"{tag}_bs5par")
    g["s5_a_re"] = dcols[0].reshape(S5_GROUPS, S5_STATE)
    g["s5_a_im"] = dcols[1].reshape(S5_GROUPS, S5_STATE)
    g["s5_log_dt"] = jnp.sum(dcols[2].reshape(S5_GROUPS, S5_STATE), axis=1)
    g["s5_b_re"] = dcols[3].reshape(S5_GROUPS, S5_STATE, S5_GROUP_CH)
    g["s5_b_im"] = dcols[4].reshape(S5_GROUPS, S5_STATE, S5_GROUP_CH)
    dz = assemble_dz(dq, dkc, dkp, dvc, dvp, ddn, dzg, dus, dy5, _row(p["s5_d"]), dzs, f"{tag}_dz")
    g["w_all"] = mm_tn(sv["h"][None], dz[None], BF16, f"{tag}_dwin")[0]
    dwin = _zall_to_win(g.pop("w_all"))
    d_model = dwin.shape[0]
    tok = on_grads({"w_in": jnp.transpose(dwin.reshape(d_model, N_DEV, IN_WIDTH // N_DEV), (1, 0, 2)),
                    "s5_glu_w": g.pop("s5_glu_w").astype(BF16).reshape(N_DEV, S5_WIDTH // N_DEV, S5_WIDTH),
                    "w_out": g.pop("w_out").reshape(N_DEV, MIX_WIDTH // N_DEV, d_model)})
    dh = mm_nt_acc([(dz[None], w_all[None])], f"{tag}_dh", tn=1024, dep=tok)
    dx, g["mix_norm_pre"] = norm_bwd(dh, sv["x"], _row(p["mix_norm_pre"]), 1.0, dxn, F32, f"{tag}_bnorm_pre")
    return dx, g


BIG = ("ff1_w_gate", "ff1_w_up", "ff1_w_down", "w_in", "s5_glu_w", "w_out", "ff2_w_gate", "ff2_w_up", "ff2_w_down")
SMALL = ("ff1_norm_pre", "ff1_norm_post", "mix_norm_pre", "attn_sinks", "dn_conv_w", "dn_a_log", "dn_dt_bias", "dn_norm_w",
         "s5_a_re", "s5_a_im", "s5_log_dt", "s5_b_re", "s5_b_im", "s5_c_re", "s5_c_im", "s5_d", "s5_glu_b",
         "mix_norm_post", "ff2_norm_pre", "ff2_norm_post")
GATHER_GROUPS = (("ff1_w_gate", "ff1_w_up"), ("ff1_w_down",), ("w_in", "s5_glu_w", "dn_conv_w"), ("w_out",),
                 ("ff2_w_gate", "ff2_w_up"), ("ff2_w_down",))
WEIGHTS = ("ff1_norm_pre", "ff1_w_gate", "ff1_w_up", "ff1_w_down", "ff1_norm_post", "mix_norm_pre", "w_in", "attn_sinks",
           "dn_conv_w", "dn_a_log", "dn_dt_bias", "dn_norm_w", "s5_a_re", "s5_a_im", "s5_log_dt", "s5_b_re", "s5_b_im",
           "s5_c_re", "s5_c_im", "s5_d", "s5_glu_w", "s5_glu_b", "w_out", "mix_norm_post", "ff2_norm_pre", "ff2_w_gate",
           "ff2_w_up", "ff2_w_down", "ff2_norm_post")


def _pack(parts):
    rows = []
    for a in parts:
        n = a.size
        r = -(-n // 1024) * 8
        rows.append(jnp.pad(a.reshape(-1), (0, r * 128 - n)).reshape(r, 128))
    return jnp.concatenate(rows, axis=0)


def _unpack(mat, shapes):
    out, off = [], 0
    for shp in shapes:
        n = int(np.prod(shp))
        r = -(-n // 1024) * 8
        out.append(mat[off:off + r].reshape(-1)[:n].reshape(shp))
        off += r
    return out


def local_step(x, target, smalls, weight, on_grads):
    depth = len(smalls)
    cos_f, sin_s = _rope_tables(x.shape[0])
    xs = x
    saved = []
    for l in range(depth):
        p = smalls[l]
        wl = functools.partial(weight, l)
        xs, s1 = _ffn_fwd(xs, _row(p["ff1_norm_pre"]), _row(p["ff1_norm_post"]), wl, "ff1")
        xs, s2 = _mix_fwd(xs, p, wl, cos_f, sin_s, "mix")
        xs, s3 = _ffn_fwd(xs, _row(p["ff2_norm_pre"]), _row(p["ff2_norm_post"]), wl, "ff2")
        saved.append((s1, s2, s3))

    loss_vec, dx = loss_and_grad(xs, target, "loss")

    small_g = [None] * depth
    for l in reversed(range(depth)):
        p = smalls[l]
        gl = functools.partial(on_grads, l)
        s1, s2, s3 = saved[l]
        dx, g3 = _ffn_bwd(dx, s3, _row(p["ff2_norm_pre"]), _row(p["ff2_norm_post"]), gl, "ff2")
        dx, g2 = _mix_bwd(dx, s2, p, gl, cos_f, sin_s, "mix")
        dx, g1 = _ffn_bwd(dx, s1, _row(p["ff1_norm_pre"]), _row(p["ff1_norm_post"]), gl, "ff1")
        sg = {n: g2[n] for n in SMALL if n in g2}
        sg.update(ff1_norm_pre=g1["g_pre"][0], ff1_norm_post=g1["g_post"][0], ff2_norm_pre=g3["g_pre"][0], ff2_norm_post=g3["g_post"][0],
                  mix_norm_pre=g2["mix_norm_pre"][0], mix_norm_post=g2["mix_norm_post"][0])
        small_g[l] = sg
    return loss_vec, dx, small_g


def kernel(x, ff1_norm_pre, ff1_w_gate, ff1_w_up, ff1_w_down, ff1_norm_post, mix_norm_pre, w_in, attn_sinks, dn_conv_w, dn_a_log, dn_dt_bias, dn_norm_w, s5_a_re, s5_a_im, s5_log_dt, s5_b_re, s5_b_im, s5_c_re, s5_c_im, s5_d, s5_glu_w, s5_glu_b, w_out, mix_norm_post, ff2_norm_pre, ff2_w_gate, ff2_w_up, ff2_w_down, ff2_norm_post, loss_target, m_ff1_norm_pre, m_ff1_w_gate, m_ff1_w_up, m_ff1_w_down, m_ff1_norm_post, m_mix_norm_pre, m_w_in, m_attn_sinks, m_dn_conv_w, m_dn_a_log, m_dn_dt_bias, m_dn_norm_w, m_s5_a_re, m_s5_a_im, m_s5_log_dt, m_s5_b_re, m_s5_b_im, m_s5_c_re, m_s5_c_im, m_s5_d, m_s5_glu_w, m_s5_glu_b, m_w_out, m_mix_norm_post, m_ff2_norm_pre, m_ff2_w_gate, m_ff2_w_up, m_ff2_w_down, m_ff2_norm_post, v_ff1_norm_pre, v_ff1_w_gate, v_ff1_w_up, v_ff1_w_down, v_ff1_norm_post, v_mix_norm_pre, v_w_in, v_attn_sinks, v_dn_conv_w, v_dn_a_log, v_dn_dt_bias, v_dn_norm_w, v_s5_a_re, v_s5_a_im, v_s5_log_dt, v_s5_b_re, v_s5_b_im, v_s5_c_re, v_s5_c_im, v_s5_d, v_s5_glu_w, v_s5_glu_b, v_w_out, v_mix_norm_post, v_ff2_norm_pre, v_ff2_w_gate, v_ff2_w_up, v_ff2_w_down, v_ff2_norm_post):
    args = dict(locals())
    W = {n: args[n] for n in WEIGHTS}
    M = {n: args["m_" + n] for n in WEIGHTS}
    V = {n: args["v_" + n] for n in WEIGHTS}
    depth = ff1_norm_pre.shape[0]
    d_model = x.shape[2]
    me = 4 * lax.axis_index("x") + 2 * lax.axis_index("y") + lax.axis_index("c")

    conv_sh = dn_conv_w.shape[2]

    def small_params(l):
        return {n: W[n][l] for n in SMALL if n != "dn_conv_w"}

    gathered_w, gather_handles, tokens = {}, {}, []
    group_of = {n: i for i, grp in enumerate(GATHER_GROUPS) for n in grp}

    def start_gather(l):
        shards = {n: cast_bf16(W[n], l, f"cast_{n}") for n in BIG}
        shards["dn_conv_w"] = dn_conv_w[l]
        handles, tok = exchange_start([[shards[n] for n in grp] for grp in GATHER_GROUPS], False, f"gather_start_l{l}")
        gather_handles.update({(l, i): h for i, h in enumerate(handles)})
        tokens.append(tok)

    def weight(l, name, after):
        if name == "token":
            return tokens.pop() if tokens else None
        key = "w_in" if name == "w_all" else name
        i = group_of[key]
        if (l, i) in gather_handles:
            got = dict(zip(GATHER_GROUPS[i], exchange_wait(gather_handles.pop((l, i)), False, f"gather_wait_l{l}_g{i}", after)))
            if i == 0 and l + 1 < depth:
                start_gather(l + 1)
            if "w_in" in got:
                got["dn_conv_w"] = jnp.transpose(got["dn_conv_w"], (1, 0, 2)).reshape(DN_CONV, N_DEV * conv_sh)
                got["w_all"] = _win_to_zall(jnp.transpose(got["w_in"], (1, 0, 2)).reshape(d_model, IN_WIDTH))
                got["s5_glu_w"] = got["s5_glu_w"].reshape(S5_WIDTH, S5_WIDTH)
            if "w_out" in got:
                got["w_out"] = got["w_out"].reshape(MIX_WIDTH, d_model)
            gathered_w.update({(l, n): a for n, a in got.items()})
        return gathered_w[(l, name)]

    stacked = {n: None for n in BIG}
    in_flight = []

    def finish_scatter(after):
        l, names, handle = in_flight.pop(0)
        recv = dict(zip(names, exchange_wait(handle, True, f"scatter_wait_l{l}_{names[0]}", after)))
        for n in names:
            stacked[n] = reduce_adamw(recv[n], W[n], M[n], V[n], l, f"adamw_{n}", stacked[n])

    def on_grads(l, grads):
        names = tuple(grads)
        (handle,), tok = exchange_start([[grads[n] for n in names]], True, f"scatter_start_l{l}_{names[0]}")
        in_flight.append((l, names, handle))
        if len(in_flight) > 1:
            finish_scatter(tok)
        return tok

    start_gather(0)
    tokens.clear()
    smalls = [small_params(l) for l in range(depth)]
    loss_vec, dx, small_g = local_step(x[0], loss_target[0], smalls, weight, on_grads)
    shapes = [(depth,) + ((DN_CONV, N_DEV * conv_sh) if n == "dn_conv_w" else tuple(W[n].shape[1:])) for n in SMALL]
    packed = _pack([jnp.stack([small_g[l][n] for l in range(depth)]) for n in SMALL])
    (small_handle,), small_tok = exchange_start([[packed]], False, "gather_small_start")
    while in_flight:
        finish_scatter(small_tok)
    gathered = exchange_wait(small_handle, False, "gather_small_wait", stacked[BIG[0]][0])[0]
    loss = lax.psum(loss_vec[0, 0], ("x", "y", "c"))

    def shard_of(n, full):
        return lax.dynamic_slice_in_dim(full, me * conv_sh, conv_sh, axis=2) if n == "dn_conv_w" else full

    conv_pad = lambda t: jnp.tile(t, (1, 1, N_DEV))
    wp = _pack([conv_pad(W[n]) if n == "dn_conv_w" else W[n] for n in SMALL])
    mp = _pack([conv_pad(M[n]) if n == "dn_conv_w" else M[n] for n in SMALL])
    vp = _pack([conv_pad(V[n]) if n == "dn_conv_w" else V[n] for n in SMALL])
    sm = reduce_adamw(gathered, wp[None], mp[None], vp[None], 0, "adamw_small")
    small_out = [dict(zip(SMALL, [shard_of(n, t) for n, t in zip(SMALL, _unpack(o[0], shapes))])) for o in sm]

    outs = []
    for kind in range(4):
        for n in WEIGHTS:
            if n in BIG:
                outs.append(stacked[n][kind])
            else:
                outs.append(small_out[kind][n])
    return (loss, dx[None], *outs)
```
